```python
import math
import jax, jax.numpy as jnp
from jax import lax
import numpy as np

D_MODEL = 1024
BATCH = 8
SEQ = 8192
DEPTH = 1

SB_HEADS = 8
SB_HEAD_DIM = 64
SB_WIDTH = SB_HEADS * SB_HEAD_DIM
MLA_HEADS = 8
MLA_NOPE_DIM = 64
MLA_ROPE_DIM = 32
MLA_V_DIM = 64
MLA_Q_RANK = 384
MLA_KV_RANK = 256
MLA_WIDTH = MLA_HEADS * MLA_V_DIM
ROPE_THETA = 10000.0
D_FF = -(-(8 * D_MODEL) // (3 * 256)) * 256
N_BRANCHES = 2
BLOCK_Q = 128
EPS = 1e-6

IN_SPLITS = [
    SB_WIDTH,
    SB_WIDTH,
    SB_WIDTH,
    MLA_Q_RANK,
    MLA_KV_RANK,
    MLA_ROPE_DIM,
    N_BRANCHES * D_MODEL,
]
IN_OFFSETS = list(np.cumsum(IN_SPLITS)[:-1])
D_IN = int(sum(IN_SPLITS))

kernel_name = "sandwich_gated_sb_mla_swiglu_block"


def rms_norm(x, g):
    xf = x.astype(jnp.float32)
    y = xf * lax.rsqrt(jnp.mean(xf * xf, axis=-1, keepdims=True) + EPS)
    return (y * g.astype(jnp.float32)).astype(x.dtype)


def rope_tables(positions, dtype):
    inv_freq = ROPE_THETA ** (-jnp.arange(0, MLA_ROPE_DIM, 2, dtype=jnp.float32) / MLA_ROPE_DIM)
    ang = positions.astype(jnp.float32)[..., None] * inv_freq
    return jnp.cos(ang)[:, :, None, :].astype(dtype), jnp.sin(ang)[:, :, None, :].astype(dtype)


def apply_rope(x, cos, sin):
    x1, x2 = jnp.split(x, 2, axis=-1)
    return jnp.concatenate([x1 * cos - x2 * sin, x2 * cos + x1 * sin], axis=-1)


def stick_breaking_block(q_blk, k_pre, v_pre, q_start):
    tq, tk = q_blk.shape[1], k_pre.shape[1]
    z = jnp.einsum('bqhd,bkhd->bhqk', q_blk, k_pre).astype(jnp.float32) / math.sqrt(SB_HEAD_DIM)
    qpos = q_start + jnp.arange(tq)
    kpos = jnp.arange(tk)
    causal = kpos[None, :] < qpos[:, None]
    log_1m_beta = jnp.where(causal, jax.nn.log_sigmoid(-z), 0.0)
    rev_excl = lax.cumsum(log_1m_beta, axis=3, reverse=True) - log_1m_beta
    log_a = jax.nn.log_sigmoid(z) + rev_excl
    a = jnp.where(causal, jnp.exp(log_a), 0.0)
    return jnp.einsum('bhqk,bkhd->bqhd', a.astype(v_pre.dtype), v_pre)


def mla_block(qn_blk, qr_blk, kn_pre, kr_pre, v_pre, q_start):
    tq, tk = qn_blk.shape[1], kn_pre.shape[1]
    s = (jnp.einsum('bqhd,bkhd->bhqk', qn_blk, kn_pre)
         + jnp.einsum('bqhr,bkr->bhqk', qr_blk, kr_pre)).astype(jnp.float32)
    s = s / math.sqrt(MLA_NOPE_DIM + MLA_ROPE_DIM)
    causal = jnp.arange(tk)[None, :] <= (q_start + jnp.arange(tq))[:, None]
    s = jnp.where(causal, s, jnp.finfo(jnp.float32).min)
    p = jax.nn.softmax(s, axis=-1)
    return jnp.einsum('bhqk,bkhd->bqhd', p.astype(v_pre.dtype), v_pre)


def _fwd_setup_inputs(seed: int = 0) -> dict:
    key = jax.random.key(seed)
    ks = jax.random.split(key, 20)

    def w(k, shape, fan_in):
        return jax.random.normal(k, shape, jnp.float32) * fan_in ** -0.5

    def gain(k, n):
        return 1.0 + 0.02 * jax.random.normal(k, (DEPTH, n), jnp.float32)

    x = jax.random.normal(ks[0], (BATCH, SEQ, D_MODEL), jnp.float32)
    offset = jax.random.randint(ks[1], (BATCH, 1), 0, 4096, dtype=jnp.int32)
    positions = (offset + jnp.arange(SEQ, dtype=jnp.int32)[None, :]).astype(jnp.int32)
    return {
        "x": x,
        "positions": positions,
        "norm_mix_pre": gain(ks[2], D_MODEL),
        "norm_mix_post": gain(ks[3], D_MODEL),
        "w_in": w(ks[4], (DEPTH, D_MODEL, D_IN), D_MODEL),
        "b_gate": 0.02 * jax.random.normal(ks[5], (DEPTH, N_BRANCHES * D_MODEL), jnp.float32),
        "q_norm": gain(ks[6], MLA_Q_RANK),
        "w_uq": w(ks[7], (DEPTH, MLA_Q_RANK, MLA_HEADS * (MLA_NOPE_DIM + MLA_ROPE_DIM)), MLA_Q_RANK),
        "kv_norm": gain(ks[8], MLA_KV_RANK),
        "w_ukv": w(ks[9], (DEPTH, MLA_KV_RANK, MLA_HEADS * (MLA_NOPE_DIM + MLA_V_DIM)), MLA_KV_RANK),
        "w_proj_sb": w(ks[10], (DEPTH, SB_WIDTH, D_MODEL), SB_WIDTH),
        "w_proj_mla": w(ks[11], (DEPTH, MLA_WIDTH, D_MODEL), MLA_WIDTH),
        "w_out": w(ks[12], (DEPTH, D_MODEL, D_MODEL), D_MODEL),
        "norm_ffn_pre": gain(ks[13], D_MODEL),
        "norm_ffn_post": gain(ks[14], D_MODEL),
        "w_gate_up": w(ks[15], (DEPTH, D_MODEL, 2 * D_FF), D_MODEL),
        "w_down": w(ks[16], (DEPTH, D_FF, D_MODEL), D_FF),
    }


def _fwd_reference(x, positions, norm_mix_pre, norm_mix_post, w_in, b_gate, q_norm, w_uq,
              kv_norm, w_ukv, w_proj_sb, w_proj_mla, w_out, norm_ffn_pre, norm_ffn_post,
              w_gate_up, w_down):
    B, S, _ = x.shape
    n_blocks = S // BLOCK_Q
    cos, sin = rope_tables(positions, x.dtype)

    for l in range(DEPTH):
        h = rms_norm(x, norm_mix_pre[l])
        proj = h @ w_in[l]
        q_sb, k_sb, v_sb, c_q, c_kv, k_rope, gate_logits = jnp.split(proj, IN_OFFSETS, axis=-1)

        q_sb = q_sb.reshape(B, S, SB_HEADS, SB_HEAD_DIM)
        k_sb = k_sb.reshape(B, S, SB_HEADS, SB_HEAD_DIM)
        v_sb = v_sb.reshape(B, S, SB_HEADS, SB_HEAD_DIM)

        q_mla = (rms_norm(c_q, q_norm[l]) @ w_uq[l]).reshape(B, S, MLA_HEADS, MLA_NOPE_DIM + MLA_ROPE_DIM)
        q_nope, q_rope = q_mla[..., :MLA_NOPE_DIM], q_mla[..., MLA_NOPE_DIM:]
        q_rope = apply_rope(q_rope, cos, sin)
        kv = (rms_norm(c_kv, kv_norm[l]) @ w_ukv[l]).reshape(B, S, MLA_HEADS, MLA_NOPE_DIM + MLA_V_DIM)
        k_nope, v_mla = kv[..., :MLA_NOPE_DIM], kv[..., MLA_NOPE_DIM:]
        k_rope = apply_rope(k_rope[:, :, None, :], cos, sin)[:, :, 0, :]

        sb_outs, mla_outs = [], []
        for i in range(n_blocks):
            s0, s1 = i * BLOCK_Q, (i + 1) * BLOCK_Q
            sb_outs.append(stick_breaking_block(q_sb[:, s0:s1], k_sb[:, :s1], v_sb[:, :s1], s0))
            mla_outs.append(mla_block(q_nope[:, s0:s1], q_rope[:, s0:s1], k_nope[:, :s1],
                                      k_rope[:, :s1], v_mla[:, :s1], s0))
        o_sb = jnp.concatenate(sb_outs, axis=1).reshape(B, S, SB_WIDTH)
        o_mla = jnp.concatenate(mla_outs, axis=1).reshape(B, S, MLA_WIDTH)

        gates = jax.nn.sigmoid(gate_logits + b_gate[l]).reshape(B, S, N_BRANCHES, D_MODEL)
        merged = gates[:, :, 0] * (o_sb @ w_proj_sb[l]) + gates[:, :, 1] * (o_mla @ w_proj_mla[l])
        y = merged @ w_out[l]
        x = x + rms_norm(y, norm_mix_post[l])

        h = rms_norm(x, norm_ffn_pre[l])
        g, u = jnp.split(h @ w_gate_up[l], 2, axis=-1)
        f = (jax.nn.silu(g) * u) @ w_down[l]
        x = x + rms_norm(f, norm_ffn_post[l])
    return x


import jax as _jax
import jax.numpy as _jnp

TWIN_FORMAT = 'train_step'
FWD_PARAMS = ['x', 'positions', 'norm_mix_pre', 'norm_mix_post', 'w_in', 'b_gate', 'q_norm', 'w_uq', 'kv_norm', 'w_ukv', 'w_proj_sb', 'w_proj_mla', 'w_out', 'norm_ffn_pre', 'norm_ffn_post', 'w_gate_up', 'w_down']
TWIN_WEIGHTS = ['norm_mix_pre', 'norm_mix_post', 'w_in', 'b_gate', 'q_norm', 'w_uq', 'kv_norm', 'w_ukv', 'w_proj_sb', 'w_proj_mla', 'w_out', 'norm_ffn_pre', 'norm_ffn_post', 'w_gate_up', 'w_down']
TWIN_DIFF_INPUT = 'x'
TWIN_INPUTS = ['x', 'positions', 'norm_mix_pre', 'norm_mix_post', 'w_in', 'b_gate', 'q_norm', 'w_uq', 'kv_norm', 'w_ukv', 'w_proj_sb', 'w_proj_mla', 'w_out', 'norm_ffn_pre', 'norm_ffn_post', 'w_gate_up', 'w_down', 'loss_target', 'm_norm_mix_pre', 'm_norm_mix_post', 'm_w_in', 'm_b_gate', 'm_q_norm', 'm_w_uq', 'm_kv_norm', 'm_w_ukv', 'm_w_proj_sb', 'm_w_proj_mla', 'm_w_out', 'm_norm_ffn_pre', 'm_norm_ffn_post', 'm_w_gate_up', 'm_w_down', 'v_norm_mix_pre', 'v_norm_mix_post', 'v_w_in', 'v_b_gate', 'v_q_norm', 'v_w_uq', 'v_kv_norm', 'v_w_ukv', 'v_w_proj_sb', 'v_w_proj_mla', 'v_w_out', 'v_norm_ffn_pre', 'v_norm_ffn_post', 'v_w_gate_up', 'v_w_down']
TWIN_OUTPUTS = ['loss', 'grad_x', 'grad_norm_mix_pre', 'grad_norm_mix_post', 'grad_w_in', 'grad_b_gate', 'grad_q_norm', 'grad_w_uq', 'grad_kv_norm', 'grad_w_ukv', 'grad_w_proj_sb', 'grad_w_proj_mla', 'grad_w_out', 'grad_norm_ffn_pre', 'grad_norm_ffn_post', 'grad_w_gate_up', 'grad_w_down', 'delta_norm_mix_pre', 'delta_norm_mix_post', 'delta_w_in', 'delta_b_gate', 'delta_q_norm', 'delta_w_uq', 'delta_kv_norm', 'delta_w_ukv', 'delta_w_proj_sb', 'delta_w_proj_mla', 'delta_w_out', 'delta_norm_ffn_pre', 'delta_norm_ffn_post', 'delta_w_gate_up', 'delta_w_down', 'new_m_norm_mix_pre', 'new_m_norm_mix_post', 'new_m_w_in', 'new_m_b_gate', 'new_m_q_norm', 'new_m_w_uq', 'new_m_kv_norm', 'new_m_w_ukv', 'new_m_w_proj_sb', 'new_m_w_proj_mla', 'new_m_w_out', 'new_m_norm_ffn_pre', 'new_m_norm_ffn_post', 'new_m_w_gate_up', 'new_m_w_down', 'new_v_norm_mix_pre', 'new_v_norm_mix_post', 'new_v_w_in', 'new_v_b_gate', 'new_v_q_norm', 'new_v_w_uq', 'new_v_kv_norm', 'new_v_w_ukv', 'new_v_w_proj_sb', 'new_v_w_proj_mla', 'new_v_w_out', 'new_v_norm_ffn_pre', 'new_v_norm_ffn_post', 'new_v_w_gate_up', 'new_v_w_down']
TWIN_LEAF_KINDS = {'loss': 'loss', 'grad_x': 'grad_x', 'grad_norm_mix_pre': 'grad_w', 'grad_norm_mix_post': 'grad_w', 'grad_w_in': 'grad_w', 'grad_b_gate': 'grad_w', 'grad_q_norm': 'grad_w', 'grad_w_uq': 'grad_w', 'grad_kv_norm': 'grad_w', 'grad_w_ukv': 'grad_w', 'grad_w_proj_sb': 'grad_w', 'grad_w_proj_mla': 'grad_w', 'grad_w_out': 'grad_w', 'grad_norm_ffn_pre': 'grad_w', 'grad_norm_ffn_post': 'grad_w', 'grad_w_gate_up': 'grad_w', 'grad_w_down': 'grad_w', 'delta_norm_mix_pre': 'delta_w', 'delta_norm_mix_post': 'delta_w', 'delta_w_in': 'delta_w', 'delta_b_gate': 'delta_w', 'delta_q_norm': 'delta_w', 'delta_w_uq': 'delta_w', 'delta_kv_norm': 'delta_w', 'delta_w_ukv': 'delta_w', 'delta_w_proj_sb': 'delta_w', 'delta_w_proj_mla': 'delta_w', 'delta_w_out': 'delta_w', 'delta_norm_ffn_pre': 'delta_w', 'delta_norm_ffn_post': 'delta_w', 'delta_w_gate_up': 'delta_w', 'delta_w_down': 'delta_w', 'new_m_norm_mix_pre': 'new_m', 'new_m_norm_mix_post': 'new_m', 'new_m_w_in': 'new_m', 'new_m_b_gate': 'new_m', 'new_m_q_norm': 'new_m', 'new_m_w_uq': 'new_m', 'new_m_kv_norm': 'new_m', 'new_m_w_ukv': 'new_m', 'new_m_w_proj_sb': 'new_m', 'new_m_w_proj_mla': 'new_m', 'new_m_w_out': 'new_m', 'new_m_norm_ffn_pre': 'new_m', 'new_m_norm_ffn_post': 'new_m', 'new_m_w_gate_up': 'new_m', 'new_m_w_down': 'new_m', 'new_v_norm_mix_pre': 'new_v', 'new_v_norm_mix_post': 'new_v', 'new_v_w_in': 'new_v', 'new_v_b_gate': 'new_v', 'new_v_q_norm': 'new_v', 'new_v_w_uq': 'new_v', 'new_v_kv_norm': 'new_v', 'new_v_w_ukv': 'new_v', 'new_v_w_proj_sb': 'new_v', 'new_v_w_proj_mla': 'new_v', 'new_v_w_out': 'new_v', 'new_v_norm_ffn_pre': 'new_v', 'new_v_norm_ffn_post': 'new_v', 'new_v_w_gate_up': 'new_v', 'new_v_w_down': 'new_v'}


def _forward(args):
    return _fwd_reference(*[args[k] for k in FWD_PARAMS])


def _output_shape():
    out = _jax.eval_shape(lambda: _forward(_fwd_setup_inputs(0)))
    return out.shape, out.dtype

N_MICROBATCH = 1
ADAM_LR = 0.001
ADAM_B1 = 0.9
ADAM_B2 = 0.999
ADAM_EPS = 1e-08
ADAM_WD = 0.01
ADAM_STEP = 10
PER_EXAMPLE_BATCH_AXIS = {'x': 0, 'positions': 0, 'loss_target': 0}
SHARED_INPUTS = []
_WEIGHT_DTYPES = {'norm_mix_pre': _jnp.float32, 'norm_mix_post': _jnp.float32, 'w_in': _jnp.float32, 'b_gate': _jnp.float32, 'q_norm': _jnp.float32, 'w_uq': _jnp.float32, 'kv_norm': _jnp.float32, 'w_ukv': _jnp.float32, 'w_proj_sb': _jnp.float32, 'w_proj_mla': _jnp.float32, 'w_out': _jnp.float32, 'norm_ffn_pre': _jnp.float32, 'norm_ffn_post': _jnp.float32, 'w_gate_up': _jnp.float32, 'w_down': _jnp.float32}
MOMENT_SCALE = {'norm_mix_pre': 7.854091e-01, 'norm_mix_post': 6.392644e+01, 'w_in': 3.963947e-01, 'b_gate': 2.579616e-01, 'q_norm': 2.609443e-01, 'w_uq': 1.880470e-01, 'kv_norm': 5.147569e-01, 'w_ukv': 2.382622e-01, 'w_proj_sb': 7.792521e-01, 'w_proj_mla': 1.949506e-01, 'w_out': 8.061688e-01, 'norm_ffn_pre': 8.007226e-01, 'norm_ffn_post': 6.388049e+01, 'w_gate_up': 3.563406e-01, 'w_down': 7.347758e-01}


def _to_microbatches(a, axis):
    t = _jnp.moveaxis(a, axis, 0)
    t = t.reshape((N_MICROBATCH, t.shape[0] // N_MICROBATCH) + t.shape[1:])
    return _jnp.moveaxis(t, 1, axis + 1)


def setup_inputs(seed: int = 0) -> dict:
    inp = _fwd_setup_inputs(seed)
    key = _jax.random.fold_in(_jax.random.key(seed), 7919)
    shape, _ = _output_shape()
    out = dict(inp)
    out["loss_target"] = _jax.random.normal(_jax.random.fold_in(key, 0), shape, _jnp.float32)
    for i, name in enumerate(TWIN_WEIGHTS):
        w = inp[name].astype(_jnp.float32)
        if MOMENT_SCALE is None:
            s = _jnp.sqrt(_jnp.mean(_jnp.square(w)) + 1e-30)
        else:
            s = MOMENT_SCALE[name]
        km, kv = _jax.random.split(_jax.random.fold_in(key, i + 1))
        out[name] = w
        out["m_" + name] = s * _jax.random.normal(km, w.shape, _jnp.float32)
        out["v_" + name] = (s * s) * _jax.random.uniform(kv, w.shape, _jnp.float32, 0.5, 1.5)
    if N_MICROBATCH > 1:
        for name, axis in PER_EXAMPLE_BATCH_AXIS.items():
            out[name] = _to_microbatches(out[name], axis)
    return {'x': out['x'], 'positions': out['positions'], 'norm_mix_pre': out['norm_mix_pre'], 'norm_mix_post': out['norm_mix_post'], 'w_in': out['w_in'], 'b_gate': out['b_gate'], 'q_norm': out['q_norm'], 'w_uq': out['w_uq'], 'kv_norm': out['kv_norm'], 'w_ukv': out['w_ukv'], 'w_proj_sb': out['w_proj_sb'], 'w_proj_mla': out['w_proj_mla'], 'w_out': out['w_out'], 'norm_ffn_pre': out['norm_ffn_pre'], 'norm_ffn_post': out['norm_ffn_post'], 'w_gate_up': out['w_gate_up'], 'w_down': out['w_down'], 'loss_target': out['loss_target'], 'm_norm_mix_pre': out['m_norm_mix_pre'], 'm_norm_mix_post': out['m_norm_mix_post'], 'm_w_in': out['m_w_in'], 'm_b_gate': out['m_b_gate'], 'm_q_norm': out['m_q_norm'], 'm_w_uq': out['m_w_uq'], 'm_kv_norm': out['m_kv_norm'], 'm_w_ukv': out['m_w_ukv'], 'm_w_proj_sb': out['m_w_proj_sb'], 'm_w_proj_mla': out['m_w_proj_mla'], 'm_w_out': out['m_w_out'], 'm_norm_ffn_pre': out['m_norm_ffn_pre'], 'm_norm_ffn_post': out['m_norm_ffn_post'], 'm_w_gate_up': out['m_w_gate_up'], 'm_w_down': out['m_w_down'], 'v_norm_mix_pre': out['v_norm_mix_pre'], 'v_norm_mix_post': out['v_norm_mix_post'], 'v_w_in': out['v_w_in'], 'v_b_gate': out['v_b_gate'], 'v_q_norm': out['v_q_norm'], 'v_w_uq': out['v_w_uq'], 'v_kv_norm': out['v_kv_norm'], 'v_w_ukv': out['v_w_ukv'], 'v_w_proj_sb': out['v_w_proj_sb'], 'v_w_proj_mla': out['v_w_proj_mla'], 'v_w_out': out['v_w_out'], 'v_norm_ffn_pre': out['v_norm_ffn_pre'], 'v_norm_ffn_post': out['v_norm_ffn_post'], 'v_w_gate_up': out['v_w_gate_up'], 'v_w_down': out['v_w_down']}


def _loss(weights, diff, rest, loss_target):
    with _jax.named_scope("forward"):
        args = {**rest, TWIN_DIFF_INPUT: diff, **{k: w.astype(_WEIGHT_DTYPES[k]) for k, w in weights.items()}}
        y = _forward(args)
    with _jax.named_scope("loss_head"):
        err = _jnp.square(y.astype(_jnp.float32) - loss_target)
        return 0.5 * _jnp.sum(_jnp.mean(err, axis=-1)) if err.ndim else 0.5 * err


def _adamw(w, g, m, v):
    m = ADAM_B1 * m + (1.0 - ADAM_B1) * g
    v = ADAM_B2 * v + (1.0 - ADAM_B2) * _jnp.square(g)
    m_hat = m / (1.0 - ADAM_B1 ** ADAM_STEP)
    v_hat = v / (1.0 - ADAM_B2 ** ADAM_STEP)
    delta = -ADAM_LR * (m_hat / (_jnp.sqrt(v_hat) + ADAM_EPS) + ADAM_WD * w)
    return delta, m, v


def reference(x, positions, norm_mix_pre, norm_mix_post, w_in, b_gate, q_norm, w_uq, kv_norm, w_ukv, w_proj_sb, w_proj_mla, w_out, norm_ffn_pre, norm_ffn_post, w_gate_up, w_down, loss_target, m_norm_mix_pre, m_norm_mix_post, m_w_in, m_b_gate, m_q_norm, m_w_uq, m_kv_norm, m_w_ukv, m_w_proj_sb, m_w_proj_mla, m_w_out, m_norm_ffn_pre, m_norm_ffn_post, m_w_gate_up, m_w_down, v_norm_mix_pre, v_norm_mix_post, v_w_in, v_b_gate, v_q_norm, v_w_uq, v_kv_norm, v_w_ukv, v_w_proj_sb, v_w_proj_mla, v_w_out, v_norm_ffn_pre, v_norm_ffn_post, v_w_gate_up, v_w_down):
    given = dict(x=x, positions=positions, norm_mix_pre=norm_mix_pre, norm_mix_post=norm_mix_post, w_in=w_in, b_gate=b_gate, q_norm=q_norm, w_uq=w_uq, kv_norm=kv_norm, w_ukv=w_ukv, w_proj_sb=w_proj_sb, w_proj_mla=w_proj_mla, w_out=w_out, norm_ffn_pre=norm_ffn_pre, norm_ffn_post=norm_ffn_post, w_gate_up=w_gate_up, w_down=w_down, loss_target=loss_target, m_norm_mix_pre=m_norm_mix_pre, m_norm_mix_post=m_norm_mix_post, m_w_in=m_w_in, m_b_gate=m_b_gate, m_q_norm=m_q_norm, m_w_uq=m_w_uq, m_kv_norm=m_kv_norm, m_w_ukv=m_w_ukv, m_w_proj_sb=m_w_proj_sb, m_w_proj_mla=m_w_proj_mla, m_w_out=m_w_out, m_norm_ffn_pre=m_norm_ffn_pre, m_norm_ffn_post=m_norm_ffn_post, m_w_gate_up=m_w_gate_up, m_w_down=m_w_down, v_norm_mix_pre=v_norm_mix_pre, v_norm_mix_post=v_norm_mix_post, v_w_in=v_w_in, v_b_gate=v_b_gate, v_q_norm=v_q_norm, v_w_uq=v_w_uq, v_kv_norm=v_kv_norm, v_w_ukv=v_w_ukv, v_w_proj_sb=v_w_proj_sb, v_w_proj_mla=v_w_proj_mla, v_w_out=v_w_out, v_norm_ffn_pre=v_norm_ffn_pre, v_norm_ffn_post=v_norm_ffn_post, v_w_gate_up=v_w_gate_up, v_w_down=v_w_down)
    weights = {n: given[n] for n in TWIN_WEIGHTS}
    shared = {n: given[n] for n in SHARED_INPUTS}
    per_example = {n: given[n] for n in ['x', 'positions']}
    grad_fn = _jax.value_and_grad(_loss, argnums=(0, 1))

    def one_microbatch(ex, loss_target):
        ex = dict(ex)
        diff = ex.pop(TWIN_DIFF_INPUT)
        return grad_fn(weights, diff, {**shared, **ex}, loss_target)

    if N_MICROBATCH == 1:
        loss, (grad_w, grad_x) = one_microbatch(per_example, given["loss_target"])
    else:
        def body(carry, xs):
            loss_sum, grad_sum = carry
            l_k, (gw_k, gx_k) = one_microbatch(xs[0], xs[1])
            with _jax.named_scope("update"):
                return (loss_sum + l_k, _jax.tree.map(_jnp.add, grad_sum, gw_k)), gx_k

        init = (_jnp.zeros((), _jnp.float32), _jax.tree.map(_jnp.zeros_like, weights))
        (loss, grad_w), grad_x = _jax.lax.scan(body, init, (per_example, given["loss_target"]))
    with _jax.named_scope("update"):
        delta_w, new_m, new_v = {}, {}, {}
        for n in TWIN_WEIGHTS:
            delta_w[n], new_m[n], new_v[n] = _adamw(weights[n], grad_w[n], given["m_" + n], given["v_" + n])
    return (loss, grad_x, *[grad_w[n] for n in TWIN_WEIGHTS], *[delta_w[n] for n in TWIN_WEIGHTS],
            *[new_m[n] for n in TWIN_WEIGHTS], *[new_v[n] for n in TWIN_WEIGHTS])
```

```python
import functools
import math

import jax
import jax.numpy as jnp
from jax import lax
from jax.experimental import pallas as pl
from jax.experimental.pallas import tpu as pltpu

F32 = jnp.float32
BF16 = jnp.bfloat16

D_MODEL = 1024
N_HEADS = 8
SB_WIDTH = 512
MLA_Q_RANK = 384
MLA_KV_RANK = 256
MLA_ROPE_DIM = 32
MLA_QK_DIM = 96
D_FF = 2816
ROPE_THETA = 10000.0
EPS = 1e-6
SB_SCALE = 1.0 / math.sqrt(64.0)
MLA_SCALE = 1.0 / math.sqrt(96.0)
NEG_BIG = -1e30

ADAM_LR = 0.001
ADAM_B1 = 0.9
ADAM_B2 = 0.999
ADAM_EPS = 1e-08
ADAM_WD = 0.01
ADAM_STEP = 10

N_DEV = 8
LANES = 128
TQ = 128
TK = 128
VMEM_LIMIT = 56 << 20

EXT_QKV = 0
EXT_CQ = 1536
EXT_CKV = 1920
EXT_KR = 2176
EXT_GL = 2304
EXT_N = 4352

SHARDED = (
    ("w_in", (1024, 532), 1), ("w_uq", (384, 96), 1), ("w_ukv", (256, 128), 1),
    ("w_proj_sb", (512, 128), 1), ("w_proj_mla", (512, 128), 1), ("w_out", (128, 1024), 0),
    ("w_gate_up", (1024, 704), 1), ("w_down", (352, 1024), 0),
)
FLAT_ROWS = 15360
SMALL = (("norm_mix_pre", 1024), ("norm_mix_post", 1024), ("b_gate", 2048), ("q_norm", 384),
         ("kv_norm", 256), ("norm_ffn_pre", 1024), ("norm_ffn_post", 1024))
SMALL_ROWS = 56


def _dot(a, b):
    return jnp.dot(a, b, preferred_element_type=F32)


def _dot_nt(a, b):
    return lax.dot_general(a, b, (((1,), (1,)), ((), ())), preferred_element_type=F32)


def _dot_tn(a, b):
    return lax.dot_general(a, b, (((0,), (0,)), ((), ())), preferred_element_type=F32)


def _rms(x):
    r = lax.rsqrt(jnp.mean(x * x, axis=-1, keepdims=True) + EPS)
    return x * r, r


def _rms_bwd(dn, n, r):
    return r * (dn - n * jnp.mean(dn * n, axis=-1, keepdims=True))


def _colsum8(x):
    return jnp.sum(x.reshape(x.shape[0] // 8, 8, x.shape[1]), axis=0)


def _split(x):
    hi = x.astype(BF16)
    return hi, (x - hi.astype(F32)).astype(BF16)


def _rot(x):
    lane = lax.broadcasted_iota(jnp.int32, x.shape, 1)
    up = pltpu.roll(x, 112, 1)
    down = pltpu.roll(x, 16, 1)
    return jnp.where((lane >= 64) & (lane < 80), -up, jnp.where((lane >= 80) & (lane < 96), down, 0.0))


def _params(sem):
    return pltpu.CompilerParams(dimension_semantics=sem, vmem_limit_bytes=VMEM_LIMIT)


def _rows_call(name, body, n_rows, tm, row_ins, const_ins, row_outs, acc_outs):
    in_specs = [pl.BlockSpec((tm, a.shape[1]), lambda i: (i, 0)) for a in row_ins]
    in_specs += [pl.BlockSpec(a.shape, lambda i: (0, 0), pipeline_mode=pl.Buffered(1)) for a in const_ins]
    out_specs = [pl.BlockSpec((tm, n), lambda i: (i, 0)) for n, _ in row_outs]
    out_specs += [pl.BlockSpec(s, lambda i: (0, 0)) for s in acc_outs]
    out_shape = [jax.ShapeDtypeStruct((n_rows, n), dt) for n, dt in row_outs]
    out_shape += [jax.ShapeDtypeStruct(s, F32) for s in acc_outs]
    return pl.pallas_call(
        body, grid=(n_rows // tm,), in_specs=in_specs, out_specs=out_specs, out_shape=out_shape,
        name=name, compiler_params=_params(("arbitrary",)),
    )(*row_ins, *const_ins)


def _in_proj(x, g_pre, w_ext, tm):
    def body(x_ref, g_ref, w_ref, qkv_ref, cq_ref, ckv_ref, kr_ref, gl_ref, h_ref):
        n, _ = _rms(x_ref[...])
        hb = (n * g_ref[...]).astype(BF16)
        h_ref[...] = hb
        for c in range(0, 1536, 512):
            qkv_ref[:, c:c + 512] = _dot(hb, w_ref[:, c:c + 512]).astype(BF16)
        cq_ref[...] = _dot(hb, w_ref[:, EXT_CQ:EXT_CKV])
        ckv_ref[...] = _dot(hb, w_ref[:, EXT_CKV:EXT_KR])
        kr_ref[...] = _dot(hb, w_ref[:, EXT_KR:EXT_GL])
        for c in range(0, 2048, 512):
            gl_ref[:, c:c + 512] = _dot(hb, w_ref[:, EXT_GL + c:EXT_GL + c + 512])

    return _rows_call("in_proj", body, x.shape[0], tm, [x], [g_pre, w_ext],
                      [(1536, BF16), (384, F32), (256, F32), (128, F32), (2048, F32), (1024, BF16)], [])


def _mla_up(cq, ckv, kr, cr, sr, q_norm, kv_norm, wa, wk, wv, tm):
    def body(cq_ref, ckv_ref, kr_ref, cr_ref, sr_ref, qn_ref, kvn_ref, wa_ref, wk_ref, wv_ref,
             q_ref, k_ref, v_ref, cqn_ref, ckvn_ref):
        nq, _ = _rms(cq_ref[...])
        cqn = (nq * qn_ref[...]).astype(BF16)
        cqn_ref[...] = cqn
        nk, _ = _rms(ckv_ref[...])
        ckvn = (nk * kvn_ref[...]).astype(BF16)
        ckvn_ref[...] = ckvn
        cr = cr_ref[...]
        sr = sr_ref[...]
        lane = lax.broadcasted_iota(jnp.int32, cr.shape, 1)
        cm = cr + (lane < 64).astype(F32)
        kr = kr_ref[...]
        krp = kr * cr + _rot(kr) * sr
        for h in range(N_HEADS):
            hs = slice(h * LANES, (h + 1) * LANES)
            a = _dot(cqn, wa_ref[:, hs])
            q_ref[:, hs] = (a * cm + _rot(a) * sr).astype(BF16)
            k_ref[:, hs] = (_dot(ckvn, wk_ref[:, hs]) + krp).astype(BF16)
        v_ref[...] = _dot(ckvn, wv_ref[...]).astype(BF16)

    return _rows_call("mla_up", body, cq.shape[0], tm, [cq, ckv, kr, cr, sr], [q_norm, kv_norm, wa, wk, wv],
                      [(1024, BF16), (1024, BF16), (512, BF16), (384, BF16), (256, BF16)], [])


def _mix_out(o_sb, o_mla, gl, x, b_gate, g_post, w_sb, w_mla, w_out, tm):
    def body(osb_ref, omla_ref, gl_ref, x_ref, b_ref, gp_ref, wsb_ref, wmla_ref, wout_ref,
             x1_ref, y_ref, mb_ref):
        psb = _dot(osb_ref[...].astype(BF16), wsb_ref[...])
        pmla = _dot(omla_ref[...].astype(BF16), wmla_ref[...])
        gates = jax.nn.sigmoid(gl_ref[...] + b_ref[...])
        mb = (gates[:, :D_MODEL] * psb + gates[:, D_MODEL:] * pmla).astype(BF16)
        mb_ref[...] = mb
        y = _dot(mb, wout_ref[...])
        y_ref[...] = y
        n, _ = _rms(y)
        x1_ref[...] = x_ref[...] + n * gp_ref[...]

    return _rows_call("mix_out", body, x.shape[0], tm, [o_sb, o_mla, gl, x], [b_gate, g_post, w_sb, w_mla, w_out],
                      [(1024, F32), (1024, F32), (1024, BF16)], [])


FF_CHUNK = 1408


def _ffn_fwd(x1, target, g_pre, g_post, w_gu, w_down, tm):
    def body(x1_ref, t_ref, gpre_ref, gpost_ref, wgu_ref, wd_ref, dx2_ref, f_ref, h2_ref, loss_ref):
        x1 = x1_ref[...]
        n, _ = _rms(x1)
        h2 = (n * gpre_ref[...]).astype(BF16)
        h2_ref[...] = h2
        f = jnp.zeros((tm, D_MODEL), F32)
        for c in range(0, D_FF, FF_CHUNK):
            g = _dot(h2, wgu_ref[:, c:c + FF_CHUNK])
            u = _dot(h2, wgu_ref[:, D_FF + c:D_FF + c + FF_CHUNK])
            act = (g * jax.nn.sigmoid(g) * u).astype(BF16)
            f = f + _dot(act, wd_ref[c:c + FF_CHUNK, :])
        f_ref[...] = f
        nf, _ = _rms(f)
        err = x1 + nf * gpost_ref[...] - t_ref[...]
        dx2_ref[...] = err * (1.0 / D_MODEL)
        e8 = _colsum8(err * err)
        part = e8[:, 0:LANES]
        for c in range(LANES, D_MODEL, LANES):
            part = part + e8[:, c:c + LANES]

        @pl.when(pl.program_id(0) == 0)
        def _():
            loss_ref[...] = jnp.zeros_like(loss_ref)

        loss_ref[...] += part

    return _rows_call("ffn_fwd", body, x1.shape[0], tm, [x1, target], [g_pre, g_post, w_gu, w_down],
                      [(1024, F32), (1024, F32), (1024, BF16)], [(8, LANES)])


def _ffn_bwd(dx2, f, x1, g_pre, g_post, w_gu, w_down, tm):
    def body(dx2_ref, f_ref, x1_ref, gpre_ref, gpost_ref, wgu_ref, wd_ref,
             dx1_ref, act_ref, dgu_ref, dfb_ref, dgpre_ref, dgpost_ref):
        @pl.when(pl.program_id(0) == 0)
        def _():
            dgpre_ref[...] = jnp.zeros_like(dgpre_ref)
            dgpost_ref[...] = jnp.zeros_like(dgpost_ref)

        dx2 = dx2_ref[...]
        nf, rf = _rms(f_ref[...])
        dgpost_ref[...] += _colsum8(dx2 * nf)
        dfb = _rms_bwd(dx2 * gpost_ref[...], nf, rf).astype(BF16)
        dfb_ref[...] = dfb
        x1 = x1_ref[...]
        n1, r1 = _rms(x1)
        h2 = (n1 * gpre_ref[...]).astype(BF16)
        dh2 = jnp.zeros((tm, D_MODEL), F32)
        for c in range(0, D_FF, FF_CHUNK):
            cs, us = slice(c, c + FF_CHUNK), slice(D_FF + c, D_FF + c + FF_CHUNK)
            g = _dot(h2, wgu_ref[:, cs])
            u = _dot(h2, wgu_ref[:, us])
            sg = jax.nn.sigmoid(g)
            si = g * sg
            act_ref[:, cs] = (si * u).astype(BF16)
            dact = _dot_nt(dfb, wd_ref[cs, :])
            dg = (dact * u * (sg * (1.0 + g * (1.0 - sg)))).astype(BF16)
            du = (dact * si).astype(BF16)
            dgu_ref[:, cs] = dg
            dgu_ref[:, us] = du
            dh2 = dh2 + _dot_nt(dg, wgu_ref[:, cs]) + _dot_nt(du, wgu_ref[:, us])
        dgpre_ref[...] += _colsum8(dh2 * n1)
        dx1_ref[...] = dx2 + _rms_bwd(dh2 * gpre_ref[...], n1, r1)

    return _rows_call("ffn_bwd", body, dx2.shape[0], tm, [dx2, f, x1], [g_pre, g_post, w_gu, w_down],
                      [(1024, F32), (D_FF, BF16), (2 * D_FF, BF16), (1024, BF16)], [(8, 1024), (8, 1024)])


def _mix_bwd(dx1, y, o_sb, o_mla, gl, b_gate, g_post, w_sb, w_mla, w_out, tm):
    def body(dx1_ref, y_ref, osb_ref, omla_ref, gl_ref, b_ref, gp_ref, wsb_ref, wmla_ref, wout_ref,
             dyb_ref, dpsb_ref, dpmla_ref, dgl_ref, dosb_ref, domla_ref, dgpost_ref, dbg_ref):
        @pl.when(pl.program_id(0) == 0)
        def _():
            dgpost_ref[...] = jnp.zeros_like(dgpost_ref)
            dbg_ref[...] = jnp.zeros_like(dbg_ref)

        dx1 = dx1_ref[...]
        ny, ry = _rms(y_ref[...])
        dgpost_ref[...] += _colsum8(dx1 * ny)
        dyb = _rms_bwd(dx1 * gp_ref[...], ny, ry).astype(BF16)
        dyb_ref[...] = dyb
        dm = _dot_nt(dyb, wout_ref[...])
        psb = _dot(osb_ref[...].astype(BF16), wsb_ref[...])
        pmla = _dot(omla_ref[...].astype(BF16), wmla_ref[...])
        gates = jax.nn.sigmoid(gl_ref[...] + b_ref[...])
        g0, g1 = gates[:, :D_MODEL], gates[:, D_MODEL:]
        dpsb = (dm * g0).astype(BF16)
        dpmla = (dm * g1).astype(BF16)
        dpsb_ref[...] = dpsb
        dpmla_ref[...] = dpmla
        dgl0 = dm * psb * g0 * (1.0 - g0)
        dgl1 = dm * pmla * g1 * (1.0 - g1)
        dgl_ref[:, :D_MODEL] = dgl0.astype(BF16)
        dgl_ref[:, D_MODEL:] = dgl1.astype(BF16)
        dbg_ref[:, :D_MODEL] += _colsum8(dgl0)
        dbg_ref[:, D_MODEL:] += _colsum8(dgl1)
        dosb_ref[...] = _dot_nt(dpsb, wsb_ref[...]).astype(BF16)
        domla_ref[...] = _dot_nt(dpmla, wmla_ref[...]).astype(BF16)

    return _rows_call("mix_bwd", body, dx1.shape[0], tm, [dx1, y, o_sb, o_mla, gl],
                      [b_gate, g_post, w_sb, w_mla, w_out],
                      [(1024, BF16), (1024, BF16), (1024, BF16), (2048, BF16), (512, BF16), (512, BF16)],
                      [(8, 1024), (8, 2048)])


def _mla_up_bwd(dq, dk, dv, cq, ckv, cr, sr, q_norm, kv_norm, wa, wk, wv, tm):
    def body(dq_ref, dk_ref, dv_ref, cq_ref, ckv_ref, cr_ref, sr_ref, qn_ref, kvn_ref, wa_ref, wk_ref, wv_ref,
             da_ref, dkb_ref, dvb_ref, dlat_ref, dqn_ref, dkvn_ref):
        @pl.when(pl.program_id(0) == 0)
        def _():
            dqn_ref[...] = jnp.zeros_like(dqn_ref)
            dkvn_ref[...] = jnp.zeros_like(dkvn_ref)

        cr = cr_ref[...]
        sr = sr_ref[...]
        lane = lax.broadcasted_iota(jnp.int32, cr.shape, 1)
        cm = cr + (lane < 64).astype(F32)
        nq, rq = _rms(cq_ref[...])
        nk, rk = _rms(ckv_ref[...])
        dcqn = jnp.zeros((tm, MLA_Q_RANK), F32)
        dckvn = jnp.zeros((tm, MLA_KV_RANK), F32)
        dkrp = jnp.zeros((tm, LANES), F32)
        for h in range(N_HEADS):
            hs = slice(h * LANES, (h + 1) * LANES)
            dqh = dq_ref[:, hs]
            da = (dqh * cm - _rot(dqh * sr)).astype(BF16)
            da_ref[:, hs] = da
            dcqn = dcqn + _dot_nt(da, wa_ref[:, hs])
            dkh = dk_ref[:, hs]
            dkb = dkh.astype(BF16)
            dkb_ref[:, hs] = dkb
            dckvn = dckvn + _dot_nt(dkb, wk_ref[:, hs])
            dkrp = dkrp + dkh
        dvb = dv_ref[...].astype(BF16)
        dvb_ref[...] = dvb
        dckvn = dckvn + _dot_nt(dvb, wv_ref[...])
        dkr = dkrp * cr - _rot(dkrp * sr)
        dqn_ref[...] += _colsum8(dcqn * nq)
        dkvn_ref[...] += _colsum8(dckvn * nk)
        dlat_ref[:, 0:384] = _rms_bwd(dcqn * qn_ref[...], nq, rq).astype(BF16)
        dlat_ref[:, 384:640] = _rms_bwd(dckvn * kvn_ref[...], nk, rk).astype(BF16)
        dlat_ref[:, 640:768] = dkr.astype(BF16)

    return _rows_call("mla_up_bwd", body, dq.shape[0], tm, [dq, dk, dv, cq, ckv, cr, sr],
                      [q_norm, kv_norm, wa, wk, wv],
                      [(1024, BF16), (1024, BF16), (512, BF16), (768, BF16)], [(8, 384), (8, 256)])


def _in_proj_bwd(x, dx1, dq_sb, dk_sb, dv_sb, dlat, dgl, g_pre, w_ext, tm):
    def body(x_ref, dx1_ref, dq_ref, dk_ref, dv_ref, dlat_ref, dgl_ref, g_ref, w_ref,
             dx_ref, dproj_ref, dg_ref):
        @pl.when(pl.program_id(0) == 0)
        def _():
            dg_ref[...] = jnp.zeros_like(dg_ref)

        dproj_ref[:, 0:512] = dq_ref[...].astype(BF16)
        dproj_ref[:, 512:1024] = dk_ref[...].astype(BF16)
        dproj_ref[:, 1024:1536] = dv_ref[...].astype(BF16)
        dproj_ref[:, EXT_CQ:EXT_GL] = dlat_ref[...]
        dproj_ref[:, EXT_GL:EXT_N] = dgl_ref[...]
        dh = jnp.zeros((tm, D_MODEL), F32)
        for c in range(0, EXT_N, 2176):
            dh = dh + _dot_nt(dproj_ref[:, c:c + 2176], w_ref[:, c:c + 2176])
        n, r = _rms(x_ref[...])
        dg_ref[...] += _colsum8(dh * n)
        dx_ref[...] = dx1_ref[...] + _rms_bwd(dh * g_ref[...], n, r)

    return _rows_call("in_proj_bwd", body, x.shape[0], tm, [x, dx1, dq_sb, dk_sb, dv_sb, dlat, dgl],
                      [g_pre, w_ext], [(1024, F32), (EXT_N, BF16)], [(8, 1024)])


def _tile_iotas():
    row = lax.broadcasted_iota(jnp.int32, (TQ, TK), 0)
    col = lax.broadcasted_iota(jnp.int32, (TQ, TK), 1)
    return row, col


def _sb_scores(qh, kb, r, tri, valid):
    z = _dot_nt(qh, kb) * SB_SCALE
    soft = jnp.log(1.0 + jnp.exp(-jnp.abs(z)))
    lsm_all = -jnp.maximum(z, 0.0) - soft
    lsm = lsm_all if valid is None else jnp.where(valid, lsm_all, 0.0)
    hi, lo = _split(lsm)
    c = _dot(hi, tri) + _dot(lo, tri) + r
    a = jnp.exp(z + lsm_all + c)
    if valid is not None:
        a = jnp.where(valid, a, 0.0)
    return z, lsm_all, lsm, a


def _sb_fwd(qkv):
    s_len = qkv.shape[0]

    def body(q_ref, k_ref, v_ref, o_ref):
        i = pl.program_id(1)
        row, col = _tile_iotas()
        lane = lax.broadcasted_iota(jnp.int32, (TQ, LANES), 1)
        tri = (row > col).astype(BF16)
        q = q_ref[...]
        zero = jnp.zeros_like(q)
        qh = (jnp.where(lane < 64, q, zero), jnp.where(lane >= 64, q, zero))

        def block(j, carry, valid):
            rs, accs = carry
            off = pl.multiple_of(j * TK, TK)
            kb = k_ref[pl.ds(off, TK), :]
            vb = v_ref[pl.ds(off, TK), :]
            new_r, new_acc = [], []
            for hh in range(2):
                _, _, lsm, a = _sb_scores(qh[hh], kb, rs[hh], tri, valid)
                a_hi, a_lo = _split(a)
                new_acc.append(accs[hh] + _dot(a_hi, vb) + _dot(a_lo, vb))
                new_r.append(rs[hh] + jnp.sum(lsm, axis=1, keepdims=True))
            return tuple(new_r), tuple(new_acc)

        z1 = jnp.zeros((TQ, 1), F32)
        zacc = jnp.zeros((TQ, LANES), F32)
        carry = block(i, ((z1, z1), (zacc, zacc)), col < row)
        carry = lax.fori_loop(0, i, lambda jj, c: block(i - 1 - jj, c, None), carry)
        accs = carry[1]
        o_ref[...] = jnp.where(lane < 64, accs[0], accs[1])

    return pl.pallas_call(
        body, grid=(4, s_len // TQ),
        in_specs=[pl.BlockSpec((TQ, LANES), lambda h, i: (i, h)),
                  pl.BlockSpec((s_len, LANES), lambda h, i: (0, 4 + h)),
                  pl.BlockSpec((s_len, LANES), lambda h, i: (0, 8 + h))],
        out_specs=pl.BlockSpec((TQ, LANES), lambda h, i: (i, h)),
        out_shape=jax.ShapeDtypeStruct((s_len, SB_WIDTH), F32),
        name="sb_attn_fwd", compiler_params=_params(("arbitrary", "arbitrary")),
    )(qkv, qkv, qkv)


def _sb_bwd(qkv, do, o):
    s_len = qkv.shape[0]

    def body(q_ref, k_ref, v_ref, do_ref, o_ref, dq_ref, dk_ref, dv_ref):
        i = pl.program_id(1)

        @pl.when(i == 0)
        def _():
            dk_ref[...] = jnp.zeros_like(dk_ref)
            dv_ref[...] = jnp.zeros_like(dv_ref)

        row, col = _tile_iotas()
        lane = lax.broadcasted_iota(jnp.int32, (TQ, LANES), 1)
        tri = (row > col).astype(BF16)
        q = q_ref[...]
        do = do_ref[...]
        zero = jnp.zeros_like(q)
        heads = (lane < 64, lane >= 64)
        qh = tuple(jnp.where(m, q, zero) for m in heads)
        doh = tuple(jnp.where(m, do, zero) for m in heads)
        prod = do.astype(F32) * o_ref[...]
        dsum = tuple(jnp.sum(jnp.where(m, prod, 0.0), axis=1, keepdims=True) for m in heads)

        def block(j, carry, valid):
            rs, gs, dqs = carry
            off = pl.multiple_of(j * TK, TK)
            kb = k_ref[pl.ds(off, TK), :]
            vb = v_ref[pl.ds(off, TK), :]
            new_r, new_g, new_dq = [], [], []
            dk_upd = jnp.zeros((TK, LANES), F32)
            dv_upd = jnp.zeros((TK, LANES), F32)
            for hh in range(2):
                z, lsm_all, lsm, a = _sb_scores(qh[hh], kb, rs[hh], tri, valid)
                g = a * _dot_nt(doh[hh], vb)
                g_hi, g_lo = _split(g)
                later = _dot(g_hi, tri) + _dot(g_lo, tri) + gs[hh]
                earlier = dsum[hh] - g - later
                beta = jnp.exp(z + lsm_all)
                dz = (g * (1.0 - beta) - beta * earlier) * SB_SCALE
                if valid is not None:
                    dz = jnp.where(valid, dz, 0.0)
                dzb = dz.astype(BF16)
                new_dq.append(dqs[hh] + _dot(dzb, kb))
                dk_upd = dk_upd + _dot_tn(dzb, qh[hh])
                dv_upd = dv_upd + _dot_tn(a.astype(BF16), doh[hh])
                new_r.append(rs[hh] + jnp.sum(lsm, axis=1, keepdims=True))
                new_g.append(gs[hh] + jnp.sum(g, axis=1, keepdims=True))
            dk_ref[pl.ds(off, TK), :] += dk_upd
            dv_ref[pl.ds(off, TK), :] += dv_upd
            return tuple(new_r), tuple(new_g), tuple(new_dq)

        z1 = jnp.zeros((TQ, 1), F32)
        zacc = jnp.zeros((TQ, LANES), F32)
        carry = block(i, ((z1, z1), (z1, z1), (zacc, zacc)), col < row)
        carry = lax.fori_loop(0, i, lambda jj, c: block(i - 1 - jj, c, None), carry)
        dqs = carry[2]
        dq_ref[...] = jnp.where(lane < 64, dqs[0], dqs[1])

    tile = pl.BlockSpec((TQ, LANES), lambda h, i: (i, h))
    full = pl.BlockSpec((s_len, LANES), lambda h, i: (0, h))
    k_full = pl.BlockSpec((s_len, LANES), lambda h, i: (0, 4 + h))
    v_full = pl.BlockSpec((s_len, LANES), lambda h, i: (0, 8 + h))
    return pl.pallas_call(
        body, grid=(4, s_len // TQ),
        in_specs=[tile, k_full, v_full, tile, tile],
        out_specs=[tile, full, full],
        out_shape=[jax.ShapeDtypeStruct((s_len, SB_WIDTH), F32)] * 3,
        name="sb_attn_bwd", compiler_params=_params(("arbitrary", "arbitrary")),
    )(qkv, qkv, qkv, do, o)


def _mla_fwd(q, k, v):
    s_len = q.shape[0]

    def body(q_ref, k_ref, v_ref, o_ref, lse_ref):
        i = pl.program_id(1)
        row, col = _tile_iotas()
        lane = lax.broadcasted_iota(jnp.int32, (TQ, LANES), 1)
        qh = (q_ref[:, 0:LANES], q_ref[:, LANES:2 * LANES])

        def block(j, carry, valid):
            ms, ls, accs = carry
            off = pl.multiple_of(j * TK, TK)
            vb = v_ref[pl.ds(off, TK), :]
            new_m, new_l, new_acc = [], [], []
            for hh in range(2):
                kb = k_ref[pl.ds(off, TK), hh * LANES:(hh + 1) * LANES]
                s = _dot_nt(qh[hh], kb) * MLA_SCALE
                if valid is not None:
                    s = jnp.where(valid, s, NEG_BIG)
                m = jnp.maximum(ms[hh], jnp.max(s, axis=1, keepdims=True))
                p = jnp.exp(s - m)
                alpha = jnp.exp(ms[hh] - m)
                new_m.append(m)
                new_l.append(alpha * ls[hh] + jnp.sum(p, axis=1, keepdims=True))
                new_acc.append(alpha * accs[hh] + _dot(p.astype(BF16), vb))
            return tuple(new_m), tuple(new_l), tuple(new_acc)

        m0 = jnp.full((TQ, 1), NEG_BIG, F32)
        l0 = jnp.zeros((TQ, 1), F32)
        zacc = jnp.zeros((TQ, LANES), F32)
        carry = block(i, ((m0, m0), (l0, l0), (zacc, zacc)), col <= row)
        ms, ls, accs = lax.fori_loop(0, i, lambda jj, c: block(i - 1 - jj, c, None), carry)
        o_ref[...] = jnp.where(lane < 64, accs[0] / ls[0], accs[1] / ls[1])
        for hh in range(2):
            lse_ref[hh] = jnp.broadcast_to(ms[hh] + jnp.log(ls[hh]), (TQ, LANES))

    return pl.pallas_call(
        body, grid=(4, s_len // TQ),
        in_specs=[pl.BlockSpec((TQ, 2 * LANES), lambda h, i: (i, h)),
                  pl.BlockSpec((s_len, 2 * LANES), lambda h, i: (0, h)),
                  pl.BlockSpec((s_len, LANES), lambda h, i: (0, h))],
        out_specs=[pl.BlockSpec((TQ, LANES), lambda h, i: (i, h)),
                   pl.BlockSpec((2, TQ, LANES), lambda h, i: (h, i, 0))],
        out_shape=[jax.ShapeDtypeStruct((s_len, 512), F32),
                   jax.ShapeDtypeStruct((N_HEADS, s_len, LANES), F32)],
        name="mla_attn_fwd", compiler_params=_params(("arbitrary", "arbitrary")),
    )(q, k, v)


def _mla_bwd(q, k, v, do, o, lse):
    s_len = q.shape[0]

    def body(q_ref, k_ref, v_ref, do_ref, o_ref, lse_ref, dq_ref, dk_ref, dv_ref):
        i = pl.program_id(1)

        @pl.when(i == 0)
        def _():
            dk_ref[...] = jnp.zeros_like(dk_ref)
            dv_ref[...] = jnp.zeros_like(dv_ref)

        row, col = _tile_iotas()
        lane = lax.broadcasted_iota(jnp.int32, (TQ, LANES), 1)
        heads = (lane < 64, lane >= 64)
        qh = (q_ref[:, 0:LANES], q_ref[:, LANES:2 * LANES])
        do = do_ref[...]
        zero = jnp.zeros_like(do)
        doh = tuple(jnp.where(m, do, zero) for m in heads)
        prod = do.astype(F32) * o_ref[...]
        dsum = tuple(jnp.sum(jnp.where(m, prod, 0.0), axis=1, keepdims=True) for m in heads)
        lse = (lse_ref[0], lse_ref[1])

        def block(j, dqs, valid):
            off = pl.multiple_of(j * TK, TK)
            vb = v_ref[pl.ds(off, TK), :]
            new_dq = []
            dv_upd = jnp.zeros((TK, LANES), F32)
            for hh in range(2):
                hs = slice(hh * LANES, (hh + 1) * LANES)
                kb = k_ref[pl.ds(off, TK), hs]
                p = jnp.exp(_dot_nt(qh[hh], kb) * MLA_SCALE - lse[hh])
                if valid is not None:
                    p = jnp.where(valid, p, 0.0)
                ds = p * (_dot_nt(doh[hh], vb) - dsum[hh])
                dsb = (ds * MLA_SCALE).astype(BF16)
                new_dq.append(dqs[hh] + _dot(dsb, kb))
                dk_ref[pl.ds(off, TK), hs] += _dot_tn(dsb, qh[hh])
                dv_upd = dv_upd + _dot_tn(p.astype(BF16), doh[hh])
            dv_ref[pl.ds(off, TK), :] += dv_upd
            return tuple(new_dq)

        zacc = jnp.zeros((TQ, LANES), F32)
        dqs = block(i, (zacc, zacc), col <= row)
        dqs = lax.fori_loop(0, i, lambda jj, c: block(i - 1 - jj, c, None), dqs)
        dq_ref[:, 0:LANES] = dqs[0]
        dq_ref[:, LANES:2 * LANES] = dqs[1]

    q_tile = pl.BlockSpec((TQ, 2 * LANES), lambda h, i: (i, h))
    o_tile = pl.BlockSpec((TQ, LANES), lambda h, i: (i, h))
    k_full = pl.BlockSpec((s_len, 2 * LANES), lambda h, i: (0, h))
    v_full = pl.BlockSpec((s_len, LANES), lambda h, i: (0, h))
    return pl.pallas_call(
        body, grid=(4, s_len // TQ),
        in_specs=[q_tile, k_full, v_full, o_tile, o_tile, pl.BlockSpec((2, TQ, LANES), lambda h, i: (h, i, 0))],
        out_specs=[q_tile, k_full, v_full],
        out_shape=[jax.ShapeDtypeStruct((s_len, 1024), F32), jax.ShapeDtypeStruct((s_len, 1024), F32),
                   jax.ShapeDtypeStruct((s_len, 512), F32)],
        name="mla_attn_bwd", compiler_params=_params(("arbitrary", "arbitrary")),
    )(q, k, v, do, o, lse)


def _pick(n, options):
    for t in options:
        if n % t == 0:
            return t
    raise ValueError(n)


def _matmul_tn(name, a, b):
    s_len, m = a.shape
    n = b.shape[1]
    tm = _pick(m, (1024, 1408, 512, 384, 256))
    tn = _pick(n, (1024, 1408, 2176, 512))
    tk = _pick(s_len, (512, 256, 128))

    def body(a_ref, b_ref, o_ref):
        @pl.when(pl.program_id(2) == 0)
        def _():
            o_ref[...] = jnp.zeros_like(o_ref)

        o_ref[...] += _dot_tn(a_ref[...], b_ref[...])

    return pl.pallas_call(
        body, grid=(m // tm, n // tn, s_len // tk),
        in_specs=[pl.BlockSpec((tk, tm), lambda i, j, l: (l, i)), pl.BlockSpec((tk, tn), lambda i, j, l: (l, j))],
        out_specs=pl.BlockSpec((tm, tn), lambda i, j, l: (i, j)),
        out_shape=jax.ShapeDtypeStruct((m, n), F32),
        name=name, compiler_params=_params(("arbitrary", "arbitrary", "arbitrary")),
    )(a, b)


def _mesh_pos():
    return lax.axis_index("x"), lax.axis_index("y"), lax.axis_index("c")


def _peer(pos, k):
    x, y, c = pos
    return (1 - x if k & 4 else x, 1 - y if k & 2 else y, 1 - c if k & 1 else c)


def _flat_index(pos):
    return 4 * pos[0] + 2 * pos[1] + pos[2]


def _all_gather(shard):
    rows = shard.shape[0]

    def body(x_ref, out_ref, send_sems, recv_sems, local_sem):
        me = _mesh_pos()
        x, y, c = me
        sibling = (x, y, 1 - c)
        chips = [(1 - x, y), (x, 1 - y), (1 - x, 1 - y)]

        def copy(k, block, to, src=None):
            slot = out_ref.at[_flat_index(block)]
            return pltpu.make_async_remote_copy(
                src_ref=slot if src is None else src, dst_ref=slot,
                send_sem=send_sems.at[k], recv_sem=recv_sems.at[k],
                device_id=to, device_id_type=pl.DeviceIdType.MESH)

        mine = pltpu.make_async_copy(x_ref, out_ref.at[_flat_index(me)], local_sem)
        mine.start()
        first = [copy(0, me, sibling, src=x_ref)]
        first += [copy(1 + j, me, (*chip, c), src=x_ref) for j, chip in enumerate(chips)]
        for cp in first:
            cp.start()
        passed = [copy(4 + j, (*chip, c), sibling) for j, chip in enumerate(chips)]
        for j, chip in enumerate(chips):
            copy(1 + j, (*chip, c), me).wait_recv()
            passed[j].start()
        copy(0, sibling, me).wait_recv()
        for j, chip in enumerate(chips):
            copy(4 + j, (*chip, 1 - c), me).wait_recv()
        for cp in first + passed:
            cp.wait_send()
        mine.wait()

    return pl.pallas_call(
        body, out_shape=jax.ShapeDtypeStruct((N_DEV, rows, LANES), shard.dtype),
        in_specs=[pl.BlockSpec(memory_space=pl.ANY)], out_specs=pl.BlockSpec(memory_space=pl.ANY),
        scratch_shapes=[pltpu.SemaphoreType.DMA((7,)), pltpu.SemaphoreType.DMA((7,)), pltpu.SemaphoreType.DMA],
        name="weights_all_gather",
    )(shard)


def _grad_exchange(big, small):
    def body(big_ref, small_ref, big_out, small_out, bsend, brecv, ssend, srecv, local_sems):
        me = _mesh_pos()
        mine = _flat_index(me)
        loc = [pltpu.make_async_copy(big_ref.at[mine], big_out.at[mine], local_sems.at[0]),
               pltpu.make_async_copy(small_ref, small_out.at[mine], local_sems.at[1])]
        for cp in loc:
            cp.start()

        def copies(k):
            peer = _peer(me, k)
            theirs = _flat_index(peer)
            send = (pltpu.make_async_remote_copy(
                        src_ref=big_ref.at[theirs], dst_ref=big_out.at[mine], send_sem=bsend.at[k - 1],
                        recv_sem=brecv.at[k - 1], device_id=peer, device_id_type=pl.DeviceIdType.MESH),
                    pltpu.make_async_remote_copy(
                        src_ref=small_ref, dst_ref=small_out.at[mine], send_sem=ssend.at[k - 1],
                        recv_sem=srecv.at[k - 1], device_id=peer, device_id_type=pl.DeviceIdType.MESH))
            recv = (pltpu.make_async_remote_copy(
                        src_ref=big_ref.at[mine], dst_ref=big_out.at[theirs], send_sem=bsend.at[k - 1],
                        recv_sem=brecv.at[k - 1], device_id=me, device_id_type=pl.DeviceIdType.MESH),
                    pltpu.make_async_remote_copy(
                        src_ref=small_ref, dst_ref=small_out.at[theirs], send_sem=ssend.at[k - 1],
                        recv_sem=srecv.at[k - 1], device_id=me, device_id_type=pl.DeviceIdType.MESH))
            return send, recv

        plan = [copies(k) for k in range(1, N_DEV)]
        for send, _ in plan:
            for cp in send:
                cp.start()
        for _, recv in plan:
            for cp in recv:
                cp.wait_recv()
        for send, _ in plan:
            for cp in send:
                cp.wait_send()
        for cp in loc:
            cp.wait()

    any_spec = pl.BlockSpec(memory_space=pl.ANY)
    return pl.pallas_call(
        body,
        out_shape=[jax.ShapeDtypeStruct(big.shape, big.dtype),
                   jax.ShapeDtypeStruct((N_DEV,) + small.shape, small.dtype)],
        in_specs=[any_spec, any_spec], out_specs=[any_spec, any_spec],
        scratch_shapes=[pltpu.SemaphoreType.DMA((7,)), pltpu.SemaphoreType.DMA((7,)),
                        pltpu.SemaphoreType.DMA((7,)), pltpu.SemaphoreType.DMA((7,)),
                        pltpu.SemaphoreType.DMA((2,))],
        name="grad_exchange",
    )(big, small)


def _slot_sum(name, slots, tr):
    rows = slots.shape[1]

    def body(s_ref, o_ref):
        acc = s_ref[0].astype(F32)
        for d in range(1, N_DEV):
            acc = acc + s_ref[d].astype(F32)
        o_ref[...] = acc

    return pl.pallas_call(
        body, grid=(rows // tr,),
        in_specs=[pl.BlockSpec((N_DEV, tr, LANES), lambda i: (0, i, 0))],
        out_specs=pl.BlockSpec((tr, LANES), lambda i: (i, 0)),
        out_shape=jax.ShapeDtypeStruct((rows, LANES), F32),
        name=name, compiler_params=_params(("arbitrary",)),
    )(slots)


def _adamw(name, w, g, m, v):
    rows, cols = w.shape
    tr = _pick(rows, (256, 128, 88, 32, 1))
    c1 = 1.0 - ADAM_B1 ** ADAM_STEP
    c2 = 1.0 - ADAM_B2 ** ADAM_STEP

    def body(w_ref, g_ref, m_ref, v_ref, d_ref, nm_ref, nv_ref):
        g = g_ref[...]
        nm = ADAM_B1 * m_ref[...] + (1.0 - ADAM_B1) * g
        nv = ADAM_B2 * v_ref[...] + (1.0 - ADAM_B2) * (g * g)
        nm_ref[...] = nm
        nv_ref[...] = nv
        d_ref[...] = -ADAM_LR * ((nm / c1) / (jnp.sqrt(nv / c2) + ADAM_EPS) + ADAM_WD * w_ref[...])

    spec = pl.BlockSpec((tr, cols), lambda i: (i, 0))
    return pl.pallas_call(
        body, grid=(rows // tr,), in_specs=[spec] * 4, out_specs=[spec] * 3,
        out_shape=[jax.ShapeDtypeStruct((rows, cols), F32)] * 3,
        name=name, compiler_params=_params(("arbitrary",)),
    )(w, g, m, v)


def _pack_shards(parts):
    flat = jnp.concatenate([p.reshape(-1) for p in parts])
    flat = jnp.pad(flat, (0, FLAT_ROWS * LANES - flat.shape[0]))
    return flat.reshape(FLAT_ROWS, LANES)


def _unpack_full(gathered):
    flat = gathered.reshape(N_DEV, FLAT_ROWS * LANES)
    out, off = {}, 0
    for name, (r, c), axis in SHARDED:
        blk = flat[:, off:off + r * c].reshape(N_DEV, r, c)
        off += r * c
        out[name] = blk.transpose(1, 0, 2).reshape(r, N_DEV * c) if axis == 1 else blk.reshape(N_DEV * r, c)
    return out

def _pack_full_grads(grads):
    parts = []
    for name, (r, c), axis in SHARDED:
        g = grads[name]
        blk = g.reshape(r, N_DEV, c).transpose(1, 0, 2) if axis == 1 else g.reshape(N_DEV, r, c)
        parts.append(blk.reshape(N_DEV, r * c))
    flat = jnp.concatenate(parts, axis=1).astype(BF16)
    flat = jnp.pad(flat, ((0, 0), (0, FLAT_ROWS * LANES - flat.shape[1])))
    return flat.reshape(N_DEV, FLAT_ROWS, LANES)


def _rope_tables(positions):
    inv_freq = ROPE_THETA ** (-jnp.arange(0, MLA_ROPE_DIM, 2, dtype=F32) / MLA_ROPE_DIM)
    ang = positions.astype(F32)[:, None] * inv_freq
    z64 = jnp.zeros((positions.shape[0], 64), F32)
    z32 = jnp.zeros((positions.shape[0], 32), F32)
    cos, sin = jnp.cos(ang), jnp.sin(ang)
    return (jnp.concatenate([z64, cos, cos, z32], axis=1), jnp.concatenate([z64, sin, sin, z32], axis=1))


def _row_tile(s_len, want):
    return _pick(s_len, (want, 256, 128))


def kernel(x, positions, norm_mix_pre, norm_mix_post, w_in, b_gate, q_norm, w_uq, kv_norm, w_ukv, w_proj_sb, w_proj_mla, w_out, norm_ffn_pre, norm_ffn_post, w_gate_up, w_down, loss_target, m_norm_mix_pre, m_norm_mix_post, m_w_in, m_b_gate, m_q_norm, m_w_uq, m_kv_norm, m_w_ukv, m_w_proj_sb, m_w_proj_mla, m_w_out, m_norm_ffn_pre, m_norm_ffn_post, m_w_gate_up, m_w_down, v_norm_mix_pre, v_norm_mix_post, v_w_in, v_b_gate, v_q_norm, v_w_uq, v_kv_norm, v_w_ukv, v_w_proj_sb, v_w_proj_mla, v_w_out, v_norm_ffn_pre, v_norm_ffn_post, v_w_gate_up, v_w_down):
    weights = dict(norm_mix_pre=norm_mix_pre, norm_mix_post=norm_mix_post, w_in=w_in, b_gate=b_gate, q_norm=q_norm,
                   w_uq=w_uq, kv_norm=kv_norm, w_ukv=w_ukv, w_proj_sb=w_proj_sb, w_proj_mla=w_proj_mla, w_out=w_out,
                   norm_ffn_pre=norm_ffn_pre, norm_ffn_post=norm_ffn_post, w_gate_up=w_gate_up, w_down=w_down)
    m_in = dict(norm_mix_pre=m_norm_mix_pre, norm_mix_post=m_norm_mix_post, w_in=m_w_in, b_gate=m_b_gate,
                q_norm=m_q_norm, w_uq=m_w_uq, kv_norm=m_kv_norm, w_ukv=m_w_ukv, w_proj_sb=m_w_proj_sb,
                w_proj_mla=m_w_proj_mla, w_out=m_w_out, norm_ffn_pre=m_norm_ffn_pre, norm_ffn_post=m_norm_ffn_post,
                w_gate_up=m_w_gate_up, w_down=m_w_down)
    v_in = dict(norm_mix_pre=v_norm_mix_pre, norm_mix_post=v_norm_mix_post, w_in=v_w_in, b_gate=v_b_gate,
                q_norm=v_q_norm, w_uq=v_w_uq, kv_norm=v_kv_norm, w_ukv=v_w_ukv, w_proj_sb=v_w_proj_sb,
                w_proj_mla=v_w_proj_mla, w_out=v_w_out, norm_ffn_pre=v_norm_ffn_pre, norm_ffn_post=v_norm_ffn_post,
                w_gate_up=v_w_gate_up, w_down=v_w_down)
    order = list(weights)

    xs = x[0]
    target = loss_target[0]
    s_len = xs.shape[0]
    tm = _row_tile(s_len, 256)
    tm_ffn = _row_tile(s_len, 256)

    shard = _pack_shards([weights[name][0].astype(BF16) for name, _, _ in SHARDED])
    full = _unpack_full(_all_gather(shard))
    wi = full["w_in"]
    zc = lambda n: jnp.zeros((D_MODEL, n), BF16)
    w_ext = jnp.concatenate([wi[:, :2176], zc(64), wi[:, 2176:2208], zc(32), wi[:, 2208:]], axis=1)
    wa = jnp.pad(full["w_uq"].reshape(MLA_Q_RANK, N_HEADS, MLA_QK_DIM), ((0, 0), (0, 0), (0, 32))
                 ).reshape(MLA_Q_RANK, N_HEADS * LANES)
    ukv = full["w_ukv"].reshape(MLA_KV_RANK, N_HEADS, LANES)
    wk = jnp.pad(ukv[:, :, :64], ((0, 0), (0, 0), (0, 64))).reshape(MLA_KV_RANK, N_HEADS * LANES)
    wv = ukv[:, :, 64:].reshape(MLA_KV_RANK, 512)
    w_sb, w_mla, w_o, w_gu, w_dn = (full["w_proj_sb"], full["w_proj_mla"], full["w_out"], full["w_gate_up"],
                                    full["w_down"])
    cr, sr = _rope_tables(positions[0])

    qkv, cq, ckv, kr, gl, hb = _in_proj(xs, norm_mix_pre, w_ext, tm)
    q_mla, k_mla, v_mla, cqn, ckvn = _mla_up(cq, ckv, kr, cr, sr, q_norm, kv_norm, wa, wk, wv, tm)
    o_sb = _sb_fwd(qkv)
    o_mla, lse = _mla_fwd(q_mla, k_mla, v_mla)
    x1, y, merged = _mix_out(o_sb, o_mla, gl, xs, b_gate, norm_mix_post, w_sb, w_mla, w_o, tm)
    dx2, f, h2, loss_part = _ffn_fwd(x1, target, norm_ffn_pre, norm_ffn_post, w_gu, w_dn, tm_ffn)
    loss = lax.psum(0.5 / D_MODEL * jnp.sum(loss_part), ("x", "y", "c"))

    dx1, act, dgu, dfb, dg_ffn_pre, dg_ffn_post = _ffn_bwd(dx2, f, x1, norm_ffn_pre, norm_ffn_post, w_gu, w_dn, tm_ffn)
    dyb, dpsb, dpmla, dgl, do_sb, do_mla, dg_mix_post, db_gate = _mix_bwd(
        dx1, y, o_sb, o_mla, gl, b_gate, norm_mix_post, w_sb, w_mla, w_o, tm)
    dq_sb, dk_sb, dv_sb = _sb_bwd(qkv, do_sb, o_sb)
    dq_mla, dk_mla, dv_mla = _mla_bwd(q_mla, k_mla, v_mla, do_mla, o_mla, lse)
    da, dkb, dvb, dlat, dg_q, dg_kv = _mla_up_bwd(dq_mla, dk_mla, dv_mla, cq, ckv, cr, sr, q_norm, kv_norm,
                                                  wa, wk, wv, tm)
    dx, dproj, dg_mix_pre = _in_proj_bwd(xs, dx1, dq_sb, dk_sb, dv_sb, dlat, dgl, norm_mix_pre, w_ext, tm)

    d_ext = _matmul_tn("dw_in", hb, dproj)
    d_wa = _matmul_tn("dw_uq", cqn, da)
    d_wk = _matmul_tn("dw_uk", ckvn, dkb)
    d_wv = _matmul_tn("dw_uv", ckvn, dvb)
    grads = {
        "w_in": jnp.concatenate([d_ext[:, :2176], d_ext[:, 2240:2272], d_ext[:, EXT_GL:]], axis=1),
        "w_uq": d_wa.reshape(MLA_Q_RANK, N_HEADS, LANES)[:, :, :MLA_QK_DIM].reshape(MLA_Q_RANK, 768),
        "w_ukv": jnp.concatenate([d_wk.reshape(MLA_KV_RANK, N_HEADS, LANES)[:, :, :64],
                                  d_wv.reshape(MLA_KV_RANK, N_HEADS, 64)], axis=2).reshape(MLA_KV_RANK, 1024),
        "w_proj_sb": _matmul_tn("dw_proj_sb", o_sb.astype(BF16), dpsb),
        "w_proj_mla": _matmul_tn("dw_proj_mla", o_mla.astype(BF16), dpmla),
        "w_out": _matmul_tn("dw_out", merged, dyb),
        "w_gate_up": _matmul_tn("dw_gate_up", h2, dgu),
        "w_down": _matmul_tn("dw_down", act, dfb),
    }
    small_parts = dict(norm_mix_pre=dg_mix_pre, norm_mix_post=dg_mix_post, b_gate=db_gate, q_norm=dg_q,
                       kv_norm=dg_kv, norm_ffn_pre=dg_ffn_pre, norm_ffn_post=dg_ffn_post)
    small = jnp.concatenate([small_parts[name].sum(axis=0) for name, _ in SMALL])
    small = jnp.pad(small, (0, SMALL_ROWS * LANES - small.shape[0])).reshape(SMALL_ROWS, LANES)

    big_slots, small_slots = _grad_exchange(_pack_full_grads(grads), small)
    g_flat = _slot_sum("grad_shard_sum", big_slots, 512).reshape(-1)
    s_flat = _slot_sum("grad_small_sum", small_slots, SMALL_ROWS).reshape(-1)
    g_out, off = {}, 0
    for name, (r, c), _ in SHARDED:
        g_out[name] = g_flat[off:off + r * c].reshape(r, c)
        off += r * c
    off = 0
    for name, n in SMALL:
        g_out[name] = s_flat[off:off + n].reshape(1, n)
        off += n

    deltas, new_m, new_v = {}, {}, {}
    for name in order:
        w2 = weights[name].reshape(g_out[name].shape)
        d, nm, nv = _adamw("adamw_" + name, w2, g_out[name], m_in[name].reshape(w2.shape), v_in[name].reshape(w2.shape))
        shape = weights[name].shape
        deltas[name], new_m[name], new_v[name] = d.reshape(shape), nm.reshape(shape), nv.reshape(shape)
        g_out[name] = g_out[name].reshape(shape)

    return (loss, dx[None], *[g_out[n] for n in order], *[deltas[n] for n in order],
            *[new_m[n] for n in order], *[new_v[n] for n in order])
```

```python
import functools
import math

import jax
import jax.numpy as jnp
from jax import lax
from jax.experimental import pallas as pl
from jax.experimental.pallas import tpu as pltpu

F32 = jnp.float32
BF16 = jnp.bfloat16

D_MODEL = 1024
N_HEADS = 8
SB_WIDTH = 512
MLA_Q_RANK = 384
MLA_KV_RANK = 256
MLA_ROPE_DIM = 32
MLA_QK_DIM = 96
D_FF = 2816
ROPE_THETA = 10000.0
EPS = 1e-6
SB_SCALE = 1.0 / math.sqrt(64.0)
MLA_SCALE = 1.0 / math.sqrt(96.0)
NEG_BIG = -1e30

ADAM_LR = 0.001
ADAM_B1 = 0.9
ADAM_B2 = 0.999
ADAM_EPS = 1e-08
ADAM_WD = 0.01
ADAM_STEP = 10

N_DEV = 8
LANES = 128
TQ = 512
TK = 128
DIAG_TILES = TQ // TK
SWEEP_UNROLL = 2
VMEM_LIMIT = 56 << 20

EXT_QKV = 0
EXT_CQ = 1536
EXT_CKV = 1920
EXT_KR = 2176
EXT_GL = 2304
EXT_N = 4352

SHARDED = (
    ("w_in", (1024, 532), 1), ("w_uq", (384, 96), 1), ("w_ukv", (256, 128), 1),
    ("w_proj_sb", (512, 128), 1), ("w_proj_mla", (512, 128), 1), ("w_out", (128, 1024), 0),
    ("w_gate_up", (1024, 704), 1), ("w_down", (352, 1024), 0),
)
FLAT_ROWS = 15360
SMALL = (("norm_mix_pre", 1024), ("norm_mix_post", 1024), ("b_gate", 2048), ("q_norm", 384),
         ("kv_norm", 256), ("norm_ffn_pre", 1024), ("norm_ffn_post", 1024))
SMALL_ROWS = 56


def _dot(a, b):
    return jnp.dot(a, b, preferred_element_type=F32)


def _dot_nt(a, b):
    return lax.dot_general(a, b, (((1,), (1,)), ((), ())), preferred_element_type=F32)


def _dot_tn(a, b):
    return lax.dot_general(a, b, (((0,), (0,)), ((), ())), preferred_element_type=F32)


def _rms(x):
    r = lax.rsqrt(jnp.mean(x * x, axis=-1, keepdims=True) + EPS)
    return x * r, r


def _rms_bwd(dn, n, r):
    return r * (dn - n * jnp.mean(dn * n, axis=-1, keepdims=True))


def _colsum8(x):
    return jnp.sum(x.reshape(x.shape[0] // 8, 8, x.shape[1]), axis=0)


def _split(x):
    hi = x.astype(BF16)
    return hi, (x - hi.astype(F32)).astype(BF16)


def _rot(x):
    lane = lax.broadcasted_iota(jnp.int32, x.shape, 1)
    up = pltpu.roll(x, 112, 1)
    down = pltpu.roll(x, 16, 1)
    return jnp.where((lane >= 64) & (lane < 80), -up, jnp.where((lane >= 80) & (lane < 96), down, 0.0))


def _params(sem):
    return pltpu.CompilerParams(dimension_semantics=sem, vmem_limit_bytes=VMEM_LIMIT)


def _rows_call(name, body, n_rows, tm, row_ins, const_ins, row_outs, acc_outs):
    in_specs = [pl.BlockSpec((tm, a.shape[1]), lambda i: (i, 0)) for a in row_ins]
    in_specs += [pl.BlockSpec(a.shape, lambda i: (0, 0), pipeline_mode=pl.Buffered(1)) for a in const_ins]
    out_specs = [pl.BlockSpec((tm, n), lambda i: (i, 0)) for n, _ in row_outs]
    out_specs += [pl.BlockSpec(s, lambda i: (0, 0)) for s in acc_outs]
    out_shape = [jax.ShapeDtypeStruct((n_rows, n), dt) for n, dt in row_outs]
    out_shape += [jax.ShapeDtypeStruct(s, F32) for s in acc_outs]
    return pl.pallas_call(
        body, grid=(n_rows // tm,), in_specs=in_specs, out_specs=out_specs, out_shape=out_shape,
        name=name, compiler_params=_params(("arbitrary",)),
    )(*row_ins, *const_ins)


def _in_proj(x, g_pre, w_ext, tm):
    def body(x_ref, g_ref, w_ref, qkv_ref, cq_ref, ckv_ref, kr_ref, gl_ref, h_ref):
        n, _ = _rms(x_ref[...])
        hb = (n * g_ref[...]).astype(BF16)
        h_ref[...] = hb
        for c in range(0, 1536, 512):
            qkv_ref[:, c:c + 512] = _dot(hb, w_ref[:, c:c + 512]).astype(BF16)
        cq_ref[...] = _dot(hb, w_ref[:, EXT_CQ:EXT_CKV])
        ckv_ref[...] = _dot(hb, w_ref[:, EXT_CKV:EXT_KR])
        kr_ref[...] = _dot(hb, w_ref[:, EXT_KR:EXT_GL])
        for c in range(0, 2048, 512):
            gl_ref[:, c:c + 512] = _dot(hb, w_ref[:, EXT_GL + c:EXT_GL + c + 512])

    return _rows_call("in_proj", body, x.shape[0], tm, [x], [g_pre, w_ext],
                      [(1536, BF16), (384, F32), (256, F32), (128, F32), (2048, F32), (1024, BF16)], [])


def _mla_up(cq, ckv, kr, cr, sr, q_norm, kv_norm, wa, wk, wv, tm):
    def body(cq_ref, ckv_ref, kr_ref, cr_ref, sr_ref, qn_ref, kvn_ref, wa_ref, wk_ref, wv_ref,
             q_ref, k_ref, v_ref, cqn_ref, ckvn_ref):
        nq, _ = _rms(cq_ref[...])
        cqn = (nq * qn_ref[...]).astype(BF16)
        cqn_ref[...] = cqn
        nk, _ = _rms(ckv_ref[...])
        ckvn = (nk * kvn_ref[...]).astype(BF16)
        ckvn_ref[...] = ckvn
        cr = cr_ref[...]
        sr = sr_ref[...]
        lane = lax.broadcasted_iota(jnp.int32, cr.shape, 1)
        cm = cr + (lane < 64).astype(F32)
        kr = kr_ref[...]
        krp = kr * cr + _rot(kr) * sr
        for h in range(N_HEADS):
            hs = slice(h * LANES, (h + 1) * LANES)
            a = _dot(cqn, wa_ref[:, hs])
            q_ref[:, hs] = (a * cm + _rot(a) * sr).astype(BF16)
            k_ref[:, hs] = (_dot(ckvn, wk_ref[:, hs]) + krp).astype(BF16)
        v_ref[...] = _dot(ckvn, wv_ref[...]).astype(BF16)

    return _rows_call("mla_up", body, cq.shape[0], tm, [cq, ckv, kr, cr, sr], [q_norm, kv_norm, wa, wk, wv],
                      [(1024, BF16), (1024, BF16), (512, BF16), (384, BF16), (256, BF16)], [])


def _mix_out(o_sb, o_mla, gl, x, b_gate, g_post, w_sb, w_mla, w_out, tm):
    def body(osb_ref, omla_ref, gl_ref, x_ref, b_ref, gp_ref, wsb_ref, wmla_ref, wout_ref,
             x1_ref, y_ref, mb_ref):
        psb = _dot(osb_ref[...].astype(BF16), wsb_ref[...])
        pmla = _dot(omla_ref[...].astype(BF16), wmla_ref[...])
        gates = jax.nn.sigmoid(gl_ref[...] + b_ref[...])
        mb = (gates[:, :D_MODEL] * psb + gates[:, D_MODEL:] * pmla).astype(BF16)
        mb_ref[...] = mb
        y = _dot(mb, wout_ref[...])
        y_ref[...] = y
        n, _ = _rms(y)
        x1_ref[...] = x_ref[...] + n * gp_ref[...]

    return _rows_call("mix_out", body, x.shape[0], tm, [o_sb, o_mla, gl, x], [b_gate, g_post, w_sb, w_mla, w_out],
                      [(1024, F32), (1024, F32), (1024, BF16)], [])


FF_CHUNK = 1408


def _ffn_fwd(x1, target, g_pre, g_post, w_gu, w_down, tm):
    def body(x1_ref, t_ref, gpre_ref, gpost_ref, wgu_ref, wd_ref, dx2_ref, f_ref, h2_ref, loss_ref):
        x1 = x1_ref[...]
        n, _ = _rms(x1)
        h2 = (n * gpre_ref[...]).astype(BF16)
        h2_ref[...] = h2
        f = jnp.zeros((tm, D_MODEL), F32)
        for c in range(0, D_FF, FF_CHUNK):
            g = _dot(h2, wgu_ref[:, c:c + FF_CHUNK])
            u = _dot(h2, wgu_ref[:, D_FF + c:D_FF + c + FF_CHUNK])
            act = (g * jax.nn.sigmoid(g) * u).astype(BF16)
            f = f + _dot(act, wd_ref[c:c + FF_CHUNK, :])
        f_ref[...] = f
        nf, _ = _rms(f)
        err = x1 + nf * gpost_ref[...] - t_ref[...]
        dx2_ref[...] = err * (1.0 / D_MODEL)
        e8 = _colsum8(err * err)
        part = e8[:, 0:LANES]
        for c in range(LANES, D_MODEL, LANES):
            part = part + e8[:, c:c + LANES]

        @pl.when(pl.program_id(0) == 0)
        def _():
            loss_ref[...] = jnp.zeros_like(loss_ref)

        loss_ref[...] += part

    return _rows_call("ffn_fwd", body, x1.shape[0], tm, [x1, target], [g_pre, g_post, w_gu, w_down],
                      [(1024, F32), (1024, F32), (1024, BF16)], [(8, LANES)])


def _ffn_bwd(dx2, f, x1, g_pre, g_post, w_gu, w_down, tm):
    def body(dx2_ref, f_ref, x1_ref, gpre_ref, gpost_ref, wgu_ref, wd_ref,
             dx1_ref, act_ref, dgu_ref, dfb_ref, dgpre_ref, dgpost_ref):
        @pl.when(pl.program_id(0) == 0)
        def _():
            dgpre_ref[...] = jnp.zeros_like(dgpre_ref)
            dgpost_ref[...] = jnp.zeros_like(dgpost_ref)

        dx2 = dx2_ref[...]
        nf, rf = _rms(f_ref[...])
        dgpost_ref[...] += _colsum8(dx2 * nf)
        dfb = _rms_bwd(dx2 * gpost_ref[...], nf, rf).astype(BF16)
        dfb_ref[...] = dfb
        x1 = x1_ref[...]
        n1, r1 = _rms(x1)
        h2 = (n1 * gpre_ref[...]).astype(BF16)
        dh2 = jnp.zeros((tm, D_MODEL), F32)
        for c in range(0, D_FF, FF_CHUNK):
            cs, us = slice(c, c + FF_CHUNK), slice(D_FF + c, D_FF + c + FF_CHUNK)
            g = _dot(h2, wgu_ref[:, cs])
            u = _dot(h2, wgu_ref[:, us])
            sg = jax.nn.sigmoid(g)
            si = g * sg
            act_ref[:, cs] = (si * u).astype(BF16)
            dact = _dot_nt(dfb, wd_ref[cs, :])
            dg = (dact * u * (sg * (1.0 + g * (1.0 - sg)))).astype(BF16)
            du = (dact * si).astype(BF16)
            dgu_ref[:, cs] = dg
            dgu_ref[:, us] = du
            dh2 = dh2 + _dot_nt(dg, wgu_ref[:, cs]) + _dot_nt(du, wgu_ref[:, us])
        dgpre_ref[...] += _colsum8(dh2 * n1)
        dx1_ref[...] = dx2 + _rms_bwd(dh2 * gpre_ref[...], n1, r1)

    return _rows_call("ffn_bwd", body, dx2.shape[0], tm, [dx2, f, x1], [g_pre, g_post, w_gu, w_down],
                      [(1024, F32), (D_FF, BF16), (2 * D_FF, BF16), (1024, BF16)], [(8, 1024), (8, 1024)])


def _mix_bwd(dx1, y, o_sb, o_mla, gl, b_gate, g_post, w_sb, w_mla, w_out, tm):
    def body(dx1_ref, y_ref, osb_ref, omla_ref, gl_ref, b_ref, gp_ref, wsb_ref, wmla_ref, wout_ref,
             dyb_ref, dpsb_ref, dpmla_ref, dgl_ref, dosb_ref, domla_ref, dgpost_ref, dbg_ref):
        @pl.when(pl.program_id(0) == 0)
        def _():
            dgpost_ref[...] = jnp.zeros_like(dgpost_ref)
            dbg_ref[...] = jnp.zeros_like(dbg_ref)

        dx1 = dx1_ref[...]
        ny, ry = _rms(y_ref[...])
        dgpost_ref[...] += _colsum8(dx1 * ny)
        dyb = _rms_bwd(dx1 * gp_ref[...], ny, ry).astype(BF16)
        dyb_ref[...] = dyb
        dm = _dot_nt(dyb, wout_ref[...])
        psb = _dot(osb_ref[...].astype(BF16), wsb_ref[...])
        pmla = _dot(omla_ref[...].astype(BF16), wmla_ref[...])
        gates = jax.nn.sigmoid(gl_ref[...] + b_ref[...])
        g0, g1 = gates[:, :D_MODEL], gates[:, D_MODEL:]
        dpsb = (dm * g0).astype(BF16)
        dpmla = (dm * g1).astype(BF16)
        dpsb_ref[...] = dpsb
        dpmla_ref[...] = dpmla
        dgl0 = dm * psb * g0 * (1.0 - g0)
        dgl1 = dm * pmla * g1 * (1.0 - g1)
        dgl_ref[:, :D_MODEL] = dgl0.astype(BF16)
        dgl_ref[:, D_MODEL:] = dgl1.astype(BF16)
        dbg_ref[:, :D_MODEL] += _colsum8(dgl0)
        dbg_ref[:, D_MODEL:] += _colsum8(dgl1)
        dosb_ref[...] = _dot_nt(dpsb, wsb_ref[...]).astype(BF16)
        domla_ref[...] = _dot_nt(dpmla, wmla_ref[...]).astype(BF16)

    return _rows_call("mix_bwd", body, dx1.shape[0], tm, [dx1, y, o_sb, o_mla, gl],
                      [b_gate, g_post, w_sb, w_mla, w_out],
                      [(1024, BF16), (1024, BF16), (1024, BF16), (2048, BF16), (512, BF16), (512, BF16)],
                      [(8, 1024), (8, 2048)])


def _mla_up_bwd(dq, dk, dv, cq, ckv, cr, sr, q_norm, kv_norm, wa, wk, wv, tm):
    def body(dq_ref, dk_ref, dv_ref, cq_ref, ckv_ref, cr_ref, sr_ref, qn_ref, kvn_ref, wa_ref, wk_ref, wv_ref,
             da_ref, dkb_ref, dvb_ref, dlat_ref, dqn_ref, dkvn_ref):
        @pl.when(pl.program_id(0) == 0)
        def _():
            dqn_ref[...] = jnp.zeros_like(dqn_ref)
            dkvn_ref[...] = jnp.zeros_like(dkvn_ref)

        cr = cr_ref[...]
        sr = sr_ref[...]
        lane = lax.broadcasted_iota(jnp.int32, cr.shape, 1)
        cm = cr + (lane < 64).astype(F32)
        nq, rq = _rms(cq_ref[...])
        nk, rk = _rms(ckv_ref[...])
        dcqn = jnp.zeros((tm, MLA_Q_RANK), F32)
        dckvn = jnp.zeros((tm, MLA_KV_RANK), F32)
        dkrp = jnp.zeros((tm, LANES), F32)
        for h in range(N_HEADS):
            hs = slice(h * LANES, (h + 1) * LANES)
            dqh = dq_ref[:, hs]
            da = (dqh * cm - _rot(dqh * sr)).astype(BF16)
            da_ref[:, hs] = da
            dcqn = dcqn + _dot_nt(da, wa_ref[:, hs])
            dkh = dk_ref[:, hs]
            dkb = dkh.astype(BF16)
            dkb_ref[:, hs] = dkb
            dckvn = dckvn + _dot_nt(dkb, wk_ref[:, hs])
            dkrp = dkrp + dkh
        dvb = dv_ref[...].astype(BF16)
        dvb_ref[...] = dvb
        dckvn = dckvn + _dot_nt(dvb, wv_ref[...])
        dkr = dkrp * cr - _rot(dkrp * sr)
        dqn_ref[...] += _colsum8(dcqn * nq)
        dkvn_ref[...] += _colsum8(dckvn * nk)
        dlat_ref[:, 0:384] = _rms_bwd(dcqn * qn_ref[...], nq, rq).astype(BF16)
        dlat_ref[:, 384:640] = _rms_bwd(dckvn * kvn_ref[...], nk, rk).astype(BF16)
        dlat_ref[:, 640:768] = dkr.astype(BF16)

    return _rows_call("mla_up_bwd", body, dq.shape[0], tm, [dq, dk, dv, cq, ckv, cr, sr],
                      [q_norm, kv_norm, wa, wk, wv],
                      [(1024, BF16), (1024, BF16), (512, BF16), (768, BF16)], [(8, 384), (8, 256)])


def _in_proj_bwd(x, dx1, dq_sb, dk_sb, dv_sb, dlat, dgl, g_pre, w_ext, tm):
    def body(x_ref, dx1_ref, dq_ref, dk_ref, dv_ref, dlat_ref, dgl_ref, g_ref, w_ref,
             dx_ref, dproj_ref, dg_ref):
        @pl.when(pl.program_id(0) == 0)
        def _():
            dg_ref[...] = jnp.zeros_like(dg_ref)

        dproj_ref[:, 0:512] = dq_ref[...].astype(BF16)
        dproj_ref[:, 512:1024] = dk_ref[...].astype(BF16)
        dproj_ref[:, 1024:1536] = dv_ref[...].astype(BF16)
        dproj_ref[:, EXT_CQ:EXT_GL] = dlat_ref[...]
        dproj_ref[:, EXT_GL:EXT_N] = dgl_ref[...]
        dh = jnp.zeros((tm, D_MODEL), F32)
        for c in range(0, EXT_N, 2176):
            dh = dh + _dot_nt(dproj_ref[:, c:c + 2176], w_ref[:, c:c + 2176])
        n, r = _rms(x_ref[...])
        dg_ref[...] += _colsum8(dh * n)
        dx_ref[...] = dx1_ref[...] + _rms_bwd(dh * g_ref[...], n, r)

    return _rows_call("in_proj_bwd", body, x.shape[0], tm, [x, dx1, dq_sb, dk_sb, dv_sb, dlat, dgl],
                      [g_pre, w_ext], [(1024, F32), (EXT_N, BF16)], [(8, 1024)])


def _head_masked(x):
    lane = lax.broadcasted_iota(jnp.int32, x.shape, 1)
    zero = jnp.zeros_like(x)
    return jnp.where(lane < 64, x, zero), jnp.where(lane >= 64, x, zero)


def _cum_weights():
    row = lax.broadcasted_iota(jnp.int32, (TK, TK), 0)
    col = lax.broadcasted_iota(jnp.int32, (TK, TK), 1)
    half = jnp.concatenate([(row > col).astype(BF16), jnp.ones((TK, TK), BF16)], axis=1)
    return jnp.concatenate([half, half], axis=0)


def _split_cat(x):
    hi, lo = _split(x)
    return jnp.concatenate([hi, lo], axis=1)


def _sweep(i, tiles, masks):
    row, col, strict = masks

    def diag(t, _):
        top = DIAG_TILES - 1 - t * SWEEP_UNROLL
        keys = [(top - u) * TK + col for u in range(SWEEP_UNROLL)]
        tiles(i * DIAG_TILES + top, [key < row if strict else key <= row for key in keys])
        return 0

    lax.fori_loop(0, DIAG_TILES // SWEEP_UNROLL, diag, 0)

    def full(g, _):
        tiles(i * DIAG_TILES - 1 - g * SWEEP_UNROLL, [None] * SWEEP_UNROLL)
        return 0

    lax.fori_loop(0, (i * DIAG_TILES) // SWEEP_UNROLL, full, 0)


def _each_tile(block):
    def tiles(top, valids):
        for u, valid in enumerate(valids):
            block(top - u, valid)
    return tiles


def _tile_masks(strict):
    return (lax.broadcasted_iota(jnp.int32, (TQ, TK), 0), lax.broadcasted_iota(jnp.int32, (TQ, TK), 1), strict)


def _sb_tile(z, r, w_cum, valid):
    soft = jnp.log(1.0 + jnp.exp(-jnp.abs(z)))
    lsm_all = -jnp.maximum(z, 0.0) - soft
    lsm = lsm_all if valid is None else jnp.where(valid, lsm_all, 0.0)
    cs = _dot(_split_cat(lsm), w_cum)
    a = jnp.exp(z + lsm_all + cs[:, :TK] + r)
    if valid is not None:
        a = jnp.where(valid, a, 0.0)
    return lsm_all, a, r + cs[:, TK:]


def _sb_fwd(qkv):
    s_len = qkv.shape[0]

    def body(q_ref, k_ref, v_ref, o_ref, acc_ref, r_ref):
        i = pl.program_id(1)
        w_cum = _cum_weights()
        q = q_ref[...] * SB_SCALE
        acc_ref[...] = jnp.zeros_like(acc_ref)
        r_ref[...] = jnp.zeros_like(r_ref)

        def block(j, valid):
            off = pl.multiple_of(j * TK, TK)
            k0, k1 = _head_masked(k_ref[pl.ds(off, TK), :])
            v0, v1 = _head_masked(v_ref[pl.ds(off, TK), :])
            z01 = _dot_nt(q, jnp.concatenate([k0, k1], axis=0))
            parts = []
            for hh in range(2):
                _, a, r = _sb_tile(z01[:, hh * TK:(hh + 1) * TK], r_ref[hh], w_cum, valid)
                r_ref[hh] = r
                parts.append(_split_cat(a))
            acc_ref[...] += _dot(jnp.concatenate(parts, axis=1), jnp.concatenate([v0, v0, v1, v1], axis=0))

        _sweep(i, _each_tile(block), _tile_masks(True))
        o_ref[...] = acc_ref[...]

    return pl.pallas_call(
        body, grid=(4, s_len // TQ),
        in_specs=[pl.BlockSpec((TQ, LANES), lambda h, i: (i, h)),
                  pl.BlockSpec((s_len, LANES), lambda h, i: (0, 4 + h)),
                  pl.BlockSpec((s_len, LANES), lambda h, i: (0, 8 + h))],
        out_specs=pl.BlockSpec((TQ, LANES), lambda h, i: (i, h)),
        out_shape=jax.ShapeDtypeStruct((s_len, SB_WIDTH), F32),
        scratch_shapes=[pltpu.VMEM((TQ, LANES), F32), pltpu.VMEM((2, TQ, LANES), F32)],
        name="sb_attn_fwd", compiler_params=_params(("arbitrary", "arbitrary")),
    )(qkv, qkv, qkv)


def _row_dots(do, o):
    prod = do.astype(F32) * o
    p0, p1 = _head_masked(prod)
    return tuple(jnp.broadcast_to(jnp.sum(p, axis=1, keepdims=True), prod.shape) for p in (p0, p1))


def _sb_bwd(qkv, do, o):
    s_len = qkv.shape[0]

    def body(q_ref, k_ref, v_ref, do_ref, o_ref, dq_ref, dk_ref, dv_ref, dq_acc, r_ref, g_ref, d_ref):
        i = pl.program_id(1)

        @pl.when(i == 0)
        def _():
            dk_ref[...] = jnp.zeros_like(dk_ref)
            dv_ref[...] = jnp.zeros_like(dv_ref)

        w_cum = _cum_weights()
        q = q_ref[...] * SB_SCALE
        do = do_ref[...]
        q_rows = jnp.concatenate(_head_masked(q), axis=0)
        do_rows = jnp.concatenate(_head_masked(do), axis=0)
        d0, d1 = _row_dots(do, o_ref[...])
        d_ref[0] = d0
        d_ref[1] = d1
        dq_acc[...] = jnp.zeros_like(dq_acc)
        r_ref[...] = jnp.zeros_like(r_ref)
        g_ref[...] = jnp.zeros_like(g_ref)

        def block(j, valid):
            off = pl.multiple_of(j * TK, TK)
            kk = jnp.concatenate(_head_masked(k_ref[pl.ds(off, TK), :]), axis=0)
            vv = jnp.concatenate(_head_masked(v_ref[pl.ds(off, TK), :]), axis=0)
            z01 = _dot_nt(q, kk)
            da01 = _dot_nt(do, vv)
            dzs, avs = [], []
            for hh in range(2):
                hs = slice(hh * TK, (hh + 1) * TK)
                z = z01[:, hs]
                lsm_all, a, r = _sb_tile(z, r_ref[hh], w_cum, valid)
                r_ref[hh] = r
                g = a * da01[:, hs]
                gs = _dot(_split_cat(g), w_cum)
                later = gs[:, :TK] + g_ref[hh]
                g_ref[hh] = g_ref[hh] + gs[:, TK:]
                earlier = d_ref[hh] - g - later
                beta = jnp.exp(z + lsm_all)
                dz = g * (1.0 - beta) - beta * earlier
                if valid is not None:
                    dz = jnp.where(valid, dz, 0.0)
                dzs.append(dz.astype(BF16))
                avs.append(a.astype(BF16))
            dq_acc[...] += _dot(jnp.concatenate(dzs, axis=1), kk)
            dk_ref[pl.ds(off, TK), :] += _dot_tn(jnp.concatenate(dzs, axis=0), q_rows)
            dv_ref[pl.ds(off, TK), :] += _dot_tn(jnp.concatenate(avs, axis=0), do_rows)

        _sweep(i, _each_tile(block), _tile_masks(True))
        dq_ref[...] = dq_acc[...] * SB_SCALE

    tile = pl.BlockSpec((TQ, LANES), lambda h, i: (i, h))
    full = pl.BlockSpec((s_len, LANES), lambda h, i: (0, h))
    k_full = pl.BlockSpec((s_len, LANES), lambda h, i: (0, 4 + h))
    v_full = pl.BlockSpec((s_len, LANES), lambda h, i: (0, 8 + h))
    return pl.pallas_call(
        body, grid=(4, s_len // TQ),
        in_specs=[tile, k_full, v_full, tile, tile],
        out_specs=[tile, full, full],
        out_shape=[jax.ShapeDtypeStruct((s_len, SB_WIDTH), F32)] * 3,
        scratch_shapes=[pltpu.VMEM((TQ, LANES), F32), pltpu.VMEM((2, TQ, LANES), F32),
                        pltpu.VMEM((2, TQ, LANES), F32), pltpu.VMEM((2, TQ, LANES), F32)],
        name="sb_attn_bwd", compiler_params=_params(("arbitrary", "arbitrary")),
    )(qkv, qkv, qkv, do, o)


def _block_diag(x):
    x0, x1 = x[:, :LANES], x[:, LANES:]
    zero = jnp.zeros_like(x0)
    return jnp.concatenate([jnp.concatenate([x0, zero], axis=1), jnp.concatenate([zero, x1], axis=1)], axis=0)


def _mla_fwd(q, k, v):
    s_len = q.shape[0]

    def body(q_ref, k_ref, v_ref, o_ref, lse_ref, acc_ref, m_ref):
        i = pl.program_id(1)
        lane = lax.broadcasted_iota(jnp.int32, (TQ, LANES), 1)
        r2 = lax.broadcasted_iota(jnp.int32, (2 * TK, LANES), 0)
        c2 = lax.broadcasted_iota(jnp.int32, (2 * TK, LANES), 1)
        head_ones = ((r2 < TK) == (c2 < 64)).astype(BF16)
        q = q_ref[...]
        acc_ref[...] = jnp.zeros_like(acc_ref)
        m_ref[...] = jnp.full(m_ref.shape, NEG_BIG, F32)

        def tiles(top, valids):
            s01s, vs = [], []
            for u in range(len(valids)):
                off = pl.multiple_of((top - u) * TK, TK)
                s01s.append(_dot_nt(q, _block_diag(k_ref[pl.ds(off, TK), :])))
                vv = jnp.concatenate(_head_masked(v_ref[pl.ds(off, TK), :]), axis=0)
                vs.append(jnp.concatenate([vv, head_ones], axis=1))
            ps = [[None, None] for _ in valids]
            alphas = []
            for hh in range(2):
                ss = []
                for u, valid in enumerate(valids):
                    s = s01s[u][:, hh * TK:(hh + 1) * TK] * MLA_SCALE
                    ss.append(s if valid is None else jnp.where(valid, s, NEG_BIG))
                m_old = m_ref[hh]
                m = jnp.maximum(m_old, jnp.max(functools.reduce(jnp.maximum, ss), axis=1, keepdims=True))
                m_ref[hh] = m
                alphas.append(jnp.exp(m_old - m))
                for u, s in enumerate(ss):
                    ps[u][hh] = jnp.exp(s - m).astype(BF16)
            scale = jnp.where(lane < 64, alphas[0], alphas[1])
            p_all = jnp.concatenate([p for pair in ps for p in pair], axis=1)
            acc_ref[...] = (acc_ref[...] * jnp.concatenate([scale, scale], axis=1)
                            + _dot(p_all, jnp.concatenate(vs, axis=0)))

        _sweep(i, tiles, _tile_masks(False))
        acc = acc_ref[...]
        den = acc[:, LANES:]
        o_ref[...] = acc[:, :LANES] / den
        for hh, mask in enumerate((lane < 64, lane >= 64)):
            l = jnp.max(jnp.where(mask, den, 0.0), axis=1, keepdims=True)
            lse_ref[hh] = jnp.broadcast_to(m_ref[hh] + jnp.log(l), (TQ, LANES))

    return pl.pallas_call(
        body, grid=(4, s_len // TQ),
        in_specs=[pl.BlockSpec((TQ, 2 * LANES), lambda h, i: (i, h)),
                  pl.BlockSpec((s_len, 2 * LANES), lambda h, i: (0, h)),
                  pl.BlockSpec((s_len, LANES), lambda h, i: (0, h))],
        out_specs=[pl.BlockSpec((TQ, LANES), lambda h, i: (i, h)),
                   pl.BlockSpec((2, TQ, LANES), lambda h, i: (h, i, 0))],
        out_shape=[jax.ShapeDtypeStruct((s_len, 512), F32),
                   jax.ShapeDtypeStruct((N_HEADS, s_len, LANES), F32)],
        scratch_shapes=[pltpu.VMEM((TQ, 2 * LANES), F32), pltpu.VMEM((2, TQ, 1), F32)],
        name="mla_attn_fwd", compiler_params=_params(("arbitrary", "arbitrary")),
    )(q, k, v)


def _mla_bwd(q, k, v, do, o, lse):
    s_len = q.shape[0]

    def body(q_ref, k_ref, v_ref, do_ref, o_ref, lse_ref, dq_ref, dk_ref, dv_ref, dq_acc, d_ref):
        i = pl.program_id(1)

        @pl.when(i == 0)
        def _():
            dk_ref[...] = jnp.zeros_like(dk_ref)
            dv_ref[...] = jnp.zeros_like(dv_ref)

        q = q_ref[...]
        do = do_ref[...]
        q_diag = _block_diag(q)
        do_rows = jnp.concatenate(_head_masked(do), axis=0)
        d0, d1 = _row_dots(do, o_ref[...])
        d_ref[0] = d0
        d_ref[1] = d1
        dq_acc[...] = jnp.zeros_like(dq_acc)

        def block(j, valid):
            off = pl.multiple_of(j * TK, TK)
            kd = _block_diag(k_ref[pl.ds(off, TK), :])
            vv = jnp.concatenate(_head_masked(v_ref[pl.ds(off, TK), :]), axis=0)
            s01 = _dot_nt(q, kd)
            dp01 = _dot_nt(do, vv)
            dss, ps = [], []
            for hh in range(2):
                hs = slice(hh * TK, (hh + 1) * TK)
                p = jnp.exp(s01[:, hs] * MLA_SCALE - lse_ref[hh])
                if valid is not None:
                    p = jnp.where(valid, p, 0.0)
                ds = p * (dp01[:, hs] - d_ref[hh])
                dss.append((ds * MLA_SCALE).astype(BF16))
                ps.append(p.astype(BF16))
            dq_acc[...] += _dot(jnp.concatenate(dss, axis=1), kd)
            dk_ref[pl.ds(off, TK), :] += _dot_tn(jnp.concatenate(dss, axis=0), q_diag)
            dv_ref[pl.ds(off, TK), :] += _dot_tn(jnp.concatenate(ps, axis=0), do_rows)

        _sweep(i, _each_tile(block), _tile_masks(False))
        dq_ref[...] = dq_acc[...]

    q_tile = pl.BlockSpec((TQ, 2 * LANES), lambda h, i: (i, h))
    o_tile = pl.BlockSpec((TQ, LANES), lambda h, i: (i, h))
    k_full = pl.BlockSpec((s_len, 2 * LANES), lambda h, i: (0, h))
    v_full = pl.BlockSpec((s_len, LANES), lambda h, i: (0, h))
    return pl.pallas_call(
        body, grid=(4, s_len // TQ),
        in_specs=[q_tile, k_full, v_full, o_tile, o_tile, pl.BlockSpec((2, TQ, LANES), lambda h, i: (h, i, 0))],
        out_specs=[q_tile, k_full, v_full],
        out_shape=[jax.ShapeDtypeStruct((s_len, 1024), F32), jax.ShapeDtypeStruct((s_len, 1024), F32),
                   jax.ShapeDtypeStruct((s_len, 512), F32)],
        scratch_shapes=[pltpu.VMEM((TQ, 2 * LANES), F32), pltpu.VMEM((2, TQ, LANES), F32)],
        name="mla_attn_bwd", compiler_params=_params(("arbitrary", "arbitrary")),
    )(q, k, v, do, o, lse)


def _pick(n, options):
    for t in options:
        if n % t == 0:
            return t
    raise ValueError(n)


def _matmul_tn(name, a, b):
    s_len, m = a.shape
    n = b.shape[1]
    tm = _pick(m, (1024, 1408, 2176, 512))
    tn = _pick(n, (1024, 512, 384, 256))
    tk = _pick(s_len, (512, 256, 128))

    def body(a_ref, b_ref, o_ref):
        @pl.when(pl.program_id(2) == 0)
        def _():
            o_ref[...] = jnp.zeros_like(o_ref)

        o_ref[...] += _dot_tn(a_ref[...], b_ref[...])

    return pl.pallas_call(
        body, grid=(m // tm, n // tn, s_len // tk),
        in_specs=[pl.BlockSpec((tk, tm), lambda i, j, l: (l, i)), pl.BlockSpec((tk, tn), lambda i, j, l: (l, j))],
        out_specs=pl.BlockSpec((tm, tn), lambda i, j, l: (i, j)),
        out_shape=jax.ShapeDtypeStruct((m, n), F32),
        name=name, compiler_params=_params(("arbitrary", "arbitrary", "arbitrary")),
    )(a, b)


def _mesh_pos():
    return lax.axis_index("x"), lax.axis_index("y"), lax.axis_index("c")


def _peer(pos, k):
    x, y, c = pos
    return (1 - x if k & 4 else x, 1 - y if k & 2 else y, 1 - c if k & 1 else c)


def _flat_index(pos):
    return 4 * pos[0] + 2 * pos[1] + pos[2]


def _all_gather(shard):
    rows = shard.shape[0]

    def body(x_ref, out_ref, send_sems, recv_sems, local_sem):
        me = _mesh_pos()
        x, y, c = me
        sibling = (x, y, 1 - c)
        chips = [(1 - x, y), (x, 1 - y), (1 - x, 1 - y)]

        def copy(k, block, to, src=None):
            slot = out_ref.at[_flat_index(block)]
            return pltpu.make_async_remote_copy(
                src_ref=slot if src is None else src, dst_ref=slot,
                send_sem=send_sems.at[k], recv_sem=recv_sems.at[k],
                device_id=to, device_id_type=pl.DeviceIdType.MESH)

        mine = pltpu.make_async_copy(x_ref, out_ref.at[_flat_index(me)], local_sem)
        mine.start()
        first = [copy(0, me, sibling, src=x_ref)]
        first += [copy(1 + j, me, (*chip, c), src=x_ref) for j, chip in enumerate(chips)]
        for cp in first:
            cp.start()
        passed = [copy(4 + j, (*chip, c), sibling) for j, chip in enumerate(chips)]
        for j, chip in enumerate(chips):
            copy(1 + j, (*chip, c), me).wait_recv()
            passed[j].start()
        copy(0, sibling, me).wait_recv()
        for j, chip in enumerate(chips):
            copy(4 + j, (*chip, 1 - c), me).wait_recv()
        for cp in first + passed:
            cp.wait_send()
        mine.wait()

    return pl.pallas_call(
        body, out_shape=jax.ShapeDtypeStruct((N_DEV, rows, LANES), shard.dtype),
        in_specs=[pl.BlockSpec(memory_space=pl.ANY)], out_specs=pl.BlockSpec(memory_space=pl.ANY),
        scratch_shapes=[pltpu.SemaphoreType.DMA((7,)), pltpu.SemaphoreType.DMA((7,)), pltpu.SemaphoreType.DMA],
        name="weights_all_gather",
    )(shard)


def _grad_exchange(big, small):
    def body(big_ref, small_ref, big_out, small_out, bsend, brecv, ssend, srecv, local_sems):
        me = _mesh_pos()
        mine = _flat_index(me)
        loc = [pltpu.make_async_copy(big_ref.at[mine], big_out.at[mine], local_sems.at[0]),
               pltpu.make_async_copy(small_ref, small_out.at[mine], local_sems.at[1])]
        for cp in loc:
            cp.start()

        def copies(k):
            peer = _peer(me, k)
            theirs = _flat_index(peer)
            send = (pltpu.make_async_remote_copy(
                        src_ref=big_ref.at[theirs], dst_ref=big_out.at[mine], send_sem=bsend.at[k - 1],
                        recv_sem=brecv.at[k - 1], device_id=peer, device_id_type=pl.DeviceIdType.MESH),
                    pltpu.make_async_remote_copy(
                        src_ref=small_ref, dst_ref=small_out.at[mine], send_sem=ssend.at[k - 1],
                        recv_sem=srecv.at[k - 1], device_id=peer, device_id_type=pl.DeviceIdType.MESH))
            recv = (pltpu.make_async_remote_copy(
                        src_ref=big_ref.at[mine], dst_ref=big_out.at[theirs], send_sem=bsend.at[k - 1],
                        recv_sem=brecv.at[k - 1], device_id=me, device_id_type=pl.DeviceIdType.MESH),
                    pltpu.make_async_remote_copy(
                        src_ref=small_ref, dst_ref=small_out.at[theirs], send_sem=ssend.at[k - 1],
                        recv_sem=srecv.at[k - 1], device_id=me, device_id_type=pl.DeviceIdType.MESH))
            return send, recv

        plan = [copies(k) for k in range(1, N_DEV)]
        for send, _ in plan:
            for cp in send:
                cp.start()
        for _, recv in plan:
            for cp in recv:
                cp.wait_recv()
        for send, _ in plan:
            for cp in send:
                cp.wait_send()
        for cp in loc:
            cp.wait()

    any_spec = pl.BlockSpec(memory_space=pl.ANY)
    return pl.pallas_call(
        body,
        out_shape=[jax.ShapeDtypeStruct(big.shape, big.dtype),
                   jax.ShapeDtypeStruct((N_DEV,) + small.shape, small.dtype)],
        in_specs=[any_spec, any_spec], out_specs=[any_spec, any_spec],
        scratch_shapes=[pltpu.SemaphoreType.DMA((7,)), pltpu.SemaphoreType.DMA((7,)),
                        pltpu.SemaphoreType.DMA((7,)), pltpu.SemaphoreType.DMA((7,)),
                        pltpu.SemaphoreType.DMA((2,))],
        name="grad_exchange",
    )(big, small)


def _slot_sum(name, slots, tr):
    rows = slots.shape[1]

    def body(s_ref, o_ref):
        acc = s_ref[0].astype(F32)
        for d in range(1, N_DEV):
            acc = acc + s_ref[d].astype(F32)
        o_ref[...] = acc

    return pl.pallas_call(
        body, grid=(rows // tr,),
        in_specs=[pl.BlockSpec((N_DEV, tr, LANES), lambda i: (0, i, 0))],
        out_specs=pl.BlockSpec((tr, LANES), lambda i: (i, 0)),
        out_shape=jax.ShapeDtypeStruct((rows, LANES), F32),
        name=name, compiler_params=_params(("arbitrary",)),
    )(slots)


def _adamw(name, w, g, m, v):
    rows, cols = w.shape
    tr = _pick(rows, (256, 128, 88, 32, 1))
    c1 = 1.0 - ADAM_B1 ** ADAM_STEP
    c2 = 1.0 - ADAM_B2 ** ADAM_STEP

    def body(w_ref, g_ref, m_ref, v_ref, d_ref, nm_ref, nv_ref):
        g = g_ref[...]
        nm = ADAM_B1 * m_ref[...] + (1.0 - ADAM_B1) * g
        nv = ADAM_B2 * v_ref[...] + (1.0 - ADAM_B2) * (g * g)
        nm_ref[...] = nm
        nv_ref[...] = nv
        d_ref[...] = -ADAM_LR * ((nm / c1) / (jnp.sqrt(nv / c2) + ADAM_EPS) + ADAM_WD * w_ref[...])

    spec = pl.BlockSpec((tr, cols), lambda i: (i, 0))
    return pl.pallas_call(
        body, grid=(rows // tr,), in_specs=[spec] * 4, out_specs=[spec] * 3,
        out_shape=[jax.ShapeDtypeStruct((rows, cols), F32)] * 3,
        name=name, compiler_params=_params(("arbitrary",)),
    )(w, g, m, v)


def _pack_shards(shards):
    parts = []
    for name, _, axis in SHARDED:
        w = shards[name].astype(BF16)
        parts.append((w.T if axis == 1 else w).reshape(-1))
    flat = jnp.concatenate(parts)
    flat = jnp.pad(flat, (0, FLAT_ROWS * LANES - flat.shape[0]))
    return flat.reshape(FLAT_ROWS, LANES)


def _unpack_full(gathered):
    flat = gathered.reshape(N_DEV, FLAT_ROWS * LANES)
    out, off = {}, 0
    for name, (r, c), axis in SHARDED:
        rr, cc = (c, r) if axis == 1 else (r, c)
        out[name] = flat[:, off:off + r * c].reshape(N_DEV * rr, cc)
        off += r * c
    return out


def _pack_full_grads(grads):
    parts = [grads[name].reshape(N_DEV, r * c) for name, (r, c), _ in SHARDED]
    flat = jnp.concatenate(parts, axis=1).astype(BF16)
    flat = jnp.pad(flat, ((0, 0), (0, FLAT_ROWS * LANES - flat.shape[1])))
    return flat.reshape(N_DEV, FLAT_ROWS, LANES)


def _rope_tables(positions):
    inv_freq = ROPE_THETA ** (-jnp.arange(0, MLA_ROPE_DIM, 2, dtype=F32) / MLA_ROPE_DIM)
    ang = positions.astype(F32)[:, None] * inv_freq
    z64 = jnp.zeros((positions.shape[0], 64), F32)
    z32 = jnp.zeros((positions.shape[0], 32), F32)
    cos, sin = jnp.cos(ang), jnp.sin(ang)
    return (jnp.concatenate([z64, cos, cos, z32], axis=1), jnp.concatenate([z64, sin, sin, z32], axis=1))


def _row_tile(s_len, want):
    return _pick(s_len, (want, 256, 128))


def kernel(x, positions, norm_mix_pre, norm_mix_post, w_in, b_gate, q_norm, w_uq, kv_norm, w_ukv, w_proj_sb, w_proj_mla, w_out, norm_ffn_pre, norm_ffn_post, w_gate_up, w_down, loss_target, m_norm_mix_pre, m_norm_mix_post, m_w_in, m_b_gate, m_q_norm, m_w_uq, m_kv_norm, m_w_ukv, m_w_proj_sb, m_w_proj_mla, m_w_out, m_norm_ffn_pre, m_norm_ffn_post, m_w_gate_up, m_w_down, v_norm_mix_pre, v_norm_mix_post, v_w_in, v_b_gate, v_q_norm, v_w_uq, v_kv_norm, v_w_ukv, v_w_proj_sb, v_w_proj_mla, v_w_out, v_norm_ffn_pre, v_norm_ffn_post, v_w_gate_up, v_w_down):
    weights = dict(norm_mix_pre=norm_mix_pre, norm_mix_post=norm_mix_post, w_in=w_in, b_gate=b_gate, q_norm=q_norm,
                   w_uq=w_uq, kv_norm=kv_norm, w_ukv=w_ukv, w_proj_sb=w_proj_sb, w_proj_mla=w_proj_mla, w_out=w_out,
                   norm_ffn_pre=norm_ffn_pre, norm_ffn_post=norm_ffn_post, w_gate_up=w_gate_up, w_down=w_down)
    m_in = dict(norm_mix_pre=m_norm_mix_pre, norm_mix_post=m_norm_mix_post, w_in=m_w_in, b_gate=m_b_gate,
                q_norm=m_q_norm, w_uq=m_w_uq, kv_norm=m_kv_norm, w_ukv=m_w_ukv, w_proj_sb=m_w_proj_sb,
                w_proj_mla=m_w_proj_mla, w_out=m_w_out, norm_ffn_pre=m_norm_ffn_pre, norm_ffn_post=m_norm_ffn_post,
                w_gate_up=m_w_gate_up, w_down=m_w_down)
    v_in = dict(norm_mix_pre=v_norm_mix_pre, norm_mix_post=v_norm_mix_post, w_in=v_w_in, b_gate=v_b_gate,
                q_norm=v_q_norm, w_uq=v_w_uq, kv_norm=v_kv_norm, w_ukv=v_w_ukv, w_proj_sb=v_w_proj_sb,
                w_proj_mla=v_w_proj_mla, w_out=v_w_out, norm_ffn_pre=v_norm_ffn_pre, norm_ffn_post=v_norm_ffn_post,
                w_gate_up=v_w_gate_up, w_down=v_w_down)
    order = list(weights)

    xs = x[0]
    target = loss_target[0]
    s_len = xs.shape[0]
    tm = _row_tile(s_len, 256)
    tm_ffn = _row_tile(s_len, 256)

    full = _unpack_full(_all_gather(_pack_shards({name: weights[name][0] for name, _, _ in SHARDED})))
    wt = full["w_in"]
    zr = lambda n: jnp.zeros((n, D_MODEL), BF16)
    w_ext = jnp.concatenate([wt[:2176], zr(64), wt[2176:2208], zr(32), wt[2208:]], axis=0).T
    wa = jnp.pad(full["w_uq"].reshape(N_HEADS, MLA_QK_DIM, MLA_Q_RANK), ((0, 0), (0, 32), (0, 0))
                 ).reshape(N_HEADS * LANES, MLA_Q_RANK).T
    ukv = full["w_ukv"].reshape(N_HEADS, LANES, MLA_KV_RANK)
    wk = jnp.pad(ukv[:, :64], ((0, 0), (0, 64), (0, 0))).reshape(N_HEADS * LANES, MLA_KV_RANK).T
    wv = ukv[:, 64:].reshape(512, MLA_KV_RANK).T
    w_sb, w_mla, w_o, w_gu, w_dn = (full["w_proj_sb"].T, full["w_proj_mla"].T, full["w_out"], full["w_gate_up"].T,
                                    full["w_down"])
    cr, sr = _rope_tables(positions[0])

    qkv, cq, ckv, kr, gl, hb = _in_proj(xs, norm_mix_pre, w_ext, tm)
    q_mla, k_mla, v_mla, cqn, ckvn = _mla_up(cq, ckv, kr, cr, sr, q_norm, kv_norm, wa, wk, wv, tm)
    o_sb = _sb_fwd(qkv)
    o_mla, lse = _mla_fwd(q_mla, k_mla, v_mla)
    x1, y, merged = _mix_out(o_sb, o_mla, gl, xs, b_gate, norm_mix_post, w_sb, w_mla, w_o, tm)
    dx2, f, h2, loss_part = _ffn_fwd(x1, target, norm_ffn_pre, norm_ffn_post, w_gu, w_dn, tm_ffn)
    loss = lax.psum(0.5 / D_MODEL * jnp.sum(loss_part), ("x", "y", "c"))

    dx1, act, dgu, dfb, dg_ffn_pre, dg_ffn_post = _ffn_bwd(dx2, f, x1, norm_ffn_pre, norm_ffn_post, w_gu, w_dn, tm_ffn)
    dyb, dpsb, dpmla, dgl, do_sb, do_mla, dg_mix_post, db_gate = _mix_bwd(
        dx1, y, o_sb, o_mla, gl, b_gate, norm_mix_post, w_sb, w_mla, w_o, tm)
    dq_sb, dk_sb, dv_sb = _sb_bwd(qkv, do_sb, o_sb)
    dq_mla, dk_mla, dv_mla = _mla_bwd(q_mla, k_mla, v_mla, do_mla, o_mla, lse)
    da, dkb, dvb, dlat, dg_q, dg_kv = _mla_up_bwd(dq_mla, dk_mla, dv_mla, cq, ckv, cr, sr, q_norm, kv_norm,
                                                  wa, wk, wv, tm)
    dx, dproj, dg_mix_pre = _in_proj_bwd(xs, dx1, dq_sb, dk_sb, dv_sb, dlat, dgl, norm_mix_pre, w_ext, tm)

    d_ext = _matmul_tn("dw_in", dproj, hb)
    d_wa = _matmul_tn("dw_uq", da, cqn)
    d_wk = _matmul_tn("dw_uk", dkb, ckvn)
    d_wv = _matmul_tn("dw_uv", dvb, ckvn)
    grads = {
        "w_in": jnp.concatenate([d_ext[:2176], d_ext[2240:2272], d_ext[EXT_GL:]], axis=0),
        "w_uq": d_wa.reshape(N_HEADS, LANES, MLA_Q_RANK)[:, :MLA_QK_DIM].reshape(768, MLA_Q_RANK),
        "w_ukv": jnp.concatenate([d_wk.reshape(N_HEADS, LANES, MLA_KV_RANK)[:, :64],
                                  d_wv.reshape(N_HEADS, 64, MLA_KV_RANK)], axis=1).reshape(1024, MLA_KV_RANK),
        "w_proj_sb": _matmul_tn("dw_proj_sb", dpsb, o_sb.astype(BF16)),
        "w_proj_mla": _matmul_tn("dw_proj_mla", dpmla, o_mla.astype(BF16)),
        "w_out": _matmul_tn("dw_out", merged, dyb),
        "w_gate_up": _matmul_tn("dw_gate_up", dgu, h2),
        "w_down": _matmul_tn("dw_down", act, dfb),
    }
    small_parts = dict(norm_mix_pre=dg_mix_pre, norm_mix_post=dg_mix_post, b_gate=db_gate, q_norm=dg_q,
                       kv_norm=dg_kv, norm_ffn_pre=dg_ffn_pre, norm_ffn_post=dg_ffn_post)
    small = jnp.concatenate([small_parts[name].sum(axis=0) for name, _ in SMALL])
    small = jnp.pad(small, (0, SMALL_ROWS * LANES - small.shape[0])).reshape(SMALL_ROWS, LANES)

    big_slots, small_slots = _grad_exchange(_pack_full_grads(grads), small)
    g_flat = _slot_sum("grad_shard_sum", big_slots, 512).reshape(-1)
    s_flat = _slot_sum("grad_small_sum", small_slots, SMALL_ROWS).reshape(-1)
    g_out, off = {}, 0
    for name, (r, c), axis in SHARDED:
        seg = g_flat[off:off + r * c]
        g_out[name] = seg.reshape(c, r).T if axis == 1 else seg.reshape(r, c)
        off += r * c
    off = 0
    for name, n in SMALL:
        g_out[name] = s_flat[off:off + n].reshape(1, n)
        off += n

    deltas, new_m, new_v = {}, {}, {}
    for name in order:
        w2 = weights[name].reshape(g_out[name].shape)
        d, nm, nv = _adamw("adamw_" + name, w2, g_out[name], m_in[name].reshape(w2.shape), v_in[name].reshape(w2.shape))
        shape = weights[name].shape
        deltas[name], new_m[name], new_v[name] = d.reshape(shape), nm.reshape(shape), nv.reshape(shape)
        g_out[name] = g_out[name].reshape(shape)

    return (loss, dx[None], *[g_out[n] for n in order], *[deltas[n] for n in order],
            *[new_m[n] for n in order], *[new_v[n] for n in order])
```

```python
import functools
import math

import jax
import jax.numpy as jnp
from jax import lax
from jax.experimental import pallas as pl
from jax.experimental.pallas import tpu as pltpu

F32 = jnp.float32
BF16 = jnp.bfloat16

D_MODEL = 1024
N_HEADS = 8
SB_WIDTH = 512
MLA_Q_RANK = 384
MLA_KV_RANK = 256
MLA_ROPE_DIM = 32
MLA_QK_DIM = 96
D_FF = 2816
ROPE_THETA = 10000.0
EPS = 1e-6
SB_SCALE = 1.0 / math.sqrt(64.0)
MLA_SCALE = 1.0 / math.sqrt(96.0)
NEG_BIG = -1e30

ADAM_LR = 0.001
ADAM_B1 = 0.9
ADAM_B2 = 0.999
ADAM_EPS = 1e-08
ADAM_WD = 0.01
ADAM_STEP = 10

N_DEV = 8
LANES = 128
TQ = 512
TK = 128
DIAG_TILES = TQ // TK
FWD_UNROLL = 4
BWD_UNROLL = 2
VMEM_LIMIT = 56 << 20

EXT_QKV = 0
EXT_CQ = 1536
EXT_CKV = 1920
EXT_KR = 2176
EXT_GL = 2304
EXT_N = 4352

SHARDED = (
    ("w_in", (1024, 532), 1), ("w_uq", (384, 96), 1), ("w_ukv", (256, 128), 1),
    ("w_proj_sb", (512, 128), 1), ("w_proj_mla", (512, 128), 1), ("w_out", (128, 1024), 0),
    ("w_gate_up", (1024, 704), 1), ("w_down", (352, 1024), 0),
)
FLAT_ROWS = 15360
SMALL = (("norm_mix_pre", 1024), ("norm_mix_post", 1024), ("b_gate", 2048), ("q_norm", 384),
         ("kv_norm", 256), ("norm_ffn_pre", 1024), ("norm_ffn_post", 1024))
SMALL_ROWS = 56


def _dot(a, b):
    return jnp.dot(a, b, preferred_element_type=F32)


def _dot_nt(a, b):
    return lax.dot_general(a, b, (((1,), (1,)), ((), ())), preferred_element_type=F32)


def _dot_tn(a, b):
    return lax.dot_general(a, b, (((0,), (0,)), ((), ())), preferred_element_type=F32)


def _rms(x):
    r = lax.rsqrt(jnp.mean(x * x, axis=-1, keepdims=True) + EPS)
    return x * r, r


def _rms_bwd(dn, n, r):
    return r * (dn - n * jnp.mean(dn * n, axis=-1, keepdims=True))


def _colsum8(x):
    return jnp.sum(x.reshape(x.shape[0] // 8, 8, x.shape[1]), axis=0)


def _split(x):
    hi = x.astype(BF16)
    return hi, (x - hi.astype(F32)).astype(BF16)


def _rot(x):
    lane = lax.broadcasted_iota(jnp.int32, x.shape, 1)
    up = pltpu.roll(x, 112, 1)
    down = pltpu.roll(x, 16, 1)
    return jnp.where((lane >= 64) & (lane < 80), -up, jnp.where((lane >= 80) & (lane < 96), down, 0.0))


def _params(sem):
    return pltpu.CompilerParams(dimension_semantics=sem, vmem_limit_bytes=VMEM_LIMIT)


def _rows_call(name, body, n_rows, tm, row_ins, const_ins, row_outs, acc_outs):
    in_specs = [pl.BlockSpec((tm, a.shape[1]), lambda i: (i, 0)) for a in row_ins]
    in_specs += [pl.BlockSpec(a.shape, lambda i: (0, 0), pipeline_mode=pl.Buffered(1)) for a in const_ins]
    out_specs = [pl.BlockSpec((tm, n), lambda i: (i, 0)) for n, _ in row_outs]
    out_specs += [pl.BlockSpec(s, lambda i: (0, 0)) for s in acc_outs]
    out_shape = [jax.ShapeDtypeStruct((n_rows, n), dt) for n, dt in row_outs]
    out_shape += [jax.ShapeDtypeStruct(s, F32) for s in acc_outs]
    return pl.pallas_call(
        body, grid=(n_rows // tm,), in_specs=in_specs, out_specs=out_specs, out_shape=out_shape,
        name=name, compiler_params=_params(("arbitrary",)),
    )(*row_ins, *const_ins)


def _in_proj(x, g_pre, w_ext, tm):
    def body(x_ref, g_ref, w_ref, qkv_ref, cq_ref, ckv_ref, kr_ref, gl_ref, h_ref):
        n, _ = _rms(x_ref[...])
        hb = (n * g_ref[...]).astype(BF16)
        h_ref[...] = hb
        for c in range(0, 1536, 512):
            qkv_ref[:, c:c + 512] = _dot(hb, w_ref[:, c:c + 512]).astype(BF16)
        cq_ref[...] = _dot(hb, w_ref[:, EXT_CQ:EXT_CKV])
        ckv_ref[...] = _dot(hb, w_ref[:, EXT_CKV:EXT_KR])
        kr_ref[...] = _dot(hb, w_ref[:, EXT_KR:EXT_GL])
        for c in range(0, 2048, 512):
            gl_ref[:, c:c + 512] = _dot(hb, w_ref[:, EXT_GL + c:EXT_GL + c + 512])

    return _rows_call("in_proj", body, x.shape[0], tm, [x], [g_pre, w_ext],
                      [(1536, BF16), (384, F32), (256, F32), (128, F32), (2048, F32), (1024, BF16)], [])


def _mla_up(cq, ckv, kr, cr, sr, q_norm, kv_norm, wa, wk, wv, tm):
    def body(cq_ref, ckv_ref, kr_ref, cr_ref, sr_ref, qn_ref, kvn_ref, wa_ref, wk_ref, wv_ref,
             q_ref, k_ref, v_ref, cqn_ref, ckvn_ref):
        nq, _ = _rms(cq_ref[...])
        cqn = (nq * qn_ref[...]).astype(BF16)
        cqn_ref[...] = cqn
        nk, _ = _rms(ckv_ref[...])
        ckvn = (nk * kvn_ref[...]).astype(BF16)
        ckvn_ref[...] = ckvn
        cr = cr_ref[...]
        sr = sr_ref[...]
        lane = lax.broadcasted_iota(jnp.int32, cr.shape, 1)
        cm = cr + (lane < 64).astype(F32)
        kr = kr_ref[...]
        krp = kr * cr + _rot(kr) * sr
        for h in range(N_HEADS):
            hs = slice(h * LANES, (h + 1) * LANES)
            a = _dot(cqn, wa_ref[:, hs])
            q_ref[:, hs] = (a * cm + _rot(a) * sr).astype(BF16)
            k_ref[:, hs] = (_dot(ckvn, wk_ref[:, hs]) + krp).astype(BF16)
        v_ref[...] = _dot(ckvn, wv_ref[...]).astype(BF16)

    return _rows_call("mla_up", body, cq.shape[0], tm, [cq, ckv, kr, cr, sr], [q_norm, kv_norm, wa, wk, wv],
                      [(1024, BF16), (1024, BF16), (512, BF16), (384, BF16), (256, BF16)], [])


def _mix_out(o_sb, o_mla, gl, x, b_gate, g_post, w_sb, w_mla, w_out, tm):
    def body(osb_ref, omla_ref, gl_ref, x_ref, b_ref, gp_ref, wsb_ref, wmla_ref, wout_ref,
             x1_ref, y_ref, mb_ref):
        psb = _dot(osb_ref[...].astype(BF16), wsb_ref[...])
        pmla = _dot(omla_ref[...].astype(BF16), wmla_ref[...])
        gates = jax.nn.sigmoid(gl_ref[...] + b_ref[...])
        mb = (gates[:, :D_MODEL] * psb + gates[:, D_MODEL:] * pmla).astype(BF16)
        mb_ref[...] = mb
        y = _dot(mb, wout_ref[...])
        y_ref[...] = y
        n, _ = _rms(y)
        x1_ref[...] = x_ref[...] + n * gp_ref[...]

    return _rows_call("mix_out", body, x.shape[0], tm, [o_sb, o_mla, gl, x], [b_gate, g_post, w_sb, w_mla, w_out],
                      [(1024, F32), (1024, F32), (1024, BF16)], [])


FF_CHUNK = 1408


def _ffn_fwd(x1, target, g_pre, g_post, w_gu, w_down, tm):
    def body(x1_ref, t_ref, gpre_ref, gpost_ref, wgu_ref, wd_ref, dx2_ref, f_ref, h2_ref, loss_ref):
        x1 = x1_ref[...]
        n, _ = _rms(x1)
        h2 = (n * gpre_ref[...]).astype(BF16)
        h2_ref[...] = h2
        f = jnp.zeros((tm, D_MODEL), F32)
        for c in range(0, D_FF, FF_CHUNK):
            g = _dot(h2, wgu_ref[:, c:c + FF_CHUNK])
            u = _dot(h2, wgu_ref[:, D_FF + c:D_FF + c + FF_CHUNK])
            act = (g * jax.nn.sigmoid(g) * u).astype(BF16)
            f = f + _dot(act, wd_ref[c:c + FF_CHUNK, :])
        f_ref[...] = f
        nf, _ = _rms(f)
        err = x1 + nf * gpost_ref[...] - t_ref[...]
        dx2_ref[...] = err * (1.0 / D_MODEL)
        e8 = _colsum8(err * err)
        part = e8[:, 0:LANES]
        for c in range(LANES, D_MODEL, LANES):
            part = part + e8[:, c:c + LANES]

        @pl.when(pl.program_id(0) == 0)
        def _():
            loss_ref[...] = jnp.zeros_like(loss_ref)

        loss_ref[...] += part

    return _rows_call("ffn_fwd", body, x1.shape[0], tm, [x1, target], [g_pre, g_post, w_gu, w_down],
                      [(1024, F32), (1024, F32), (1024, BF16)], [(8, LANES)])


def _ffn_bwd(dx2, f, x1, g_pre, g_post, w_gu, w_down, tm):
    def body(dx2_ref, f_ref, x1_ref, gpre_ref, gpost_ref, wgu_ref, wd_ref,
             dx1_ref, act_ref, dgu_ref, dfb_ref, dgpre_ref, dgpost_ref):
        @pl.when(pl.program_id(0) == 0)
        def _():
            dgpre_ref[...] = jnp.zeros_like(dgpre_ref)
            dgpost_ref[...] = jnp.zeros_like(dgpost_ref)

        dx2 = dx2_ref[...]
        nf, rf = _rms(f_ref[...])
        dgpost_ref[...] += _colsum8(dx2 * nf)
        dfb = _rms_bwd(dx2 * gpost_ref[...], nf, rf).astype(BF16)
        dfb_ref[...] = dfb
        x1 = x1_ref[...]
        n1, r1 = _rms(x1)
        h2 = (n1 * gpre_ref[...]).astype(BF16)
        dh2 = jnp.zeros((tm, D_MODEL), F32)
        for c in range(0, D_FF, FF_CHUNK):
            cs, us = slice(c, c + FF_CHUNK), slice(D_FF + c, D_FF + c + FF_CHUNK)
            g = _dot(h2, wgu_ref[:, cs])
            u = _dot(h2, wgu_ref[:, us])
            sg = jax.nn.sigmoid(g)
            si = g * sg
            act_ref[:, cs] = (si * u).astype(BF16)
            dact = _dot_nt(dfb, wd_ref[cs, :])
            dg = (dact * u * (sg * (1.0 + g * (1.0 - sg)))).astype(BF16)
            du = (dact * si).astype(BF16)
            dgu_ref[:, cs] = dg
            dgu_ref[:, us] = du
            dh2 = dh2 + _dot_nt(dg, wgu_ref[:, cs]) + _dot_nt(du, wgu_ref[:, us])
        dgpre_ref[...] += _colsum8(dh2 * n1)
        dx1_ref[...] = dx2 + _rms_bwd(dh2 * gpre_ref[...], n1, r1)

    return _rows_call("ffn_bwd", body, dx2.shape[0], tm, [dx2, f, x1], [g_pre, g_post, w_gu, w_down],
                      [(1024, F32), (D_FF, BF16), (2 * D_FF, BF16), (1024, BF16)], [(8, 1024), (8, 1024)])


def _mix_bwd(dx1, y, o_sb, o_mla, gl, b_gate, g_post, w_sb, w_mla, w_out, tm):
    def body(dx1_ref, y_ref, osb_ref, omla_ref, gl_ref, b_ref, gp_ref, wsb_ref, wmla_ref, wout_ref,
             dyb_ref, dpsb_ref, dpmla_ref, dgl_ref, dosb_ref, domla_ref, dgpost_ref, dbg_ref):
        @pl.when(pl.program_id(0) == 0)
        def _():
            dgpost_ref[...] = jnp.zeros_like(dgpost_ref)
            dbg_ref[...] = jnp.zeros_like(dbg_ref)

        dx1 = dx1_ref[...]
        ny, ry = _rms(y_ref[...])
        dgpost_ref[...] += _colsum8(dx1 * ny)
        dyb = _rms_bwd(dx1 * gp_ref[...], ny, ry).astype(BF16)
        dyb_ref[...] = dyb
        dm = _dot_nt(dyb, wout_ref[...])
        psb = _dot(osb_ref[...].astype(BF16), wsb_ref[...])
        pmla = _dot(omla_ref[...].astype(BF16), wmla_ref[...])
        gates = jax.nn.sigmoid(gl_ref[...] + b_ref[...])
        g0, g1 = gates[:, :D_MODEL], gates[:, D_MODEL:]
        dpsb = (dm * g0).astype(BF16)
        dpmla = (dm * g1).astype(BF16)
        dpsb_ref[...] = dpsb
        dpmla_ref[...] = dpmla
        dgl0 = dm * psb * g0 * (1.0 - g0)
        dgl1 = dm * pmla * g1 * (1.0 - g1)
        dgl_ref[:, :D_MODEL] = dgl0.astype(BF16)
        dgl_ref[:, D_MODEL:] = dgl1.astype(BF16)
        dbg_ref[:, :D_MODEL] += _colsum8(dgl0)
        dbg_ref[:, D_MODEL:] += _colsum8(dgl1)
        dosb_ref[...] = _dot_nt(dpsb, wsb_ref[...]).astype(BF16)
        domla_ref[...] = _dot_nt(dpmla, wmla_ref[...]).astype(BF16)

    return _rows_call("mix_bwd", body, dx1.shape[0], tm, [dx1, y, o_sb, o_mla, gl],
                      [b_gate, g_post, w_sb, w_mla, w_out],
                      [(1024, BF16), (1024, BF16), (1024, BF16), (2048, BF16), (512, BF16), (512, BF16)],
                      [(8, 1024), (8, 2048)])


def _mla_up_bwd(dq, dk, dv, cq, ckv, cr, sr, q_norm, kv_norm, wa, wk, wv, tm):
    def body(dq_ref, dk_ref, dv_ref, cq_ref, ckv_ref, cr_ref, sr_ref, qn_ref, kvn_ref, wa_ref, wk_ref, wv_ref,
             da_ref, dkb_ref, dvb_ref, dlat_ref, dqn_ref, dkvn_ref):
        @pl.when(pl.program_id(0) == 0)
        def _():
            dqn_ref[...] = jnp.zeros_like(dqn_ref)
            dkvn_ref[...] = jnp.zeros_like(dkvn_ref)

        cr = cr_ref[...]
        sr = sr_ref[...]
        lane = lax.broadcasted_iota(jnp.int32, cr.shape, 1)
        cm = cr + (lane < 64).astype(F32)
        nq, rq = _rms(cq_ref[...])
        nk, rk = _rms(ckv_ref[...])
        dcqn = jnp.zeros((tm, MLA_Q_RANK), F32)
        dckvn = jnp.zeros((tm, MLA_KV_RANK), F32)
        dkrp = jnp.zeros((tm, LANES), F32)
        for h in range(N_HEADS):
            hs = slice(h * LANES, (h + 1) * LANES)
            dqh = dq_ref[:, hs]
            da = (dqh * cm - _rot(dqh * sr)).astype(BF16)
            da_ref[:, hs] = da
            dcqn = dcqn + _dot_nt(da, wa_ref[:, hs])
            dkh = dk_ref[:, hs]
            dkb = dkh.astype(BF16)
            dkb_ref[:, hs] = dkb
            dckvn = dckvn + _dot_nt(dkb, wk_ref[:, hs])
            dkrp = dkrp + dkh
        dvb = dv_ref[...].astype(BF16)
        dvb_ref[...] = dvb
        dckvn = dckvn + _dot_nt(dvb, wv_ref[...])
        dkr = dkrp * cr - _rot(dkrp * sr)
        dqn_ref[...] += _colsum8(dcqn * nq)
        dkvn_ref[...] += _colsum8(dckvn * nk)
        dlat_ref[:, 0:384] = _rms_bwd(dcqn * qn_ref[...], nq, rq).astype(BF16)
        dlat_ref[:, 384:640] = _rms_bwd(dckvn * kvn_ref[...], nk, rk).astype(BF16)
        dlat_ref[:, 640:768] = dkr.astype(BF16)

    return _rows_call("mla_up_bwd", body, dq.shape[0], tm, [dq, dk, dv, cq, ckv, cr, sr],
                      [q_norm, kv_norm, wa, wk, wv],
                      [(1024, BF16), (1024, BF16), (512, BF16), (768, BF16)], [(8, 384), (8, 256)])


def _in_proj_bwd(x, dx1, dq_sb, dk_sb, dv_sb, dlat, dgl, g_pre, w_ext, tm):
    def body(x_ref, dx1_ref, dq_ref, dk_ref, dv_ref, dlat_ref, dgl_ref, g_ref, w_ref,
             dx_ref, dproj_ref, dg_ref):
        @pl.when(pl.program_id(0) == 0)
        def _():
            dg_ref[...] = jnp.zeros_like(dg_ref)

        dproj_ref[:, 0:512] = dq_ref[...].astype(BF16)
        dproj_ref[:, 512:1024] = dk_ref[...].astype(BF16)
        dproj_ref[:, 1024:1536] = dv_ref[...].astype(BF16)
        dproj_ref[:, EXT_CQ:EXT_GL] = dlat_ref[...]
        dproj_ref[:, EXT_GL:EXT_N] = dgl_ref[...]
        dh = jnp.zeros((tm, D_MODEL), F32)
        for c in range(0, EXT_N, 2176):
            dh = dh + _dot_nt(dproj_ref[:, c:c + 2176], w_ref[:, c:c + 2176])
        n, r = _rms(x_ref[...])
        dg_ref[...] += _colsum8(dh * n)
        dx_ref[...] = dx1_ref[...] + _rms_bwd(dh * g_ref[...], n, r)

    return _rows_call("in_proj_bwd", body, x.shape[0], tm, [x, dx1, dq_sb, dk_sb, dv_sb, dlat, dgl],
                      [g_pre, w_ext], [(1024, F32), (EXT_N, BF16)], [(8, 1024)])


def _head_masked(x):
    lane = lax.broadcasted_iota(jnp.int32, x.shape, 1)
    zero = jnp.zeros_like(x)
    return jnp.where(lane < 64, x, zero), jnp.where(lane >= 64, x, zero)


def _cum_weights():
    row = lax.broadcasted_iota(jnp.int32, (TK, TK), 0)
    col = lax.broadcasted_iota(jnp.int32, (TK, TK), 1)
    half = jnp.concatenate([(row > col).astype(BF16), jnp.ones((TK, TK), BF16)], axis=1)
    return jnp.concatenate([half, half], axis=0)


def _split_cat(x):
    hi, lo = _split(x)
    return jnp.concatenate([hi, lo], axis=1)


def _sweep(i, tiles, unroll):
    col = lax.broadcasted_iota(jnp.int32, (TQ, TK), 1)

    def diag(t, _):
        top = DIAG_TILES - 1 - t * unroll
        tiles(i * DIAG_TILES + top, [(top - u) * TK + col for u in range(unroll)])
        return 0

    lax.fori_loop(0, DIAG_TILES // unroll, diag, 0)

    def full(g, _):
        tiles(i * DIAG_TILES - 1 - g * unroll, [None] * unroll)
        return 0

    lax.fori_loop(0, (i * DIAG_TILES) // unroll, full, 0)


def _causal(key, strict):
    if key is None:
        return None
    row = lax.broadcasted_iota(jnp.int32, (TQ, TK), 0)
    return key < row if strict else key <= row


def _each_tile(block, strict):
    def tiles(top, keys):
        for u, key in enumerate(keys):
            block(top - u, _causal(key, strict))
    return tiles


def _sb_logs(z, valid, w_cum):
    soft = jnp.log(1.0 + jnp.exp(-jnp.abs(z)))
    lsm_all = -jnp.maximum(z, 0.0) - soft
    lsm = lsm_all if valid is None else jnp.where(valid, lsm_all, 0.0)
    return lsm_all, _dot(_split_cat(lsm), w_cum)


def _sb_weights(z, lsm_all, cs, r, valid):
    a = jnp.exp(z + lsm_all + cs[:, :TK] + r)
    if valid is not None:
        a = jnp.where(valid, a, 0.0)
    return a, r + cs[:, TK:]


def _block_diag(x):
    x0, x1 = x[:, :LANES], x[:, LANES:]
    zero = jnp.zeros_like(x0)
    return jnp.concatenate([jnp.concatenate([x0, zero], axis=1), jnp.concatenate([zero, x1], axis=1)], axis=0)


def _attn_fwd(qkv, q, k, v):
    s_len = qkv.shape[0]

    def body(qs_ref, ks_ref, vs_ref, qm_ref, km_ref, vm_ref, osb_ref, omla_ref, lse_ref,
             sacc_ref, r_ref, macc_ref, m_ref):
        i = pl.program_id(1)
        lane = lax.broadcasted_iota(jnp.int32, (TQ, LANES), 1)
        r2 = lax.broadcasted_iota(jnp.int32, (2 * TK, LANES), 0)
        c2 = lax.broadcasted_iota(jnp.int32, (2 * TK, LANES), 1)
        head_ones = ((r2 < TK) == (c2 < 64)).astype(BF16)
        w_cum = _cum_weights()
        qs = qs_ref[...] * SB_SCALE
        qm = qm_ref[...]
        sacc_ref[...] = jnp.zeros_like(sacc_ref)
        r_ref[...] = jnp.zeros_like(r_ref)
        macc_ref[...] = jnp.zeros_like(macc_ref)
        m_ref[...] = jnp.full(m_ref.shape, NEG_BIG, F32)

        def sb_scores(top, valids):
            out = []
            for u, valid in enumerate(valids):
                off = pl.multiple_of((top - u) * TK, TK)
                z01 = _dot_nt(qs, jnp.concatenate(_head_masked(ks_ref[pl.ds(off, TK), :]), axis=0))
                out.append([(z,) + _sb_logs(z, valid, w_cum) for z in (z01[:, :TK], z01[:, TK:])])
            return out

        def sb_accumulate(top, valids, scores):
            parts = [[None, None] for _ in valids]
            for hh in range(2):
                r = r_ref[hh]
                for u, valid in enumerate(valids):
                    a, r = _sb_weights(*scores[u][hh], r, valid)
                    parts[u][hh] = _split_cat(a)
                r_ref[hh] = r
            vs = []
            for u in range(len(valids)):
                off = pl.multiple_of((top - u) * TK, TK)
                v0, v1 = _head_masked(vs_ref[pl.ds(off, TK), :])
                vs += [v0, v0, v1, v1]
            sacc_ref[...] += _dot(jnp.concatenate([p for pair in parts for p in pair], axis=1),
                                  jnp.concatenate(vs, axis=0))

        def mla_scores(top, valids):
            s01s = []
            for u in range(len(valids)):
                off = pl.multiple_of((top - u) * TK, TK)
                s01s.append(_dot_nt(qm, _block_diag(km_ref[pl.ds(off, TK), :])))
            heads = []
            for hh in range(2):
                ss = []
                for u, valid in enumerate(valids):
                    s = s01s[u][:, hh * TK:(hh + 1) * TK] * MLA_SCALE
                    ss.append(s if valid is None else jnp.where(valid, s, NEG_BIG))
                m_old = m_ref[hh]
                m = jnp.maximum(m_old, jnp.max(functools.reduce(jnp.maximum, ss), axis=1, keepdims=True))
                m_ref[hh] = m
                heads.append((ss, m, jnp.exp(m_old - m)))
            return heads

        def mla_accumulate(top, heads):
            vs = []
            for u in range(len(heads[0][0])):
                off = pl.multiple_of((top - u) * TK, TK)
                vv = jnp.concatenate(_head_masked(vm_ref[pl.ds(off, TK), :]), axis=0)
                vs.append(jnp.concatenate([vv, head_ones], axis=1))
            ps = [[jnp.exp(s - m).astype(BF16) for s in ss] for ss, m, _ in heads]
            scale = jnp.where(lane < 64, heads[0][2], heads[1][2])
            p_all = jnp.concatenate([ps[hh][u] for u in range(len(vs)) for hh in range(2)], axis=1)
            macc_ref[...] = (macc_ref[...] * jnp.concatenate([scale, scale], axis=1)
                             + _dot(p_all, jnp.concatenate(vs, axis=0)))

        def tiles(top, keys):
            strict = [_causal(key, True) for key in keys]
            heads = mla_scores(top, [_causal(key, False) for key in keys])
            scores = sb_scores(top, strict)
            mla_accumulate(top, heads)
            sb_accumulate(top, strict, scores)

        _sweep(i, tiles, FWD_UNROLL)
        osb_ref[...] = sacc_ref[...]
        acc = macc_ref[...]
        den = acc[:, LANES:]
        omla_ref[...] = acc[:, :LANES] / den
        for hh, mask in enumerate((lane < 64, lane >= 64)):
            l = jnp.max(jnp.where(mask, den, 0.0), axis=1, keepdims=True)
            lse_ref[hh] = jnp.broadcast_to(m_ref[hh] + jnp.log(l), (TQ, LANES))

    tile = pl.BlockSpec((TQ, LANES), lambda h, i: (i, h))
    return pl.pallas_call(
        body, grid=(4, s_len // TQ),
        in_specs=[tile,
                  pl.BlockSpec((s_len, LANES), lambda h, i: (0, 4 + h)),
                  pl.BlockSpec((s_len, LANES), lambda h, i: (0, 8 + h)),
                  pl.BlockSpec((TQ, 2 * LANES), lambda h, i: (i, h)),
                  pl.BlockSpec((s_len, 2 * LANES), lambda h, i: (0, h)),
                  pl.BlockSpec((s_len, LANES), lambda h, i: (0, h))],
        out_specs=[tile, tile, pl.BlockSpec((2, TQ, LANES), lambda h, i: (h, i, 0))],
        out_shape=[jax.ShapeDtypeStruct((s_len, SB_WIDTH), F32), jax.ShapeDtypeStruct((s_len, 512), F32),
                   jax.ShapeDtypeStruct((N_HEADS, s_len, LANES), F32)],
        scratch_shapes=[pltpu.VMEM((TQ, LANES), F32), pltpu.VMEM((2, TQ, LANES), F32),
                        pltpu.VMEM((TQ, 2 * LANES), F32), pltpu.VMEM((2, TQ, 1), F32)],
        name="attn_fwd", compiler_params=_params(("arbitrary", "arbitrary")),
    )(qkv, qkv, qkv, q, k, v)


def _row_dots(do, o):
    prod = do.astype(F32) * o
    p0, p1 = _head_masked(prod)
    return tuple(jnp.broadcast_to(jnp.sum(p, axis=1, keepdims=True), prod.shape) for p in (p0, p1))


def _sb_bwd(qkv, do, o):
    s_len = qkv.shape[0]

    def body(q_ref, k_ref, v_ref, do_ref, o_ref, dq_ref, dk_ref, dv_ref, dq_acc, r_ref, g_ref, d_ref):
        i = pl.program_id(1)

        @pl.when(i == 0)
        def _():
            dk_ref[...] = jnp.zeros_like(dk_ref)
            dv_ref[...] = jnp.zeros_like(dv_ref)

        w_cum = _cum_weights()
        q = q_ref[...] * SB_SCALE
        do = do_ref[...]
        q_rows = jnp.concatenate(_head_masked(q), axis=0)
        do_rows = jnp.concatenate(_head_masked(do), axis=0)
        d0, d1 = _row_dots(do, o_ref[...])
        d_ref[0] = d0
        d_ref[1] = d1
        dq_acc[...] = jnp.zeros_like(dq_acc)
        r_ref[...] = jnp.zeros_like(r_ref)
        g_ref[...] = jnp.zeros_like(g_ref)

        def tiles(top, keys):
            n = len(keys)
            valids = [_causal(key, True) for key in keys]
            offs = [pl.multiple_of((top - u) * TK, TK) for u in range(n)]
            kks = [jnp.concatenate(_head_masked(k_ref[pl.ds(off, TK), :]), axis=0) for off in offs]
            vvs = [jnp.concatenate(_head_masked(v_ref[pl.ds(off, TK), :]), axis=0) for off in offs]
            z01s = [_dot_nt(q, kk) for kk in kks]
            da01s = [_dot_nt(do, vv) for vv in vvs]
            heads = (slice(0, TK), slice(TK, 2 * TK))
            logs = [[_sb_logs(z01s[u][:, hs], valids[u], w_cum) for hs in heads] for u in range(n)]
            dzs = [[None, None] for _ in range(n)]
            avs = [[None, None] for _ in range(n)]
            for hh, hs in enumerate(heads):
                r = r_ref[hh]
                weights = []
                for u in range(n):
                    lsm_all, cs = logs[u][hh]
                    a, r = _sb_weights(z01s[u][:, hs], lsm_all, cs, r, valids[u])
                    g = a * da01s[u][:, hs]
                    weights.append((a, g, _dot(_split_cat(g), w_cum)))
                r_ref[hh] = r
                carried = g_ref[hh]
                for u in range(n):
                    a, g, gs = weights[u]
                    upto = d_ref[hh] - (gs[:, :TK] + carried)
                    carried = carried + gs[:, TK:]
                    dz = g - jnp.exp(z01s[u][:, hs] + logs[u][hh][0]) * upto
                    if valids[u] is not None:
                        dz = jnp.where(valids[u], dz, 0.0)
                    dzs[u][hh] = dz.astype(BF16)
                    avs[u][hh] = a.astype(BF16)
                g_ref[hh] = carried
            dq_acc[...] += _dot(jnp.concatenate([dz for pair in dzs for dz in pair], axis=1),
                                jnp.concatenate(kks, axis=0))
            for u in range(n):
                dk_ref[pl.ds(offs[u], TK), :] += _dot_tn(jnp.concatenate(dzs[u], axis=0), q_rows)
                dv_ref[pl.ds(offs[u], TK), :] += _dot_tn(jnp.concatenate(avs[u], axis=0), do_rows)

        _sweep(i, tiles, BWD_UNROLL)
        dq_ref[...] = dq_acc[...] * SB_SCALE

    tile = pl.BlockSpec((TQ, LANES), lambda h, i: (i, h))
    full = pl.BlockSpec((s_len, LANES), lambda h, i: (0, h))
    k_full = pl.BlockSpec((s_len, LANES), lambda h, i: (0, 4 + h))
    v_full = pl.BlockSpec((s_len, LANES), lambda h, i: (0, 8 + h))
    return pl.pallas_call(
        body, grid=(4, s_len // TQ),
        in_specs=[tile, k_full, v_full, tile, tile],
        out_specs=[tile, full, full],
        out_shape=[jax.ShapeDtypeStruct((s_len, SB_WIDTH), F32)] * 3,
        scratch_shapes=[pltpu.VMEM((TQ, LANES), F32), pltpu.VMEM((2, TQ, LANES), F32),
                        pltpu.VMEM((2, TQ, LANES), F32), pltpu.VMEM((2, TQ, LANES), F32)],
        name="sb_attn_bwd", compiler_params=_params(("arbitrary", "arbitrary")),
    )(qkv, qkv, qkv, do, o)


def _mla_bwd(q, k, v, do, o, lse):
    s_len = q.shape[0]

    def body(q_ref, k_ref, v_ref, do_ref, o_ref, lse_ref, dq_ref, dk_ref, dv_ref, dq_acc, d_ref):
        i = pl.program_id(1)

        @pl.when(i == 0)
        def _():
            dk_ref[...] = jnp.zeros_like(dk_ref)
            dv_ref[...] = jnp.zeros_like(dv_ref)

        q = q_ref[...]
        do = do_ref[...]
        q_diag = _block_diag(q)
        do_rows = jnp.concatenate(_head_masked(do), axis=0)
        d0, d1 = _row_dots(do, o_ref[...])
        d_ref[0] = d0
        d_ref[1] = d1
        dq_acc[...] = jnp.zeros_like(dq_acc)

        def block(j, valid):
            off = pl.multiple_of(j * TK, TK)
            kd = _block_diag(k_ref[pl.ds(off, TK), :])
            vv = jnp.concatenate(_head_masked(v_ref[pl.ds(off, TK), :]), axis=0)
            s01 = _dot_nt(q, kd)
            dp01 = _dot_nt(do, vv)
            dss, ps = [], []
            for hh in range(2):
                hs = slice(hh * TK, (hh + 1) * TK)
                p = jnp.exp(s01[:, hs] * MLA_SCALE - lse_ref[hh])
                if valid is not None:
                    p = jnp.where(valid, p, 0.0)
                ds = p * (dp01[:, hs] - d_ref[hh])
                dss.append((ds * MLA_SCALE).astype(BF16))
                ps.append(p.astype(BF16))
            dq_acc[...] += _dot(jnp.concatenate(dss, axis=1), kd)
            dk_ref[pl.ds(off, TK), :] += _dot_tn(jnp.concatenate(dss, axis=0), q_diag)
            dv_ref[pl.ds(off, TK), :] += _dot_tn(jnp.concatenate(ps, axis=0), do_rows)

        _sweep(i, _each_tile(block, False), BWD_UNROLL)
        dq_ref[...] = dq_acc[...]

    q_tile = pl.BlockSpec((TQ, 2 * LANES), lambda h, i: (i, h))
    o_tile = pl.BlockSpec((TQ, LANES), lambda h, i: (i, h))
    k_full = pl.BlockSpec((s_len, 2 * LANES), lambda h, i: (0, h))
    v_full = pl.BlockSpec((s_len, LANES), lambda h, i: (0, h))
    return pl.pallas_call(
        body, grid=(4, s_len // TQ),
        in_specs=[q_tile, k_full, v_full, o_tile, o_tile, pl.BlockSpec((2, TQ, LANES), lambda h, i: (h, i, 0))],
        out_specs=[q_tile, k_full, v_full],
        out_shape=[jax.ShapeDtypeStruct((s_len, 1024), F32), jax.ShapeDtypeStruct((s_len, 1024), F32),
                   jax.ShapeDtypeStruct((s_len, 512), F32)],
        scratch_shapes=[pltpu.VMEM((TQ, 2 * LANES), F32), pltpu.VMEM((2, TQ, LANES), F32)],
        name="mla_attn_bwd", compiler_params=_params(("arbitrary", "arbitrary")),
    )(q, k, v, do, o, lse)


def _pick(n, options):
    for t in options:
        if n % t == 0:
            return t
    raise ValueError(n)


def _matmul_tn(name, a, b):
    s_len, m = a.shape
    n = b.shape[1]
    tm = _pick(m, (1024, 1408, 2176, 512))
    tn = _pick(n, (1024, 512, 384, 256))
    tk = _pick(s_len, (512, 256, 128))

    def body(a_ref, b_ref, o_ref):
        @pl.when(pl.program_id(2) == 0)
        def _():
            o_ref[...] = jnp.zeros_like(o_ref)

        o_ref[...] += _dot_tn(a_ref[...], b_ref[...])

    return pl.pallas_call(
        body, grid=(m // tm, n // tn, s_len // tk),
        in_specs=[pl.BlockSpec((tk, tm), lambda i, j, l: (l, i)), pl.BlockSpec((tk, tn), lambda i, j, l: (l, j))],
        out_specs=pl.BlockSpec((tm, tn), lambda i, j, l: (i, j)),
        out_shape=jax.ShapeDtypeStruct((m, n), F32),
        name=name, compiler_params=_params(("arbitrary", "arbitrary", "arbitrary")),
    )(a, b)


def _mesh_pos():
    return lax.axis_index("x"), lax.axis_index("y"), lax.axis_index("c")


def _peer(pos, k):
    x, y, c = pos
    return (1 - x if k & 4 else x, 1 - y if k & 2 else y, 1 - c if k & 1 else c)


def _flat_index(pos):
    return 4 * pos[0] + 2 * pos[1] + pos[2]


def _all_gather(shard):
    rows = shard.shape[0]

    def body(x_ref, out_ref, send_sems, recv_sems, local_sem):
        me = _mesh_pos()
        x, y, c = me
        sibling = (x, y, 1 - c)
        chips = [(1 - x, y), (x, 1 - y), (1 - x, 1 - y)]

        def copy(k, block, to, src=None):
            slot = out_ref.at[_flat_index(block)]
            return pltpu.make_async_remote_copy(
                src_ref=slot if src is None else src, dst_ref=slot,
                send_sem=send_sems.at[k], recv_sem=recv_sems.at[k],
                device_id=to, device_id_type=pl.DeviceIdType.MESH)

        mine = pltpu.make_async_copy(x_ref, out_ref.at[_flat_index(me)], local_sem)
        mine.start()
        first = [copy(0, me, sibling, src=x_ref)]
        first += [copy(1 + j, me, (*chip, c), src=x_ref) for j, chip in enumerate(chips)]
        for cp in first:
            cp.start()
        passed = [copy(4 + j, (*chip, c), sibling) for j, chip in enumerate(chips)]
        for j, chip in enumerate(chips):
            copy(1 + j, (*chip, c), me).wait_recv()
            passed[j].start()
        copy(0, sibling, me).wait_recv()
        for j, chip in enumerate(chips):
            copy(4 + j, (*chip, 1 - c), me).wait_recv()
        for cp in first + passed:
            cp.wait_send()
        mine.wait()

    return pl.pallas_call(
        body, out_shape=jax.ShapeDtypeStruct((N_DEV, rows, LANES), shard.dtype),
        in_specs=[pl.BlockSpec(memory_space=pl.ANY)], out_specs=pl.BlockSpec(memory_space=pl.ANY),
        scratch_shapes=[pltpu.SemaphoreType.DMA((7,)), pltpu.SemaphoreType.DMA((7,)), pltpu.SemaphoreType.DMA],
        name="weights_all_gather",
    )(shard)


def _grad_exchange(big, small):
    def body(big_ref, small_ref, big_out, small_out, bsend, brecv, ssend, srecv, local_sems):
        me = _mesh_pos()
        mine = _flat_index(me)
        loc = [pltpu.make_async_copy(big_ref.at[mine], big_out.at[mine], local_sems.at[0]),
               pltpu.make_async_copy(small_ref, small_out.at[mine], local_sems.at[1])]
        for cp in loc:
            cp.start()

        def copies(k):
            peer = _peer(me, k)
            theirs = _flat_index(peer)
            send = (pltpu.make_async_remote_copy(
                        src_ref=big_ref.at[theirs], dst_ref=big_out.at[mine], send_sem=bsend.at[k - 1],
                        recv_sem=brecv.at[k - 1], device_id=peer, device_id_type=pl.DeviceIdType.MESH),
                    pltpu.make_async_remote_copy(
                        src_ref=small_ref, dst_ref=small_out.at[mine], send_sem=ssend.at[k - 1],
                        recv_sem=srecv.at[k - 1], device_id=peer, device_id_type=pl.DeviceIdType.MESH))
            recv = (pltpu.make_async_remote_copy(
                        src_ref=big_ref.at[mine], dst_ref=big_out.at[theirs], send_sem=bsend.at[k - 1],
                        recv_sem=brecv.at[k - 1], device_id=me, device_id_type=pl.DeviceIdType.MESH),
                    pltpu.make_async_remote_copy(
                        src_ref=small_ref, dst_ref=small_out.at[theirs], send_sem=ssend.at[k - 1],
                        recv_sem=srecv.at[k - 1], device_id=me, device_id_type=pl.DeviceIdType.MESH))
            return send, recv

        plan = [copies(k) for k in range(1, N_DEV)]
        for send, _ in plan:
            for cp in send:
                cp.start()
        for _, recv in plan:
            for cp in recv:
                cp.wait_recv()
        for send, _ in plan:
            for cp in send:
                cp.wait_send()
        for cp in loc:
            cp.wait()

    any_spec = pl.BlockSpec(memory_space=pl.ANY)
    return pl.pallas_call(
        body,
        out_shape=[jax.ShapeDtypeStruct(big.shape, big.dtype),
                   jax.ShapeDtypeStruct((N_DEV,) + small.shape, small.dtype)],
        in_specs=[any_spec, any_spec], out_specs=[any_spec, any_spec],
        scratch_shapes=[pltpu.SemaphoreType.DMA((7,)), pltpu.SemaphoreType.DMA((7,)),
                        pltpu.SemaphoreType.DMA((7,)), pltpu.SemaphoreType.DMA((7,)),
                        pltpu.SemaphoreType.DMA((2,))],
        name="grad_exchange",
    )(big, small)


def _slot_sum(name, slots, tr):
    rows = slots.shape[1]

    def body(s_ref, o_ref):
        acc = s_ref[0].astype(F32)
        for d in range(1, N_DEV):
            acc = acc + s_ref[d].astype(F32)
        o_ref[...] = acc

    return pl.pallas_call(
        body, grid=(rows // tr,),
        in_specs=[pl.BlockSpec((N_DEV, tr, LANES), lambda i: (0, i, 0))],
        out_specs=pl.BlockSpec((tr, LANES), lambda i: (i, 0)),
        out_shape=jax.ShapeDtypeStruct((rows, LANES), F32),
        name=name, compiler_params=_params(("arbitrary",)),
    )(slots)


def _adamw(name, w, g, m, v):
    rows, cols = w.shape
    tr = _pick(rows, (256, 128, 88, 32, 1))
    c1 = 1.0 - ADAM_B1 ** ADAM_STEP
    c2 = 1.0 - ADAM_B2 ** ADAM_STEP

    def body(w_ref, g_ref, m_ref, v_ref, d_ref, nm_ref, nv_ref):
        g = g_ref[...]
        nm = ADAM_B1 * m_ref[...] + (1.0 - ADAM_B1) * g
        nv = ADAM_B2 * v_ref[...] + (1.0 - ADAM_B2) * (g * g)
        nm_ref[...] = nm
        nv_ref[...] = nv
        d_ref[...] = -ADAM_LR * ((nm / c1) / (jnp.sqrt(nv / c2) + ADAM_EPS) + ADAM_WD * w_ref[...])

    spec = pl.BlockSpec((tr, cols), lambda i: (i, 0))
    return pl.pallas_call(
        body, grid=(rows // tr,), in_specs=[spec] * 4, out_specs=[spec] * 3,
        out_shape=[jax.ShapeDtypeStruct((rows, cols), F32)] * 3,
        name=name, compiler_params=_params(("arbitrary",)),
    )(w, g, m, v)


def _pack_shards(shards):
    parts = []
    for name, _, axis in SHARDED:
        w = shards[name].astype(BF16)
        parts.append((w.T if axis == 1 else w).reshape(-1))
    flat = jnp.concatenate(parts)
    flat = jnp.pad(flat, (0, FLAT_ROWS * LANES - flat.shape[0]))
    return flat.reshape(FLAT_ROWS, LANES)


def _unpack_full(gathered):
    flat = gathered.reshape(N_DEV, FLAT_ROWS * LANES)
    out, off = {}, 0
    for name, (r, c), axis in SHARDED:
        rr, cc = (c, r) if axis == 1 else (r, c)
        out[name] = flat[:, off:off + r * c].reshape(N_DEV * rr, cc)
        off += r * c
    return out


def _pack_full_grads(grads):
    parts = [grads[name].reshape(N_DEV, r * c) for name, (r, c), _ in SHARDED]
    flat = jnp.concatenate(parts, axis=1).astype(BF16)
    flat = jnp.pad(flat, ((0, 0), (0, FLAT_ROWS * LANES - flat.shape[1])))
    return flat.reshape(N_DEV, FLAT_ROWS, LANES)


def _rope_tables(positions):
    inv_freq = ROPE_THETA ** (-jnp.arange(0, MLA_ROPE_DIM, 2, dtype=F32) / MLA_ROPE_DIM)
    ang = positions.astype(F32)[:, None] * inv_freq
    z64 = jnp.zeros((positions.shape[0], 64), F32)
    z32 = jnp.zeros((positions.shape[0], 32), F32)
    cos, sin = jnp.cos(ang), jnp.sin(ang)
    return (jnp.concatenate([z64, cos, cos, z32], axis=1), jnp.concatenate([z64, sin, sin, z32], axis=1))


def _row_tile(s_len, want):
    return _pick(s_len, (want, 256, 128))


def kernel(x, positions, norm_mix_pre, norm_mix_post, w_in, b_gate, q_norm, w_uq, kv_norm, w_ukv, w_proj_sb, w_proj_mla, w_out, norm_ffn_pre, norm_ffn_post, w_gate_up, w_down, loss_target, m_norm_mix_pre, m_norm_mix_post, m_w_in, m_b_gate, m_q_norm, m_w_uq, m_kv_norm, m_w_ukv, m_w_proj_sb, m_w_proj_mla, m_w_out, m_norm_ffn_pre, m_norm_ffn_post, m_w_gate_up, m_w_down, v_norm_mix_pre, v_norm_mix_post, v_w_in, v_b_gate, v_q_norm, v_w_uq, v_kv_norm, v_w_ukv, v_w_proj_sb, v_w_proj_mla, v_w_out, v_norm_ffn_pre, v_norm_ffn_post, v_w_gate_up, v_w_down):
    weights = dict(norm_mix_pre=norm_mix_pre, norm_mix_post=norm_mix_post, w_in=w_in, b_gate=b_gate, q_norm=q_norm,
                   w_uq=w_uq, kv_norm=kv_norm, w_ukv=w_ukv, w_proj_sb=w_proj_sb, w_proj_mla=w_proj_mla, w_out=w_out,
                   norm_ffn_pre=norm_ffn_pre, norm_ffn_post=norm_ffn_post, w_gate_up=w_gate_up, w_down=w_down)
    m_in = dict(norm_mix_pre=m_norm_mix_pre, norm_mix_post=m_norm_mix_post, w_in=m_w_in, b_gate=m_b_gate,
                q_norm=m_q_norm, w_uq=m_w_uq, kv_norm=m_kv_norm, w_ukv=m_w_ukv, w_proj_sb=m_w_proj_sb,
                w_proj_mla=m_w_proj_mla, w_out=m_w_out, norm_ffn_pre=m_norm_ffn_pre, norm_ffn_post=m_norm_ffn_post,
                w_gate_up=m_w_gate_up, w_down=m_w_down)
    v_in = dict(norm_mix_pre=v_norm_mix_pre, norm_mix_post=v_norm_mix_post, w_in=v_w_in, b_gate=v_b_gate,
                q_norm=v_q_norm, w_uq=v_w_uq, kv_norm=v_kv_norm, w_ukv=v_w_ukv, w_proj_sb=v_w_proj_sb,
                w_proj_mla=v_w_proj_mla, w_out=v_w_out, norm_ffn_pre=v_norm_ffn_pre, norm_ffn_post=v_norm_ffn_post,
                w_gate_up=v_w_gate_up, w_down=v_w_down)
    order = list(weights)

    xs = x[0]
    target = loss_target[0]
    s_len = xs.shape[0]
    tm = _row_tile(s_len, 256)
    tm_ffn = _row_tile(s_len, 256)

    full = _unpack_full(_all_gather(_pack_shards({name: weights[name][0] for name, _, _ in SHARDED})))
    wt = full["w_in"]
    zr = lambda n: jnp.zeros((n, D_MODEL), BF16)
    w_ext = jnp.concatenate([wt[:2176], zr(64), wt[2176:2208], zr(32), wt[2208:]], axis=0).T
    wa = jnp.pad(full["w_uq"].reshape(N_HEADS, MLA_QK_DIM, MLA_Q_RANK), ((0, 0), (0, 32), (0, 0))
                 ).reshape(N_HEADS * LANES, MLA_Q_RANK).T
    ukv = full["w_ukv"].reshape(N_HEADS, LANES, MLA_KV_RANK)
    wk = jnp.pad(ukv[:, :64], ((0, 0), (0, 64), (0, 0))).reshape(N_HEADS * LANES, MLA_KV_RANK).T
    wv = ukv[:, 64:].reshape(512, MLA_KV_RANK).T
    w_sb, w_mla, w_o, w_gu, w_dn = (full["w_proj_sb"].T, full["w_proj_mla"].T, full["w_out"], full["w_gate_up"].T,
                                    full["w_down"])
    cr, sr = _rope_tables(positions[0])

    qkv, cq, ckv, kr, gl, hb = _in_proj(xs, norm_mix_pre, w_ext, tm)
    q_mla, k_mla, v_mla, cqn, ckvn = _mla_up(cq, ckv, kr, cr, sr, q_norm, kv_norm, wa, wk, wv, tm)
    o_sb, o_mla, lse = _attn_fwd(qkv, q_mla, k_mla, v_mla)
    x1, y, merged = _mix_out(o_sb, o_mla, gl, xs, b_gate, norm_mix_post, w_sb, w_mla, w_o, tm)
    dx2, f, h2, loss_part = _ffn_fwd(x1, target, norm_ffn_pre, norm_ffn_post, w_gu, w_dn, tm_ffn)
    loss_local = (0.5 / D_MODEL * jnp.sum(loss_part)).reshape(1)

    dx1, act, dgu, dfb, dg_ffn_pre, dg_ffn_post = _ffn_bwd(dx2, f, x1, norm_ffn_pre, norm_ffn_post, w_gu, w_dn, tm_ffn)
    dyb, dpsb, dpmla, dgl, do_sb, do_mla, dg_mix_post, db_gate = _mix_bwd(
        dx1, y, o_sb, o_mla, gl, b_gate, norm_mix_post, w_sb, w_mla, w_o, tm)
    dq_sb, dk_sb, dv_sb = _sb_bwd(qkv, do_sb, o_sb)
    dq_mla, dk_mla, dv_mla = _mla_bwd(q_mla, k_mla, v_mla, do_mla, o_mla, lse)
    da, dkb, dvb, dlat, dg_q, dg_kv = _mla_up_bwd(dq_mla, dk_mla, dv_mla, cq, ckv, cr, sr, q_norm, kv_norm,
                                                  wa, wk, wv, tm)
    dx, dproj, dg_mix_pre = _in_proj_bwd(xs, dx1, dq_sb, dk_sb, dv_sb, dlat, dgl, norm_mix_pre, w_ext, tm)

    d_ext = _matmul_tn("dw_in", dproj, hb)
    d_wa = _matmul_tn("dw_uq", da, cqn)
    d_wk = _matmul_tn("dw_uk", dkb, ckvn)
    d_wv = _matmul_tn("dw_uv", dvb, ckvn)
    grads = {
        "w_in": jnp.concatenate([d_ext[:2176], d_ext[2240:2272], d_ext[EXT_GL:]], axis=0),
        "w_uq": d_wa.reshape(N_HEADS, LANES, MLA_Q_RANK)[:, :MLA_QK_DIM].reshape(768, MLA_Q_RANK),
        "w_ukv": jnp.concatenate([d_wk.reshape(N_HEADS, LANES, MLA_KV_RANK)[:, :64],
                                  d_wv.reshape(N_HEADS, 64, MLA_KV_RANK)], axis=1).reshape(1024, MLA_KV_RANK),
        "w_proj_sb": _matmul_tn("dw_proj_sb", dpsb, o_sb.astype(BF16)),
        "w_proj_mla": _matmul_tn("dw_proj_mla", dpmla, o_mla.astype(BF16)),
        "w_out": _matmul_tn("dw_out", merged, dyb),
        "w_gate_up": _matmul_tn("dw_gate_up", dgu, h2),
        "w_down": _matmul_tn("dw_down", act, dfb),
    }
    small_parts = dict(norm_mix_pre=dg_mix_pre, norm_mix_post=dg_mix_post, b_gate=db_gate, q_norm=dg_q,
                       kv_norm=dg_kv, norm_ffn_pre=dg_ffn_pre, norm_ffn_post=dg_ffn_post)
    small = jnp.concatenate([small_parts[name].sum(axis=0) for name, _ in SMALL] + [loss_local])
    small = jnp.pad(small, (0, SMALL_ROWS * LANES - small.shape[0])).reshape(SMALL_ROWS, LANES)

    big_slots, small_slots = _grad_exchange(_pack_full_grads(grads), small)
    g_flat = _slot_sum("grad_shard_sum", big_slots, 512).reshape(-1)
    s_flat = _slot_sum("grad_small_sum", small_slots, SMALL_ROWS).reshape(-1)
    g_out, off = {}, 0
    for name, (r, c), axis in SHARDED:
        seg = g_flat[off:off + r * c]
        g_out[name] = seg.reshape(c, r).T if axis == 1 else seg.reshape(r, c)
        off += r * c
    off = 0
    for name, n in SMALL:
        g_out[name] = s_flat[off:off + n].reshape(1, n)
        off += n
    loss = s_flat[off]

    deltas, new_m, new_v = {}, {}, {}
    for name in order:
        w2 = weights[name].reshape(g_out[name].shape)
        d, nm, nv = _adamw("adamw_" + name, w2, g_out[name], m_in[name].reshape(w2.shape), v_in[name].reshape(w2.shape))
        shape = weights[name].shape
        deltas[name], new_m[name], new_v[name] = d.reshape(shape), nm.reshape(shape), nv.reshape(shape)
        g_out[name] = g_out[name].reshape(shape)

    return (loss, dx[None], *[g_out[n] for n in order], *[deltas[n] for n in order],
            *[new_m[n] for n in order], *[new_v[n] for n in order])
```

```python
import functools
import math

import jax
import jax.numpy as jnp
from jax import lax
from jax.experimental import pallas as pl
from jax.experimental.pallas import tpu as pltpu

F32 = jnp.float32
BF16 = jnp.bfloat16

D_MODEL = 1024
N_HEADS = 8
SB_WIDTH = 512
MLA_Q_RANK = 384
MLA_KV_RANK = 256
MLA_ROPE_DIM = 32
MLA_QK_DIM = 96
D_FF = 2816
ROPE_THETA = 10000.0
EPS = 1e-6
SB_SCALE = 1.0 / math.sqrt(64.0)
MLA_SCALE = 1.0 / math.sqrt(96.0)
NEG_BIG = -1e30

ADAM_LR = 0.001
ADAM_B1 = 0.9
ADAM_B2 = 0.999
ADAM_EPS = 1e-08
ADAM_WD = 0.01
ADAM_STEP = 10

N_DEV = 8
LANES = 128
TQ = 512
TK = 128
DIAG_TILES = TQ // TK
FWD_UNROLL = 4
BWD_UNROLL = 2
VMEM_LIMIT = 56 << 20

EXT_QKV = 0
EXT_CQ = 1536
EXT_CKV = 1920
EXT_KR = 2176
EXT_GL = 2304
EXT_N = 4352

SHARDED = (
    ("w_in", (1024, 532), 1), ("w_uq", (384, 96), 1), ("w_ukv", (256, 128), 1),
    ("w_proj_sb", (512, 128), 1), ("w_proj_mla", (512, 128), 1), ("w_out", (128, 1024), 0),
    ("w_gate_up", (1024, 704), 1), ("w_down", (352, 1024), 0),
)
FLAT_ROWS = 15360
SMALL = (("norm_mix_pre", 1024), ("norm_mix_post", 1024), ("b_gate", 2048), ("q_norm", 384),
         ("kv_norm", 256), ("norm_ffn_pre", 1024), ("norm_ffn_post", 1024))
SMALL_ROWS = 56


def _dot(a, b):
    return jnp.dot(a, b, preferred_element_type=F32)


def _dot_nt(a, b):
    return lax.dot_general(a, b, (((1,), (1,)), ((), ())), preferred_element_type=F32)


def _dot_tn(a, b):
    return lax.dot_general(a, b, (((0,), (0,)), ((), ())), preferred_element_type=F32)


def _rms(x):
    r = lax.rsqrt(jnp.mean(x * x, axis=-1, keepdims=True) + EPS)
    return x * r, r


def _rms_bwd(dn, n, r):
    return r * (dn - n * jnp.mean(dn * n, axis=-1, keepdims=True))


def _colsum8(x):
    return jnp.sum(x.reshape(x.shape[0] // 8, 8, x.shape[1]), axis=0)


def _split(x):
    hi = x.astype(BF16)
    return hi, (x - hi.astype(F32)).astype(BF16)


def _rot(x):
    lane = lax.broadcasted_iota(jnp.int32, x.shape, 1)
    up = pltpu.roll(x, 112, 1)
    down = pltpu.roll(x, 16, 1)
    return jnp.where((lane >= 64) & (lane < 80), -up, jnp.where((lane >= 80) & (lane < 96), down, 0.0))


def _params(sem):
    return pltpu.CompilerParams(dimension_semantics=sem, vmem_limit_bytes=VMEM_LIMIT)


def _rows_call(name, body, n_rows, tm, row_ins, const_ins, row_outs, acc_outs):
    in_specs = [pl.BlockSpec((tm, a.shape[1]), lambda i: (i, 0)) for a in row_ins]
    in_specs += [pl.BlockSpec(a.shape, lambda i: (0, 0), pipeline_mode=pl.Buffered(1)) for a in const_ins]
    out_specs = [pl.BlockSpec((tm, n), lambda i: (i, 0)) for n, _ in row_outs]
    out_specs += [pl.BlockSpec(s, lambda i: (0, 0)) for s in acc_outs]
    out_shape = [jax.ShapeDtypeStruct((n_rows, n), dt) for n, dt in row_outs]
    out_shape += [jax.ShapeDtypeStruct(s, F32) for s in acc_outs]
    return pl.pallas_call(
        body, grid=(n_rows // tm,), in_specs=in_specs, out_specs=out_specs, out_shape=out_shape,
        name=name, compiler_params=_params(("arbitrary",)),
    )(*row_ins, *const_ins)


def _in_proj(x, g_pre, w_ext, tm):
    def body(x_ref, g_ref, w_ref, qkv_ref, cq_ref, ckv_ref, kr_ref, gl_ref, h_ref):
        n, _ = _rms(x_ref[...])
        hb = (n * g_ref[...]).astype(BF16)
        h_ref[...] = hb
        for c in range(0, 1536, 512):
            qkv_ref[:, c:c + 512] = _dot(hb, w_ref[:, c:c + 512]).astype(BF16)
        cq_ref[...] = _dot(hb, w_ref[:, EXT_CQ:EXT_CKV])
        ckv_ref[...] = _dot(hb, w_ref[:, EXT_CKV:EXT_KR])
        kr_ref[...] = _dot(hb, w_ref[:, EXT_KR:EXT_GL])
        for c in range(0, 2048, 512):
            gl_ref[:, c:c + 512] = _dot(hb, w_ref[:, EXT_GL + c:EXT_GL + c + 512])

    return _rows_call("in_proj", body, x.shape[0], tm, [x], [g_pre, w_ext],
                      [(1536, BF16), (384, F32), (256, F32), (128, F32), (2048, F32), (1024, BF16)], [])


def _mla_up(cq, ckv, kr, cr, sr, q_norm, kv_norm, wa, wk, wv, tm):
    def body(cq_ref, ckv_ref, kr_ref, cr_ref, sr_ref, qn_ref, kvn_ref, wa_ref, wk_ref, wv_ref,
             q_ref, k_ref, v_ref, cqn_ref, ckvn_ref):
        nq, _ = _rms(cq_ref[...])
        cqn = (nq * qn_ref[...]).astype(BF16)
        cqn_ref[...] = cqn
        nk, _ = _rms(ckv_ref[...])
        ckvn = (nk * kvn_ref[...]).astype(BF16)
        ckvn_ref[...] = ckvn
        cr = cr_ref[...]
        sr = sr_ref[...]
        lane = lax.broadcasted_iota(jnp.int32, cr.shape, 1)
        cm = cr + (lane < 64).astype(F32)
        kr = kr_ref[...]
        krp = kr * cr + _rot(kr) * sr
        for h in range(N_HEADS):
            hs = slice(h * LANES, (h + 1) * LANES)
            a = _dot(cqn, wa_ref[:, hs])
            q_ref[:, hs] = (a * cm + _rot(a) * sr).astype(BF16)
            k_ref[:, hs] = (_dot(ckvn, wk_ref[:, hs]) + krp).astype(BF16)
        v_ref[...] = _dot(ckvn, wv_ref[...]).astype(BF16)

    return _rows_call("mla_up", body, cq.shape[0], tm, [cq, ckv, kr, cr, sr], [q_norm, kv_norm, wa, wk, wv],
                      [(1024, BF16), (1024, BF16), (512, BF16), (384, BF16), (256, BF16)], [])


def _mix_out(o_sb, o_mla, gl, x, b_gate, g_post, w_sb, w_mla, w_out, tm):
    def body(osb_ref, omla_ref, gl_ref, x_ref, b_ref, gp_ref, wsb_ref, wmla_ref, wout_ref,
             x1_ref, y_ref, mb_ref):
        psb = _dot(osb_ref[...].astype(BF16), wsb_ref[...])
        pmla = _dot(omla_ref[...].astype(BF16), wmla_ref[...])
        gates = jax.nn.sigmoid(gl_ref[...] + b_ref[...])
        mb = (gates[:, :D_MODEL] * psb + gates[:, D_MODEL:] * pmla).astype(BF16)
        mb_ref[...] = mb
        y = _dot(mb, wout_ref[...])
        y_ref[...] = y
        n, _ = _rms(y)
        x1_ref[...] = x_ref[...] + n * gp_ref[...]

    return _rows_call("mix_out", body, x.shape[0], tm, [o_sb, o_mla, gl, x], [b_gate, g_post, w_sb, w_mla, w_out],
                      [(1024, F32), (1024, F32), (1024, BF16)], [])


FF_CHUNK = 1408


def _ffn_fwd(x1, target, g_pre, g_post, w_gu, w_down, tm):
    def body(x1_ref, t_ref, gpre_ref, gpost_ref, wgu_ref, wd_ref, dx2_ref, f_ref, h2_ref, loss_ref):
        x1 = x1_ref[...]
        n, _ = _rms(x1)
        h2 = (n * gpre_ref[...]).astype(BF16)
        h2_ref[...] = h2
        f = jnp.zeros((tm, D_MODEL), F32)
        for c in range(0, D_FF, FF_CHUNK):
            g = _dot(h2, wgu_ref[:, c:c + FF_CHUNK])
            u = _dot(h2, wgu_ref[:, D_FF + c:D_FF + c + FF_CHUNK])
            act = (g * jax.nn.sigmoid(g) * u).astype(BF16)
            f = f + _dot(act, wd_ref[c:c + FF_CHUNK, :])
        f_ref[...] = f
        nf, _ = _rms(f)
        err = x1 + nf * gpost_ref[...] - t_ref[...]
        dx2_ref[...] = err * (1.0 / D_MODEL)
        e8 = _colsum8(err * err)
        part = e8[:, 0:LANES]
        for c in range(LANES, D_MODEL, LANES):
            part = part + e8[:, c:c + LANES]

        @pl.when(pl.program_id(0) == 0)
        def _():
            loss_ref[...] = jnp.zeros_like(loss_ref)

        loss_ref[...] += part

    return _rows_call("ffn_fwd", body, x1.shape[0], tm, [x1, target], [g_pre, g_post, w_gu, w_down],
                      [(1024, F32), (1024, F32), (1024, BF16)], [(8, LANES)])


def _ffn_bwd(dx2, f, x1, g_pre, g_post, w_gu, w_down, tm):
    def body(dx2_ref, f_ref, x1_ref, gpre_ref, gpost_ref, wgu_ref, wd_ref,
             dx1_ref, act_ref, dgu_ref, dfb_ref, dgpre_ref, dgpost_ref):
        @pl.when(pl.program_id(0) == 0)
        def _():
            dgpre_ref[...] = jnp.zeros_like(dgpre_ref)
            dgpost_ref[...] = jnp.zeros_like(dgpost_ref)

        dx2 = dx2_ref[...]
        nf, rf = _rms(f_ref[...])
        dgpost_ref[...] += _colsum8(dx2 * nf)
        dfb = _rms_bwd(dx2 * gpost_ref[...], nf, rf).astype(BF16)
        dfb_ref[...] = dfb
        x1 = x1_ref[...]
        n1, r1 = _rms(x1)
        h2 = (n1 * gpre_ref[...]).astype(BF16)
        dh2 = jnp.zeros((tm, D_MODEL), F32)
        for c in range(0, D_FF, FF_CHUNK):
            cs, us = slice(c, c + FF_CHUNK), slice(D_FF + c, D_FF + c + FF_CHUNK)
            g = _dot(h2, wgu_ref[:, cs])
            u = _dot(h2, wgu_ref[:, us])
            sg = jax.nn.sigmoid(g)
            si = g * sg
            act_ref[:, cs] = (si * u).astype(BF16)
            dact = _dot_nt(dfb, wd_ref[cs, :])
            dg = (dact * u * (sg * (1.0 + g * (1.0 - sg)))).astype(BF16)
            du = (dact * si).astype(BF16)
            dgu_ref[:, cs] = dg
            dgu_ref[:, us] = du
            dh2 = dh2 + _dot_nt(dg, wgu_ref[:, cs]) + _dot_nt(du, wgu_ref[:, us])
        dgpre_ref[...] += _colsum8(dh2 * n1)
        dx1_ref[...] = dx2 + _rms_bwd(dh2 * gpre_ref[...], n1, r1)

    return _rows_call("ffn_bwd", body, dx2.shape[0], tm, [dx2, f, x1], [g_pre, g_post, w_gu, w_down],
                      [(1024, F32), (D_FF, BF16), (2 * D_FF, BF16), (1024, BF16)], [(8, 1024), (8, 1024)])


def _mix_bwd(dx1, y, o_sb, o_mla, gl, b_gate, g_post, w_sb, w_mla, w_out, tm):
    def body(dx1_ref, y_ref, osb_ref, omla_ref, gl_ref, b_ref, gp_ref, wsb_ref, wmla_ref, wout_ref,
             dyb_ref, dpsb_ref, dpmla_ref, dgl_ref, dosb_ref, domla_ref, dgpost_ref, dbg_ref):
        @pl.when(pl.program_id(0) == 0)
        def _():
            dgpost_ref[...] = jnp.zeros_like(dgpost_ref)
            dbg_ref[...] = jnp.zeros_like(dbg_ref)

        dx1 = dx1_ref[...]
        ny, ry = _rms(y_ref[...])
        dgpost_ref[...] += _colsum8(dx1 * ny)
        dyb = _rms_bwd(dx1 * gp_ref[...], ny, ry).astype(BF16)
        dyb_ref[...] = dyb
        dm = _dot_nt(dyb, wout_ref[...])
        psb = _dot(osb_ref[...].astype(BF16), wsb_ref[...])
        pmla = _dot(omla_ref[...].astype(BF16), wmla_ref[...])
        gates = jax.nn.sigmoid(gl_ref[...] + b_ref[...])
        g0, g1 = gates[:, :D_MODEL], gates[:, D_MODEL:]
        dpsb = (dm * g0).astype(BF16)
        dpmla = (dm * g1).astype(BF16)
        dpsb_ref[...] = dpsb
        dpmla_ref[...] = dpmla
        dgl0 = dm * psb * g0 * (1.0 - g0)
        dgl1 = dm * pmla * g1 * (1.0 - g1)
        dgl_ref[:, :D_MODEL] = dgl0.astype(BF16)
        dgl_ref[:, D_MODEL:] = dgl1.astype(BF16)
        dbg_ref[:, :D_MODEL] += _colsum8(dgl0)
        dbg_ref[:, D_MODEL:] += _colsum8(dgl1)
        dosb_ref[...] = _dot_nt(dpsb, wsb_ref[...]).astype(BF16)
        domla_ref[...] = _dot_nt(dpmla, wmla_ref[...]).astype(BF16)

    return _rows_call("mix_bwd", body, dx1.shape[0], tm, [dx1, y, o_sb, o_mla, gl],
                      [b_gate, g_post, w_sb, w_mla, w_out],
                      [(1024, BF16), (1024, BF16), (1024, BF16), (2048, BF16), (512, BF16), (512, BF16)],
                      [(8, 1024), (8, 2048)])


def _mla_up_bwd(dq, dk, dv, cq, ckv, cr, sr, q_norm, kv_norm, wa, wk, wv, tm):
    def body(dq_ref, dk_ref, dv_ref, cq_ref, ckv_ref, cr_ref, sr_ref, qn_ref, kvn_ref, wa_ref, wk_ref, wv_ref,
             da_ref, dkb_ref, dvb_ref, dlat_ref, dqn_ref, dkvn_ref):
        @pl.when(pl.program_id(0) == 0)
        def _():
            dqn_ref[...] = jnp.zeros_like(dqn_ref)
            dkvn_ref[...] = jnp.zeros_like(dkvn_ref)

        cr = cr_ref[...]
        sr = sr_ref[...]
        lane = lax.broadcasted_iota(jnp.int32, cr.shape, 1)
        cm = cr + (lane < 64).astype(F32)
        nq, rq = _rms(cq_ref[...])
        nk, rk = _rms(ckv_ref[...])
        dcqn = jnp.zeros((tm, MLA_Q_RANK), F32)
        dckvn = jnp.zeros((tm, MLA_KV_RANK), F32)
        dkrp = jnp.zeros((tm, LANES), F32)
        for h in range(N_HEADS):
            hs = slice(h * LANES, (h + 1) * LANES)
            dqh = dq_ref[:, hs]
            da = (dqh * cm - _rot(dqh * sr)).astype(BF16)
            da_ref[:, hs] = da
            dcqn = dcqn + _dot_nt(da, wa_ref[:, hs])
            dkh = dk_ref[:, hs]
            dkb = dkh.astype(BF16)
            dkb_ref[:, hs] = dkb
            dckvn = dckvn + _dot_nt(dkb, wk_ref[:, hs])
            dkrp = dkrp + dkh
        dvb = dv_ref[...].astype(BF16)
        dvb_ref[...] = dvb
        dckvn = dckvn + _dot_nt(dvb, wv_ref[...])
        dkr = dkrp * cr - _rot(dkrp * sr)
        dqn_ref[...] += _colsum8(dcqn * nq)
        dkvn_ref[...] += _colsum8(dckvn * nk)
        dlat_ref[:, 0:384] = _rms_bwd(dcqn * qn_ref[...], nq, rq).astype(BF16)
        dlat_ref[:, 384:640] = _rms_bwd(dckvn * kvn_ref[...], nk, rk).astype(BF16)
        dlat_ref[:, 640:768] = dkr.astype(BF16)

    return _rows_call("mla_up_bwd", body, dq.shape[0], tm, [dq, dk, dv, cq, ckv, cr, sr],
                      [q_norm, kv_norm, wa, wk, wv],
                      [(1024, BF16), (1024, BF16), (512, BF16), (768, BF16)], [(8, 384), (8, 256)])


def _in_proj_bwd(x, dx1, dq_sb, dk_sb, dv_sb, dlat, dgl, g_pre, w_ext, tm):
    def body(x_ref, dx1_ref, dq_ref, dk_ref, dv_ref, dlat_ref, dgl_ref, g_ref, w_ref,
             dx_ref, dproj_ref, dg_ref):
        @pl.when(pl.program_id(0) == 0)
        def _():
            dg_ref[...] = jnp.zeros_like(dg_ref)

        dproj_ref[:, 0:512] = dq_ref[...].astype(BF16)
        dproj_ref[:, 512:1024] = dk_ref[...].astype(BF16)
        dproj_ref[:, 1024:1536] = dv_ref[...].astype(BF16)
        dproj_ref[:, EXT_CQ:EXT_GL] = dlat_ref[...]
        dproj_ref[:, EXT_GL:EXT_N] = dgl_ref[...]
        dh = jnp.zeros((tm, D_MODEL), F32)
        for c in range(0, EXT_N, 2176):
            dh = dh + _dot_nt(dproj_ref[:, c:c + 2176], w_ref[:, c:c + 2176])
        n, r = _rms(x_ref[...])
        dg_ref[...] += _colsum8(dh * n)
        dx_ref[...] = dx1_ref[...] + _rms_bwd(dh * g_ref[...], n, r)

    return _rows_call("in_proj_bwd", body, x.shape[0], tm, [x, dx1, dq_sb, dk_sb, dv_sb, dlat, dgl],
                      [g_pre, w_ext], [(1024, F32), (EXT_N, BF16)], [(8, 1024)])


def _head_masked(x):
    lane = lax.broadcasted_iota(jnp.int32, x.shape, 1)
    zero = jnp.zeros_like(x)
    return jnp.where(lane < 64, x, zero), jnp.where(lane >= 64, x, zero)


def _cum_weights():
    row = lax.broadcasted_iota(jnp.int32, (TK, TK), 0)
    col = lax.broadcasted_iota(jnp.int32, (TK, TK), 1)
    half = jnp.concatenate([(row > col).astype(BF16), jnp.ones((TK, TK), BF16)], axis=1)
    return jnp.concatenate([half, half], axis=0)


def _split_cat(x):
    hi, lo = _split(x)
    return jnp.concatenate([hi, lo], axis=1)


def _sweep(i, tiles, unroll):
    col = lax.broadcasted_iota(jnp.int32, (TQ, TK), 1)

    def diag(t, _):
        top = DIAG_TILES - 1 - t * unroll
        tiles(i * DIAG_TILES + top, [(top - u) * TK + col for u in range(unroll)])
        return 0

    lax.fori_loop(0, DIAG_TILES // unroll, diag, 0)

    def full(g, _):
        tiles(i * DIAG_TILES - 1 - g * unroll, [None] * unroll)
        return 0

    lax.fori_loop(0, (i * DIAG_TILES) // unroll, full, 0)


def _causal(key, strict):
    if key is None:
        return None
    row = lax.broadcasted_iota(jnp.int32, (TQ, TK), 0)
    return key < row if strict else key <= row


def _sb_logs(z, valid, w_cum):
    soft = jnp.log(1.0 + jnp.exp(-jnp.abs(z)))
    lsm = -jnp.maximum(z, 0.0) - soft
    cat = _split_cat(lsm if valid is None else jnp.where(valid, lsm, 0.0))
    return z + lsm, _dot(cat, w_cum)


def _sb_weight(log_beta, cs, r, valid):
    a = jnp.exp(log_beta + cs[:, :TK] + r)
    if valid is not None:
        a = jnp.where(valid, a, 0.0)
    return a, r + cs[:, TK:]


def _block_diag(x):
    x0, x1 = x[:, :LANES], x[:, LANES:]
    zero = jnp.zeros_like(x0)
    return jnp.concatenate([jnp.concatenate([x0, zero], axis=1), jnp.concatenate([zero, x1], axis=1)], axis=0)


def _attn_fwd(qkv, q, k, v):
    s_len = qkv.shape[0]

    def body(qs_ref, ks_ref, vs_ref, qm_ref, km_ref, vm_ref, osb_ref, omla_ref, lse_ref,
             sacc_ref, r_ref, macc_ref, m_ref):
        i = pl.program_id(1)
        lane = lax.broadcasted_iota(jnp.int32, (TQ, LANES), 1)
        r2 = lax.broadcasted_iota(jnp.int32, (2 * TK, LANES), 0)
        c2 = lax.broadcasted_iota(jnp.int32, (2 * TK, LANES), 1)
        head_ones = ((r2 < TK) == (c2 < 64)).astype(BF16)
        w_cum = _cum_weights()
        qs = qs_ref[...] * SB_SCALE
        qm = qm_ref[...]
        sacc_ref[...] = jnp.zeros_like(sacc_ref)
        r_ref[...] = jnp.zeros_like(r_ref)
        macc_ref[...] = jnp.zeros_like(macc_ref)
        m_ref[...] = jnp.full(m_ref.shape, NEG_BIG, F32)

        def sb_scores(top, valids):
            out = []
            for u, valid in enumerate(valids):
                off = pl.multiple_of((top - u) * TK, TK)
                z01 = _dot_nt(qs, jnp.concatenate(_head_masked(ks_ref[pl.ds(off, TK), :]), axis=0))
                out.append([_sb_logs(z, valid, w_cum) for z in (z01[:, :TK], z01[:, TK:])])
            return out

        def sb_accumulate(top, valids, scores):
            parts = [[None, None] for _ in valids]
            for hh in range(2):
                r = r_ref[hh]
                for u, valid in enumerate(valids):
                    a, r = _sb_weight(*scores[u][hh], r, valid)
                    parts[u][hh] = _split_cat(a)
                r_ref[hh] = r
            vs = []
            for u in range(len(valids)):
                off = pl.multiple_of((top - u) * TK, TK)
                v0, v1 = _head_masked(vs_ref[pl.ds(off, TK), :])
                vs += [v0, v0, v1, v1]
            sacc_ref[...] += _dot(jnp.concatenate([p for pair in parts for p in pair], axis=1),
                                  jnp.concatenate(vs, axis=0))

        def mla_scores(top, valids):
            s01s = []
            for u in range(len(valids)):
                off = pl.multiple_of((top - u) * TK, TK)
                s01s.append(_dot_nt(qm, _block_diag(km_ref[pl.ds(off, TK), :])))
            heads = []
            for hh in range(2):
                ss = []
                for u, valid in enumerate(valids):
                    s = s01s[u][:, hh * TK:(hh + 1) * TK] * MLA_SCALE
                    ss.append(s if valid is None else jnp.where(valid, s, NEG_BIG))
                m_old = m_ref[hh]
                m = jnp.maximum(m_old, jnp.max(functools.reduce(jnp.maximum, ss), axis=1, keepdims=True))
                m_ref[hh] = m
                heads.append((ss, m, jnp.exp(m_old - m)))
            return heads

        def mla_accumulate(top, heads):
            vs = []
            for u in range(len(heads[0][0])):
                off = pl.multiple_of((top - u) * TK, TK)
                vv = jnp.concatenate(_head_masked(vm_ref[pl.ds(off, TK), :]), axis=0)
                vs.append(jnp.concatenate([vv, head_ones], axis=1))
            ps = [[jnp.exp(s - m).astype(BF16) for s in ss] for ss, m, _ in heads]
            scale = jnp.where(lane < 64, heads[0][2], heads[1][2])
            p_all = jnp.concatenate([ps[hh][u] for u in range(len(vs)) for hh in range(2)], axis=1)
            macc_ref[...] = (macc_ref[...] * jnp.concatenate([scale, scale], axis=1)
                             + _dot(p_all, jnp.concatenate(vs, axis=0)))

        def tiles(top, keys):
            strict = [_causal(key, True) for key in keys]
            heads = mla_scores(top, [_causal(key, False) for key in keys])
            scores = sb_scores(top, strict)
            mla_accumulate(top, heads)
            sb_accumulate(top, strict, scores)

        _sweep(i, tiles, FWD_UNROLL)
        osb_ref[...] = sacc_ref[...]
        acc = macc_ref[...]
        den = acc[:, LANES:]
        omla_ref[...] = acc[:, :LANES] / den
        for hh, mask in enumerate((lane < 64, lane >= 64)):
            l = jnp.max(jnp.where(mask, den, 0.0), axis=1, keepdims=True)
            lse_ref[hh] = jnp.broadcast_to(m_ref[hh] + jnp.log(l), (TQ, LANES))

    tile = pl.BlockSpec((TQ, LANES), lambda h, i: (i, h))
    return pl.pallas_call(
        body, grid=(4, s_len // TQ),
        in_specs=[tile,
                  pl.BlockSpec((s_len, LANES), lambda h, i: (0, 4 + h)),
                  pl.BlockSpec((s_len, LANES), lambda h, i: (0, 8 + h)),
                  pl.BlockSpec((TQ, 2 * LANES), lambda h, i: (i, h)),
                  pl.BlockSpec((s_len, 2 * LANES), lambda h, i: (0, h)),
                  pl.BlockSpec((s_len, LANES), lambda h, i: (0, h))],
        out_specs=[tile, tile, pl.BlockSpec((2, TQ, LANES), lambda h, i: (h, i, 0))],
        out_shape=[jax.ShapeDtypeStruct((s_len, SB_WIDTH), F32), jax.ShapeDtypeStruct((s_len, 512), F32),
                   jax.ShapeDtypeStruct((N_HEADS, s_len, LANES), F32)],
        scratch_shapes=[pltpu.VMEM((TQ, LANES), F32), pltpu.VMEM((2, TQ, LANES), F32),
                        pltpu.VMEM((TQ, 2 * LANES), F32), pltpu.VMEM((2, TQ, 1), F32)],
        name="attn_fwd", compiler_params=_params(("arbitrary", "arbitrary")),
    )(qkv, qkv, qkv, q, k, v)


def _row_dots(do, o):
    prod = do.astype(F32) * o
    p0, p1 = _head_masked(prod)
    return tuple(jnp.broadcast_to(jnp.sum(p, axis=1, keepdims=True), prod.shape) for p in (p0, p1))


def _attn_bwd(qkv, do_sb, o_sb, q, k, v, do_mla, o_mla, lse):
    s_len = qkv.shape[0]
    n_q = s_len // TQ

    def body(qs_ref, ks_ref, vs_ref, dos_ref, os_ref, qm_ref, km_ref, vm_ref, dom_ref, om_ref, lse_ref,
             dqs_ref, dqm_ref, dks_hbm, dvs_hbm, dkm_hbm, dvm_hbm,
             dqs_acc, dqm_acc, dks_acc, dvs_acc, dkm_acc, dvm_acc, r_ref, g_ref, ds_ref, dm_ref, out_sems):
        h = pl.program_id(0)
        i = pl.program_id(1)

        @pl.when(i == 0)
        def _():
            for acc in (dks_acc, dvs_acc, dkm_acc, dvm_acc):
                acc[...] = jnp.zeros_like(acc)

        w_cum = _cum_weights()
        qs = qs_ref[...] * SB_SCALE
        dos = dos_ref[...]
        qs_rows = jnp.concatenate(_head_masked(qs), axis=0)
        dos_rows = jnp.concatenate(_head_masked(dos), axis=0)
        ds_ref[0], ds_ref[1] = _row_dots(dos, os_ref[...])
        qm = qm_ref[...]
        dom = dom_ref[...]
        qm_diag = _block_diag(qm)
        dom_rows = jnp.concatenate(_head_masked(dom), axis=0)
        dm_ref[0], dm_ref[1] = _row_dots(dom, om_ref[...])
        for ref in (dqs_acc, dqm_acc, r_ref, g_ref):
            ref[...] = jnp.zeros_like(ref)
        heads = (slice(0, TK), slice(TK, 2 * TK))

        def weigh(log_beta, cs, r, da, valid):
            a, r = _sb_weight(log_beta, cs, r, valid)
            g = a * da
            return a.astype(BF16), g, _split_cat(g), r

        def logit_grad(g, gs, carried, d, log_beta, valid):
            upto = d - (gs[:, :TK] + carried)
            dz = g - jnp.exp(log_beta) * upto
            if valid is not None:
                dz = jnp.where(valid, dz, 0.0)
            return dz.astype(BF16), carried + gs[:, TK:]

        def tiles(top, keys):
            n = len(keys)
            strict = [_causal(key, True) for key in keys]
            loose = [_causal(key, False) for key in keys]
            offs = [pl.multiple_of((top - u) * TK, TK) for u in range(n)]
            kds = [_block_diag(km_ref[pl.ds(off, TK), :]) for off in offs]
            vms = [jnp.concatenate(_head_masked(vm_ref[pl.ds(off, TK), :]), axis=0) for off in offs]
            s01s = [_dot_nt(qm, kd) for kd in kds]
            dp01s = [_dot_nt(dom, vv) for vv in vms]
            kks = [jnp.concatenate(_head_masked(ks_ref[pl.ds(off, TK), :]), axis=0) for off in offs]
            vvs = [jnp.concatenate(_head_masked(vs_ref[pl.ds(off, TK), :]), axis=0) for off in offs]
            z01s = [_dot_nt(qs, kk) for kk in kks]
            da01s = [_dot_nt(dos, vv) for vv in vvs]
            logs = [[_sb_logs(z01s[u][:, hs], strict[u], w_cum) for hs in heads] for u in range(n)]

            dss = [[None, None] for _ in range(n)]
            ps = [[None, None] for _ in range(n)]
            for u in range(n):
                for hh, hs in enumerate(heads):
                    p = jnp.exp(s01s[u][:, hs] * MLA_SCALE - lse_ref[hh])
                    if loose[u] is not None:
                        p = jnp.where(loose[u], p, 0.0)
                    dss[u][hh] = (p * (dp01s[u][:, hs] - dm_ref[hh]) * MLA_SCALE).astype(BF16)
                    ps[u][hh] = p.astype(BF16)

            dzs = [[None, None] for _ in range(n)]
            avs = [[None, None] for _ in range(n)]
            sums = [[None, None] for _ in range(n)]
            for hh, hs in enumerate(heads):
                r = r_ref[hh]
                for u in range(n):
                    avs[u][hh], g, cat, r = weigh(*logs[u][hh], r, da01s[u][:, hs], strict[u])
                    sums[u][hh] = (g, _dot(cat, w_cum))
                r_ref[hh] = r

            dqm_acc[...] += _dot(jnp.concatenate([d for pair in dss for d in pair], axis=1),
                                 jnp.concatenate(kds, axis=0))
            span = pl.ds(offs[-1], n * TK)
            by_key = lambda pairs: jnp.concatenate([jnp.concatenate(pair, axis=0) for pair in pairs[::-1]], axis=1)
            dkm_acc[span, :] += _dot_tn(by_key(dss), qm_diag)
            dvm_acc[span, :] += _dot_tn(by_key(ps), dom_rows)

            for hh in range(2):
                carried = g_ref[hh]
                for u in range(n):
                    dzs[u][hh], carried = logit_grad(*sums[u][hh], carried, ds_ref[hh], logs[u][hh][0], strict[u])
                g_ref[hh] = carried
            dqs_acc[...] += _dot(jnp.concatenate([dz for pair in dzs for dz in pair], axis=1),
                                 jnp.concatenate(kks, axis=0))
            dks_acc[span, :] += _dot_tn(by_key(dzs), qs_rows)
            dvs_acc[span, :] += _dot_tn(by_key(avs), dos_rows)

        _sweep(i, tiles, BWD_UNROLL)
        dqs_ref[...] = dqs_acc[...] * SB_SCALE
        dqm_ref[...] = dqm_acc[...]

        @pl.when(i == n_q - 1)
        def _():
            narrow = pl.ds(pl.multiple_of(h * LANES, LANES), LANES)
            wide = pl.ds(pl.multiple_of(h * 2 * LANES, 2 * LANES), 2 * LANES)
            copies = [pltpu.make_async_copy(dks_acc, dks_hbm.at[:, narrow], out_sems.at[0]),
                      pltpu.make_async_copy(dvs_acc, dvs_hbm.at[:, narrow], out_sems.at[1]),
                      pltpu.make_async_copy(dkm_acc, dkm_hbm.at[:, wide], out_sems.at[2]),
                      pltpu.make_async_copy(dvm_acc, dvm_hbm.at[:, narrow], out_sems.at[3])]
            for cp in copies:
                cp.start()
            for cp in copies:
                cp.wait()

    tile = pl.BlockSpec((TQ, LANES), lambda h, i: (i, h))
    wide_tile = pl.BlockSpec((TQ, 2 * LANES), lambda h, i: (i, h))
    once = pl.Buffered(1)
    hbm = pl.BlockSpec(memory_space=pl.ANY)
    return pl.pallas_call(
        body, grid=(4, n_q),
        in_specs=[tile,
                  pl.BlockSpec((s_len, LANES), lambda h, i: (0, 4 + h), pipeline_mode=once),
                  pl.BlockSpec((s_len, LANES), lambda h, i: (0, 8 + h), pipeline_mode=once),
                  tile, tile, wide_tile,
                  pl.BlockSpec((s_len, 2 * LANES), lambda h, i: (0, h), pipeline_mode=once),
                  pl.BlockSpec((s_len, LANES), lambda h, i: (0, h), pipeline_mode=once),
                  tile, tile, pl.BlockSpec((2, TQ, LANES), lambda h, i: (h, i, 0))],
        out_specs=[tile, wide_tile, hbm, hbm, hbm, hbm],
        out_shape=[jax.ShapeDtypeStruct((s_len, SB_WIDTH), F32), jax.ShapeDtypeStruct((s_len, 1024), F32),
                   jax.ShapeDtypeStruct((s_len, SB_WIDTH), F32), jax.ShapeDtypeStruct((s_len, SB_WIDTH), F32),
                   jax.ShapeDtypeStruct((s_len, 1024), F32), jax.ShapeDtypeStruct((s_len, 512), F32)],
        scratch_shapes=[pltpu.VMEM((TQ, LANES), F32), pltpu.VMEM((TQ, 2 * LANES), F32),
                        pltpu.VMEM((s_len, LANES), F32), pltpu.VMEM((s_len, LANES), F32),
                        pltpu.VMEM((s_len, 2 * LANES), F32), pltpu.VMEM((s_len, LANES), F32),
                        pltpu.VMEM((2, TQ, LANES), F32), pltpu.VMEM((2, TQ, LANES), F32),
                        pltpu.VMEM((2, TQ, LANES), F32), pltpu.VMEM((2, TQ, LANES), F32),
                        pltpu.SemaphoreType.DMA((4,))],
        name="attn_bwd", compiler_params=_params(("arbitrary", "arbitrary")),
    )(qkv, qkv, qkv, do_sb, o_sb, q, k, v, do_mla, o_mla, lse)


def _pick(n, options):
    for t in options:
        if n % t == 0:
            return t
    raise ValueError(n)


def _matmul_tn(name, a, b):
    s_len, m = a.shape
    n = b.shape[1]
    tm = _pick(m, (1024, 1408, 2176, 512))
    tn = _pick(n, (1024, 512, 384, 256))
    tk = _pick(s_len, (512, 256, 128))

    def body(a_ref, b_ref, o_ref):
        @pl.when(pl.program_id(2) == 0)
        def _():
            o_ref[...] = jnp.zeros_like(o_ref)

        o_ref[...] += _dot_tn(a_ref[...], b_ref[...])

    return pl.pallas_call(
        body, grid=(m // tm, n // tn, s_len // tk),
        in_specs=[pl.BlockSpec((tk, tm), lambda i, j, l: (l, i)), pl.BlockSpec((tk, tn), lambda i, j, l: (l, j))],
        out_specs=pl.BlockSpec((tm, tn), lambda i, j, l: (i, j)),
        out_shape=jax.ShapeDtypeStruct((m, n), F32),
        name=name, compiler_params=_params(("arbitrary", "arbitrary", "arbitrary")),
    )(a, b)


def _mesh_pos():
    return lax.axis_index("x"), lax.axis_index("y"), lax.axis_index("c")


def _peer(pos, k):
    x, y, c = pos
    return (1 - x if k & 4 else x, 1 - y if k & 2 else y, 1 - c if k & 1 else c)


def _flat_index(pos):
    return 4 * pos[0] + 2 * pos[1] + pos[2]


def _all_gather(shard):
    rows = shard.shape[0]

    def body(x_ref, out_ref, send_sems, recv_sems, local_sem):
        me = _mesh_pos()
        x, y, c = me
        sibling = (x, y, 1 - c)
        chips = [(1 - x, y), (x, 1 - y), (1 - x, 1 - y)]

        def copy(k, block, to, src=None):
            slot = out_ref.at[_flat_index(block)]
            return pltpu.make_async_remote_copy(
                src_ref=slot if src is None else src, dst_ref=slot,
                send_sem=send_sems.at[k], recv_sem=recv_sems.at[k],
                device_id=to, device_id_type=pl.DeviceIdType.MESH)

        mine = pltpu.make_async_copy(x_ref, out_ref.at[_flat_index(me)], local_sem)
        mine.start()
        first = [copy(0, me, sibling, src=x_ref)]
        first += [copy(1 + j, me, (*chip, c), src=x_ref) for j, chip in enumerate(chips)]
        for cp in first:
            cp.start()
        passed = [copy(4 + j, (*chip, c), sibling) for j, chip in enumerate(chips)]
        for j, chip in enumerate(chips):
            copy(1 + j, (*chip, c), me).wait_recv()
            passed[j].start()
        copy(0, sibling, me).wait_recv()
        for j, chip in enumerate(chips):
            copy(4 + j, (*chip, 1 - c), me).wait_recv()
        for cp in first + passed:
            cp.wait_send()
        mine.wait()

    return pl.pallas_call(
        body, out_shape=jax.ShapeDtypeStruct((N_DEV, rows, LANES), shard.dtype),
        in_specs=[pl.BlockSpec(memory_space=pl.ANY)], out_specs=pl.BlockSpec(memory_space=pl.ANY),
        scratch_shapes=[pltpu.SemaphoreType.DMA((7,)), pltpu.SemaphoreType.DMA((7,)), pltpu.SemaphoreType.DMA],
        name="weights_all_gather",
    )(shard)


def _grad_exchange(big, small):
    def body(big_ref, small_ref, big_out, small_out, bsend, brecv, ssend, srecv, local_sems):
        me = _mesh_pos()
        mine = _flat_index(me)
        loc = [pltpu.make_async_copy(big_ref.at[mine], big_out.at[mine], local_sems.at[0]),
               pltpu.make_async_copy(small_ref, small_out.at[mine], local_sems.at[1])]
        for cp in loc:
            cp.start()

        def copies(k):
            peer = _peer(me, k)
            theirs = _flat_index(peer)
            send = (pltpu.make_async_remote_copy(
                        src_ref=big_ref.at[theirs], dst_ref=big_out.at[mine], send_sem=bsend.at[k - 1],
                        recv_sem=brecv.at[k - 1], device_id=peer, device_id_type=pl.DeviceIdType.MESH),
                    pltpu.make_async_remote_copy(
                        src_ref=small_ref, dst_ref=small_out.at[mine], send_sem=ssend.at[k - 1],
                        recv_sem=srecv.at[k - 1], device_id=peer, device_id_type=pl.DeviceIdType.MESH))
            recv = (pltpu.make_async_remote_copy(
                        src_ref=big_ref.at[mine], dst_ref=big_out.at[theirs], send_sem=bsend.at[k - 1],
                        recv_sem=brecv.at[k - 1], device_id=me, device_id_type=pl.DeviceIdType.MESH),
                    pltpu.make_async_remote_copy(
                        src_ref=small_ref, dst_ref=small_out.at[theirs], send_sem=ssend.at[k - 1],
                        recv_sem=srecv.at[k - 1], device_id=me, device_id_type=pl.DeviceIdType.MESH))
            return send, recv

        plan = [copies(k) for k in range(1, N_DEV)]
        for send, _ in plan:
            for cp in send:
                cp.start()
        for _, recv in plan:
            for cp in recv:
                cp.wait_recv()
        for send, _ in plan:
            for cp in send:
                cp.wait_send()
        for cp in loc:
            cp.wait()

    any_spec = pl.BlockSpec(memory_space=pl.ANY)
    return pl.pallas_call(
        body,
        out_shape=[jax.ShapeDtypeStruct(big.shape, big.dtype),
                   jax.ShapeDtypeStruct((N_DEV,) + small.shape, small.dtype)],
        in_specs=[any_spec, any_spec], out_specs=[any_spec, any_spec],
        scratch_shapes=[pltpu.SemaphoreType.DMA((7,)), pltpu.SemaphoreType.DMA((7,)),
                        pltpu.SemaphoreType.DMA((7,)), pltpu.SemaphoreType.DMA((7,)),
                        pltpu.SemaphoreType.DMA((2,))],
        name="grad_exchange",
    )(big, small)


def _slot_sum(name, slots, tr):
    rows = slots.shape[1]

    def body(s_ref, o_ref):
        acc = s_ref[0].astype(F32)
        for d in range(1, N_DEV):
            acc = acc + s_ref[d].astype(F32)
        o_ref[...] = acc

    return pl.pallas_call(
        body, grid=(rows // tr,),
        in_specs=[pl.BlockSpec((N_DEV, tr, LANES), lambda i: (0, i, 0))],
        out_specs=pl.BlockSpec((tr, LANES), lambda i: (i, 0)),
        out_shape=jax.ShapeDtypeStruct((rows, LANES), F32),
        name=name, compiler_params=_params(("arbitrary",)),
    )(slots)


def _adamw(name, w, g, m, v):
    rows, cols = w.shape
    tr = _pick(rows, (256, 128, 88, 32, 1))
    c1 = 1.0 - ADAM_B1 ** ADAM_STEP
    c2 = 1.0 - ADAM_B2 ** ADAM_STEP

    def body(w_ref, g_ref, m_ref, v_ref, d_ref, nm_ref, nv_ref):
        g = g_ref[...]
        nm = ADAM_B1 * m_ref[...] + (1.0 - ADAM_B1) * g
        nv = ADAM_B2 * v_ref[...] + (1.0 - ADAM_B2) * (g * g)
        nm_ref[...] = nm
        nv_ref[...] = nv
        d_ref[...] = -ADAM_LR * ((nm / c1) / (jnp.sqrt(nv / c2) + ADAM_EPS) + ADAM_WD * w_ref[...])

    spec = pl.BlockSpec((tr, cols), lambda i: (i, 0))
    return pl.pallas_call(
        body, grid=(rows // tr,), in_specs=[spec] * 4, out_specs=[spec] * 3,
        out_shape=[jax.ShapeDtypeStruct((rows, cols), F32)] * 3,
        name=name, compiler_params=_params(("arbitrary",)),
    )(w, g, m, v)


def _pack_shards(shards):
    parts = []
    for name, _, axis in SHARDED:
        w = shards[name].astype(BF16)
        parts.append((w.T if axis == 1 else w).reshape(-1))
    flat = jnp.concatenate(parts)
    flat = jnp.pad(flat, (0, FLAT_ROWS * LANES - flat.shape[0]))
    return flat.reshape(FLAT_ROWS, LANES)


def _unpack_full(gathered):
    flat = gathered.reshape(N_DEV, FLAT_ROWS * LANES)
    out, off = {}, 0
    for name, (r, c), axis in SHARDED:
        rr, cc = (c, r) if axis == 1 else (r, c)
        out[name] = flat[:, off:off + r * c].reshape(N_DEV * rr, cc)
        off += r * c
    return out


def _pack_full_grads(grads):
    parts = [grads[name].reshape(N_DEV, r * c) for name, (r, c), _ in SHARDED]
    flat = jnp.concatenate(parts, axis=1).astype(BF16)
    flat = jnp.pad(flat, ((0, 0), (0, FLAT_ROWS * LANES - flat.shape[1])))
    return flat.reshape(N_DEV, FLAT_ROWS, LANES)


def _rope_tables(positions):
    inv_freq = ROPE_THETA ** (-jnp.arange(0, MLA_ROPE_DIM, 2, dtype=F32) / MLA_ROPE_DIM)
    ang = positions.astype(F32)[:, None] * inv_freq
    z64 = jnp.zeros((positions.shape[0], 64), F32)
    z32 = jnp.zeros((positions.shape[0], 32), F32)
    cos, sin = jnp.cos(ang), jnp.sin(ang)
    return (jnp.concatenate([z64, cos, cos, z32], axis=1), jnp.concatenate([z64, sin, sin, z32], axis=1))


def _row_tile(s_len, want):
    return _pick(s_len, (want, 256, 128))


def kernel(x, positions, norm_mix_pre, norm_mix_post, w_in, b_gate, q_norm, w_uq, kv_norm, w_ukv, w_proj_sb, w_proj_mla, w_out, norm_ffn_pre, norm_ffn_post, w_gate_up, w_down, loss_target, m_norm_mix_pre, m_norm_mix_post, m_w_in, m_b_gate, m_q_norm, m_w_uq, m_kv_norm, m_w_ukv, m_w_proj_sb, m_w_proj_mla, m_w_out, m_norm_ffn_pre, m_norm_ffn_post, m_w_gate_up, m_w_down, v_norm_mix_pre, v_norm_mix_post, v_w_in, v_b_gate, v_q_norm, v_w_uq, v_kv_norm, v_w_ukv, v_w_proj_sb, v_w_proj_mla, v_w_out, v_norm_ffn_pre, v_norm_ffn_post, v_w_gate_up, v_w_down):
    weights = dict(norm_mix_pre=norm_mix_pre, norm_mix_post=norm_mix_post, w_in=w_in, b_gate=b_gate, q_norm=q_norm,
                   w_uq=w_uq, kv_norm=kv_norm, w_ukv=w_ukv, w_proj_sb=w_proj_sb, w_proj_mla=w_proj_mla, w_out=w_out,
                   norm_ffn_pre=norm_ffn_pre, norm_ffn_post=norm_ffn_post, w_gate_up=w_gate_up, w_down=w_down)
    m_in = dict(norm_mix_pre=m_norm_mix_pre, norm_mix_post=m_norm_mix_post, w_in=m_w_in, b_gate=m_b_gate,
                q_norm=m_q_norm, w_uq=m_w_uq, kv_norm=m_kv_norm, w_ukv=m_w_ukv, w_proj_sb=m_w_proj_sb,
                w_proj_mla=m_w_proj_mla, w_out=m_w_out, norm_ffn_pre=m_norm_ffn_pre, norm_ffn_post=m_norm_ffn_post,
                w_gate_up=m_w_gate_up, w_down=m_w_down)
    v_in = dict(norm_mix_pre=v_norm_mix_pre, norm_mix_post=v_norm_mix_post, w_in=v_w_in, b_gate=v_b_gate,
                q_norm=v_q_norm, w_uq=v_w_uq, kv_norm=v_kv_norm, w_ukv=v_w_ukv, w_proj_sb=v_w_proj_sb,
                w_proj_mla=v_w_proj_mla, w_out=v_w_out, norm_ffn_pre=v_norm_ffn_pre, norm_ffn_post=v_norm_ffn_post,
                w_gate_up=v_w_gate_up, w_down=v_w_down)
    order = list(weights)

    xs = x[0]
    target = loss_target[0]
    s_len = xs.shape[0]
    tm = _row_tile(s_len, 256)
    tm_ffn = _row_tile(s_len, 256)

    full = _unpack_full(_all_gather(_pack_shards({name: weights[name][0] for name, _, _ in SHARDED})))
    wt = full["w_in"]
    zr = lambda n: jnp.zeros((n, D_MODEL), BF16)
    w_ext = jnp.concatenate([wt[:2176], zr(64), wt[2176:2208], zr(32), wt[2208:]], axis=0).T
    wa = jnp.pad(full["w_uq"].reshape(N_HEADS, MLA_QK_DIM, MLA_Q_RANK), ((0, 0), (0, 32), (0, 0))
                 ).reshape(N_HEADS * LANES, MLA_Q_RANK).T
    ukv = full["w_ukv"].reshape(N_HEADS, LANES, MLA_KV_RANK)
    wk = jnp.pad(ukv[:, :64], ((0, 0), (0, 64), (0, 0))).reshape(N_HEADS * LANES, MLA_KV_RANK).T
    wv = ukv[:, 64:].reshape(512, MLA_KV_RANK).T
    w_sb, w_mla, w_o, w_gu, w_dn = (full["w_proj_sb"].T, full["w_proj_mla"].T, full["w_out"], full["w_gate_up"].T,
                                    full["w_down"])
    cr, sr = _rope_tables(positions[0])

    qkv, cq, ckv, kr, gl, hb = _in_proj(xs, norm_mix_pre, w_ext, tm)
    q_mla, k_mla, v_mla, cqn, ckvn = _mla_up(cq, ckv, kr, cr, sr, q_norm, kv_norm, wa, wk, wv, tm)
    o_sb, o_mla, lse = _attn_fwd(qkv, q_mla, k_mla, v_mla)
    x1, y, merged = _mix_out(o_sb, o_mla, gl, xs, b_gate, norm_mix_post, w_sb, w_mla, w_o, tm)
    dx2, f, h2, loss_part = _ffn_fwd(x1, target, norm_ffn_pre, norm_ffn_post, w_gu, w_dn, tm_ffn)
    loss_local = (0.5 / D_MODEL * jnp.sum(loss_part)).reshape(1)

    dx1, act, dgu, dfb, dg_ffn_pre, dg_ffn_post = _ffn_bwd(dx2, f, x1, norm_ffn_pre, norm_ffn_post, w_gu, w_dn, tm_ffn)
    dyb, dpsb, dpmla, dgl, do_sb, do_mla, dg_mix_post, db_gate = _mix_bwd(
        dx1, y, o_sb, o_mla, gl, b_gate, norm_mix_post, w_sb, w_mla, w_o, tm)
    dq_sb, dq_mla, dk_sb, dv_sb, dk_mla, dv_mla = _attn_bwd(qkv, do_sb, o_sb, q_mla, k_mla, v_mla, do_mla, o_mla, lse)
    da, dkb, dvb, dlat, dg_q, dg_kv = _mla_up_bwd(dq_mla, dk_mla, dv_mla, cq, ckv, cr, sr, q_norm, kv_norm,
                                                  wa, wk, wv, tm)
    dx, dproj, dg_mix_pre = _in_proj_bwd(xs, dx1, dq_sb, dk_sb, dv_sb, dlat, dgl, norm_mix_pre, w_ext, tm)

    d_ext = _matmul_tn("dw_in", dproj, hb)
    d_wa = _matmul_tn("dw_uq", da, cqn)
    d_wk = _matmul_tn("dw_uk", dkb, ckvn)
    d_wv = _matmul_tn("dw_uv", dvb, ckvn)
    grads = {
        "w_in": jnp.concatenate([d_ext[:2176], d_ext[2240:2272], d_ext[EXT_GL:]], axis=0),
        "w_uq": d_wa.reshape(N_HEADS, LANES, MLA_Q_RANK)[:, :MLA_QK_DIM].reshape(768, MLA_Q_RANK),
        "w_ukv": jnp.concatenate([d_wk.reshape(N_HEADS, LANES, MLA_KV_RANK)[:, :64],
                                  d_wv.reshape(N_HEADS, 64, MLA_KV_RANK)], axis=1).reshape(1024, MLA_KV_RANK),
        "w_proj_sb": _matmul_tn("dw_proj_sb", dpsb, o_sb.astype(BF16)),
        "w_proj_mla": _matmul_tn("dw_proj_mla", dpmla, o_mla.astype(BF16)),
        "w_out": _matmul_tn("dw_out", merged, dyb),
        "w_gate_up": _matmul_tn("dw_gate_up", dgu, h2),
        "w_down": _matmul_tn("dw_down", act, dfb),
    }
    small_parts = dict(norm_mix_pre=dg_mix_pre, norm_mix_post=dg_mix_post, b_gate=db_gate, q_norm=dg_q,
                       kv_norm=dg_kv, norm_ffn_pre=dg_ffn_pre, norm_ffn_post=dg_ffn_post)
    small = jnp.concatenate([small_parts[name].sum(axis=0) for name, _ in SMALL] + [loss_local])
    small = jnp.pad(small, (0, SMALL_ROWS * LANES - small.shape[0])).reshape(SMALL_ROWS, LANES)

    big_slots, small_slots = _grad_exchange(_pack_full_grads(grads), small)
    g_flat = _slot_sum("grad_shard_sum", big_slots, 512).reshape(-1)
    s_flat = _slot_sum("grad_small_sum", small_slots, SMALL_ROWS).reshape(-1)
    g_out, off = {}, 0
    for name, (r, c), axis in SHARDED:
        seg = g_flat[off:off + r * c]
        g_out[name] = seg.reshape(c, r).T if axis == 1 else seg.reshape(r, c)
        off += r * c
    off = 0
    for name, n in SMALL:
        g_out[name] = s_flat[off:off + n].reshape(1, n)
        off += n
    loss = s_flat[off]

    deltas, new_m, new_v = {}, {}, {}
    for name in order:
        w2 = weights[name].reshape(g_out[name].shape)
        d, nm, nv = _adamw("adamw_" + name, w2, g_out[name], m_in[name].reshape(w2.shape), v_in[name].reshape(w2.shape))
        shape = weights[name].shape
        deltas[name], new_m[name], new_v[name] = d.reshape(shape), nm.reshape(shape), nv.reshape(shape)
        g_out[name] = g_out[name].reshape(shape)

    return (loss, dx[None], *[g_out[n] for n in order], *[deltas[n] for n in order],
            *[new_m[n] for n in order], *[new_v[n] for n in order])
```

```python
import functools
import math

import jax
import jax.numpy as jnp
from jax import lax
from jax.experimental import pallas as pl
from jax.experimental.pallas import tpu as pltpu

F32 = jnp.float32
BF16 = jnp.bfloat16

D_MODEL = 1024
N_HEADS = 8
SB_WIDTH = 512
MLA_Q_RANK = 384
MLA_KV_RANK = 256
MLA_ROPE_DIM = 32
MLA_QK_DIM = 96
D_FF = 2816
ROPE_THETA = 10000.0
EPS = 1e-6
SB_SCALE = 1.0 / math.sqrt(64.0)
MLA_SCALE = 1.0 / math.sqrt(96.0)
NEG_BIG = -1e30

ADAM_LR = 0.001
ADAM_B1 = 0.9
ADAM_B2 = 0.999
ADAM_EPS = 1e-08
ADAM_WD = 0.01
ADAM_STEP = 10

N_DEV = 8
LANES = 128
TQ = 512
TK = 128
DIAG_TILES = TQ // TK
FWD_UNROLL = 4
BWD_UNROLL = 2
VMEM_LIMIT = 56 << 20

EXT_QKV = 0
EXT_CQ = 1536
EXT_CKV = 1920
EXT_KR = 2176
EXT_GL = 2304
EXT_N = 4352

EARLY = (("w_in", (1024, 532), 1), ("w_uq", (384, 96), 1), ("w_ukv", (256, 128), 1))
LATE = (("w_proj_sb", (512, 128), 1), ("w_proj_mla", (512, 128), 1), ("w_out", (128, 1024), 0),
        ("w_gate_up", (1024, 704), 1), ("w_down", (352, 1024), 0))
SHARDED = EARLY + LATE
EARLY_ROWS = 4800
LATE_ROWS = 10752
SMALL = (("norm_mix_pre", 1024), ("norm_mix_post", 1024), ("b_gate", 2048), ("q_norm", 384),
         ("kv_norm", 256), ("norm_ffn_pre", 1024), ("norm_ffn_post", 1024))
SMALL_ROWS = 56


def _dot(a, b):
    return jnp.dot(a, b, preferred_element_type=F32)


def _dot_nt(a, b):
    return lax.dot_general(a, b, (((1,), (1,)), ((), ())), preferred_element_type=F32)


def _dot_tn(a, b):
    return lax.dot_general(a, b, (((0,), (0,)), ((), ())), preferred_element_type=F32)


def _rms(x):
    r = lax.rsqrt(jnp.mean(x * x, axis=-1, keepdims=True) + EPS)
    return x * r, r


def _rms_bwd(dn, n, r):
    return r * (dn - n * jnp.mean(dn * n, axis=-1, keepdims=True))


def _colsum8(x):
    return jnp.sum(x.reshape(x.shape[0] // 8, 8, x.shape[1]), axis=0)


def _split(x):
    hi = x.astype(BF16)
    return hi, (x - hi.astype(F32)).astype(BF16)


def _rot(x):
    lane = lax.broadcasted_iota(jnp.int32, x.shape, 1)
    up = pltpu.roll(x, 112, 1)
    down = pltpu.roll(x, 16, 1)
    return jnp.where((lane >= 64) & (lane < 80), -up, jnp.where((lane >= 80) & (lane < 96), down, 0.0))


def _params(sem):
    return pltpu.CompilerParams(dimension_semantics=sem, vmem_limit_bytes=VMEM_LIMIT)


def _rows_call(name, body, n_rows, tm, row_ins, const_ins, row_outs, acc_outs):
    in_specs = [pl.BlockSpec((tm, a.shape[1]), lambda i: (i, 0)) for a in row_ins]
    in_specs += [pl.BlockSpec(a.shape, lambda i: (0, 0), pipeline_mode=pl.Buffered(1)) for a in const_ins]
    out_specs = [pl.BlockSpec((tm, n), lambda i: (i, 0)) for n, _ in row_outs]
    out_specs += [pl.BlockSpec(s, lambda i: (0, 0)) for s in acc_outs]
    out_shape = [jax.ShapeDtypeStruct((n_rows, n), dt) for n, dt in row_outs]
    out_shape += [jax.ShapeDtypeStruct(s, F32) for s in acc_outs]
    return pl.pallas_call(
        body, grid=(n_rows // tm,), in_specs=in_specs, out_specs=out_specs, out_shape=out_shape,
        name=name, compiler_params=_params(("arbitrary",)),
    )(*row_ins, *const_ins)


def _in_proj(x, g_pre, w_ext, tm):
    def body(x_ref, g_ref, w_ref, qkv_ref, cq_ref, ckv_ref, kr_ref, gl_ref, h_ref):
        n, _ = _rms(x_ref[...])
        hb = (n * g_ref[...]).astype(BF16)
        h_ref[...] = hb
        for c in range(0, 1536, 512):
            qkv_ref[:, c:c + 512] = _dot(hb, w_ref[:, c:c + 512]).astype(BF16)
        cq_ref[...] = _dot(hb, w_ref[:, EXT_CQ:EXT_CKV])
        ckv_ref[...] = _dot(hb, w_ref[:, EXT_CKV:EXT_KR])
        kr_ref[...] = _dot(hb, w_ref[:, EXT_KR:EXT_GL])
        for c in range(0, 2048, 512):
            gl_ref[:, c:c + 512] = _dot(hb, w_ref[:, EXT_GL + c:EXT_GL + c + 512])

    return _rows_call("in_proj", body, x.shape[0], tm, [x], [g_pre, w_ext],
                      [(1536, BF16), (384, F32), (256, F32), (128, F32), (2048, F32), (1024, BF16)], [])


def _mla_up(cq, ckv, kr, cr, sr, q_norm, kv_norm, wa, wk, wv, tm):
    def body(cq_ref, ckv_ref, kr_ref, cr_ref, sr_ref, qn_ref, kvn_ref, wa_ref, wk_ref, wv_ref,
             q_ref, k_ref, v_ref, cqn_ref, ckvn_ref):
        nq, _ = _rms(cq_ref[...])
        cqn = (nq * qn_ref[...]).astype(BF16)
        cqn_ref[...] = cqn
        nk, _ = _rms(ckv_ref[...])
        ckvn = (nk * kvn_ref[...]).astype(BF16)
        ckvn_ref[...] = ckvn
        cr = cr_ref[...]
        sr = sr_ref[...]
        lane = lax.broadcasted_iota(jnp.int32, cr.shape, 1)
        cm = cr + (lane < 64).astype(F32)
        kr = kr_ref[...]
        krp = kr * cr + _rot(kr) * sr
        for h in range(N_HEADS):
            hs = slice(h * LANES, (h + 1) * LANES)
            a = _dot(cqn, wa_ref[:, hs])
            q_ref[:, hs] = (a * cm + _rot(a) * sr).astype(BF16)
            k_ref[:, hs] = (_dot(ckvn, wk_ref[:, hs]) + krp).astype(BF16)
        v_ref[...] = _dot(ckvn, wv_ref[...]).astype(BF16)

    return _rows_call("mla_up", body, cq.shape[0], tm, [cq, ckv, kr, cr, sr], [q_norm, kv_norm, wa, wk, wv],
                      [(1024, BF16), (1024, BF16), (512, BF16), (384, BF16), (256, BF16)], [])


def _mix_out(o_sb, o_mla, gl, x, b_gate, g_post, w_sb, w_mla, w_out, tm):
    def body(osb_ref, omla_ref, gl_ref, x_ref, b_ref, gp_ref, wsb_ref, wmla_ref, wout_ref,
             x1_ref, y_ref, mb_ref):
        psb = _dot(osb_ref[...].astype(BF16), wsb_ref[...])
        pmla = _dot(omla_ref[...].astype(BF16), wmla_ref[...])
        gates = jax.nn.sigmoid(gl_ref[...] + b_ref[...])
        mb = (gates[:, :D_MODEL] * psb + gates[:, D_MODEL:] * pmla).astype(BF16)
        mb_ref[...] = mb
        y = _dot(mb, wout_ref[...])
        y_ref[...] = y
        n, _ = _rms(y)
        x1_ref[...] = x_ref[...] + n * gp_ref[...]

    return _rows_call("mix_out", body, x.shape[0], tm, [o_sb, o_mla, gl, x], [b_gate, g_post, w_sb, w_mla, w_out],
                      [(1024, F32), (1024, F32), (1024, BF16)], [])


FF_CHUNK = 1408


def _ffn_fwd(x1, target, g_pre, g_post, w_gu, w_down, tm):
    def body(x1_ref, t_ref, gpre_ref, gpost_ref, wgu_ref, wd_ref, dx2_ref, f_ref, h2_ref, loss_ref):
        x1 = x1_ref[...]
        n, _ = _rms(x1)
        h2 = (n * gpre_ref[...]).astype(BF16)
        h2_ref[...] = h2
        f = jnp.zeros((tm, D_MODEL), F32)
        for c in range(0, D_FF, FF_CHUNK):
            g = _dot(h2, wgu_ref[:, c:c + FF_CHUNK])
            u = _dot(h2, wgu_ref[:, D_FF + c:D_FF + c + FF_CHUNK])
            act = (g * jax.nn.sigmoid(g) * u).astype(BF16)
            f = f + _dot(act, wd_ref[c:c + FF_CHUNK, :])
        f_ref[...] = f
        nf, _ = _rms(f)
        err = x1 + nf * gpost_ref[...] - t_ref[...]
        dx2_ref[...] = err * (1.0 / D_MODEL)
        e8 = _colsum8(err * err)
        part = e8[:, 0:LANES]
        for c in range(LANES, D_MODEL, LANES):
            part = part + e8[:, c:c + LANES]

        @pl.when(pl.program_id(0) == 0)
        def _():
            loss_ref[...] = jnp.zeros_like(loss_ref)

        loss_ref[...] += part

    return _rows_call("ffn_fwd", body, x1.shape[0], tm, [x1, target], [g_pre, g_post, w_gu, w_down],
                      [(1024, F32), (1024, F32), (1024, BF16)], [(8, LANES)])


def _ffn_bwd(dx2, f, x1, g_pre, g_post, w_gu, w_down, tm):
    def body(dx2_ref, f_ref, x1_ref, gpre_ref, gpost_ref, wgu_ref, wd_ref,
             dx1_ref, act_ref, dgu_ref, dfb_ref, dgpre_ref, dgpost_ref):
        @pl.when(pl.program_id(0) == 0)
        def _():
            dgpre_ref[...] = jnp.zeros_like(dgpre_ref)
            dgpost_ref[...] = jnp.zeros_like(dgpost_ref)

        dx2 = dx2_ref[...]
        nf, rf = _rms(f_ref[...])
        dgpost_ref[...] += _colsum8(dx2 * nf)
        dfb = _rms_bwd(dx2 * gpost_ref[...], nf, rf).astype(BF16)
        dfb_ref[...] = dfb
        x1 = x1_ref[...]
        n1, r1 = _rms(x1)
        h2 = (n1 * gpre_ref[...]).astype(BF16)
        dh2 = jnp.zeros((tm, D_MODEL), F32)
        for c in range(0, D_FF, FF_CHUNK):
            cs, us = slice(c, c + FF_CHUNK), slice(D_FF + c, D_FF + c + FF_CHUNK)
            g = _dot(h2, wgu_ref[:, cs])
            u = _dot(h2, wgu_ref[:, us])
            sg = jax.nn.sigmoid(g)
            si = g * sg
            act_ref[:, cs] = (si * u).astype(BF16)
            dact = _dot_nt(dfb, wd_ref[cs, :])
            dg = (dact * u * (sg * (1.0 + g * (1.0 - sg)))).astype(BF16)
            du = (dact * si).astype(BF16)
            dgu_ref[:, cs] = dg
            dgu_ref[:, us] = du
            dh2 = dh2 + _dot_nt(dg, wgu_ref[:, cs]) + _dot_nt(du, wgu_ref[:, us])
        dgpre_ref[...] += _colsum8(dh2 * n1)
        dx1_ref[...] = dx2 + _rms_bwd(dh2 * gpre_ref[...], n1, r1)

    return _rows_call("ffn_bwd", body, dx2.shape[0], tm, [dx2, f, x1], [g_pre, g_post, w_gu, w_down],
                      [(1024, F32), (D_FF, BF16), (2 * D_FF, BF16), (1024, BF16)], [(8, 1024), (8, 1024)])


def _mix_bwd(dx1, y, o_sb, o_mla, gl, b_gate, g_post, w_sb, w_mla, w_out, tm):
    def body(dx1_ref, y_ref, osb_ref, omla_ref, gl_ref, b_ref, gp_ref, wsb_ref, wmla_ref, wout_ref,
             dyb_ref, dpsb_ref, dpmla_ref, dgl_ref, dosb_ref, domla_ref, dgpost_ref, dbg_ref):
        @pl.when(pl.program_id(0) == 0)
        def _():
            dgpost_ref[...] = jnp.zeros_like(dgpost_ref)
            dbg_ref[...] = jnp.zeros_like(dbg_ref)

        dx1 = dx1_ref[...]
        ny, ry = _rms(y_ref[...])
        dgpost_ref[...] += _colsum8(dx1 * ny)
        dyb = _rms_bwd(dx1 * gp_ref[...], ny, ry).astype(BF16)
        dyb_ref[...] = dyb
        dm = _dot_nt(dyb, wout_ref[...])
        psb = _dot(osb_ref[...].astype(BF16), wsb_ref[...])
        pmla = _dot(omla_ref[...].astype(BF16), wmla_ref[...])
        gates = jax.nn.sigmoid(gl_ref[...] + b_ref[...])
        g0, g1 = gates[:, :D_MODEL], gates[:, D_MODEL:]
        dpsb = (dm * g0).astype(BF16)
        dpmla = (dm * g1).astype(BF16)
        dpsb_ref[...] = dpsb
        dpmla_ref[...] = dpmla
        dgl0 = dm * psb * g0 * (1.0 - g0)
        dgl1 = dm * pmla * g1 * (1.0 - g1)
        dgl_ref[:, :D_MODEL] = dgl0.astype(BF16)
        dgl_ref[:, D_MODEL:] = dgl1.astype(BF16)
        dbg_ref[:, :D_MODEL] += _colsum8(dgl0)
        dbg_ref[:, D_MODEL:] += _colsum8(dgl1)
        dosb_ref[...] = _dot_nt(dpsb, wsb_ref[...]).astype(BF16)
        domla_ref[...] = _dot_nt(dpmla, wmla_ref[...]).astype(BF16)

    return _rows_call("mix_bwd", body, dx1.shape[0], tm, [dx1, y, o_sb, o_mla, gl],
                      [b_gate, g_post, w_sb, w_mla, w_out],
                      [(1024, BF16), (1024, BF16), (1024, BF16), (2048, BF16), (512, BF16), (512, BF16)],
                      [(8, 1024), (8, 2048)])


def _mla_up_bwd(dq, dk, dv, cq, ckv, cr, sr, q_norm, kv_norm, wa, wk, wv, tm):
    def body(dq_ref, dk_ref, dv_ref, cq_ref, ckv_ref, cr_ref, sr_ref, qn_ref, kvn_ref, wa_ref, wk_ref, wv_ref,
             da_ref, dkb_ref, dvb_ref, dlat_ref, dqn_ref, dkvn_ref):
        @pl.when(pl.program_id(0) == 0)
        def _():
            dqn_ref[...] = jnp.zeros_like(dqn_ref)
            dkvn_ref[...] = jnp.zeros_like(dkvn_ref)

        cr = cr_ref[...]
        sr = sr_ref[...]
        lane = lax.broadcasted_iota(jnp.int32, cr.shape, 1)
        cm = cr + (lane < 64).astype(F32)
        nq, rq = _rms(cq_ref[...])
        nk, rk = _rms(ckv_ref[...])
        dcqn = jnp.zeros((tm, MLA_Q_RANK), F32)
        dckvn = jnp.zeros((tm, MLA_KV_RANK), F32)
        dkrp = jnp.zeros((tm, LANES), F32)
        for h in range(N_HEADS):
            hs = slice(h * LANES, (h + 1) * LANES)
            dqh = dq_ref[:, hs]
            da = (dqh * cm - _rot(dqh * sr)).astype(BF16)
            da_ref[:, hs] = da
            dcqn = dcqn + _dot_nt(da, wa_ref[:, hs])
            dkh = dk_ref[:, hs]
            dkb = dkh.astype(BF16)
            dkb_ref[:, hs] = dkb
            dckvn = dckvn + _dot_nt(dkb, wk_ref[:, hs])
            dkrp = dkrp + dkh
        dvb = dv_ref[...].astype(BF16)
        dvb_ref[...] = dvb
        dckvn = dckvn + _dot_nt(dvb, wv_ref[...])
        dkr = dkrp * cr - _rot(dkrp * sr)
        dqn_ref[...] += _colsum8(dcqn * nq)
        dkvn_ref[...] += _colsum8(dckvn * nk)
        dlat_ref[:, 0:384] = _rms_bwd(dcqn * qn_ref[...], nq, rq).astype(BF16)
        dlat_ref[:, 384:640] = _rms_bwd(dckvn * kvn_ref[...], nk, rk).astype(BF16)
        dlat_ref[:, 640:768] = dkr.astype(BF16)

    return _rows_call("mla_up_bwd", body, dq.shape[0], tm, [dq, dk, dv, cq, ckv, cr, sr],
                      [q_norm, kv_norm, wa, wk, wv],
                      [(1024, BF16), (1024, BF16), (512, BF16), (768, BF16)], [(8, 384), (8, 256)])


def _in_proj_bwd(x, dx1, dq_sb, dk_sb, dv_sb, dlat, dgl, g_pre, w_ext, tm):
    def body(x_ref, dx1_ref, dq_ref, dk_ref, dv_ref, dlat_ref, dgl_ref, g_ref, w_ref,
             dx_ref, dproj_ref, dg_ref):
        @pl.when(pl.program_id(0) == 0)
        def _():
            dg_ref[...] = jnp.zeros_like(dg_ref)

        dproj_ref[:, 0:512] = dq_ref[...].astype(BF16)
        dproj_ref[:, 512:1024] = dk_ref[...].astype(BF16)
        dproj_ref[:, 1024:1536] = dv_ref[...].astype(BF16)
        dproj_ref[:, EXT_CQ:EXT_GL] = dlat_ref[...]
        dproj_ref[:, EXT_GL:EXT_N] = dgl_ref[...]
        dh = jnp.zeros((tm, D_MODEL), F32)
        for c in range(0, EXT_N, 2176):
            dh = dh + _dot_nt(dproj_ref[:, c:c + 2176], w_ref[:, c:c + 2176])
        n, r = _rms(x_ref[...])
        dg_ref[...] += _colsum8(dh * n)
        dx_ref[...] = dx1_ref[...] + _rms_bwd(dh * g_ref[...], n, r)

    return _rows_call("in_proj_bwd", body, x.shape[0], tm, [x, dx1, dq_sb, dk_sb, dv_sb, dlat, dgl],
                      [g_pre, w_ext], [(1024, F32), (EXT_N, BF16)], [(8, 1024)])


def _head_masked(x):
    lane = lax.broadcasted_iota(jnp.int32, x.shape, 1)
    zero = jnp.zeros_like(x)
    return jnp.where(lane < 64, x, zero), jnp.where(lane >= 64, x, zero)


def _cum_weights():
    row = lax.broadcasted_iota(jnp.int32, (TK, TK), 0)
    col = lax.broadcasted_iota(jnp.int32, (TK, TK), 1)
    half = jnp.concatenate([(row > col).astype(BF16), jnp.ones((TK, TK), BF16)], axis=1)
    return jnp.concatenate([half, half], axis=0)


def _split_cat(x):
    hi, lo = _split(x)
    return jnp.concatenate([hi, lo], axis=1)


def _sweep(i, tiles, unroll):
    col = lax.broadcasted_iota(jnp.int32, (TQ, TK), 1)

    def diag(t, _):
        top = DIAG_TILES - 1 - t * unroll
        tiles(i * DIAG_TILES + top, [(top - u) * TK + col for u in range(unroll)])
        return 0

    lax.fori_loop(0, DIAG_TILES // unroll, diag, 0)

    def full(g, _):
        tiles(i * DIAG_TILES - 1 - g * unroll, [None] * unroll)
        return 0

    lax.fori_loop(0, (i * DIAG_TILES) // unroll, full, 0)


def _causal(key, strict):
    if key is None:
        return None
    row = lax.broadcasted_iota(jnp.int32, (TQ, TK), 0)
    return key < row if strict else key <= row


def _sb_logs(z, valid, w_cum):
    soft = jnp.log(1.0 + jnp.exp(-jnp.abs(z)))
    lsm = -jnp.maximum(z, 0.0) - soft
    cat = _split_cat(lsm if valid is None else jnp.where(valid, lsm, 0.0))
    return z + lsm, _dot(cat, w_cum)


def _sb_weight(log_beta, cs, r, valid):
    a = jnp.exp(log_beta + cs[:, :TK] + r)
    if valid is not None:
        a = jnp.where(valid, a, 0.0)
    return a, r + cs[:, TK:]


def _block_diag(x):
    x0, x1 = x[:, :LANES], x[:, LANES:]
    zero = jnp.zeros_like(x0)
    return jnp.concatenate([jnp.concatenate([x0, zero], axis=1), jnp.concatenate([zero, x1], axis=1)], axis=0)


def _attn_fwd(qkv, q, k, v):
    s_len = qkv.shape[0]

    def body(qs_ref, ks_ref, vs_ref, qm_ref, km_ref, vm_ref, osb_ref, omla_ref, lse_ref,
             sacc_ref, r_ref, macc_ref, m_ref):
        i = pl.program_id(1)
        lane = lax.broadcasted_iota(jnp.int32, (TQ, LANES), 1)
        r2 = lax.broadcasted_iota(jnp.int32, (2 * TK, LANES), 0)
        c2 = lax.broadcasted_iota(jnp.int32, (2 * TK, LANES), 1)
        head_ones = ((r2 < TK) == (c2 < 64)).astype(BF16)
        w_cum = _cum_weights()
        qs = qs_ref[...] * SB_SCALE
        qm = qm_ref[...]
        sacc_ref[...] = jnp.zeros_like(sacc_ref)
        r_ref[...] = jnp.zeros_like(r_ref)
        macc_ref[...] = jnp.zeros_like(macc_ref)
        m_ref[...] = jnp.full(m_ref.shape, NEG_BIG, F32)

        def sb_scores(top, valids):
            out = []
            for u, valid in enumerate(valids):
                off = pl.multiple_of((top - u) * TK, TK)
                z01 = _dot_nt(qs, jnp.concatenate(_head_masked(ks_ref[pl.ds(off, TK), :]), axis=0))
                out.append([_sb_logs(z, valid, w_cum) for z in (z01[:, :TK], z01[:, TK:])])
            return out

        def sb_accumulate(top, valids, scores):
            parts = [[None, None] for _ in valids]
            for hh in range(2):
                r = r_ref[hh]
                for u, valid in enumerate(valids):
                    a, r = _sb_weight(*scores[u][hh], r, valid)
                    parts[u][hh] = _split_cat(a)
                r_ref[hh] = r
            vs = []
            for u in range(len(valids)):
                off = pl.multiple_of((top - u) * TK, TK)
                v0, v1 = _head_masked(vs_ref[pl.ds(off, TK), :])
                vs += [v0, v0, v1, v1]
            sacc_ref[...] += _dot(jnp.concatenate([p for pair in parts for p in pair], axis=1),
                                  jnp.concatenate(vs, axis=0))

        def mla_scores(top, valids):
            s01s = []
            for u in range(len(valids)):
                off = pl.multiple_of((top - u) * TK, TK)
                s01s.append(_dot_nt(qm, _block_diag(km_ref[pl.ds(off, TK), :])))
            heads = []
            for hh in range(2):
                ss = []
                for u, valid in enumerate(valids):
                    s = s01s[u][:, hh * TK:(hh + 1) * TK] * MLA_SCALE
                    ss.append(s if valid is None else jnp.where(valid, s, NEG_BIG))
                m_old = m_ref[hh]
                m = jnp.maximum(m_old, jnp.max(functools.reduce(jnp.maximum, ss), axis=1, keepdims=True))
                m_ref[hh] = m
                heads.append((ss, m, jnp.exp(m_old - m)))
            return heads

        def mla_accumulate(top, heads):
            vs = []
            for u in range(len(heads[0][0])):
                off = pl.multiple_of((top - u) * TK, TK)
                vv = jnp.concatenate(_head_masked(vm_ref[pl.ds(off, TK), :]), axis=0)
                vs.append(jnp.concatenate([vv, head_ones], axis=1))
            ps = [[jnp.exp(s - m).astype(BF16) for s in ss] for ss, m, _ in heads]
            scale = jnp.where(lane < 64, heads[0][2], heads[1][2])
            p_all = jnp.concatenate([ps[hh][u] for u in range(len(vs)) for hh in range(2)], axis=1)
            macc_ref[...] = (macc_ref[...] * jnp.concatenate([scale, scale], axis=1)
                             + _dot(p_all, jnp.concatenate(vs, axis=0)))

        def tiles(top, keys):
            strict = [_causal(key, True) for key in keys]
            heads = mla_scores(top, [_causal(key, False) for key in keys])
            scores = sb_scores(top, strict)
            mla_accumulate(top, heads)
            sb_accumulate(top, strict, scores)

        _sweep(i, tiles, FWD_UNROLL)
        osb_ref[...] = sacc_ref[...]
        acc = macc_ref[...]
        den = acc[:, LANES:]
        omla_ref[...] = acc[:, :LANES] / den
        for hh, mask in enumerate((lane < 64, lane >= 64)):
            l = jnp.max(jnp.where(mask, den, 0.0), axis=1, keepdims=True)
            lse_ref[hh] = jnp.broadcast_to(m_ref[hh] + jnp.log(l), (TQ, LANES))

    tile = pl.BlockSpec((TQ, LANES), lambda h, i: (i, h))
    return pl.pallas_call(
        body, grid=(4, s_len // TQ),
        in_specs=[tile,
                  pl.BlockSpec((s_len, LANES), lambda h, i: (0, 4 + h)),
                  pl.BlockSpec((s_len, LANES), lambda h, i: (0, 8 + h)),
                  pl.BlockSpec((TQ, 2 * LANES), lambda h, i: (i, h)),
                  pl.BlockSpec((s_len, 2 * LANES), lambda h, i: (0, h)),
                  pl.BlockSpec((s_len, LANES), lambda h, i: (0, h))],
        out_specs=[tile, tile, pl.BlockSpec((2, TQ, LANES), lambda h, i: (h, i, 0))],
        out_shape=[jax.ShapeDtypeStruct((s_len, SB_WIDTH), F32), jax.ShapeDtypeStruct((s_len, 512), F32),
                   jax.ShapeDtypeStruct((N_HEADS, s_len, LANES), F32)],
        scratch_shapes=[pltpu.VMEM((TQ, LANES), F32), pltpu.VMEM((2, TQ, LANES), F32),
                        pltpu.VMEM((TQ, 2 * LANES), F32), pltpu.VMEM((2, TQ, 1), F32)],
        name="attn_fwd", compiler_params=_params(("arbitrary", "arbitrary")),
    )(qkv, qkv, qkv, q, k, v)


def _row_dots(do, o):
    prod = do.astype(F32) * o
    p0, p1 = _head_masked(prod)
    return tuple(jnp.broadcast_to(jnp.sum(p, axis=1, keepdims=True), prod.shape) for p in (p0, p1))


def _attn_bwd(qkv, do_sb, o_sb, q, k, v, do_mla, o_mla, lse, after):
    s_len = qkv.shape[0]
    n_q = s_len // TQ

    def body(qs_ref, ks_ref, vs_ref, dos_ref, os_ref, qm_ref, km_ref, vm_ref, dom_ref, om_ref, lse_ref, after_ref,
             dqs_ref, dqm_ref, dks_hbm, dvs_hbm, dkm_hbm, dvm_hbm,
             dqs_acc, dqm_acc, dks_acc, dvs_acc, dkm_acc, dvm_acc, r_ref, g_ref, ds_ref, dm_ref, out_sems):
        h = pl.program_id(0)
        i = pl.program_id(1)

        @pl.when(i == 0)
        def _():
            for acc in (dks_acc, dvs_acc, dkm_acc, dvm_acc):
                acc[...] = jnp.zeros_like(acc)

        w_cum = _cum_weights()
        qs = qs_ref[...] * SB_SCALE
        dos = dos_ref[...]
        qs_rows = jnp.concatenate(_head_masked(qs), axis=0)
        dos_rows = jnp.concatenate(_head_masked(dos), axis=0)
        ds_ref[0], ds_ref[1] = _row_dots(dos, os_ref[...])
        qm = qm_ref[...]
        dom = dom_ref[...]
        qm_diag = _block_diag(qm)
        dom_rows = jnp.concatenate(_head_masked(dom), axis=0)
        dm_ref[0], dm_ref[1] = _row_dots(dom, om_ref[...])
        for ref in (dqs_acc, dqm_acc, r_ref, g_ref):
            ref[...] = jnp.zeros_like(ref)
        heads = (slice(0, TK), slice(TK, 2 * TK))

        def weigh(log_beta, cs, r, da, valid):
            a, r = _sb_weight(log_beta, cs, r, valid)
            g = a * da
            return a.astype(BF16), g, _split_cat(g), r

        def logit_grad(g, gs, carried, d, log_beta, valid):
            upto = d - (gs[:, :TK] + carried)
            dz = g - jnp.exp(log_beta) * upto
            if valid is not None:
                dz = jnp.where(valid, dz, 0.0)
            return dz.astype(BF16), carried + gs[:, TK:]

        def tiles(top, keys):
            n = len(keys)
            strict = [_causal(key, True) for key in keys]
            loose = [_causal(key, False) for key in keys]
            offs = [pl.multiple_of((top - u) * TK, TK) for u in range(n)]
            kds = [_block_diag(km_ref[pl.ds(off, TK), :]) for off in offs]
            vms = [jnp.concatenate(_head_masked(vm_ref[pl.ds(off, TK), :]), axis=0) for off in offs]
            s01s = [_dot_nt(qm, kd) for kd in kds]
            dp01s = [_dot_nt(dom, vv) for vv in vms]
            kks = [jnp.concatenate(_head_masked(ks_ref[pl.ds(off, TK), :]), axis=0) for off in offs]
            vvs = [jnp.concatenate(_head_masked(vs_ref[pl.ds(off, TK), :]), axis=0) for off in offs]
            z01s = [_dot_nt(qs, kk) for kk in kks]
            da01s = [_dot_nt(dos, vv) for vv in vvs]
            logs = [[_sb_logs(z01s[u][:, hs], strict[u], w_cum) for hs in heads] for u in range(n)]

            dss = [[None, None] for _ in range(n)]
            ps = [[None, None] for _ in range(n)]
            for u in range(n):
                for hh, hs in enumerate(heads):
                    p = jnp.exp(s01s[u][:, hs] * MLA_SCALE - lse_ref[hh])
                    if loose[u] is not None:
                        p = jnp.where(loose[u], p, 0.0)
                    dss[u][hh] = (p * (dp01s[u][:, hs] - dm_ref[hh]) * MLA_SCALE).astype(BF16)
                    ps[u][hh] = p.astype(BF16)

            dzs = [[None, None] for _ in range(n)]
            avs = [[None, None] for _ in range(n)]
            sums = [[None, None] for _ in range(n)]
            for hh, hs in enumerate(heads):
                r = r_ref[hh]
                for u in range(n):
                    avs[u][hh], g, cat, r = weigh(*logs[u][hh], r, da01s[u][:, hs], strict[u])
                    sums[u][hh] = (g, _dot(cat, w_cum))
                r_ref[hh] = r

            dqm_acc[...] += _dot(jnp.concatenate([d for pair in dss for d in pair], axis=1),
                                 jnp.concatenate(kds, axis=0))
            span = pl.ds(offs[-1], n * TK)
            by_key = lambda pairs: jnp.concatenate([jnp.concatenate(pair, axis=0) for pair in pairs[::-1]], axis=1)
            dkm_acc[span, :] += _dot_tn(by_key(dss), qm_diag)
            dvm_acc[span, :] += _dot_tn(by_key(ps), dom_rows)

            for hh in range(2):
                carried = g_ref[hh]
                for u in range(n):
                    dzs[u][hh], carried = logit_grad(*sums[u][hh], carried, ds_ref[hh], logs[u][hh][0], strict[u])
                g_ref[hh] = carried
            dqs_acc[...] += _dot(jnp.concatenate([dz for pair in dzs for dz in pair], axis=1),
                                 jnp.concatenate(kks, axis=0))
            dks_acc[span, :] += _dot_tn(by_key(dzs), qs_rows)
            dvs_acc[span, :] += _dot_tn(by_key(avs), dos_rows)

        _sweep(i, tiles, BWD_UNROLL)
        dqs_ref[...] = dqs_acc[...] * SB_SCALE
        dqm_ref[...] = dqm_acc[...]

        @pl.when(i == n_q - 1)
        def _():
            narrow = pl.ds(pl.multiple_of(h * LANES, LANES), LANES)
            wide = pl.ds(pl.multiple_of(h * 2 * LANES, 2 * LANES), 2 * LANES)
            copies = [pltpu.make_async_copy(dks_acc, dks_hbm.at[:, narrow], out_sems.at[0]),
                      pltpu.make_async_copy(dvs_acc, dvs_hbm.at[:, narrow], out_sems.at[1]),
                      pltpu.make_async_copy(dkm_acc, dkm_hbm.at[:, wide], out_sems.at[2]),
                      pltpu.make_async_copy(dvm_acc, dvm_hbm.at[:, narrow], out_sems.at[3])]
            for cp in copies:
                cp.start()
            for cp in copies:
                cp.wait()

    tile = pl.BlockSpec((TQ, LANES), lambda h, i: (i, h))
    wide_tile = pl.BlockSpec((TQ, 2 * LANES), lambda h, i: (i, h))
    once = pl.Buffered(1)
    hbm = pl.BlockSpec(memory_space=pl.ANY)
    return pl.pallas_call(
        body, grid=(4, n_q),
        in_specs=[tile,
                  pl.BlockSpec((s_len, LANES), lambda h, i: (0, 4 + h), pipeline_mode=once),
                  pl.BlockSpec((s_len, LANES), lambda h, i: (0, 8 + h), pipeline_mode=once),
                  tile, tile, wide_tile,
                  pl.BlockSpec((s_len, 2 * LANES), lambda h, i: (0, h), pipeline_mode=once),
                  pl.BlockSpec((s_len, LANES), lambda h, i: (0, h), pipeline_mode=once),
                  tile, tile, pl.BlockSpec((2, TQ, LANES), lambda h, i: (h, i, 0)),
                  pl.BlockSpec((8, LANES), lambda h, i: (0, 0))],
        out_specs=[tile, wide_tile, hbm, hbm, hbm, hbm],
        out_shape=[jax.ShapeDtypeStruct((s_len, SB_WIDTH), F32), jax.ShapeDtypeStruct((s_len, 1024), F32),
                   jax.ShapeDtypeStruct((s_len, SB_WIDTH), F32), jax.ShapeDtypeStruct((s_len, SB_WIDTH), F32),
                   jax.ShapeDtypeStruct((s_len, 1024), F32), jax.ShapeDtypeStruct((s_len, 512), F32)],
        scratch_shapes=[pltpu.VMEM((TQ, LANES), F32), pltpu.VMEM((TQ, 2 * LANES), F32),
                        pltpu.VMEM((s_len, LANES), F32), pltpu.VMEM((s_len, LANES), F32),
                        pltpu.VMEM((s_len, 2 * LANES), F32), pltpu.VMEM((s_len, LANES), F32),
                        pltpu.VMEM((2, TQ, LANES), F32), pltpu.VMEM((2, TQ, LANES), F32),
                        pltpu.VMEM((2, TQ, LANES), F32), pltpu.VMEM((2, TQ, LANES), F32),
                        pltpu.SemaphoreType.DMA((4,))],
        name="attn_bwd", compiler_params=_params(("arbitrary", "arbitrary")),
    )(qkv, qkv, qkv, do_sb, o_sb, q, k, v, do_mla, o_mla, lse, after)


def _pick(n, options):
    for t in options:
        if n % t == 0:
            return t
    raise ValueError(n)


def _matmul_tn(name, a, b):
    s_len, m = a.shape
    n = b.shape[1]
    tm = _pick(m, (1024, 1408, 2176, 512))
    tn = _pick(n, (1024, 512, 384, 256))
    tk = _pick(s_len, (512, 256, 128))

    def body(a_ref, b_ref, o_ref):
        @pl.when(pl.program_id(2) == 0)
        def _():
            o_ref[...] = jnp.zeros_like(o_ref)

        o_ref[...] += _dot_tn(a_ref[...], b_ref[...])

    return pl.pallas_call(
        body, grid=(m // tm, n // tn, s_len // tk),
        in_specs=[pl.BlockSpec((tk, tm), lambda i, j, l: (l, i)), pl.BlockSpec((tk, tn), lambda i, j, l: (l, j))],
        out_specs=pl.BlockSpec((tm, tn), lambda i, j, l: (i, j)),
        out_shape=jax.ShapeDtypeStruct((m, n), F32),
        name=name, compiler_params=_params(("arbitrary", "arbitrary", "arbitrary")),
    )(a, b)


def _mesh_pos():
    return lax.axis_index("x"), lax.axis_index("y"), lax.axis_index("c")


def _peer(pos, k):
    x, y, c = pos
    return (1 - x if k & 4 else x, 1 - y if k & 2 else y, 1 - c if k & 1 else c)


def _flat_index(pos):
    return 4 * pos[0] + 2 * pos[1] + pos[2]


def _all_gather(shard):
    rows = shard.shape[0]

    def body(x_ref, out_ref, send_sems, recv_sems, local_sem):
        me = _mesh_pos()
        x, y, c = me
        sibling = (x, y, 1 - c)
        chips = [(1 - x, y), (x, 1 - y), (1 - x, 1 - y)]

        def copy(k, block, to, src=None):
            slot = out_ref.at[_flat_index(block)]
            return pltpu.make_async_remote_copy(
                src_ref=slot if src is None else src, dst_ref=slot,
                send_sem=send_sems.at[k], recv_sem=recv_sems.at[k],
                device_id=to, device_id_type=pl.DeviceIdType.MESH)

        mine = pltpu.make_async_copy(x_ref, out_ref.at[_flat_index(me)], local_sem)
        mine.start()
        first = [copy(0, me, sibling, src=x_ref)]
        first += [copy(1 + j, me, (*chip, c), src=x_ref) for j, chip in enumerate(chips)]
        for cp in first:
            cp.start()
        passed = [copy(4 + j, (*chip, c), sibling) for j, chip in enumerate(chips)]
        for j, chip in enumerate(chips):
            copy(1 + j, (*chip, c), me).wait_recv()
            passed[j].start()
        copy(0, sibling, me).wait_recv()
        for j, chip in enumerate(chips):
            copy(4 + j, (*chip, 1 - c), me).wait_recv()
        for cp in first + passed:
            cp.wait_send()
        mine.wait()

    return pl.pallas_call(
        body, out_shape=jax.ShapeDtypeStruct((N_DEV, rows, LANES), shard.dtype),
        in_specs=[pl.BlockSpec(memory_space=pl.ANY)], out_specs=pl.BlockSpec(memory_space=pl.ANY),
        scratch_shapes=[pltpu.SemaphoreType.DMA((7,)), pltpu.SemaphoreType.DMA((7,)), pltpu.SemaphoreType.DMA],
        name="weights_all_gather",
    )(shard)


def _grad_exchange(big, small):
    def body(big_ref, small_ref, big_out, small_out, bsend, brecv, ssend, srecv, local_sems):
        me = _mesh_pos()
        mine = _flat_index(me)
        loc = [pltpu.make_async_copy(big_ref.at[mine], big_out.at[mine], local_sems.at[0]),
               pltpu.make_async_copy(small_ref, small_out.at[mine], local_sems.at[1])]
        for cp in loc:
            cp.start()

        def copies(k):
            peer = _peer(me, k)
            theirs = _flat_index(peer)
            send = (pltpu.make_async_remote_copy(
                        src_ref=big_ref.at[theirs], dst_ref=big_out.at[mine], send_sem=bsend.at[k - 1],
                        recv_sem=brecv.at[k - 1], device_id=peer, device_id_type=pl.DeviceIdType.MESH),
                    pltpu.make_async_remote_copy(
                        src_ref=small_ref, dst_ref=small_out.at[mine], send_sem=ssend.at[k - 1],
                        recv_sem=srecv.at[k - 1], device_id=peer, device_id_type=pl.DeviceIdType.MESH))
            recv = (pltpu.make_async_remote_copy(
                        src_ref=big_ref.at[mine], dst_ref=big_out.at[theirs], send_sem=bsend.at[k - 1],
                        recv_sem=brecv.at[k - 1], device_id=me, device_id_type=pl.DeviceIdType.MESH),
                    pltpu.make_async_remote_copy(
                        src_ref=small_ref, dst_ref=small_out.at[theirs], send_sem=ssend.at[k - 1],
                        recv_sem=srecv.at[k - 1], device_id=me, device_id_type=pl.DeviceIdType.MESH))
            return send, recv

        plan = [copies(k) for k in range(1, N_DEV)]
        for send, _ in plan:
            for cp in send:
                cp.start()
        for _, recv in plan:
            for cp in recv:
                cp.wait_recv()
        for send, _ in plan:
            for cp in send:
                cp.wait_send()
        for cp in loc:
            cp.wait()

    any_spec = pl.BlockSpec(memory_space=pl.ANY)
    return pl.pallas_call(
        body,
        out_shape=[jax.ShapeDtypeStruct(big.shape, big.dtype),
                   jax.ShapeDtypeStruct((N_DEV,) + small.shape, small.dtype)],
        in_specs=[any_spec, any_spec], out_specs=[any_spec, any_spec],
        scratch_shapes=[pltpu.SemaphoreType.DMA((7,)), pltpu.SemaphoreType.DMA((7,)),
                        pltpu.SemaphoreType.DMA((7,)), pltpu.SemaphoreType.DMA((7,)),
                        pltpu.SemaphoreType.DMA((2,))],
        name="grad_exchange",
    )(big, small)


def _push_start(name, src, per_peer):
    rows = src.shape[-2]

    def body(src_ref, land_ref, send_sems, recv_sems, src_thru, land_thru, token):
        me = _mesh_pos()
        mine = _flat_index(me)
        for k in range(1, N_DEV):
            peer = _peer(me, k)
            pltpu.make_async_remote_copy(
                src_ref=src_ref.at[_flat_index(peer)] if per_peer else src_ref, dst_ref=land_ref.at[mine],
                send_sem=send_sems.at[k - 1], recv_sem=recv_sems.at[k - 1],
                device_id=peer, device_id_type=pl.DeviceIdType.MESH).start()
        token[...] = jnp.zeros_like(token)

    hbm = pl.BlockSpec(memory_space=pltpu.HBM)
    sem = pl.BlockSpec(memory_space=pltpu.SEMAPHORE)
    land = lax.empty((N_DEV, rows, LANES), src.dtype)
    return pl.pallas_call(
        body, name=name,
        out_shape=(pltpu.SemaphoreType.DMA((N_DEV - 1,)), pltpu.SemaphoreType.DMA((N_DEV - 1,)),
                   pltpu.HBM(src.shape, src.dtype), pltpu.HBM(land.shape, land.dtype),
                   jax.ShapeDtypeStruct((8, LANES), F32)),
        in_specs=(hbm, hbm), out_specs=(sem, sem, hbm, hbm, pl.BlockSpec(memory_space=pltpu.VMEM)),
        input_output_aliases={0: 2, 1: 3},
        compiler_params=pltpu.CompilerParams(has_side_effects=pltpu.SideEffectType.DATAFLOW_SIDE_EFFECTING),
    )(pltpu.with_memory_space_constraint(src, pltpu.HBM), pltpu.with_memory_space_constraint(land, pltpu.HBM))


def _push_wait(name, started, per_peer, after):
    send_sems, recv_sems, src_thru, land_thru, _ = started

    def body(src_ref, land_ref, send_sems, recv_sems, after_ref, src_out, land_out):
        me = _mesh_pos()
        for k in range(1, N_DEV):
            theirs = _flat_index(_peer(me, k))
            copy = pltpu.make_async_remote_copy(
                src_ref=src_ref.at[theirs] if per_peer else src_ref, dst_ref=land_ref.at[theirs],
                send_sem=send_sems.at[k - 1], recv_sem=recv_sems.at[k - 1],
                device_id=me, device_id_type=pl.DeviceIdType.MESH)
            copy.wait_send()
            copy.wait_recv()

    hbm = pl.BlockSpec(memory_space=pltpu.HBM)
    sem = pl.BlockSpec(memory_space=pltpu.SEMAPHORE)
    return pl.pallas_call(
        body, name=name,
        out_shape=(pltpu.HBM(src_thru.shape, src_thru.dtype), pltpu.HBM(land_thru.shape, land_thru.dtype)),
        in_specs=(hbm, hbm, sem, sem, pl.BlockSpec(memory_space=pl.ANY)), out_specs=(hbm, hbm),
        input_output_aliases={0: 0, 1: 1},
        compiler_params=pltpu.CompilerParams(has_side_effects=pltpu.SideEffectType.DATAFLOW_SIDE_EFFECTING),
    )(src_thru, land_thru, send_sems, recv_sems, after)


def _slot_sum(name, slots, tr):
    rows = slots.shape[1]

    def body(s_ref, o_ref):
        acc = s_ref[0].astype(F32)
        for d in range(1, N_DEV):
            acc = acc + s_ref[d].astype(F32)
        o_ref[...] = acc

    return pl.pallas_call(
        body, grid=(rows // tr,),
        in_specs=[pl.BlockSpec((N_DEV, tr, LANES), lambda i: (0, i, 0))],
        out_specs=pl.BlockSpec((tr, LANES), lambda i: (i, 0)),
        out_shape=jax.ShapeDtypeStruct((rows, LANES), F32),
        name=name, compiler_params=_params(("arbitrary",)),
    )(slots)


def _adamw(name, w, g, m, v):
    rows, cols = w.shape
    tr = _pick(rows, (256, 128, 88, 32, 1))
    c1 = 1.0 - ADAM_B1 ** ADAM_STEP
    c2 = 1.0 - ADAM_B2 ** ADAM_STEP

    def body(w_ref, g_ref, m_ref, v_ref, d_ref, nm_ref, nv_ref):
        g = g_ref[...]
        nm = ADAM_B1 * m_ref[...] + (1.0 - ADAM_B1) * g
        nv = ADAM_B2 * v_ref[...] + (1.0 - ADAM_B2) * (g * g)
        nm_ref[...] = nm
        nv_ref[...] = nv
        d_ref[...] = -ADAM_LR * ((nm / c1) / (jnp.sqrt(nv / c2) + ADAM_EPS) + ADAM_WD * w_ref[...])

    spec = pl.BlockSpec((tr, cols), lambda i: (i, 0))
    return pl.pallas_call(
        body, grid=(rows // tr,), in_specs=[spec] * 4, out_specs=[spec] * 3,
        out_shape=[jax.ShapeDtypeStruct((rows, cols), F32)] * 3,
        name=name, compiler_params=_params(("arbitrary",)),
    )(w, g, m, v)


def _pack_shards(shards, group, rows):
    parts = []
    for name, _, axis in group:
        w = shards[name].astype(BF16)
        parts.append((w.T if axis == 1 else w).reshape(-1))
    flat = jnp.concatenate(parts)
    flat = jnp.pad(flat, (0, rows * LANES - flat.shape[0]))
    return flat.reshape(rows, LANES)


def _unpack_full(gathered, group):
    flat = gathered.reshape(N_DEV, -1)
    out, off = {}, 0
    for name, (r, c), axis in group:
        rr, cc = (c, r) if axis == 1 else (r, c)
        out[name] = flat[:, off:off + r * c].reshape(N_DEV * rr, cc)
        off += r * c
    return out


def _pack_full_grads(grads, group, rows):
    parts = [grads[name].reshape(N_DEV, r * c) for name, (r, c), _ in group]
    flat = jnp.concatenate(parts, axis=1).astype(BF16)
    flat = jnp.pad(flat, ((0, 0), (0, rows * LANES - flat.shape[1])))
    return flat.reshape(N_DEV, rows, LANES)


def _unpack_shard_grads(flat, group):
    flat = flat.reshape(-1)
    out, off = {}, 0
    for name, (r, c), axis in group:
        seg = flat[off:off + r * c]
        out[name] = seg.reshape(c, r).T if axis == 1 else seg.reshape(r, c)
        off += r * c
    return out


def _own_slot(slots, own):
    mine = _flat_index(_mesh_pos())
    return lax.dynamic_update_slice(slots, own[None], (mine, 0, 0))


def _rope_tables(positions):
    inv_freq = ROPE_THETA ** (-jnp.arange(0, MLA_ROPE_DIM, 2, dtype=F32) / MLA_ROPE_DIM)
    ang = positions.astype(F32)[:, None] * inv_freq
    z64 = jnp.zeros((positions.shape[0], 64), F32)
    z32 = jnp.zeros((positions.shape[0], 32), F32)
    cos, sin = jnp.cos(ang), jnp.sin(ang)
    return (jnp.concatenate([z64, cos, cos, z32], axis=1), jnp.concatenate([z64, sin, sin, z32], axis=1))


def _row_tile(s_len, want):
    return _pick(s_len, (want, 256, 128))


def kernel(x, positions, norm_mix_pre, norm_mix_post, w_in, b_gate, q_norm, w_uq, kv_norm, w_ukv, w_proj_sb, w_proj_mla, w_out, norm_ffn_pre, norm_ffn_post, w_gate_up, w_down, loss_target, m_norm_mix_pre, m_norm_mix_post, m_w_in, m_b_gate, m_q_norm, m_w_uq, m_kv_norm, m_w_ukv, m_w_proj_sb, m_w_proj_mla, m_w_out, m_norm_ffn_pre, m_norm_ffn_post, m_w_gate_up, m_w_down, v_norm_mix_pre, v_norm_mix_post, v_w_in, v_b_gate, v_q_norm, v_w_uq, v_kv_norm, v_w_ukv, v_w_proj_sb, v_w_proj_mla, v_w_out, v_norm_ffn_pre, v_norm_ffn_post, v_w_gate_up, v_w_down):
    weights = dict(norm_mix_pre=norm_mix_pre, norm_mix_post=norm_mix_post, w_in=w_in, b_gate=b_gate, q_norm=q_norm,
                   w_uq=w_uq, kv_norm=kv_norm, w_ukv=w_ukv, w_proj_sb=w_proj_sb, w_proj_mla=w_proj_mla, w_out=w_out,
                   norm_ffn_pre=norm_ffn_pre, norm_ffn_post=norm_ffn_post, w_gate_up=w_gate_up, w_down=w_down)
    m_in = dict(norm_mix_pre=m_norm_mix_pre, norm_mix_post=m_norm_mix_post, w_in=m_w_in, b_gate=m_b_gate,
                q_norm=m_q_norm, w_uq=m_w_uq, kv_norm=m_kv_norm, w_ukv=m_w_ukv, w_proj_sb=m_w_proj_sb,
                w_proj_mla=m_w_proj_mla, w_out=m_w_out, norm_ffn_pre=m_norm_ffn_pre, norm_ffn_post=m_norm_ffn_post,
                w_gate_up=m_w_gate_up, w_down=m_w_down)
    v_in = dict(norm_mix_pre=v_norm_mix_pre, norm_mix_post=v_norm_mix_post, w_in=v_w_in, b_gate=v_b_gate,
                q_norm=v_q_norm, w_uq=v_w_uq, kv_norm=v_kv_norm, w_ukv=v_w_ukv, w_proj_sb=v_w_proj_sb,
                w_proj_mla=v_w_proj_mla, w_out=v_w_out, norm_ffn_pre=v_norm_ffn_pre, norm_ffn_post=v_norm_ffn_post,
                w_gate_up=v_w_gate_up, w_down=v_w_down)
    order = list(weights)

    xs = x[0]
    target = loss_target[0]
    s_len = xs.shape[0]
    tm = _row_tile(s_len, 256)
    tm_ffn = _row_tile(s_len, 256)

    shards = {name: weights[name][0] for name, _, _ in SHARDED}
    late_shard = _pack_shards(shards, LATE, LATE_ROWS)
    late_weights = _push_start("weights_late_start", late_shard, False)
    full = _unpack_full(_all_gather(_pack_shards(shards, EARLY, EARLY_ROWS)), EARLY)
    wt = full["w_in"]
    zr = lambda n: jnp.zeros((n, D_MODEL), BF16)
    w_ext = jnp.concatenate([wt[:2176], zr(64), wt[2176:2208], zr(32), wt[2208:]], axis=0).T
    wa = jnp.pad(full["w_uq"].reshape(N_HEADS, MLA_QK_DIM, MLA_Q_RANK), ((0, 0), (0, 32), (0, 0))
                 ).reshape(N_HEADS * LANES, MLA_Q_RANK).T
    ukv = full["w_ukv"].reshape(N_HEADS, LANES, MLA_KV_RANK)
    wk = jnp.pad(ukv[:, :64], ((0, 0), (0, 64), (0, 0))).reshape(N_HEADS * LANES, MLA_KV_RANK).T
    wv = ukv[:, 64:].reshape(512, MLA_KV_RANK).T
    g_mix_pre = norm_mix_pre + late_weights[4][0:1, 0:1]
    cr, sr = _rope_tables(positions[0])

    qkv, cq, ckv, kr, gl, hb = _in_proj(xs, g_mix_pre, w_ext, tm)
    q_mla, k_mla, v_mla, cqn, ckvn = _mla_up(cq, ckv, kr, cr, sr, q_norm, kv_norm, wa, wk, wv, tm)
    o_sb, o_mla, lse = _attn_fwd(qkv, q_mla, k_mla, v_mla)
    late_shard, late_slots = _push_wait("weights_late_wait", late_weights, False, o_sb)
    full = _unpack_full(_own_slot(late_slots, late_shard), LATE)
    w_sb, w_mla, w_o, w_gu, w_dn = (full["w_proj_sb"].T, full["w_proj_mla"].T, full["w_out"], full["w_gate_up"].T,
                                    full["w_down"])
    x1, y, merged = _mix_out(o_sb, o_mla, gl, xs, b_gate, norm_mix_post, w_sb, w_mla, w_o, tm)
    dx2, f, h2, loss_part = _ffn_fwd(x1, target, norm_ffn_pre, norm_ffn_post, w_gu, w_dn, tm_ffn)
    loss_local = (0.5 / D_MODEL * jnp.sum(loss_part)).reshape(1)

    dx1, act, dgu, dfb, dg_ffn_pre, dg_ffn_post = _ffn_bwd(dx2, f, x1, norm_ffn_pre, norm_ffn_post, w_gu, w_dn, tm_ffn)
    dyb, dpsb, dpmla, dgl, do_sb, do_mla, dg_mix_post, db_gate = _mix_bwd(
        dx1, y, o_sb, o_mla, gl, b_gate, norm_mix_post, w_sb, w_mla, w_o, tm)
    late_grads = _pack_full_grads({
        "w_proj_sb": _matmul_tn("dw_proj_sb", dpsb, o_sb.astype(BF16)),
        "w_proj_mla": _matmul_tn("dw_proj_mla", dpmla, o_mla.astype(BF16)),
        "w_out": _matmul_tn("dw_out", merged, dyb),
        "w_gate_up": _matmul_tn("dw_gate_up", dgu, h2),
        "w_down": _matmul_tn("dw_down", act, dfb),
    }, LATE, LATE_ROWS)
    late_exchange = _push_start("grads_late_start", late_grads, True)
    dq_sb, dq_mla, dk_sb, dv_sb, dk_mla, dv_mla = _attn_bwd(qkv, do_sb, o_sb, q_mla, k_mla, v_mla, do_mla, o_mla, lse,
                                                             late_exchange[4])
    da, dkb, dvb, dlat, dg_q, dg_kv = _mla_up_bwd(dq_mla, dk_mla, dv_mla, cq, ckv, cr, sr, q_norm, kv_norm,
                                                  wa, wk, wv, tm)
    dx, dproj, dg_mix_pre = _in_proj_bwd(xs, dx1, dq_sb, dk_sb, dv_sb, dlat, dgl, norm_mix_pre, w_ext, tm)
    d_ext = _matmul_tn("dw_in", dproj, hb)
    d_wa = _matmul_tn("dw_uq", da, cqn)
    d_wk = _matmul_tn("dw_uk", dkb, ckvn)
    d_wv = _matmul_tn("dw_uv", dvb, ckvn)
    early_grads = _pack_full_grads({
        "w_in": jnp.concatenate([d_ext[:2176], d_ext[2240:2272], d_ext[EXT_GL:]], axis=0),
        "w_uq": d_wa.reshape(N_HEADS, LANES, MLA_Q_RANK)[:, :MLA_QK_DIM].reshape(768, MLA_Q_RANK),
        "w_ukv": jnp.concatenate([d_wk.reshape(N_HEADS, LANES, MLA_KV_RANK)[:, :64],
                                  d_wv.reshape(N_HEADS, 64, MLA_KV_RANK)], axis=1).reshape(1024, MLA_KV_RANK),
    }, EARLY, EARLY_ROWS)
    small_parts = dict(norm_mix_pre=dg_mix_pre, norm_mix_post=dg_mix_post, b_gate=db_gate, q_norm=dg_q,
                       kv_norm=dg_kv, norm_ffn_pre=dg_ffn_pre, norm_ffn_post=dg_ffn_post)
    small = jnp.concatenate([small_parts[name].sum(axis=0) for name, _ in SMALL] + [loss_local])
    small = jnp.pad(small, (0, SMALL_ROWS * LANES - small.shape[0])).reshape(SMALL_ROWS, LANES)

    early_slots, small_slots = _grad_exchange(early_grads, small)
    late_grads, late_slots = _push_wait("grads_late_wait", late_exchange, True, early_slots)
    mine = _flat_index(_mesh_pos())
    late_slots = _own_slot(late_slots, lax.dynamic_index_in_dim(late_grads, mine, 0, keepdims=False))
    g_out = _unpack_shard_grads(_slot_sum("grad_early_sum", early_slots, 960), EARLY)
    g_out.update(_unpack_shard_grads(_slot_sum("grad_late_sum", late_slots, 512), LATE))
    s_flat = _slot_sum("grad_small_sum", small_slots, SMALL_ROWS).reshape(-1)
    off = 0
    for name, n in SMALL:
        g_out[name] = s_flat[off:off + n].reshape(1, n)
        off += n
    loss = s_flat[off]

    deltas, new_m, new_v = {}, {}, {}
    for name in order:
        w2 = weights[name].reshape(g_out[name].shape)
        d, nm, nv = _adamw("adamw_" + name, w2, g_out[name], m_in[name].reshape(w2.shape), v_in[name].reshape(w2.shape))
        shape = weights[name].shape
        deltas[name], new_m[name], new_v[name] = d.reshape(shape), nm.reshape(shape), nv.reshape(shape)
        g_out[name] = g_out[name].reshape(shape)

    return (loss, dx[None], *[g_out[n] for n in order], *[deltas[n] for n in order],
            *[new_m[n] for n in order], *[new_v[n] for n in order])
```

```python
import functools
import math

import jax
import jax.numpy as jnp
from jax import lax
from jax.experimental import pallas as pl
from jax.experimental.pallas import tpu as pltpu

F32 = jnp.float32
BF16 = jnp.bfloat16

D_MODEL = 1024
N_HEADS = 8
SB_WIDTH = 512
MLA_Q_RANK = 384
MLA_KV_RANK = 256
MLA_ROPE_DIM = 32
MLA_QK_DIM = 96
D_FF = 2816
ROPE_THETA = 10000.0
EPS = 1e-6
SB_SCALE = 1.0 / math.sqrt(64.0)
MLA_SCALE = 1.0 / math.sqrt(96.0)
NEG_BIG = -1e30

ADAM_LR = 0.001
ADAM_B1 = 0.9
ADAM_B2 = 0.999
ADAM_EPS = 1e-08
ADAM_WD = 0.01
ADAM_STEP = 10

N_DEV = 8
LANES = 128
TQ = 512
TK = 128
DIAG_TILES = TQ // TK
FWD_UNROLL = 4
BWD_UNROLL = 2
VMEM_LIMIT = 56 << 20

EXT_QKV = 0
EXT_CQ = 1536
EXT_CKV = 1920
EXT_KR = 2176
EXT_GL = 2304
EXT_N = 4352

EARLY = (("w_in", (1024, 532), 1), ("w_uq", (384, 96), 1), ("w_ukv", (256, 128), 1))
LATE = (("w_proj_sb", (512, 128), 1), ("w_proj_mla", (512, 128), 1), ("w_out", (128, 1024), 0),
        ("w_gate_up", (1024, 704), 1), ("w_down", (352, 1024), 0))
SHARDED = EARLY + LATE
EARLY_ROWS = 4800
LATE_ROWS = 10752
SMALL = (("norm_mix_pre", 1024), ("norm_mix_post", 1024), ("b_gate", 2048), ("q_norm", 384),
         ("kv_norm", 256), ("norm_ffn_pre", 1024), ("norm_ffn_post", 1024))
SMALL_ROWS = 56


def _dot(a, b):
    return jnp.dot(a, b, preferred_element_type=F32)


def _dot_nt(a, b):
    return lax.dot_general(a, b, (((1,), (1,)), ((), ())), preferred_element_type=F32)


def _dot_tn(a, b):
    return lax.dot_general(a, b, (((0,), (0,)), ((), ())), preferred_element_type=F32)


def _rms(x):
    r = lax.rsqrt(jnp.mean(x * x, axis=-1, keepdims=True) + EPS)
    return x * r, r


def _rms_bwd(dn, n, r):
    return r * (dn - n * jnp.mean(dn * n, axis=-1, keepdims=True))


def _colsum8(x):
    return jnp.sum(x.reshape(x.shape[0] // 8, 8, x.shape[1]), axis=0)


def _split(x):
    hi = x.astype(BF16)
    return hi, (x - hi.astype(F32)).astype(BF16)


def _rot(x):
    lane = lax.broadcasted_iota(jnp.int32, x.shape, 1)
    up = pltpu.roll(x, 112, 1)
    down = pltpu.roll(x, 16, 1)
    return jnp.where((lane >= 64) & (lane < 80), -up, jnp.where((lane >= 80) & (lane < 96), down, 0.0))


def _params(sem):
    return pltpu.CompilerParams(dimension_semantics=sem, vmem_limit_bytes=VMEM_LIMIT)


def _rows_call(name, body, n_rows, tm, row_ins, const_ins, row_outs, acc_outs):
    in_specs = [pl.BlockSpec((tm, a.shape[1]), lambda i: (i, 0)) for a in row_ins]
    in_specs += [pl.BlockSpec(a.shape, lambda i: (0, 0), pipeline_mode=pl.Buffered(1)) for a in const_ins]
    out_specs = [pl.BlockSpec((tm, n), lambda i: (i, 0)) for n, _ in row_outs]
    out_specs += [pl.BlockSpec(s, lambda i: (0, 0)) for s in acc_outs]
    out_shape = [jax.ShapeDtypeStruct((n_rows, n), dt) for n, dt in row_outs]
    out_shape += [jax.ShapeDtypeStruct(s, F32) for s in acc_outs]
    return pl.pallas_call(
        body, grid=(n_rows // tm,), in_specs=in_specs, out_specs=out_specs, out_shape=out_shape,
        name=name, compiler_params=_params(("arbitrary",)),
    )(*row_ins, *const_ins)


def _in_proj(x, g_pre, w_ext, tm):
    def body(x_ref, g_ref, w_ref, qkv_ref, cq_ref, ckv_ref, kr_ref, gl_ref, h_ref):
        n, _ = _rms(x_ref[...])
        hb = (n * g_ref[...]).astype(BF16)
        h_ref[...] = hb
        for c in range(0, 1536, 512):
            qkv_ref[:, c:c + 512] = _dot(hb, w_ref[:, c:c + 512]).astype(BF16)
        cq_ref[...] = _dot(hb, w_ref[:, EXT_CQ:EXT_CKV])
        ckv_ref[...] = _dot(hb, w_ref[:, EXT_CKV:EXT_KR])
        kr_ref[...] = _dot(hb, w_ref[:, EXT_KR:EXT_GL])
        for c in range(0, 2048, 512):
            gl_ref[:, c:c + 512] = _dot(hb, w_ref[:, EXT_GL + c:EXT_GL + c + 512])

    return _rows_call("in_proj", body, x.shape[0], tm, [x], [g_pre, w_ext],
                      [(1536, BF16), (384, F32), (256, F32), (128, F32), (2048, F32), (1024, BF16)], [])


def _mla_up(cq, ckv, kr, cr, sr, q_norm, kv_norm, wa, wk, wv, tm):
    def body(cq_ref, ckv_ref, kr_ref, cr_ref, sr_ref, qn_ref, kvn_ref, wa_ref, wk_ref, wv_ref,
             q_ref, k_ref, v_ref, cqn_ref, ckvn_ref):
        nq, _ = _rms(cq_ref[...])
        cqn = (nq * qn_ref[...]).astype(BF16)
        cqn_ref[...] = cqn
        nk, _ = _rms(ckv_ref[...])
        ckvn = (nk * kvn_ref[...]).astype(BF16)
        ckvn_ref[...] = ckvn
        cr = cr_ref[...]
        sr = sr_ref[...]
        lane = lax.broadcasted_iota(jnp.int32, cr.shape, 1)
        cm = cr + (lane < 64).astype(F32)
        kr = kr_ref[...]
        krp = kr * cr + _rot(kr) * sr
        for h in range(N_HEADS):
            hs = slice(h * LANES, (h + 1) * LANES)
            a = _dot(cqn, wa_ref[:, hs])
            q_ref[:, hs] = (a * cm + _rot(a) * sr).astype(BF16)
            k_ref[:, hs] = (_dot(ckvn, wk_ref[:, hs]) + krp).astype(BF16)
        v_ref[...] = _dot(ckvn, wv_ref[...]).astype(BF16)

    return _rows_call("mla_up", body, cq.shape[0], tm, [cq, ckv, kr, cr, sr], [q_norm, kv_norm, wa, wk, wv],
                      [(1024, BF16), (1024, BF16), (512, BF16), (384, BF16), (256, BF16)], [])


def _mix_out(o_sb, o_mla, gl, x, b_gate, g_post, w_sb, w_mla, w_out, tm):
    def body(osb_ref, omla_ref, gl_ref, x_ref, b_ref, gp_ref, wsb_ref, wmla_ref, wout_ref,
             x1_ref, y_ref, mb_ref, osbb_ref, omlab_ref):
        osb = osb_ref[...].astype(BF16)
        omla = omla_ref[...].astype(BF16)
        osbb_ref[...] = osb
        omlab_ref[...] = omla
        psb = _dot(osb, wsb_ref[...])
        pmla = _dot(omla, wmla_ref[...])
        gates = jax.nn.sigmoid(gl_ref[...] + b_ref[...])
        mb = (gates[:, :D_MODEL] * psb + gates[:, D_MODEL:] * pmla).astype(BF16)
        mb_ref[...] = mb
        y = _dot(mb, wout_ref[...])
        y_ref[...] = y
        n, _ = _rms(y)
        x1_ref[...] = x_ref[...] + n * gp_ref[...]

    return _rows_call("mix_out", body, x.shape[0], tm, [o_sb, o_mla, gl, x], [b_gate, g_post, w_sb, w_mla, w_out],
                      [(1024, F32), (1024, F32), (1024, BF16), (512, BF16), (512, BF16)], [])


FF_CHUNK = 1408


def _ffn_fwd(x1, target, g_pre, g_post, w_gu, w_down, tm):
    def body(x1_ref, t_ref, gpre_ref, gpost_ref, wgu_ref, wd_ref, dx2_ref, f_ref, h2_ref, loss_ref):
        x1 = x1_ref[...]
        n, _ = _rms(x1)
        h2 = (n * gpre_ref[...]).astype(BF16)
        h2_ref[...] = h2
        f = jnp.zeros((tm, D_MODEL), F32)
        for c in range(0, D_FF, FF_CHUNK):
            g = _dot(h2, wgu_ref[:, c:c + FF_CHUNK])
            u = _dot(h2, wgu_ref[:, D_FF + c:D_FF + c + FF_CHUNK])
            act = (g * jax.nn.sigmoid(g) * u).astype(BF16)
            f = f + _dot(act, wd_ref[c:c + FF_CHUNK, :])
        f_ref[...] = f
        nf, _ = _rms(f)
        err = x1 + nf * gpost_ref[...] - t_ref[...]
        dx2_ref[...] = err * (1.0 / D_MODEL)
        e8 = _colsum8(err * err)
        part = e8[:, 0:LANES]
        for c in range(LANES, D_MODEL, LANES):
            part = part + e8[:, c:c + LANES]

        @pl.when(pl.program_id(0) == 0)
        def _():
            loss_ref[...] = jnp.zeros_like(loss_ref)

        loss_ref[...] += part

    return _rows_call("ffn_fwd", body, x1.shape[0], tm, [x1, target], [g_pre, g_post, w_gu, w_down],
                      [(1024, F32), (1024, F32), (1024, BF16)], [(8, LANES)])


def _ffn_bwd(dx2, f, x1, g_pre, g_post, w_gu, w_down, tm):
    def body(dx2_ref, f_ref, x1_ref, gpre_ref, gpost_ref, wgu_ref, wd_ref,
             dx1_ref, act_ref, dgu_ref, dfb_ref, dgpre_ref, dgpost_ref):
        @pl.when(pl.program_id(0) == 0)
        def _():
            dgpre_ref[...] = jnp.zeros_like(dgpre_ref)
            dgpost_ref[...] = jnp.zeros_like(dgpost_ref)

        dx2 = dx2_ref[...]
        nf, rf = _rms(f_ref[...])
        dgpost_ref[...] += _colsum8(dx2 * nf)
        dfb = _rms_bwd(dx2 * gpost_ref[...], nf, rf).astype(BF16)
        dfb_ref[...] = dfb
        x1 = x1_ref[...]
        n1, r1 = _rms(x1)
        h2 = (n1 * gpre_ref[...]).astype(BF16)
        dh2 = jnp.zeros((tm, D_MODEL), F32)
        for c in range(0, D_FF, FF_CHUNK):
            cs, us = slice(c, c + FF_CHUNK), slice(D_FF + c, D_FF + c + FF_CHUNK)
            g = _dot(h2, wgu_ref[:, cs])
            u = _dot(h2, wgu_ref[:, us])
            sg = jax.nn.sigmoid(g)
            si = g * sg
            act_ref[:, cs] = (si * u).astype(BF16)
            dact = _dot_nt(dfb, wd_ref[cs, :])
            dg = (dact * u * (sg * (1.0 + g * (1.0 - sg)))).astype(BF16)
            du = (dact * si).astype(BF16)
            dgu_ref[:, cs] = dg
            dgu_ref[:, us] = du
            dh2 = dh2 + _dot_nt(dg, wgu_ref[:, cs]) + _dot_nt(du, wgu_ref[:, us])
        dgpre_ref[...] += _colsum8(dh2 * n1)
        dx1_ref[...] = dx2 + _rms_bwd(dh2 * gpre_ref[...], n1, r1)

    return _rows_call("ffn_bwd", body, dx2.shape[0], tm, [dx2, f, x1], [g_pre, g_post, w_gu, w_down],
                      [(1024, F32), (D_FF, BF16), (2 * D_FF, BF16), (1024, BF16)], [(8, 1024), (8, 1024)])


def _mix_bwd(dx1, y, o_sb, o_mla, gl, b_gate, g_post, w_sb, w_mla, w_out, tm):
    def body(dx1_ref, y_ref, osb_ref, omla_ref, gl_ref, b_ref, gp_ref, wsb_ref, wmla_ref, wout_ref,
             dyb_ref, dpsb_ref, dpmla_ref, dgl_ref, dosb_ref, domla_ref, dgpost_ref, dbg_ref):
        @pl.when(pl.program_id(0) == 0)
        def _():
            dgpost_ref[...] = jnp.zeros_like(dgpost_ref)
            dbg_ref[...] = jnp.zeros_like(dbg_ref)

        dx1 = dx1_ref[...]
        ny, ry = _rms(y_ref[...])
        dgpost_ref[...] += _colsum8(dx1 * ny)
        dyb = _rms_bwd(dx1 * gp_ref[...], ny, ry).astype(BF16)
        dyb_ref[...] = dyb
        dm = _dot_nt(dyb, wout_ref[...])
        psb = _dot(osb_ref[...].astype(BF16), wsb_ref[...])
        pmla = _dot(omla_ref[...].astype(BF16), wmla_ref[...])
        gates = jax.nn.sigmoid(gl_ref[...] + b_ref[...])
        g0, g1 = gates[:, :D_MODEL], gates[:, D_MODEL:]
        dpsb = (dm * g0).astype(BF16)
        dpmla = (dm * g1).astype(BF16)
        dpsb_ref[...] = dpsb
        dpmla_ref[...] = dpmla
        dgl0 = dm * psb * g0 * (1.0 - g0)
        dgl1 = dm * pmla * g1 * (1.0 - g1)
        dgl_ref[:, :D_MODEL] = dgl0.astype(BF16)
        dgl_ref[:, D_MODEL:] = dgl1.astype(BF16)
        dbg_ref[:, :D_MODEL] += _colsum8(dgl0)
        dbg_ref[:, D_MODEL:] += _colsum8(dgl1)
        dosb_ref[...] = _dot_nt(dpsb, wsb_ref[...]).astype(BF16)
        domla_ref[...] = _dot_nt(dpmla, wmla_ref[...]).astype(BF16)

    return _rows_call("mix_bwd", body, dx1.shape[0], tm, [dx1, y, o_sb, o_mla, gl],
                      [b_gate, g_post, w_sb, w_mla, w_out],
                      [(1024, BF16), (1024, BF16), (1024, BF16), (2048, BF16), (512, BF16), (512, BF16)],
                      [(8, 1024), (8, 2048)])


def _mla_up_bwd(dq, dk, dv, cq, ckv, cr, sr, q_norm, kv_norm, wa, wk, wv, tm):
    def body(dq_ref, dk_ref, dv_ref, cq_ref, ckv_ref, cr_ref, sr_ref, qn_ref, kvn_ref, wa_ref, wk_ref, wv_ref,
             da_ref, dkb_ref, dvb_ref, dlat_ref, dqn_ref, dkvn_ref):
        @pl.when(pl.program_id(0) == 0)
        def _():
            dqn_ref[...] = jnp.zeros_like(dqn_ref)
            dkvn_ref[...] = jnp.zeros_like(dkvn_ref)

        cr = cr_ref[...]
        sr = sr_ref[...]
        lane = lax.broadcasted_iota(jnp.int32, cr.shape, 1)
        cm = cr + (lane < 64).astype(F32)
        nq, rq = _rms(cq_ref[...])
        nk, rk = _rms(ckv_ref[...])
        dcqn = jnp.zeros((tm, MLA_Q_RANK), F32)
        dckvn = jnp.zeros((tm, MLA_KV_RANK), F32)
        dkrp = jnp.zeros((tm, LANES), F32)
        for h in range(N_HEADS):
            hs = slice(h * LANES, (h + 1) * LANES)
            dqh = dq_ref[:, hs]
            da = (dqh * cm - _rot(dqh * sr)).astype(BF16)
            da_ref[:, hs] = da
            dcqn = dcqn + _dot_nt(da, wa_ref[:, hs])
            dkh = dk_ref[:, hs]
            dkb = dkh.astype(BF16)
            dkb_ref[:, hs] = dkb
            dckvn = dckvn + _dot_nt(dkb, wk_ref[:, hs])
            dkrp = dkrp + dkh
        dvb = dv_ref[...].astype(BF16)
        dvb_ref[...] = dvb
        dckvn = dckvn + _dot_nt(dvb, wv_ref[...])
        dkr = dkrp * cr - _rot(dkrp * sr)
        dqn_ref[...] += _colsum8(dcqn * nq)
        dkvn_ref[...] += _colsum8(dckvn * nk)
        dlat_ref[:, 0:384] = _rms_bwd(dcqn * qn_ref[...], nq, rq).astype(BF16)
        dlat_ref[:, 384:640] = _rms_bwd(dckvn * kvn_ref[...], nk, rk).astype(BF16)
        dlat_ref[:, 640:768] = dkr.astype(BF16)

    return _rows_call("mla_up_bwd", body, dq.shape[0], tm, [dq, dk, dv, cq, ckv, cr, sr],
                      [q_norm, kv_norm, wa, wk, wv],
                      [(1024, BF16), (1024, BF16), (512, BF16), (768, BF16)], [(8, 384), (8, 256)])


def _in_proj_bwd(x, dx1, dq_sb, dk_sb, dv_sb, dlat, dgl, g_pre, w_ext, tm):
    def body(x_ref, dx1_ref, dq_ref, dk_ref, dv_ref, dlat_ref, dgl_ref, g_ref, w_ref,
             dx_ref, dproj_ref, dg_ref):
        @pl.when(pl.program_id(0) == 0)
        def _():
            dg_ref[...] = jnp.zeros_like(dg_ref)

        dproj_ref[:, 0:512] = dq_ref[...].astype(BF16)
        dproj_ref[:, 512:1024] = dk_ref[...].astype(BF16)
        dproj_ref[:, 1024:1536] = dv_ref[...].astype(BF16)
        dproj_ref[:, EXT_CQ:EXT_GL] = dlat_ref[...]
        dproj_ref[:, EXT_GL:EXT_N] = dgl_ref[...]
        dh = jnp.zeros((tm, D_MODEL), F32)
        for c in range(0, EXT_N, 2176):
            dh = dh + _dot_nt(dproj_ref[:, c:c + 2176], w_ref[:, c:c + 2176])
        n, r = _rms(x_ref[...])
        dg_ref[...] += _colsum8(dh * n)
        dx_ref[...] = dx1_ref[...] + _rms_bwd(dh * g_ref[...], n, r)

    return _rows_call("in_proj_bwd", body, x.shape[0], tm, [x, dx1, dq_sb, dk_sb, dv_sb, dlat, dgl],
                      [g_pre, w_ext], [(1024, F32), (EXT_N, BF16)], [(8, 1024)])


def _head_masked(x):
    lane = lax.broadcasted_iota(jnp.int32, x.shape, 1)
    zero = jnp.zeros_like(x)
    return jnp.where(lane < 64, x, zero), jnp.where(lane >= 64, x, zero)


def _cum_weights():
    row = lax.broadcasted_iota(jnp.int32, (TK, TK), 0)
    col = lax.broadcasted_iota(jnp.int32, (TK, TK), 1)
    half = jnp.concatenate([(row > col).astype(BF16), jnp.ones((TK, TK), BF16)], axis=1)
    return jnp.concatenate([half, half], axis=0)


def _split_cat(x):
    hi, lo = _split(x)
    return jnp.concatenate([hi, lo], axis=1)


def _sweep(i, tiles, unroll):
    col = lax.broadcasted_iota(jnp.int32, (TQ, TK), 1)

    def diag(t, _):
        top = DIAG_TILES - 1 - t * unroll
        tiles(i * DIAG_TILES + top, [(top - u) * TK + col for u in range(unroll)])
        return 0

    lax.fori_loop(0, DIAG_TILES // unroll, diag, 0)

    def full(g, _):
        tiles(i * DIAG_TILES - 1 - g * unroll, [None] * unroll)
        return 0

    lax.fori_loop(0, (i * DIAG_TILES) // unroll, full, 0)


def _causal(key, strict):
    if key is None:
        return None
    row = lax.broadcasted_iota(jnp.int32, (TQ, TK), 0)
    return key < row if strict else key <= row


def _sb_logs(z, valid, w_cum):
    soft = jnp.log(1.0 + jnp.exp(-jnp.abs(z)))
    lsm = -jnp.maximum(z, 0.0) - soft
    cat = _split_cat(lsm if valid is None else jnp.where(valid, lsm, 0.0))
    return z + lsm, _dot(cat, w_cum)


def _sb_weight(log_beta, cs, r, valid):
    a = jnp.exp(log_beta + cs[:, :TK] + r)
    if valid is not None:
        a = jnp.where(valid, a, 0.0)
    return a, r + cs[:, TK:]


def _block_diag(x):
    x0, x1 = x[:, :LANES], x[:, LANES:]
    zero = jnp.zeros_like(x0)
    return jnp.concatenate([jnp.concatenate([x0, zero], axis=1), jnp.concatenate([zero, x1], axis=1)], axis=0)


def _attn_fwd(qkv, q, k, v):
    s_len = qkv.shape[0]

    def body(qs_ref, ks_ref, vs_ref, qm_ref, km_ref, vm_ref, osb_ref, omla_ref, lse_ref,
             sacc_ref, r_ref, macc_ref, m_ref):
        i = pl.program_id(1)
        lane = lax.broadcasted_iota(jnp.int32, (TQ, LANES), 1)
        r2 = lax.broadcasted_iota(jnp.int32, (2 * TK, LANES), 0)
        c2 = lax.broadcasted_iota(jnp.int32, (2 * TK, LANES), 1)
        head_ones = ((r2 < TK) == (c2 < 64)).astype(BF16)
        w_cum = _cum_weights()
        qs = qs_ref[...] * SB_SCALE
        qm = qm_ref[...]
        sacc_ref[...] = jnp.zeros_like(sacc_ref)
        r_ref[...] = jnp.zeros_like(r_ref)
        macc_ref[...] = jnp.zeros_like(macc_ref)
        m_ref[...] = jnp.full(m_ref.shape, NEG_BIG, F32)

        def sb_scores(top, valids):
            out = []
            for u, valid in enumerate(valids):
                off = pl.multiple_of((top - u) * TK, TK)
                z01 = _dot_nt(qs, jnp.concatenate(_head_masked(ks_ref[pl.ds(off, TK), :]), axis=0))
                out.append([_sb_logs(z, valid, w_cum) for z in (z01[:, :TK], z01[:, TK:])])
            return out

        def sb_accumulate(top, valids, scores):
            parts = [[None, None] for _ in valids]
            for hh in range(2):
                r = r_ref[hh]
                for u, valid in enumerate(valids):
                    a, r = _sb_weight(*scores[u][hh], r, valid)
                    parts[u][hh] = _split_cat(a)
                r_ref[hh] = r
            vs = []
            for u in range(len(valids)):
                off = pl.multiple_of((top - u) * TK, TK)
                v0, v1 = _head_masked(vs_ref[pl.ds(off, TK), :])
                vs += [v0, v0, v1, v1]
            sacc_ref[...] += _dot(jnp.concatenate([p for pair in parts for p in pair], axis=1),
                                  jnp.concatenate(vs, axis=0))

        def mla_scores(top, valids):
            s01s = []
            for u in range(len(valids)):
                off = pl.multiple_of((top - u) * TK, TK)
                s01s.append(_dot_nt(qm, _block_diag(km_ref[pl.ds(off, TK), :])))
            heads = []
            for hh in range(2):
                ss = []
                for u, valid in enumerate(valids):
                    s = s01s[u][:, hh * TK:(hh + 1) * TK] * MLA_SCALE
                    ss.append(s if valid is None else jnp.where(valid, s, NEG_BIG))
                m_old = m_ref[hh]
                m = jnp.maximum(m_old, jnp.max(functools.reduce(jnp.maximum, ss), axis=1, keepdims=True))
                m_ref[hh] = m
                heads.append((ss, m, jnp.exp(m_old - m)))
            return heads

        def mla_accumulate(top, heads):
            vs = []
            for u in range(len(heads[0][0])):
                off = pl.multiple_of((top - u) * TK, TK)
                vv = jnp.concatenate(_head_masked(vm_ref[pl.ds(off, TK), :]), axis=0)
                vs.append(jnp.concatenate([vv, head_ones], axis=1))
            ps = [[jnp.exp(s - m).astype(BF16) for s in ss] for ss, m, _ in heads]
            scale = jnp.where(lane < 64, heads[0][2], heads[1][2])
            p_all = jnp.concatenate([ps[hh][u] for u in range(len(vs)) for hh in range(2)], axis=1)
            macc_ref[...] = (macc_ref[...] * jnp.concatenate([scale, scale], axis=1)
                             + _dot(p_all, jnp.concatenate(vs, axis=0)))

        def tiles(top, keys):
            strict = [_causal(key, True) for key in keys]
            heads = mla_scores(top, [_causal(key, False) for key in keys])
            scores = sb_scores(top, strict)
            mla_accumulate(top, heads)
            sb_accumulate(top, strict, scores)

        _sweep(i, tiles, FWD_UNROLL)
        osb_ref[...] = sacc_ref[...]
        acc = macc_ref[...]
        den = acc[:, LANES:]
        omla_ref[...] = acc[:, :LANES] / den
        for hh, mask in enumerate((lane < 64, lane >= 64)):
            l = jnp.max(jnp.where(mask, den, 0.0), axis=1, keepdims=True)
            lse_ref[hh] = jnp.broadcast_to(m_ref[hh] + jnp.log(l), (TQ, LANES))

    tile = pl.BlockSpec((TQ, LANES), lambda h, i: (i, h))
    return pl.pallas_call(
        body, grid=(4, s_len // TQ),
        in_specs=[tile,
                  pl.BlockSpec((s_len, LANES), lambda h, i: (0, 4 + h)),
                  pl.BlockSpec((s_len, LANES), lambda h, i: (0, 8 + h)),
                  pl.BlockSpec((TQ, 2 * LANES), lambda h, i: (i, h)),
                  pl.BlockSpec((s_len, 2 * LANES), lambda h, i: (0, h)),
                  pl.BlockSpec((s_len, LANES), lambda h, i: (0, h))],
        out_specs=[tile, tile, pl.BlockSpec((2, TQ, LANES), lambda h, i: (h, i, 0))],
        out_shape=[jax.ShapeDtypeStruct((s_len, SB_WIDTH), F32), jax.ShapeDtypeStruct((s_len, 512), F32),
                   jax.ShapeDtypeStruct((N_HEADS, s_len, LANES), F32)],
        scratch_shapes=[pltpu.VMEM((TQ, LANES), F32), pltpu.VMEM((2, TQ, LANES), F32),
                        pltpu.VMEM((TQ, 2 * LANES), F32), pltpu.VMEM((2, TQ, 1), F32)],
        name="attn_fwd", compiler_params=_params(("arbitrary", "arbitrary")),
    )(qkv, qkv, qkv, q, k, v)


def _row_dots(do, o):
    prod = do.astype(F32) * o
    p0, p1 = _head_masked(prod)
    return tuple(jnp.broadcast_to(jnp.sum(p, axis=1, keepdims=True), prod.shape) for p in (p0, p1))


def _attn_bwd(qkv, do_sb, o_sb, q, k, v, do_mla, o_mla, lse, after):
    s_len = qkv.shape[0]
    n_q = s_len // TQ

    def body(qs_ref, ks_ref, vs_ref, dos_ref, os_ref, qm_ref, km_ref, vm_ref, dom_ref, om_ref, lse_ref, after_ref,
             dqs_ref, dqm_ref, dks_hbm, dvs_hbm, dkm_hbm, dvm_hbm,
             dqs_acc, dqm_acc, dks_acc, dvs_acc, dkm_acc, dvm_acc, r_ref, g_ref, ds_ref, dm_ref, out_sems):
        h = pl.program_id(0)
        i = pl.program_id(1)

        @pl.when(i == 0)
        def _():
            for acc in (dks_acc, dvs_acc, dkm_acc, dvm_acc):
                acc[...] = jnp.zeros_like(acc)

        w_cum = _cum_weights()
        qs = qs_ref[...] * SB_SCALE
        dos = dos_ref[...]
        qs_rows = jnp.concatenate(_head_masked(qs), axis=0)
        dos_rows = jnp.concatenate(_head_masked(dos), axis=0)
        ds_ref[0], ds_ref[1] = _row_dots(dos, os_ref[...])
        qm = qm_ref[...]
        dom = dom_ref[...]
        qm_diag = _block_diag(qm)
        dom_rows = jnp.concatenate(_head_masked(dom), axis=0)
        dm_ref[0], dm_ref[1] = _row_dots(dom, om_ref[...])
        for ref in (dqs_acc, dqm_acc, r_ref, g_ref):
            ref[...] = jnp.zeros_like(ref)
        heads = (slice(0, TK), slice(TK, 2 * TK))

        def weigh(log_beta, cs, r, da, valid):
            a, r = _sb_weight(log_beta, cs, r, valid)
            g = a * da
            return a.astype(BF16), g, _split_cat(g), r

        def logit_grad(g, gs, carried, d, log_beta, valid):
            upto = d - (gs[:, :TK] + carried)
            dz = g - jnp.exp(log_beta) * upto
            if valid is not None:
                dz = jnp.where(valid, dz, 0.0)
            return dz.astype(BF16), carried + gs[:, TK:]

        def tiles(top, keys):
            n = len(keys)
            strict = [_causal(key, True) for key in keys]
            loose = [_causal(key, False) for key in keys]
            offs = [pl.multiple_of((top - u) * TK, TK) for u in range(n)]
            kds = [_block_diag(km_ref[pl.ds(off, TK), :]) for off in offs]
            vms = [jnp.concatenate(_head_masked(vm_ref[pl.ds(off, TK), :]), axis=0) for off in offs]
            s01s = [_dot_nt(qm, kd) for kd in kds]
            dp01s = [_dot_nt(dom, vv) for vv in vms]
            kks = [jnp.concatenate(_head_masked(ks_ref[pl.ds(off, TK), :]), axis=0) for off in offs]
            vvs = [jnp.concatenate(_head_masked(vs_ref[pl.ds(off, TK), :]), axis=0) for off in offs]
            z01s = [_dot_nt(qs, kk) for kk in kks]
            da01s = [_dot_nt(dos, vv) for vv in vvs]
            logs = [[_sb_logs(z01s[u][:, hs], strict[u], w_cum) for hs in heads] for u in range(n)]

            dss = [[None, None] for _ in range(n)]
            ps = [[None, None] for _ in range(n)]
            for u in range(n):
                for hh, hs in enumerate(heads):
                    p = jnp.exp(s01s[u][:, hs] * MLA_SCALE - lse_ref[hh])
                    if loose[u] is not None:
                        p = jnp.where(loose[u], p, 0.0)
                    dss[u][hh] = (p * (dp01s[u][:, hs] - dm_ref[hh]) * MLA_SCALE).astype(BF16)
                    ps[u][hh] = p.astype(BF16)

            dzs = [[None, None] for _ in range(n)]
            avs = [[None, None] for _ in range(n)]
            sums = [[None, None] for _ in range(n)]
            for hh, hs in enumerate(heads):
                r = r_ref[hh]
                for u in range(n):
                    avs[u][hh], g, cat, r = weigh(*logs[u][hh], r, da01s[u][:, hs], strict[u])
                    sums[u][hh] = (g, _dot(cat, w_cum))
                r_ref[hh] = r

            dqm_acc[...] += _dot(jnp.concatenate([d for pair in dss for d in pair], axis=1),
                                 jnp.concatenate(kds, axis=0))
            span = pl.ds(offs[-1], n * TK)
            by_key = lambda pairs: jnp.concatenate([jnp.concatenate(pair, axis=0) for pair in pairs[::-1]], axis=1)
            dkm_acc[span, :] += _dot_tn(by_key(dss), qm_diag)
            dvm_acc[span, :] += _dot_tn(by_key(ps), dom_rows)

            for hh in range(2):
                carried = g_ref[hh]
                for u in range(n):
                    dzs[u][hh], carried = logit_grad(*sums[u][hh], carried, ds_ref[hh], logs[u][hh][0], strict[u])
                g_ref[hh] = carried
            dqs_acc[...] += _dot(jnp.concatenate([dz for pair in dzs for dz in pair], axis=1),
                                 jnp.concatenate(kks, axis=0))
            dks_acc[span, :] += _dot_tn(by_key(dzs), qs_rows)
            dvs_acc[span, :] += _dot_tn(by_key(avs), dos_rows)

        _sweep(i, tiles, BWD_UNROLL)
        dqs_ref[...] = dqs_acc[...] * SB_SCALE
        dqm_ref[...] = dqm_acc[...]

        @pl.when(i == n_q - 1)
        def _():
            narrow = pl.ds(pl.multiple_of(h * LANES, LANES), LANES)
            wide = pl.ds(pl.multiple_of(h * 2 * LANES, 2 * LANES), 2 * LANES)
            copies = [pltpu.make_async_copy(dks_acc, dks_hbm.at[:, narrow], out_sems.at[0]),
                      pltpu.make_async_copy(dvs_acc, dvs_hbm.at[:, narrow], out_sems.at[1]),
                      pltpu.make_async_copy(dkm_acc, dkm_hbm.at[:, wide], out_sems.at[2]),
                      pltpu.make_async_copy(dvm_acc, dvm_hbm.at[:, narrow], out_sems.at[3])]
            for cp in copies:
                cp.start()
            for cp in copies:
                cp.wait()

    tile = pl.BlockSpec((TQ, LANES), lambda h, i: (i, h))
    wide_tile = pl.BlockSpec((TQ, 2 * LANES), lambda h, i: (i, h))
    once = pl.Buffered(1)
    hbm = pl.BlockSpec(memory_space=pl.ANY)
    return pl.pallas_call(
        body, grid=(4, n_q),
        in_specs=[tile,
                  pl.BlockSpec((s_len, LANES), lambda h, i: (0, 4 + h), pipeline_mode=once),
                  pl.BlockSpec((s_len, LANES), lambda h, i: (0, 8 + h), pipeline_mode=once),
                  tile, tile, wide_tile,
                  pl.BlockSpec((s_len, 2 * LANES), lambda h, i: (0, h), pipeline_mode=once),
                  pl.BlockSpec((s_len, LANES), lambda h, i: (0, h), pipeline_mode=once),
                  tile, tile, pl.BlockSpec((2, TQ, LANES), lambda h, i: (h, i, 0)),
                  pl.BlockSpec((8, LANES), lambda h, i: (0, 0))],
        out_specs=[tile, wide_tile, hbm, hbm, hbm, hbm],
        out_shape=[jax.ShapeDtypeStruct((s_len, SB_WIDTH), F32), jax.ShapeDtypeStruct((s_len, 1024), F32),
                   jax.ShapeDtypeStruct((s_len, SB_WIDTH), F32), jax.ShapeDtypeStruct((s_len, SB_WIDTH), F32),
                   jax.ShapeDtypeStruct((s_len, 1024), F32), jax.ShapeDtypeStruct((s_len, 512), F32)],
        scratch_shapes=[pltpu.VMEM((TQ, LANES), F32), pltpu.VMEM((TQ, 2 * LANES), F32),
                        pltpu.VMEM((s_len, LANES), F32), pltpu.VMEM((s_len, LANES), F32),
                        pltpu.VMEM((s_len, 2 * LANES), F32), pltpu.VMEM((s_len, LANES), F32),
                        pltpu.VMEM((2, TQ, LANES), F32), pltpu.VMEM((2, TQ, LANES), F32),
                        pltpu.VMEM((2, TQ, LANES), F32), pltpu.VMEM((2, TQ, LANES), F32),
                        pltpu.SemaphoreType.DMA((4,))],
        name="attn_bwd", compiler_params=_params(("arbitrary", "arbitrary")),
    )(qkv, qkv, qkv, do_sb, o_sb, q, k, v, do_mla, o_mla, lse, after)


def _pick(n, options):
    for t in options:
        if n % t == 0:
            return t
    raise ValueError(n)


def _matmul_tn(name, a, b):
    s_len, m = a.shape
    n = b.shape[1]
    tm = _pick(m, (1024, 1408, 2176, 512))
    tn = _pick(n, (1024, 512, 384, 256))
    tk = _pick(s_len, (1024, 512, 256, 128))

    def body(a_ref, b_ref, o_ref):
        @pl.when(pl.program_id(2) == 0)
        def _():
            o_ref[...] = jnp.zeros_like(o_ref)

        o_ref[...] += _dot_tn(a_ref[...], b_ref[...])

    return pl.pallas_call(
        body, grid=(m // tm, n // tn, s_len // tk),
        in_specs=[pl.BlockSpec((tk, tm), lambda i, j, l: (l, i)), pl.BlockSpec((tk, tn), lambda i, j, l: (l, j))],
        out_specs=pl.BlockSpec((tm, tn), lambda i, j, l: (i, j)),
        out_shape=jax.ShapeDtypeStruct((m, n), F32),
        name=name, compiler_params=_params(("arbitrary", "arbitrary", "arbitrary")),
    )(a, b)


def _mesh_pos():
    return lax.axis_index("x"), lax.axis_index("y"), lax.axis_index("c")


def _peer(pos, k):
    x, y, c = pos
    return (1 - x if k & 4 else x, 1 - y if k & 2 else y, 1 - c if k & 1 else c)


def _flat_index(pos):
    return 4 * pos[0] + 2 * pos[1] + pos[2]


def _all_gather(shard):
    rows = shard.shape[0]

    def body(x_ref, out_ref, send_sems, recv_sems, local_sem):
        me = _mesh_pos()
        x, y, c = me
        sibling = (x, y, 1 - c)
        chips = [(1 - x, y), (x, 1 - y), (1 - x, 1 - y)]

        def copy(k, block, to, src=None):
            slot = out_ref.at[_flat_index(block)]
            return pltpu.make_async_remote_copy(
                src_ref=slot if src is None else src, dst_ref=slot,
                send_sem=send_sems.at[k], recv_sem=recv_sems.at[k],
                device_id=to, device_id_type=pl.DeviceIdType.MESH)

        mine = pltpu.make_async_copy(x_ref, out_ref.at[_flat_index(me)], local_sem)
        mine.start()
        first = [copy(0, me, sibling, src=x_ref)]
        first += [copy(1 + j, me, (*chip, c), src=x_ref) for j, chip in enumerate(chips)]
        for cp in first:
            cp.start()
        passed = [copy(4 + j, (*chip, c), sibling) for j, chip in enumerate(chips)]
        for j, chip in enumerate(chips):
            copy(1 + j, (*chip, c), me).wait_recv()
            passed[j].start()
        copy(0, sibling, me).wait_recv()
        for j, chip in enumerate(chips):
            copy(4 + j, (*chip, 1 - c), me).wait_recv()
        for cp in first + passed:
            cp.wait_send()
        mine.wait()

    return pl.pallas_call(
        body, out_shape=jax.ShapeDtypeStruct((N_DEV, rows, LANES), shard.dtype),
        in_specs=[pl.BlockSpec(memory_space=pl.ANY)], out_specs=pl.BlockSpec(memory_space=pl.ANY),
        scratch_shapes=[pltpu.SemaphoreType.DMA((7,)), pltpu.SemaphoreType.DMA((7,)), pltpu.SemaphoreType.DMA],
        name="weights_all_gather",
    )(shard)


def _grad_exchange(big, small):
    def body(big_ref, small_ref, big_out, small_out, bsend, brecv, ssend, srecv, local_sems):
        me = _mesh_pos()
        mine = _flat_index(me)
        loc = [pltpu.make_async_copy(big_ref.at[mine], big_out.at[mine], local_sems.at[0]),
               pltpu.make_async_copy(small_ref, small_out.at[mine], local_sems.at[1])]
        for cp in loc:
            cp.start()

        def copies(k):
            peer = _peer(me, k)
            theirs = _flat_index(peer)
            send = (pltpu.make_async_remote_copy(
                        src_ref=big_ref.at[theirs], dst_ref=big_out.at[mine], send_sem=bsend.at[k - 1],
                        recv_sem=brecv.at[k - 1], device_id=peer, device_id_type=pl.DeviceIdType.MESH),
                    pltpu.make_async_remote_copy(
                        src_ref=small_ref, dst_ref=small_out.at[mine], send_sem=ssend.at[k - 1],
                        recv_sem=srecv.at[k - 1], device_id=peer, device_id_type=pl.DeviceIdType.MESH))
            recv = (pltpu.make_async_remote_copy(
                        src_ref=big_ref.at[mine], dst_ref=big_out.at[theirs], send_sem=bsend.at[k - 1],
                        recv_sem=brecv.at[k - 1], device_id=me, device_id_type=pl.DeviceIdType.MESH),
                    pltpu.make_async_remote_copy(
                        src_ref=small_ref, dst_ref=small_out.at[theirs], send_sem=ssend.at[k - 1],
                        recv_sem=srecv.at[k - 1], device_id=me, device_id_type=pl.DeviceIdType.MESH))
            return send, recv

        plan = [copies(k) for k in range(1, N_DEV)]
        for send, _ in plan:
            for cp in send:
                cp.start()
        for _, recv in plan:
            for cp in recv:
                cp.wait_recv()
        for send, _ in plan:
            for cp in send:
                cp.wait_send()
        for cp in loc:
            cp.wait()

    any_spec = pl.BlockSpec(memory_space=pl.ANY)
    return pl.pallas_call(
        body,
        out_shape=[jax.ShapeDtypeStruct(big.shape, big.dtype),
                   jax.ShapeDtypeStruct((N_DEV,) + small.shape, small.dtype)],
        in_specs=[any_spec, any_spec], out_specs=[any_spec, any_spec],
        scratch_shapes=[pltpu.SemaphoreType.DMA((7,)), pltpu.SemaphoreType.DMA((7,)),
                        pltpu.SemaphoreType.DMA((7,)), pltpu.SemaphoreType.DMA((7,)),
                        pltpu.SemaphoreType.DMA((2,))],
        name="grad_exchange",
    )(big, small)


def _push_start(name, src, per_peer):
    rows = src.shape[-2]

    def body(src_ref, land_ref, send_sems, recv_sems, src_thru, land_thru, token):
        me = _mesh_pos()
        mine = _flat_index(me)
        for k in range(1, N_DEV):
            peer = _peer(me, k)
            pltpu.make_async_remote_copy(
                src_ref=src_ref.at[_flat_index(peer)] if per_peer else src_ref, dst_ref=land_ref.at[mine],
                send_sem=send_sems.at[k - 1], recv_sem=recv_sems.at[k - 1],
                device_id=peer, device_id_type=pl.DeviceIdType.MESH).start()
        token[...] = jnp.zeros_like(token)

    hbm = pl.BlockSpec(memory_space=pltpu.HBM)
    sem = pl.BlockSpec(memory_space=pltpu.SEMAPHORE)
    land = lax.empty((N_DEV, rows, LANES), src.dtype)
    return pl.pallas_call(
        body, name=name,
        out_shape=(pltpu.SemaphoreType.DMA((N_DEV - 1,)), pltpu.SemaphoreType.DMA((N_DEV - 1,)),
                   pltpu.HBM(src.shape, src.dtype), pltpu.HBM(land.shape, land.dtype),
                   jax.ShapeDtypeStruct((8, LANES), F32)),
        in_specs=(hbm, hbm), out_specs=(sem, sem, hbm, hbm, pl.BlockSpec(memory_space=pltpu.VMEM)),
        input_output_aliases={0: 2, 1: 3},
        compiler_params=pltpu.CompilerParams(has_side_effects=pltpu.SideEffectType.DATAFLOW_SIDE_EFFECTING),
    )(pltpu.with_memory_space_constraint(src, pltpu.HBM), pltpu.with_memory_space_constraint(land, pltpu.HBM))


def _push_wait(name, started, per_peer, after):
    send_sems, recv_sems, src_thru, land_thru, _ = started

    def body(src_ref, land_ref, send_sems, recv_sems, after_ref, src_out, land_out):
        me = _mesh_pos()
        for k in range(1, N_DEV):
            theirs = _flat_index(_peer(me, k))
            copy = pltpu.make_async_remote_copy(
                src_ref=src_ref.at[theirs] if per_peer else src_ref, dst_ref=land_ref.at[theirs],
                send_sem=send_sems.at[k - 1], recv_sem=recv_sems.at[k - 1],
                device_id=me, device_id_type=pl.DeviceIdType.MESH)
            copy.wait_send()
            copy.wait_recv()

    hbm = pl.BlockSpec(memory_space=pltpu.HBM)
    sem = pl.BlockSpec(memory_space=pltpu.SEMAPHORE)
    return pl.pallas_call(
        body, name=name,
        out_shape=(pltpu.HBM(src_thru.shape, src_thru.dtype), pltpu.HBM(land_thru.shape, land_thru.dtype)),
        in_specs=(hbm, hbm, sem, sem, pl.BlockSpec(memory_space=pl.ANY)), out_specs=(hbm, hbm),
        input_output_aliases={0: 0, 1: 1},
        compiler_params=pltpu.CompilerParams(has_side_effects=pltpu.SideEffectType.DATAFLOW_SIDE_EFFECTING),
    )(src_thru, land_thru, send_sems, recv_sems, after)


def _slot_sum(name, slots, tr):
    rows = slots.shape[1]

    def body(s_ref, o_ref):
        acc = s_ref[0].astype(F32)
        for d in range(1, N_DEV):
            acc = acc + s_ref[d].astype(F32)
        o_ref[...] = acc

    return pl.pallas_call(
        body, grid=(rows // tr,),
        in_specs=[pl.BlockSpec((N_DEV, tr, LANES), lambda i: (0, i, 0))],
        out_specs=pl.BlockSpec((tr, LANES), lambda i: (i, 0)),
        out_shape=jax.ShapeDtypeStruct((rows, LANES), F32),
        name=name, compiler_params=_params(("arbitrary",)),
    )(slots)


def _adamw(name, w, g, m, v):
    rows, cols = w.shape
    tr = _pick(rows, (256, 128, 88, 32, 1))
    c1 = 1.0 - ADAM_B1 ** ADAM_STEP
    c2 = 1.0 - ADAM_B2 ** ADAM_STEP

    def body(w_ref, g_ref, m_ref, v_ref, d_ref, nm_ref, nv_ref):
        g = g_ref[...]
        nm = ADAM_B1 * m_ref[...] + (1.0 - ADAM_B1) * g
        nv = ADAM_B2 * v_ref[...] + (1.0 - ADAM_B2) * (g * g)
        nm_ref[...] = nm
        nv_ref[...] = nv
        d_ref[...] = -ADAM_LR * ((nm / c1) / (jnp.sqrt(nv / c2) + ADAM_EPS) + ADAM_WD * w_ref[...])

    spec = pl.BlockSpec((tr, cols), lambda i: (i, 0))
    return pl.pallas_call(
        body, grid=(rows // tr,), in_specs=[spec] * 4, out_specs=[spec] * 3,
        out_shape=[jax.ShapeDtypeStruct((rows, cols), F32)] * 3,
        name=name, compiler_params=_params(("arbitrary",)),
    )(w, g, m, v)


def _pack_shards(shards, group, rows):
    parts = []
    for name, _, axis in group:
        w = shards[name].astype(BF16)
        parts.append((w.T if axis == 1 else w).reshape(-1))
    flat = jnp.concatenate(parts)
    flat = jnp.pad(flat, (0, rows * LANES - flat.shape[0]))
    return flat.reshape(rows, LANES)


def _unpack_full(gathered, group):
    flat = gathered.reshape(N_DEV, -1)
    out, off = {}, 0
    for name, (r, c), axis in group:
        rr, cc = (c, r) if axis == 1 else (r, c)
        out[name] = flat[:, off:off + r * c].reshape(N_DEV * rr, cc)
        off += r * c
    return out


def _pack_full_grads(grads, group, rows):
    parts = [grads[name].reshape(N_DEV, r * c) for name, (r, c), _ in group]
    flat = jnp.concatenate(parts, axis=1).astype(BF16)
    flat = jnp.pad(flat, ((0, 0), (0, rows * LANES - flat.shape[1])))
    return flat.reshape(N_DEV, rows, LANES)


def _unpack_shard_grads(flat, group):
    flat = flat.reshape(-1)
    out, off = {}, 0
    for name, (r, c), axis in group:
        seg = flat[off:off + r * c]
        out[name] = seg.reshape(c, r).T if axis == 1 else seg.reshape(r, c)
        off += r * c
    return out


def _own_slot(slots, own):
    mine = _flat_index(_mesh_pos())
    return lax.dynamic_update_slice(slots, own[None], (mine, 0, 0))


def _rope_tables(positions):
    inv_freq = ROPE_THETA ** (-jnp.arange(0, MLA_ROPE_DIM, 2, dtype=F32) / MLA_ROPE_DIM)
    ang = positions.astype(F32)[:, None] * inv_freq
    z64 = jnp.zeros((positions.shape[0], 64), F32)
    z32 = jnp.zeros((positions.shape[0], 32), F32)
    cos, sin = jnp.cos(ang), jnp.sin(ang)
    return (jnp.concatenate([z64, cos, cos, z32], axis=1), jnp.concatenate([z64, sin, sin, z32], axis=1))


def _row_tile(s_len, want):
    return _pick(s_len, (want, 256, 128))


def kernel(x, positions, norm_mix_pre, norm_mix_post, w_in, b_gate, q_norm, w_uq, kv_norm, w_ukv, w_proj_sb, w_proj_mla, w_out, norm_ffn_pre, norm_ffn_post, w_gate_up, w_down, loss_target, m_norm_mix_pre, m_norm_mix_post, m_w_in, m_b_gate, m_q_norm, m_w_uq, m_kv_norm, m_w_ukv, m_w_proj_sb, m_w_proj_mla, m_w_out, m_norm_ffn_pre, m_norm_ffn_post, m_w_gate_up, m_w_down, v_norm_mix_pre, v_norm_mix_post, v_w_in, v_b_gate, v_q_norm, v_w_uq, v_kv_norm, v_w_ukv, v_w_proj_sb, v_w_proj_mla, v_w_out, v_norm_ffn_pre, v_norm_ffn_post, v_w_gate_up, v_w_down):
    weights = dict(norm_mix_pre=norm_mix_pre, norm_mix_post=norm_mix_post, w_in=w_in, b_gate=b_gate, q_norm=q_norm,
                   w_uq=w_uq, kv_norm=kv_norm, w_ukv=w_ukv, w_proj_sb=w_proj_sb, w_proj_mla=w_proj_mla, w_out=w_out,
                   norm_ffn_pre=norm_ffn_pre, norm_ffn_post=norm_ffn_post, w_gate_up=w_gate_up, w_down=w_down)
    m_in = dict(norm_mix_pre=m_norm_mix_pre, norm_mix_post=m_norm_mix_post, w_in=m_w_in, b_gate=m_b_gate,
                q_norm=m_q_norm, w_uq=m_w_uq, kv_norm=m_kv_norm, w_ukv=m_w_ukv, w_proj_sb=m_w_proj_sb,
                w_proj_mla=m_w_proj_mla, w_out=m_w_out, norm_ffn_pre=m_norm_ffn_pre, norm_ffn_post=m_norm_ffn_post,
                w_gate_up=m_w_gate_up, w_down=m_w_down)
    v_in = dict(norm_mix_pre=v_norm_mix_pre, norm_mix_post=v_norm_mix_post, w_in=v_w_in, b_gate=v_b_gate,
                q_norm=v_q_norm, w_uq=v_w_uq, kv_norm=v_kv_norm, w_ukv=v_w_ukv, w_proj_sb=v_w_proj_sb,
                w_proj_mla=v_w_proj_mla, w_out=v_w_out, norm_ffn_pre=v_norm_ffn_pre, norm_ffn_post=v_norm_ffn_post,
                w_gate_up=v_w_gate_up, w_down=v_w_down)
    order = list(weights)

    xs = x[0]
    target = loss_target[0]
    s_len = xs.shape[0]
    tm_fwd = _row_tile(s_len, 512)
    tm = _row_tile(s_len, 256)
    tm_ffn = tm

    shards = {name: weights[name][0] for name, _, _ in SHARDED}
    late_shard = _pack_shards(shards, LATE, LATE_ROWS)
    late_weights = _push_start("weights_late_start", late_shard, False)
    full = _unpack_full(_all_gather(_pack_shards(shards, EARLY, EARLY_ROWS)), EARLY)
    wt = full["w_in"]
    zr = lambda n: jnp.zeros((n, D_MODEL), BF16)
    w_ext = jnp.concatenate([wt[:2176], zr(64), wt[2176:2208], zr(32), wt[2208:]], axis=0).T
    wa = jnp.pad(full["w_uq"].reshape(N_HEADS, MLA_QK_DIM, MLA_Q_RANK), ((0, 0), (0, 32), (0, 0))
                 ).reshape(N_HEADS * LANES, MLA_Q_RANK).T
    ukv = full["w_ukv"].reshape(N_HEADS, LANES, MLA_KV_RANK)
    wk = jnp.pad(ukv[:, :64], ((0, 0), (0, 64), (0, 0))).reshape(N_HEADS * LANES, MLA_KV_RANK).T
    wv = ukv[:, 64:].reshape(512, MLA_KV_RANK).T
    g_mix_pre = norm_mix_pre + late_weights[4][0:1, 0:1]
    cr, sr = _rope_tables(positions[0])

    qkv, cq, ckv, kr, gl, hb = _in_proj(xs, g_mix_pre, w_ext, tm_fwd)
    q_mla, k_mla, v_mla, cqn, ckvn = _mla_up(cq, ckv, kr, cr, sr, q_norm, kv_norm, wa, wk, wv, tm_fwd)
    o_sb, o_mla, lse = _attn_fwd(qkv, q_mla, k_mla, v_mla)
    late_shard, late_slots = _push_wait("weights_late_wait", late_weights, False, o_sb)
    full = _unpack_full(_own_slot(late_slots, late_shard), LATE)
    w_sb, w_mla, w_o, w_gu, w_dn = (full["w_proj_sb"].T, full["w_proj_mla"].T, full["w_out"], full["w_gate_up"].T,
                                    full["w_down"])
    x1, y, merged, o_sb_b, o_mla_b = _mix_out(o_sb, o_mla, gl, xs, b_gate, norm_mix_post, w_sb, w_mla, w_o, tm_fwd)
    dx2, f, h2, loss_part = _ffn_fwd(x1, target, norm_ffn_pre, norm_ffn_post, w_gu, w_dn, tm_fwd)
    loss_local = (0.5 / D_MODEL * jnp.sum(loss_part)).reshape(1)

    dx1, act, dgu, dfb, dg_ffn_pre, dg_ffn_post = _ffn_bwd(dx2, f, x1, norm_ffn_pre, norm_ffn_post, w_gu, w_dn, tm_ffn)
    dyb, dpsb, dpmla, dgl, do_sb, do_mla, dg_mix_post, db_gate = _mix_bwd(
        dx1, y, o_sb, o_mla, gl, b_gate, norm_mix_post, w_sb, w_mla, w_o, tm)
    late_grads = _pack_full_grads({
        "w_proj_sb": _matmul_tn("dw_proj_sb", dpsb, o_sb_b),
        "w_proj_mla": _matmul_tn("dw_proj_mla", dpmla, o_mla_b),
        "w_out": _matmul_tn("dw_out", merged, dyb),
        "w_gate_up": _matmul_tn("dw_gate_up", dgu, h2),
        "w_down": _matmul_tn("dw_down", act, dfb),
    }, LATE, LATE_ROWS)
    late_exchange = _push_start("grads_late_start", late_grads, True)
    dq_sb, dq_mla, dk_sb, dv_sb, dk_mla, dv_mla = _attn_bwd(qkv, do_sb, o_sb, q_mla, k_mla, v_mla, do_mla, o_mla, lse,
                                                             late_exchange[4])
    da, dkb, dvb, dlat, dg_q, dg_kv = _mla_up_bwd(dq_mla, dk_mla, dv_mla, cq, ckv, cr, sr, q_norm, kv_norm,
                                                  wa, wk, wv, tm)
    dx, dproj, dg_mix_pre = _in_proj_bwd(xs, dx1, dq_sb, dk_sb, dv_sb, dlat, dgl, norm_mix_pre, w_ext, tm)
    d_ext = _matmul_tn("dw_in", dproj, hb)
    d_wa = _matmul_tn("dw_uq", da, cqn)
    d_wk = _matmul_tn("dw_uk", dkb, ckvn)
    d_wv = _matmul_tn("dw_uv", dvb, ckvn)
    early_grads = _pack_full_grads({
        "w_in": jnp.concatenate([d_ext[:2176], d_ext[2240:2272], d_ext[EXT_GL:]], axis=0),
        "w_uq": d_wa.reshape(N_HEADS, LANES, MLA_Q_RANK)[:, :MLA_QK_DIM].reshape(768, MLA_Q_RANK),
        "w_ukv": jnp.concatenate([d_wk.reshape(N_HEADS, LANES, MLA_KV_RANK)[:, :64],
                                  d_wv.reshape(N_HEADS, 64, MLA_KV_RANK)], axis=1).reshape(1024, MLA_KV_RANK),
    }, EARLY, EARLY_ROWS)
    small_parts = dict(norm_mix_pre=dg_mix_pre, norm_mix_post=dg_mix_post, b_gate=db_gate, q_norm=dg_q,
                       kv_norm=dg_kv, norm_ffn_pre=dg_ffn_pre, norm_ffn_post=dg_ffn_post)
    small = jnp.concatenate([small_parts[name].sum(axis=0) for name, _ in SMALL] + [loss_local])
    small = jnp.pad(small, (0, SMALL_ROWS * LANES - small.shape[0])).reshape(SMALL_ROWS, LANES)

    early_slots, small_slots = _grad_exchange(early_grads, small)
    late_grads, late_slots = _push_wait("grads_late_wait", late_exchange, True, early_slots)
    mine = _flat_index(_mesh_pos())
    late_slots = _own_slot(late_slots, lax.dynamic_index_in_dim(late_grads, mine, 0, keepdims=False))
    g_out = _unpack_shard_grads(_slot_sum("grad_early_sum", early_slots, 960), EARLY)
    g_out.update(_unpack_shard_grads(_slot_sum("grad_late_sum", late_slots, 512), LATE))
    s_flat = _slot_sum("grad_small_sum", small_slots, SMALL_ROWS).reshape(-1)
    off = 0
    for name, n in SMALL:
        g_out[name] = s_flat[off:off + n].reshape(1, n)
        off += n
    loss = s_flat[off]

    deltas, new_m, new_v = {}, {}, {}
    for name in order:
        w2 = weights[name].reshape(g_out[name].shape)
        d, nm, nv = _adamw("adamw_" + name, w2, g_out[name], m_in[name].reshape(w2.shape), v_in[name].reshape(w2.shape))
        shape = weights[name].shape
        deltas[name], new_m[name], new_v[name] = d.reshape(shape), nm.reshape(shape), nv.reshape(shape)
        g_out[name] = g_out[name].reshape(shape)

    return (loss, dx[None], *[g_out[n] for n in order], *[deltas[n] for n in order],
            *[new_m[n] for n in order], *[new_v[n] for n in order])
```

```python
import functools
import math

import jax
import jax.numpy as jnp
from jax import lax
from jax.experimental import pallas as pl
from jax.experimental.pallas import tpu as pltpu

F32 = jnp.float32
BF16 = jnp.bfloat16

D_MODEL = 1024
N_HEADS = 8
SB_WIDTH = 512
MLA_Q_RANK = 384
MLA_KV_RANK = 256
MLA_ROPE_DIM = 32
MLA_QK_DIM = 96
D_FF = 2816
ROPE_THETA = 10000.0
EPS = 1e-6
SB_SCALE = 1.0 / math.sqrt(64.0)
MLA_SCALE = 1.0 / math.sqrt(96.0)
NEG_BIG = -1e30

ADAM_LR = 0.001
ADAM_B1 = 0.9
ADAM_B2 = 0.999
ADAM_EPS = 1e-08
ADAM_WD = 0.01
ADAM_STEP = 10

N_DEV = 8
LANES = 128
TQ = 512
TK = 128
DIAG_TILES = TQ // TK
FWD_UNROLL = 4
BWD_UNROLL = 2
VMEM_LIMIT = 56 << 20

EXT_QKV = 0
EXT_CQ = 1536
EXT_CKV = 1920
EXT_KR = 2176
EXT_GL = 2304
EXT_N = 4352

EARLY = (("w_in", (1024, 532), 1), ("w_uq", (384, 96), 1), ("w_ukv", (256, 128), 1))
LATE = (("w_proj_sb", (512, 128), 1), ("w_proj_mla", (512, 128), 1), ("w_out", (128, 1024), 0),
        ("w_gate_up", (1024, 704), 1), ("w_down", (352, 1024), 0))
SHARDED = EARLY + LATE
EARLY_ROWS = 4800
LATE_ROWS = 10752
SMALL = (("norm_mix_pre", 1024), ("norm_mix_post", 1024), ("b_gate", 2048), ("q_norm", 384),
         ("kv_norm", 256), ("norm_ffn_pre", 1024), ("norm_ffn_post", 1024))
SMALL_ROWS = 56


def _dot(a, b):
    return jnp.dot(a, b, preferred_element_type=F32)


def _dot_nt(a, b):
    return lax.dot_general(a, b, (((1,), (1,)), ((), ())), preferred_element_type=F32)


def _dot_tn(a, b):
    return lax.dot_general(a, b, (((0,), (0,)), ((), ())), preferred_element_type=F32)


def _rms(x):
    r = lax.rsqrt(jnp.mean(x * x, axis=-1, keepdims=True) + EPS)
    return x * r, r


def _rms_bwd(dn, n, r):
    return r * (dn - n * jnp.mean(dn * n, axis=-1, keepdims=True))


def _colsum8(x):
    return jnp.sum(x.reshape(x.shape[0] // 8, 8, x.shape[1]), axis=0)


def _split(x):
    hi = x.astype(BF16)
    return hi, (x - hi.astype(F32)).astype(BF16)


def _rot(x):
    lane = lax.broadcasted_iota(jnp.int32, x.shape, 1)
    up = pltpu.roll(x, 112, 1)
    down = pltpu.roll(x, 16, 1)
    return jnp.where((lane >= 64) & (lane < 80), -up, jnp.where((lane >= 80) & (lane < 96), down, 0.0))


def _params(sem):
    return pltpu.CompilerParams(dimension_semantics=sem, vmem_limit_bytes=VMEM_LIMIT)


def _rows_call(name, body, n_rows, tm, row_ins, const_ins, row_outs, acc_outs):
    in_specs = [pl.BlockSpec((tm, a.shape[1]), lambda i: (i, 0)) for a in row_ins]
    in_specs += [pl.BlockSpec(a.shape, lambda i: (0, 0), pipeline_mode=pl.Buffered(1)) for a in const_ins]
    out_specs = [pl.BlockSpec((tm, n), lambda i: (i, 0)) for n, _ in row_outs]
    out_specs += [pl.BlockSpec(s, lambda i: (0, 0)) for s in acc_outs]
    out_shape = [jax.ShapeDtypeStruct((n_rows, n), dt) for n, dt in row_outs]
    out_shape += [jax.ShapeDtypeStruct(s, F32) for s in acc_outs]
    return pl.pallas_call(
        body, grid=(n_rows // tm,), in_specs=in_specs, out_specs=out_specs, out_shape=out_shape,
        name=name, compiler_params=_params(("arbitrary",)),
    )(*row_ins, *const_ins)


def _in_proj(x, g_pre, w_ext, tm):
    def body(x_ref, g_ref, w_ref, qkv_ref, cq_ref, ckv_ref, kr_ref, gl_ref, h_ref):
        n, _ = _rms(x_ref[...])
        hb = (n * g_ref[...]).astype(BF16)
        h_ref[...] = hb
        for c in range(0, 1536, 512):
            qkv_ref[:, c:c + 512] = _dot(hb, w_ref[:, c:c + 512]).astype(BF16)
        cq_ref[...] = _dot(hb, w_ref[:, EXT_CQ:EXT_CKV])
        ckv_ref[...] = _dot(hb, w_ref[:, EXT_CKV:EXT_KR])
        kr_ref[...] = _dot(hb, w_ref[:, EXT_KR:EXT_GL])
        for c in range(0, 2048, 512):
            gl_ref[:, c:c + 512] = _dot(hb, w_ref[:, EXT_GL + c:EXT_GL + c + 512])

    return _rows_call("in_proj", body, x.shape[0], tm, [x], [g_pre, w_ext],
                      [(1536, BF16), (384, F32), (256, F32), (128, F32), (2048, F32), (1024, BF16)], [])


def _mla_up(cq, ckv, kr, cr, sr, q_norm, kv_norm, wa, wk, wv, tm):
    def body(cq_ref, ckv_ref, kr_ref, cr_ref, sr_ref, qn_ref, kvn_ref, wa_ref, wk_ref, wv_ref,
             q_ref, k_ref, v_ref, cqn_ref, ckvn_ref):
        nq, _ = _rms(cq_ref[...])
        cqn = (nq * qn_ref[...]).astype(BF16)
        cqn_ref[...] = cqn
        nk, _ = _rms(ckv_ref[...])
        ckvn = (nk * kvn_ref[...]).astype(BF16)
        ckvn_ref[...] = ckvn
        cr = cr_ref[...]
        sr = sr_ref[...]
        lane = lax.broadcasted_iota(jnp.int32, cr.shape, 1)
        cm = cr + (lane < 64).astype(F32)
        kr = kr_ref[...]
        krp = kr * cr + _rot(kr) * sr
        for h in range(N_HEADS):
            hs = slice(h * LANES, (h + 1) * LANES)
            a = _dot(cqn, wa_ref[:, hs])
            q_ref[:, hs] = (a * cm + _rot(a) * sr).astype(BF16)
            k_ref[:, hs] = (_dot(ckvn, wk_ref[:, hs]) + krp).astype(BF16)
        v_ref[...] = _dot(ckvn, wv_ref[...]).astype(BF16)

    return _rows_call("mla_up", body, cq.shape[0], tm, [cq, ckv, kr, cr, sr], [q_norm, kv_norm, wa, wk, wv],
                      [(1024, BF16), (1024, BF16), (512, BF16), (384, BF16), (256, BF16)], [])


def _mix_out(o_sb, o_mla, gl, x, b_gate, g_post, w_sb, w_mla, w_out, tm):
    def body(osb_ref, omla_ref, gl_ref, x_ref, b_ref, gp_ref, wsb_ref, wmla_ref, wout_ref,
             x1_ref, y_ref, mb_ref, osbb_ref, omlab_ref):
        osb = osb_ref[...].astype(BF16)
        omla = omla_ref[...].astype(BF16)
        osbb_ref[...] = osb
        omlab_ref[...] = omla
        psb = _dot(osb, wsb_ref[...])
        pmla = _dot(omla, wmla_ref[...])
        gates = jax.nn.sigmoid(gl_ref[...] + b_ref[...])
        mb = (gates[:, :D_MODEL] * psb + gates[:, D_MODEL:] * pmla).astype(BF16)
        mb_ref[...] = mb
        y = _dot(mb, wout_ref[...])
        y_ref[...] = y
        n, _ = _rms(y)
        x1_ref[...] = x_ref[...] + n * gp_ref[...]

    return _rows_call("mix_out", body, x.shape[0], tm, [o_sb, o_mla, gl, x], [b_gate, g_post, w_sb, w_mla, w_out],
                      [(1024, F32), (1024, F32), (1024, BF16), (512, BF16), (512, BF16)], [])


FF_CHUNK = 1408


def _ffn_fwd(x1, target, g_pre, g_post, w_gu, w_down, tm):
    def body(x1_ref, t_ref, gpre_ref, gpost_ref, wgu_ref, wd_ref, dx2_ref, f_ref, h2_ref, loss_ref):
        x1 = x1_ref[...]
        n, _ = _rms(x1)
        h2 = (n * gpre_ref[...]).astype(BF16)
        h2_ref[...] = h2
        f = jnp.zeros((tm, D_MODEL), F32)
        for c in range(0, D_FF, FF_CHUNK):
            g = _dot(h2, wgu_ref[:, c:c + FF_CHUNK])
            u = _dot(h2, wgu_ref[:, D_FF + c:D_FF + c + FF_CHUNK])
            act = (g * jax.nn.sigmoid(g) * u).astype(BF16)
            f = f + _dot(act, wd_ref[c:c + FF_CHUNK, :])
        f_ref[...] = f
        nf, _ = _rms(f)
        err = x1 + nf * gpost_ref[...] - t_ref[...]
        dx2_ref[...] = err * (1.0 / D_MODEL)
        e8 = _colsum8(err * err)
        part = e8[:, 0:LANES]
        for c in range(LANES, D_MODEL, LANES):
            part = part + e8[:, c:c + LANES]

        @pl.when(pl.program_id(0) == 0)
        def _():
            loss_ref[...] = jnp.zeros_like(loss_ref)

        loss_ref[...] += part

    return _rows_call("ffn_fwd", body, x1.shape[0], tm, [x1, target], [g_pre, g_post, w_gu, w_down],
                      [(1024, F32), (1024, F32), (1024, BF16)], [(8, LANES)])


def _ffn_bwd(dx2, f, x1, g_pre, g_post, w_gu, w_down, tm):
    def body(dx2_ref, f_ref, x1_ref, gpre_ref, gpost_ref, wgu_ref, wd_ref,
             dx1_ref, act_ref, dgu_ref, dfb_ref, dgpre_ref, dgpost_ref):
        @pl.when(pl.program_id(0) == 0)
        def _():
            dgpre_ref[...] = jnp.zeros_like(dgpre_ref)
            dgpost_ref[...] = jnp.zeros_like(dgpost_ref)

        dx2 = dx2_ref[...]
        nf, rf = _rms(f_ref[...])
        dgpost_ref[...] += _colsum8(dx2 * nf)
        dfb = _rms_bwd(dx2 * gpost_ref[...], nf, rf).astype(BF16)
        dfb_ref[...] = dfb
        x1 = x1_ref[...]
        n1, r1 = _rms(x1)
        h2 = (n1 * gpre_ref[...]).astype(BF16)
        dh2 = jnp.zeros((tm, D_MODEL), F32)
        for c in range(0, D_FF, FF_CHUNK):
            cs, us = slice(c, c + FF_CHUNK), slice(D_FF + c, D_FF + c + FF_CHUNK)
            g = _dot(h2, wgu_ref[:, cs])
            u = _dot(h2, wgu_ref[:, us])
            sg = jax.nn.sigmoid(g)
            si = g * sg
            act_ref[:, cs] = (si * u).astype(BF16)
            dact = _dot_nt(dfb, wd_ref[cs, :])
            dg = (dact * u * (sg * (1.0 + g * (1.0 - sg)))).astype(BF16)
            du = (dact * si).astype(BF16)
            dgu_ref[:, cs] = dg
            dgu_ref[:, us] = du
            dh2 = dh2 + _dot_nt(dg, wgu_ref[:, cs]) + _dot_nt(du, wgu_ref[:, us])
        dgpre_ref[...] += _colsum8(dh2 * n1)
        dx1_ref[...] = dx2 + _rms_bwd(dh2 * gpre_ref[...], n1, r1)

    return _rows_call("ffn_bwd", body, dx2.shape[0], tm, [dx2, f, x1], [g_pre, g_post, w_gu, w_down],
                      [(1024, F32), (D_FF, BF16), (2 * D_FF, BF16), (1024, BF16)], [(8, 1024), (8, 1024)])


def _mix_bwd(dx1, y, o_sb, o_mla, gl, b_gate, g_post, w_sb, w_mla, w_out, tm):
    def body(dx1_ref, y_ref, osb_ref, omla_ref, gl_ref, b_ref, gp_ref, wsb_ref, wmla_ref, wout_ref,
             dyb_ref, dpsb_ref, dpmla_ref, dgl_ref, dosb_ref, domla_ref, dgpost_ref, dbg_ref):
        @pl.when(pl.program_id(0) == 0)
        def _():
            dgpost_ref[...] = jnp.zeros_like(dgpost_ref)
            dbg_ref[...] = jnp.zeros_like(dbg_ref)

        dx1 = dx1_ref[...]
        ny, ry = _rms(y_ref[...])
        dgpost_ref[...] += _colsum8(dx1 * ny)
        dyb = _rms_bwd(dx1 * gp_ref[...], ny, ry).astype(BF16)
        dyb_ref[...] = dyb
        dm = _dot_nt(dyb, wout_ref[...])
        psb = _dot(osb_ref[...].astype(BF16), wsb_ref[...])
        pmla = _dot(omla_ref[...].astype(BF16), wmla_ref[...])
        gates = jax.nn.sigmoid(gl_ref[...] + b_ref[...])
        g0, g1 = gates[:, :D_MODEL], gates[:, D_MODEL:]
        dpsb = (dm * g0).astype(BF16)
        dpmla = (dm * g1).astype(BF16)
        dpsb_ref[...] = dpsb
        dpmla_ref[...] = dpmla
        dgl0 = dm * psb * g0 * (1.0 - g0)
        dgl1 = dm * pmla * g1 * (1.0 - g1)
        dgl_ref[:, :D_MODEL] = dgl0.astype(BF16)
        dgl_ref[:, D_MODEL:] = dgl1.astype(BF16)
        dbg_ref[:, :D_MODEL] += _colsum8(dgl0)
        dbg_ref[:, D_MODEL:] += _colsum8(dgl1)
        dosb_ref[...] = _dot_nt(dpsb, wsb_ref[...]).astype(BF16)
        domla_ref[...] = _dot_nt(dpmla, wmla_ref[...]).astype(BF16)

    return _rows_call("mix_bwd", body, dx1.shape[0], tm, [dx1, y, o_sb, o_mla, gl],
                      [b_gate, g_post, w_sb, w_mla, w_out],
                      [(1024, BF16), (1024, BF16), (1024, BF16), (2048, BF16), (512, BF16), (512, BF16)],
                      [(8, 1024), (8, 2048)])


def _mla_up_bwd(dq, dk, dv, cq, ckv, cr, sr, q_norm, kv_norm, wa, wk, wv, tm):
    def body(dq_ref, dk_ref, dv_ref, cq_ref, ckv_ref, cr_ref, sr_ref, qn_ref, kvn_ref, wa_ref, wk_ref, wv_ref,
             da_ref, dkb_ref, dvb_ref, dlat_ref, dqn_ref, dkvn_ref):
        @pl.when(pl.program_id(0) == 0)
        def _():
            dqn_ref[...] = jnp.zeros_like(dqn_ref)
            dkvn_ref[...] = jnp.zeros_like(dkvn_ref)

        cr = cr_ref[...]
        sr = sr_ref[...]
        lane = lax.broadcasted_iota(jnp.int32, cr.shape, 1)
        cm = cr + (lane < 64).astype(F32)
        nq, rq = _rms(cq_ref[...])
        nk, rk = _rms(ckv_ref[...])
        dcqn = jnp.zeros((tm, MLA_Q_RANK), F32)
        dckvn = jnp.zeros((tm, MLA_KV_RANK), F32)
        dkrp = jnp.zeros((tm, LANES), F32)
        for h in range(N_HEADS):
            hs = slice(h * LANES, (h + 1) * LANES)
            dqh = dq_ref[:, hs]
            da = (dqh * cm - _rot(dqh * sr)).astype(BF16)
            da_ref[:, hs] = da
            dcqn = dcqn + _dot_nt(da, wa_ref[:, hs])
            dkh = dk_ref[:, hs]
            dkb = dkh.astype(BF16)
            dkb_ref[:, hs] = dkb
            dckvn = dckvn + _dot_nt(dkb, wk_ref[:, hs])
            dkrp = dkrp + dkh
        dvb = dv_ref[...].astype(BF16)
        dvb_ref[...] = dvb
        dckvn = dckvn + _dot_nt(dvb, wv_ref[...])
        dkr = dkrp * cr - _rot(dkrp * sr)
        dqn_ref[...] += _colsum8(dcqn * nq)
        dkvn_ref[...] += _colsum8(dckvn * nk)
        dlat_ref[:, 0:384] = _rms_bwd(dcqn * qn_ref[...], nq, rq).astype(BF16)
        dlat_ref[:, 384:640] = _rms_bwd(dckvn * kvn_ref[...], nk, rk).astype(BF16)
        dlat_ref[:, 640:768] = dkr.astype(BF16)

    return _rows_call("mla_up_bwd", body, dq.shape[0], tm, [dq, dk, dv, cq, ckv, cr, sr],
                      [q_norm, kv_norm, wa, wk, wv],
                      [(1024, BF16), (1024, BF16), (512, BF16), (768, BF16)], [(8, 384), (8, 256)])


def _in_proj_bwd(x, dx1, dq_sb, dk_sb, dv_sb, dlat, dgl, g_pre, w_ext, tm):
    def body(x_ref, dx1_ref, dq_ref, dk_ref, dv_ref, dlat_ref, dgl_ref, g_ref, w_ref,
             dx_ref, dproj_ref, dg_ref):
        @pl.when(pl.program_id(0) == 0)
        def _():
            dg_ref[...] = jnp.zeros_like(dg_ref)

        dproj_ref[:, 0:512] = dq_ref[...].astype(BF16)
        dproj_ref[:, 512:1024] = dk_ref[...].astype(BF16)
        dproj_ref[:, 1024:1536] = dv_ref[...].astype(BF16)
        dproj_ref[:, EXT_CQ:EXT_GL] = dlat_ref[...]
        dproj_ref[:, EXT_GL:EXT_N] = dgl_ref[...]
        dh = jnp.zeros((tm, D_MODEL), F32)
        for c in range(0, EXT_N, 2176):
            dh = dh + _dot_nt(dproj_ref[:, c:c + 2176], w_ref[:, c:c + 2176])
        n, r = _rms(x_ref[...])
        dg_ref[...] += _colsum8(dh * n)
        dx_ref[...] = dx1_ref[...] + _rms_bwd(dh * g_ref[...], n, r)

    return _rows_call("in_proj_bwd", body, x.shape[0], tm, [x, dx1, dq_sb, dk_sb, dv_sb, dlat, dgl],
                      [g_pre, w_ext], [(1024, F32), (EXT_N, BF16)], [(8, 1024)])


def _head_masked(x):
    lane = lax.broadcasted_iota(jnp.int32, x.shape, 1)
    zero = jnp.zeros_like(x)
    return jnp.where(lane < 64, x, zero), jnp.where(lane >= 64, x, zero)


def _cum_weights():
    row = lax.broadcasted_iota(jnp.int32, (TK, TK), 0)
    col = lax.broadcasted_iota(jnp.int32, (TK, TK), 1)
    half = jnp.concatenate([(row > col).astype(BF16), jnp.ones((TK, TK), BF16)], axis=1)
    return jnp.concatenate([half, half], axis=0)


def _split_cat(x):
    hi, lo = _split(x)
    return jnp.concatenate([hi, lo], axis=1)


def _sweep(i, tiles, unroll, skip_rows=False):
    if skip_rows:
        for t in range(DIAG_TILES // unroll):
            top = DIAG_TILES - 1 - t * unroll
            r0 = (top - unroll + 1) * TK
            col = lax.broadcasted_iota(jnp.int32, (TQ - r0, TK), 1)
            tiles(i * DIAG_TILES + top, [(top - u) * TK - r0 + col for u in range(unroll)], r0)
    else:
        col = lax.broadcasted_iota(jnp.int32, (TQ, TK), 1)

        def diag(t, _):
            top = DIAG_TILES - 1 - t * unroll
            tiles(i * DIAG_TILES + top, [(top - u) * TK + col for u in range(unroll)])
            return 0

        lax.fori_loop(0, DIAG_TILES // unroll, diag, 0)

    def full(g, _):
        tiles(i * DIAG_TILES - 1 - g * unroll, [None] * unroll)
        return 0

    lax.fori_loop(0, (i * DIAG_TILES) // unroll, full, 0)


def _causal(key, strict):
    if key is None:
        return None
    row = lax.broadcasted_iota(jnp.int32, key.shape, 0)
    return key < row if strict else key <= row


def _sb_logs(z, valid, w_cum):
    soft = jnp.log(1.0 + jnp.exp(-jnp.abs(z)))
    lsm = -jnp.maximum(z, 0.0) - soft
    cat = _split_cat(lsm if valid is None else jnp.where(valid, lsm, 0.0))
    return z + lsm, _dot(cat, w_cum)


def _sb_weight(log_beta, cs, r, valid):
    a = jnp.exp(log_beta + cs[:, :TK] + r)
    if valid is not None:
        a = jnp.where(valid, a, 0.0)
    return a, r + cs[:, TK:]


def _block_diag(x):
    x0, x1 = x[:, :LANES], x[:, LANES:]
    zero = jnp.zeros_like(x0)
    return jnp.concatenate([jnp.concatenate([x0, zero], axis=1), jnp.concatenate([zero, x1], axis=1)], axis=0)


def _attn_fwd(qkv, q, k, v):
    s_len = qkv.shape[0]

    def body(qs_ref, ks_ref, vs_ref, qm_ref, km_ref, vm_ref, osb_ref, omla_ref, lse_ref,
             sacc_ref, r_ref, macc_ref, m_ref):
        i = pl.program_id(1)
        lane = lax.broadcasted_iota(jnp.int32, (TQ, LANES), 1)
        r2 = lax.broadcasted_iota(jnp.int32, (2 * TK, LANES), 0)
        c2 = lax.broadcasted_iota(jnp.int32, (2 * TK, LANES), 1)
        head_ones = ((r2 < TK) == (c2 < 64)).astype(BF16)
        w_cum = _cum_weights()
        qs = qs_ref[...] * SB_SCALE
        qm = qm_ref[...]
        sacc_ref[...] = jnp.zeros_like(sacc_ref)
        r_ref[...] = jnp.zeros_like(r_ref)
        macc_ref[...] = jnp.zeros_like(macc_ref)
        m_ref[...] = jnp.full(m_ref.shape, NEG_BIG, F32)

        def sb_scores(top, valids):
            out = []
            for u, valid in enumerate(valids):
                off = pl.multiple_of((top - u) * TK, TK)
                z01 = _dot_nt(qs, jnp.concatenate(_head_masked(ks_ref[pl.ds(off, TK), :]), axis=0))
                out.append([_sb_logs(z, valid, w_cum) for z in (z01[:, :TK], z01[:, TK:])])
            return out

        def sb_accumulate(top, valids, scores):
            parts = [[None, None] for _ in valids]
            for hh in range(2):
                r = r_ref[hh]
                for u, valid in enumerate(valids):
                    a, r = _sb_weight(*scores[u][hh], r, valid)
                    parts[u][hh] = _split_cat(a)
                r_ref[hh] = r
            vs = []
            for u in range(len(valids)):
                off = pl.multiple_of((top - u) * TK, TK)
                v0, v1 = _head_masked(vs_ref[pl.ds(off, TK), :])
                vs += [v0, v0, v1, v1]
            sacc_ref[...] += _dot(jnp.concatenate([p for pair in parts for p in pair], axis=1),
                                  jnp.concatenate(vs, axis=0))

        def mla_scores(top, valids):
            s01s = []
            for u in range(len(valids)):
                off = pl.multiple_of((top - u) * TK, TK)
                s01s.append(_dot_nt(qm, _block_diag(km_ref[pl.ds(off, TK), :])))
            heads = []
            for hh in range(2):
                ss = []
                for u, valid in enumerate(valids):
                    s = s01s[u][:, hh * TK:(hh + 1) * TK] * MLA_SCALE
                    ss.append(s if valid is None else jnp.where(valid, s, NEG_BIG))
                m_old = m_ref[hh]
                m = jnp.maximum(m_old, jnp.max(functools.reduce(jnp.maximum, ss), axis=1, keepdims=True))
                m_ref[hh] = m
                heads.append((ss, m, jnp.exp(m_old - m)))
            return heads

        def mla_accumulate(top, heads):
            vs = []
            for u in range(len(heads[0][0])):
                off = pl.multiple_of((top - u) * TK, TK)
                vv = jnp.concatenate(_head_masked(vm_ref[pl.ds(off, TK), :]), axis=0)
                vs.append(jnp.concatenate([vv, head_ones], axis=1))
            ps = [[jnp.exp(s - m).astype(BF16) for s in ss] for ss, m, _ in heads]
            scale = jnp.where(lane < 64, heads[0][2], heads[1][2])
            p_all = jnp.concatenate([ps[hh][u] for u in range(len(vs)) for hh in range(2)], axis=1)
            macc_ref[...] = (macc_ref[...] * jnp.concatenate([scale, scale], axis=1)
                             + _dot(p_all, jnp.concatenate(vs, axis=0)))

        def tiles(top, keys):
            strict = [_causal(key, True) for key in keys]
            heads = mla_scores(top, [_causal(key, False) for key in keys])
            scores = sb_scores(top, strict)
            mla_accumulate(top, heads)
            sb_accumulate(top, strict, scores)

        _sweep(i, tiles, FWD_UNROLL)
        osb_ref[...] = sacc_ref[...]
        acc = macc_ref[...]
        den = acc[:, LANES:]
        omla_ref[...] = acc[:, :LANES] / den
        for hh, mask in enumerate((lane < 64, lane >= 64)):
            l = jnp.max(jnp.where(mask, den, 0.0), axis=1, keepdims=True)
            lse_ref[hh] = jnp.broadcast_to(m_ref[hh] + jnp.log(l), (TQ, LANES))

    tile = pl.BlockSpec((TQ, LANES), lambda h, i: (i, h))
    return pl.pallas_call(
        body, grid=(4, s_len // TQ),
        in_specs=[tile,
                  pl.BlockSpec((s_len, LANES), lambda h, i: (0, 4 + h)),
                  pl.BlockSpec((s_len, LANES), lambda h, i: (0, 8 + h)),
                  pl.BlockSpec((TQ, 2 * LANES), lambda h, i: (i, h)),
                  pl.BlockSpec((s_len, 2 * LANES), lambda h, i: (0, h)),
                  pl.BlockSpec((s_len, LANES), lambda h, i: (0, h))],
        out_specs=[tile, tile, pl.BlockSpec((2, TQ, LANES), lambda h, i: (h, i, 0))],
        out_shape=[jax.ShapeDtypeStruct((s_len, SB_WIDTH), F32), jax.ShapeDtypeStruct((s_len, 512), F32),
                   jax.ShapeDtypeStruct((N_HEADS, s_len, LANES), F32)],
        scratch_shapes=[pltpu.VMEM((TQ, LANES), F32), pltpu.VMEM((2, TQ, LANES), F32),
                        pltpu.VMEM((TQ, 2 * LANES), F32), pltpu.VMEM((2, TQ, 1), F32)],
        name="attn_fwd", compiler_params=_params(("arbitrary", "arbitrary")),
    )(qkv, qkv, qkv, q, k, v)


def _row_dots(do, o):
    prod = do.astype(F32) * o
    p0, p1 = _head_masked(prod)
    return tuple(jnp.broadcast_to(jnp.sum(p, axis=1, keepdims=True), prod.shape) for p in (p0, p1))


def _attn_bwd(qkv, do_sb, o_sb, q, k, v, do_mla, o_mla, lse, after):
    s_len = qkv.shape[0]
    n_q = s_len // TQ

    def body(qs_ref, ks_ref, vs_ref, dos_ref, os_ref, qm_ref, km_ref, vm_ref, dom_ref, om_ref, lse_ref, after_ref,
             dqs_ref, dqm_ref, dks_hbm, dvs_hbm, dkm_hbm, dvm_hbm,
             dqs_acc, dqm_acc, dks_acc, dvs_acc, dkm_acc, dvm_acc, r_ref, g_ref, ds_ref, dm_ref, out_sems):
        h = pl.program_id(0)
        i = pl.program_id(1)

        @pl.when(i == 0)
        def _():
            for acc in (dks_acc, dvs_acc, dkm_acc, dvm_acc):
                acc[...] = jnp.zeros_like(acc)

        w_cum = _cum_weights()
        qs = qs_ref[...] * SB_SCALE
        dos = dos_ref[...]
        ds_ref[0], ds_ref[1] = _row_dots(dos, os_ref[...])
        qm = qm_ref[...]
        dom = dom_ref[...]
        dm_ref[0], dm_ref[1] = _row_dots(dom, om_ref[...])
        qs_heads, dos_heads, dom_heads = _head_masked(qs), _head_masked(dos), _head_masked(dom)
        by_head = lambda pair, rows: jnp.concatenate([x[rows] for x in pair], axis=0)
        all_rows = slice(0, TQ)
        qs_rows, dos_rows, dom_rows = (by_head(p, all_rows) for p in (qs_heads, dos_heads, dom_heads))
        qm_diag = _block_diag(qm)
        for ref in (dqs_acc, dqm_acc, r_ref, g_ref):
            ref[...] = jnp.zeros_like(ref)
        heads = (slice(0, TK), slice(TK, 2 * TK))

        def weigh(log_beta, cs, r, da, valid):
            a, r = _sb_weight(log_beta, cs, r, valid)
            g = a * da
            return a.astype(BF16), g, _split_cat(g), r

        def logit_grad(g, gs, carried, d, log_beta, valid):
            upto = d - (gs[:, :TK] + carried)
            dz = g - jnp.exp(log_beta) * upto
            if valid is not None:
                dz = jnp.where(valid, dz, 0.0)
            return dz.astype(BF16), carried + gs[:, TK:]

        def tiles(top, keys, r0=0):
            n = len(keys)
            rows = slice(r0, TQ)
            strict = [_causal(key, True) for key in keys]
            loose = [_causal(key, False) for key in keys]
            offs = [pl.multiple_of((top - u) * TK, TK) for u in range(n)]
            kds = [_block_diag(km_ref[pl.ds(off, TK), :]) for off in offs]
            vms = [jnp.concatenate(_head_masked(vm_ref[pl.ds(off, TK), :]), axis=0) for off in offs]
            s01s = [_dot_nt(qm[rows], kd) for kd in kds]
            dp01s = [_dot_nt(dom[rows], vv) for vv in vms]
            kks = [jnp.concatenate(_head_masked(ks_ref[pl.ds(off, TK), :]), axis=0) for off in offs]
            vvs = [jnp.concatenate(_head_masked(vs_ref[pl.ds(off, TK), :]), axis=0) for off in offs]
            z01s = [_dot_nt(qs[rows], kk) for kk in kks]
            da01s = [_dot_nt(dos[rows], vv) for vv in vvs]
            logs = [[_sb_logs(z01s[u][:, hs], strict[u], w_cum) for hs in heads] for u in range(n)]

            dss = [[None, None] for _ in range(n)]
            ps = [[None, None] for _ in range(n)]
            for u in range(n):
                for hh, hs in enumerate(heads):
                    p = jnp.exp(s01s[u][:, hs] * MLA_SCALE - lse_ref[hh, rows, :])
                    if loose[u] is not None:
                        p = jnp.where(loose[u], p, 0.0)
                    dss[u][hh] = (p * (dp01s[u][:, hs] - dm_ref[hh, rows, :]) * MLA_SCALE).astype(BF16)
                    ps[u][hh] = p.astype(BF16)

            dzs = [[None, None] for _ in range(n)]
            avs = [[None, None] for _ in range(n)]
            sums = [[None, None] for _ in range(n)]
            for hh, hs in enumerate(heads):
                r = r_ref[hh, rows, :]
                for u in range(n):
                    avs[u][hh], g, cat, r = weigh(*logs[u][hh], r, da01s[u][:, hs], strict[u])
                    sums[u][hh] = (g, _dot(cat, w_cum))
                r_ref[hh, rows, :] = r

            dqm_acc[rows, :] += _dot(jnp.concatenate([d for pair in dss for d in pair], axis=1),
                                     jnp.concatenate(kds, axis=0))
            span = pl.ds(offs[-1], n * TK)
            by_key = lambda pairs: jnp.concatenate([jnp.concatenate(pair, axis=0) for pair in pairs[::-1]], axis=1)
            whole = r0 == 0
            dkm_acc[span, :] += _dot_tn(by_key(dss), qm_diag if whole else _block_diag(qm[rows]))
            dvm_acc[span, :] += _dot_tn(by_key(ps), dom_rows if whole else by_head(dom_heads, rows))

            for hh in range(2):
                carried = g_ref[hh, rows, :]
                for u in range(n):
                    dzs[u][hh], carried = logit_grad(*sums[u][hh], carried, ds_ref[hh, rows, :], logs[u][hh][0],
                                                     strict[u])
                g_ref[hh, rows, :] = carried
            dqs_acc[rows, :] += _dot(jnp.concatenate([dz for pair in dzs for dz in pair], axis=1),
                                     jnp.concatenate(kks, axis=0))
            dks_acc[span, :] += _dot_tn(by_key(dzs), qs_rows if whole else by_head(qs_heads, rows))
            dvs_acc[span, :] += _dot_tn(by_key(avs), dos_rows if whole else by_head(dos_heads, rows))

        _sweep(i, tiles, BWD_UNROLL, skip_rows=True)
        dqs_ref[...] = dqs_acc[...] * SB_SCALE
        dqm_ref[...] = dqm_acc[...]

        @pl.when(i == n_q - 1)
        def _():
            narrow = pl.ds(pl.multiple_of(h * LANES, LANES), LANES)
            wide = pl.ds(pl.multiple_of(h * 2 * LANES, 2 * LANES), 2 * LANES)
            copies = [pltpu.make_async_copy(dks_acc, dks_hbm.at[:, narrow], out_sems.at[0]),
                      pltpu.make_async_copy(dvs_acc, dvs_hbm.at[:, narrow], out_sems.at[1]),
                      pltpu.make_async_copy(dkm_acc, dkm_hbm.at[:, wide], out_sems.at[2]),
                      pltpu.make_async_copy(dvm_acc, dvm_hbm.at[:, narrow], out_sems.at[3])]
            for cp in copies:
                cp.start()
            for cp in copies:
                cp.wait()

    tile = pl.BlockSpec((TQ, LANES), lambda h, i: (i, h))
    wide_tile = pl.BlockSpec((TQ, 2 * LANES), lambda h, i: (i, h))
    once = pl.Buffered(1)
    hbm = pl.BlockSpec(memory_space=pl.ANY)
    return pl.pallas_call(
        body, grid=(4, n_q),
        in_specs=[tile,
                  pl.BlockSpec((s_len, LANES), lambda h, i: (0, 4 + h), pipeline_mode=once),
                  pl.BlockSpec((s_len, LANES), lambda h, i: (0, 8 + h), pipeline_mode=once),
                  tile, tile, wide_tile,
                  pl.BlockSpec((s_len, 2 * LANES), lambda h, i: (0, h), pipeline_mode=once),
                  pl.BlockSpec((s_len, LANES), lambda h, i: (0, h), pipeline_mode=once),
                  tile, tile, pl.BlockSpec((2, TQ, LANES), lambda h, i: (h, i, 0)),
                  pl.BlockSpec((8, LANES), lambda h, i: (0, 0))],
        out_specs=[tile, wide_tile, hbm, hbm, hbm, hbm],
        out_shape=[jax.ShapeDtypeStruct((s_len, SB_WIDTH), F32), jax.ShapeDtypeStruct((s_len, 1024), F32),
                   jax.ShapeDtypeStruct((s_len, SB_WIDTH), F32), jax.ShapeDtypeStruct((s_len, SB_WIDTH), F32),
                   jax.ShapeDtypeStruct((s_len, 1024), F32), jax.ShapeDtypeStruct((s_len, 512), F32)],
        scratch_shapes=[pltpu.VMEM((TQ, LANES), F32), pltpu.VMEM((TQ, 2 * LANES), F32),
                        pltpu.VMEM((s_len, LANES), F32), pltpu.VMEM((s_len, LANES), F32),
                        pltpu.VMEM((s_len, 2 * LANES), F32), pltpu.VMEM((s_len, LANES), F32),
                        pltpu.VMEM((2, TQ, LANES), F32), pltpu.VMEM((2, TQ, LANES), F32),
                        pltpu.VMEM((2, TQ, LANES), F32), pltpu.VMEM((2, TQ, LANES), F32),
                        pltpu.SemaphoreType.DMA((4,))],
        name="attn_bwd", compiler_params=_params(("arbitrary", "arbitrary")),
    )(qkv, qkv, qkv, do_sb, o_sb, q, k, v, do_mla, o_mla, lse, after)


def _pick(n, options):
    for t in options:
        if n % t == 0:
            return t
    raise ValueError(n)


def _matmul_tn(name, a, b):
    s_len, m = a.shape
    n = b.shape[1]
    tm = _pick(m, (1024, 1408, 2176, 512))
    tn = _pick(n, (1024, 512, 384, 256))
    tk = _pick(s_len, (1024, 512, 256, 128))

    def body(a_ref, b_ref, o_ref):
        @pl.when(pl.program_id(2) == 0)
        def _():
            o_ref[...] = jnp.zeros_like(o_ref)

        o_ref[...] += _dot_tn(a_ref[...], b_ref[...])

    return pl.pallas_call(
        body, grid=(m // tm, n // tn, s_len // tk),
        in_specs=[pl.BlockSpec((tk, tm), lambda i, j, l: (l, i)), pl.BlockSpec((tk, tn), lambda i, j, l: (l, j))],
        out_specs=pl.BlockSpec((tm, tn), lambda i, j, l: (i, j)),
        out_shape=jax.ShapeDtypeStruct((m, n), F32),
        name=name, compiler_params=_params(("arbitrary", "arbitrary", "arbitrary")),
    )(a, b)


def _mesh_pos():
    return lax.axis_index("x"), lax.axis_index("y"), lax.axis_index("c")


def _peer(pos, k):
    x, y, c = pos
    return (1 - x if k & 4 else x, 1 - y if k & 2 else y, 1 - c if k & 1 else c)


def _flat_index(pos):
    return 4 * pos[0] + 2 * pos[1] + pos[2]


def _all_gather(shard):
    rows = shard.shape[0]

    def body(x_ref, out_ref, send_sems, recv_sems, local_sem):
        me = _mesh_pos()
        x, y, c = me
        sibling = (x, y, 1 - c)
        chips = [(1 - x, y), (x, 1 - y), (1 - x, 1 - y)]

        def copy(k, block, to, src=None):
            slot = out_ref.at[_flat_index(block)]
            return pltpu.make_async_remote_copy(
                src_ref=slot if src is None else src, dst_ref=slot,
                send_sem=send_sems.at[k], recv_sem=recv_sems.at[k],
                device_id=to, device_id_type=pl.DeviceIdType.MESH)

        mine = pltpu.make_async_copy(x_ref, out_ref.at[_flat_index(me)], local_sem)
        mine.start()
        first = [copy(0, me, sibling, src=x_ref)]
        first += [copy(1 + j, me, (*chip, c), src=x_ref) for j, chip in enumerate(chips)]
        for cp in first:
            cp.start()
        passed = [copy(4 + j, (*chip, c), sibling) for j, chip in enumerate(chips)]
        for j, chip in enumerate(chips):
            copy(1 + j, (*chip, c), me).wait_recv()
            passed[j].start()
        copy(0, sibling, me).wait_recv()
        for j, chip in enumerate(chips):
            copy(4 + j, (*chip, 1 - c), me).wait_recv()
        for cp in first + passed:
            cp.wait_send()
        mine.wait()

    return pl.pallas_call(
        body, out_shape=jax.ShapeDtypeStruct((N_DEV, rows, LANES), shard.dtype),
        in_specs=[pl.BlockSpec(memory_space=pl.ANY)], out_specs=pl.BlockSpec(memory_space=pl.ANY),
        scratch_shapes=[pltpu.SemaphoreType.DMA((7,)), pltpu.SemaphoreType.DMA((7,)), pltpu.SemaphoreType.DMA],
        name="weights_all_gather",
    )(shard)


def _grad_exchange(big, small):
    def body(big_ref, small_ref, big_out, small_out, bsend, brecv, ssend, srecv, local_sems):
        me = _mesh_pos()
        mine = _flat_index(me)
        loc = [pltpu.make_async_copy(big_ref.at[mine], big_out.at[mine], local_sems.at[0]),
               pltpu.make_async_copy(small_ref, small_out.at[mine], local_sems.at[1])]
        for cp in loc:
            cp.start()

        def copies(k):
            peer = _peer(me, k)
            theirs = _flat_index(peer)
            send = (pltpu.make_async_remote_copy(
                        src_ref=big_ref.at[theirs], dst_ref=big_out.at[mine], send_sem=bsend.at[k - 1],
                        recv_sem=brecv.at[k - 1], device_id=peer, device_id_type=pl.DeviceIdType.MESH),
                    pltpu.make_async_remote_copy(
                        src_ref=small_ref, dst_ref=small_out.at[mine], send_sem=ssend.at[k - 1],
                        recv_sem=srecv.at[k - 1], device_id=peer, device_id_type=pl.DeviceIdType.MESH))
            recv = (pltpu.make_async_remote_copy(
                        src_ref=big_ref.at[mine], dst_ref=big_out.at[theirs], send_sem=bsend.at[k - 1],
                        recv_sem=brecv.at[k - 1], device_id=me, device_id_type=pl.DeviceIdType.MESH),
                    pltpu.make_async_remote_copy(
                        src_ref=small_ref, dst_ref=small_out.at[theirs], send_sem=ssend.at[k - 1],
                        recv_sem=srecv.at[k - 1], device_id=me, device_id_type=pl.DeviceIdType.MESH))
            return send, recv

        plan = [copies(k) for k in range(1, N_DEV)]
        for send, _ in plan:
            for cp in send:
                cp.start()
        for _, recv in plan:
            for cp in recv:
                cp.wait_recv()
        for send, _ in plan:
            for cp in send:
                cp.wait_send()
        for cp in loc:
            cp.wait()

    any_spec = pl.BlockSpec(memory_space=pl.ANY)
    return pl.pallas_call(
        body,
        out_shape=[jax.ShapeDtypeStruct(big.shape, big.dtype),
                   jax.ShapeDtypeStruct((N_DEV,) + small.shape, small.dtype)],
        in_specs=[any_spec, any_spec], out_specs=[any_spec, any_spec],
        scratch_shapes=[pltpu.SemaphoreType.DMA((7,)), pltpu.SemaphoreType.DMA((7,)),
                        pltpu.SemaphoreType.DMA((7,)), pltpu.SemaphoreType.DMA((7,)),
                        pltpu.SemaphoreType.DMA((2,))],
        name="grad_exchange",
    )(big, small)


def _push_start(name, src, per_peer):
    rows = src.shape[-2]

    def body(src_ref, land_ref, send_sems, recv_sems, src_thru, land_thru, token):
        me = _mesh_pos()
        mine = _flat_index(me)
        for k in range(1, N_DEV):
            peer = _peer(me, k)
            pltpu.make_async_remote_copy(
                src_ref=src_ref.at[_flat_index(peer)] if per_peer else src_ref, dst_ref=land_ref.at[mine],
                send_sem=send_sems.at[k - 1], recv_sem=recv_sems.at[k - 1],
                device_id=peer, device_id_type=pl.DeviceIdType.MESH).start()
        token[...] = jnp.zeros_like(token)

    hbm = pl.BlockSpec(memory_space=pltpu.HBM)
    sem = pl.BlockSpec(memory_space=pltpu.SEMAPHORE)
    land = lax.empty((N_DEV, rows, LANES), src.dtype)
    return pl.pallas_call(
        body, name=name,
        out_shape=(pltpu.SemaphoreType.DMA((N_DEV - 1,)), pltpu.SemaphoreType.DMA((N_DEV - 1,)),
                   pltpu.HBM(src.shape, src.dtype), pltpu.HBM(land.shape, land.dtype),
                   jax.ShapeDtypeStruct((8, LANES), F32)),
        in_specs=(hbm, hbm), out_specs=(sem, sem, hbm, hbm, pl.BlockSpec(memory_space=pltpu.VMEM)),
        input_output_aliases={0: 2, 1: 3},
        compiler_params=pltpu.CompilerParams(has_side_effects=pltpu.SideEffectType.DATAFLOW_SIDE_EFFECTING),
    )(pltpu.with_memory_space_constraint(src, pltpu.HBM), pltpu.with_memory_space_constraint(land, pltpu.HBM))


def _push_wait(name, started, per_peer, after):
    send_sems, recv_sems, src_thru, land_thru, _ = started

    def body(src_ref, land_ref, send_sems, recv_sems, after_ref, src_out, land_out):
        me = _mesh_pos()
        for k in range(1, N_DEV):
            theirs = _flat_index(_peer(me, k))
            copy = pltpu.make_async_remote_copy(
                src_ref=src_ref.at[theirs] if per_peer else src_ref, dst_ref=land_ref.at[theirs],
                send_sem=send_sems.at[k - 1], recv_sem=recv_sems.at[k - 1],
                device_id=me, device_id_type=pl.DeviceIdType.MESH)
            copy.wait_send()
            copy.wait_recv()

    hbm = pl.BlockSpec(memory_space=pltpu.HBM)
    sem = pl.BlockSpec(memory_space=pltpu.SEMAPHORE)
    return pl.pallas_call(
        body, name=name,
        out_shape=(pltpu.HBM(src_thru.shape, src_thru.dtype), pltpu.HBM(land_thru.shape, land_thru.dtype)),
        in_specs=(hbm, hbm, sem, sem, pl.BlockSpec(memory_space=pl.ANY)), out_specs=(hbm, hbm),
        input_output_aliases={0: 0, 1: 1},
        compiler_params=pltpu.CompilerParams(has_side_effects=pltpu.SideEffectType.DATAFLOW_SIDE_EFFECTING),
    )(src_thru, land_thru, send_sems, recv_sems, after)


def _slot_sum(name, slots, tr):
    rows = slots.shape[1]

    def body(s_ref, o_ref):
        acc = s_ref[0].astype(F32)
        for d in range(1, N_DEV):
            acc = acc + s_ref[d].astype(F32)
        o_ref[...] = acc

    return pl.pallas_call(
        body, grid=(rows // tr,),
        in_specs=[pl.BlockSpec((N_DEV, tr, LANES), lambda i: (0, i, 0))],
        out_specs=pl.BlockSpec((tr, LANES), lambda i: (i, 0)),
        out_shape=jax.ShapeDtypeStruct((rows, LANES), F32),
        name=name, compiler_params=_params(("arbitrary",)),
    )(slots)


def _adamw(name, w, g, m, v):
    rows, cols = w.shape
    tr = _pick(rows, (256, 128, 88, 32, 1))
    c1 = 1.0 - ADAM_B1 ** ADAM_STEP
    c2 = 1.0 - ADAM_B2 ** ADAM_STEP

    def body(w_ref, g_ref, m_ref, v_ref, d_ref, nm_ref, nv_ref):
        g = g_ref[...]
        nm = ADAM_B1 * m_ref[...] + (1.0 - ADAM_B1) * g
        nv = ADAM_B2 * v_ref[...] + (1.0 - ADAM_B2) * (g * g)
        nm_ref[...] = nm
        nv_ref[...] = nv
        d_ref[...] = -ADAM_LR * ((nm / c1) / (jnp.sqrt(nv / c2) + ADAM_EPS) + ADAM_WD * w_ref[...])

    spec = pl.BlockSpec((tr, cols), lambda i: (i, 0))
    return pl.pallas_call(
        body, grid=(rows // tr,), in_specs=[spec] * 4, out_specs=[spec] * 3,
        out_shape=[jax.ShapeDtypeStruct((rows, cols), F32)] * 3,
        name=name, compiler_params=_params(("arbitrary",)),
    )(w, g, m, v)


def _pack_shards(shards, group, rows):
    parts = []
    for name, _, axis in group:
        w = shards[name].astype(BF16)
        parts.append((w.T if axis == 1 else w).reshape(-1))
    flat = jnp.concatenate(parts)
    flat = jnp.pad(flat, (0, rows * LANES - flat.shape[0]))
    return flat.reshape(rows, LANES)


def _unpack_full(gathered, group):
    flat = gathered.reshape(N_DEV, -1)
    out, off = {}, 0
    for name, (r, c), axis in group:
        rr, cc = (c, r) if axis == 1 else (r, c)
        out[name] = flat[:, off:off + r * c].reshape(N_DEV * rr, cc)
        off += r * c
    return out


def _pack_full_grads(grads, group, rows):
    parts = [grads[name].reshape(N_DEV, r * c) for name, (r, c), _ in group]
    flat = jnp.concatenate(parts, axis=1).astype(BF16)
    flat = jnp.pad(flat, ((0, 0), (0, rows * LANES - flat.shape[1])))
    return flat.reshape(N_DEV, rows, LANES)


def _unpack_shard_grads(flat, group):
    flat = flat.reshape(-1)
    out, off = {}, 0
    for name, (r, c), axis in group:
        seg = flat[off:off + r * c]
        out[name] = seg.reshape(c, r).T if axis == 1 else seg.reshape(r, c)
        off += r * c
    return out


def _own_slot(slots, own):
    mine = _flat_index(_mesh_pos())
    return lax.dynamic_update_slice(slots, own[None], (mine, 0, 0))


def _rope_tables(positions):
    inv_freq = ROPE_THETA ** (-jnp.arange(0, MLA_ROPE_DIM, 2, dtype=F32) / MLA_ROPE_DIM)
    ang = positions.astype(F32)[:, None] * inv_freq
    z64 = jnp.zeros((positions.shape[0], 64), F32)
    z32 = jnp.zeros((positions.shape[0], 32), F32)
    cos, sin = jnp.cos(ang), jnp.sin(ang)
    return (jnp.concatenate([z64, cos, cos, z32], axis=1), jnp.concatenate([z64, sin, sin, z32], axis=1))


def _row_tile(s_len, want):
    return _pick(s_len, (want, 256, 128))


def kernel(x, positions, norm_mix_pre, norm_mix_post, w_in, b_gate, q_norm, w_uq, kv_norm, w_ukv, w_proj_sb, w_proj_mla, w_out, norm_ffn_pre, norm_ffn_post, w_gate_up, w_down, loss_target, m_norm_mix_pre, m_norm_mix_post, m_w_in, m_b_gate, m_q_norm, m_w_uq, m_kv_norm, m_w_ukv, m_w_proj_sb, m_w_proj_mla, m_w_out, m_norm_ffn_pre, m_norm_ffn_post, m_w_gate_up, m_w_down, v_norm_mix_pre, v_norm_mix_post, v_w_in, v_b_gate, v_q_norm, v_w_uq, v_kv_norm, v_w_ukv, v_w_proj_sb, v_w_proj_mla, v_w_out, v_norm_ffn_pre, v_norm_ffn_post, v_w_gate_up, v_w_down):
    weights = dict(norm_mix_pre=norm_mix_pre, norm_mix_post=norm_mix_post, w_in=w_in, b_gate=b_gate, q_norm=q_norm,
                   w_uq=w_uq, kv_norm=kv_norm, w_ukv=w_ukv, w_proj_sb=w_proj_sb, w_proj_mla=w_proj_mla, w_out=w_out,
                   norm_ffn_pre=norm_ffn_pre, norm_ffn_post=norm_ffn_post, w_gate_up=w_gate_up, w_down=w_down)
    m_in = dict(norm_mix_pre=m_norm_mix_pre, norm_mix_post=m_norm_mix_post, w_in=m_w_in, b_gate=m_b_gate,
                q_norm=m_q_norm, w_uq=m_w_uq, kv_norm=m_kv_norm, w_ukv=m_w_ukv, w_proj_sb=m_w_proj_sb,
                w_proj_mla=m_w_proj_mla, w_out=m_w_out, norm_ffn_pre=m_norm_ffn_pre, norm_ffn_post=m_norm_ffn_post,
                w_gate_up=m_w_gate_up, w_down=m_w_down)
    v_in = dict(norm_mix_pre=v_norm_mix_pre, norm_mix_post=v_norm_mix_post, w_in=v_w_in, b_gate=v_b_gate,
                q_norm=v_q_norm, w_uq=v_w_uq, kv_norm=v_kv_norm, w_ukv=v_w_ukv, w_proj_sb=v_w_proj_sb,
                w_proj_mla=v_w_proj_mla, w_out=v_w_out, norm_ffn_pre=v_norm_ffn_pre, norm_ffn_post=v_norm_ffn_post,
                w_gate_up=v_w_gate_up, w_down=v_w_down)
    order = list(weights)

    xs = x[0]
    target = loss_target[0]
    s_len = xs.shape[0]
    tm_fwd = _row_tile(s_len, 512)
    tm = _row_tile(s_len, 256)
    tm_ffn = tm

    shards = {name: weights[name][0] for name, _, _ in SHARDED}
    late_shard = _pack_shards(shards, LATE, LATE_ROWS)
    late_weights = _push_start("weights_late_start", late_shard, False)
    full = _unpack_full(_all_gather(_pack_shards(shards, EARLY, EARLY_ROWS)), EARLY)
    wt = full["w_in"]
    zr = lambda n: jnp.zeros((n, D_MODEL), BF16)
    w_ext = jnp.concatenate([wt[:2176], zr(64), wt[2176:2208], zr(32), wt[2208:]], axis=0).T
    wa = jnp.pad(full["w_uq"].reshape(N_HEADS, MLA_QK_DIM, MLA_Q_RANK), ((0, 0), (0, 32), (0, 0))
                 ).reshape(N_HEADS * LANES, MLA_Q_RANK).T
    ukv = full["w_ukv"].reshape(N_HEADS, LANES, MLA_KV_RANK)
    wk = jnp.pad(ukv[:, :64], ((0, 0), (0, 64), (0, 0))).reshape(N_HEADS * LANES, MLA_KV_RANK).T
    wv = ukv[:, 64:].reshape(512, MLA_KV_RANK).T
    g_mix_pre = norm_mix_pre + late_weights[4][0:1, 0:1]
    cr, sr = _rope_tables(positions[0])

    qkv, cq, ckv, kr, gl, hb = _in_proj(xs, g_mix_pre, w_ext, tm_fwd)
    q_mla, k_mla, v_mla, cqn, ckvn = _mla_up(cq, ckv, kr, cr, sr, q_norm, kv_norm, wa, wk, wv, tm_fwd)
    o_sb, o_mla, lse = _attn_fwd(qkv, q_mla, k_mla, v_mla)
    late_shard, late_slots = _push_wait("weights_late_wait", late_weights, False, o_sb)
    full = _unpack_full(_own_slot(late_slots, late_shard), LATE)
    w_sb, w_mla, w_o, w_gu, w_dn = (full["w_proj_sb"].T, full["w_proj_mla"].T, full["w_out"], full["w_gate_up"].T,
                                    full["w_down"])
    x1, y, merged, o_sb_b, o_mla_b = _mix_out(o_sb, o_mla, gl, xs, b_gate, norm_mix_post, w_sb, w_mla, w_o, tm_fwd)
    dx2, f, h2, loss_part = _ffn_fwd(x1, target, norm_ffn_pre, norm_ffn_post, w_gu, w_dn, tm_fwd)
    loss_local = (0.5 / D_MODEL * jnp.sum(loss_part)).reshape(1)

    dx1, act, dgu, dfb, dg_ffn_pre, dg_ffn_post = _ffn_bwd(dx2, f, x1, norm_ffn_pre, norm_ffn_post, w_gu, w_dn, tm_ffn)
    dyb, dpsb, dpmla, dgl, do_sb, do_mla, dg_mix_post, db_gate = _mix_bwd(
        dx1, y, o_sb, o_mla, gl, b_gate, norm_mix_post, w_sb, w_mla, w_o, tm)
    late_grads = _pack_full_grads({
        "w_proj_sb": _matmul_tn("dw_proj_sb", dpsb, o_sb_b),
        "w_proj_mla": _matmul_tn("dw_proj_mla", dpmla, o_mla_b),
        "w_out": _matmul_tn("dw_out", merged, dyb),
        "w_gate_up": _matmul_tn("dw_gate_up", dgu, h2),
        "w_down": _matmul_tn("dw_down", act, dfb),
    }, LATE, LATE_ROWS)
    late_exchange = _push_start("grads_late_start", late_grads, True)
    dq_sb, dq_mla, dk_sb, dv_sb, dk_mla, dv_mla = _attn_bwd(qkv, do_sb, o_sb, q_mla, k_mla, v_mla, do_mla, o_mla, lse,
                                                             late_exchange[4])
    da, dkb, dvb, dlat, dg_q, dg_kv = _mla_up_bwd(dq_mla, dk_mla, dv_mla, cq, ckv, cr, sr, q_norm, kv_norm,
                                                  wa, wk, wv, tm)
    dx, dproj, dg_mix_pre = _in_proj_bwd(xs, dx1, dq_sb, dk_sb, dv_sb, dlat, dgl, norm_mix_pre, w_ext, tm)
    d_ext = _matmul_tn("dw_in", dproj, hb)
    d_wa = _matmul_tn("dw_uq", da, cqn)
    d_wk = _matmul_tn("dw_uk", dkb, ckvn)
    d_wv = _matmul_tn("dw_uv", dvb, ckvn)
    early_grads = _pack_full_grads({
        "w_in": jnp.concatenate([d_ext[:2176], d_ext[2240:2272], d_ext[EXT_GL:]], axis=0),
        "w_uq": d_wa.reshape(N_HEADS, LANES, MLA_Q_RANK)[:, :MLA_QK_DIM].reshape(768, MLA_Q_RANK),
        "w_ukv": jnp.concatenate([d_wk.reshape(N_HEADS, LANES, MLA_KV_RANK)[:, :64],
                                  d_wv.reshape(N_HEADS, 64, MLA_KV_RANK)], axis=1).reshape(1024, MLA_KV_RANK),
    }, EARLY, EARLY_ROWS)
    small_parts = dict(norm_mix_pre=dg_mix_pre, norm_mix_post=dg_mix_post, b_gate=db_gate, q_norm=dg_q,
                       kv_norm=dg_kv, norm_ffn_pre=dg_ffn_pre, norm_ffn_post=dg_ffn_post)
    small = jnp.concatenate([small_parts[name].sum(axis=0) for name, _ in SMALL] + [loss_local])
    small = jnp.pad(small, (0, SMALL_ROWS * LANES - small.shape[0])).reshape(SMALL_ROWS, LANES)

    early_slots, small_slots = _grad_exchange(early_grads, small)
    late_grads, late_slots = _push_wait("grads_late_wait", late_exchange, True, early_slots)
    mine = _flat_index(_mesh_pos())
    late_slots = _own_slot(late_slots, lax.dynamic_index_in_dim(late_grads, mine, 0, keepdims=False))
    g_out = _unpack_shard_grads(_slot_sum("grad_early_sum", early_slots, 960), EARLY)
    g_out.update(_unpack_shard_grads(_slot_sum("grad_late_sum", late_slots, 512), LATE))
    s_flat = _slot_sum("grad_small_sum", small_slots, SMALL_ROWS).reshape(-1)
    off = 0
    for name, n in SMALL:
        g_out[name] = s_flat[off:off + n].reshape(1, n)
        off += n
    loss = s_flat[off]

    deltas, new_m, new_v = {}, {}, {}
    for name in order:
        w2 = weights[name].reshape(g_out[name].shape)
        d, nm, nv = _adamw("adamw_" + name, w2, g_out[name], m_in[name].reshape(w2.shape), v_in[name].reshape(w2.shape))
        shape = weights[name].shape
        deltas[name], new_m[name], new_v[name] = d.reshape(shape), nm.reshape(shape), nv.reshape(shape)
        g_out[name] = g_out[name].reshape(shape)

    return (loss, dx[None], *[g_out[n] for n in order], *[deltas[n] for n in order],
            *[new_m[n] for n in order], *[new_v[n] for n in order])
```

```python
import functools
import math

import jax
import jax.numpy as jnp
from jax import lax
from jax.experimental import pallas as pl
from jax.experimental.pallas import tpu as pltpu

F32 = jnp.float32
BF16 = jnp.bfloat16

D_MODEL = 1024
N_HEADS = 8
SB_WIDTH = 512
MLA_Q_RANK = 384
MLA_KV_RANK = 256
MLA_ROPE_DIM = 32
MLA_QK_DIM = 96
D_FF = 2816
ROPE_THETA = 10000.0
EPS = 1e-6
SB_SCALE = 1.0 / math.sqrt(64.0)
MLA_SCALE = 1.0 / math.sqrt(96.0)
NEG_BIG = -1e30

ADAM_LR = 0.001
ADAM_B1 = 0.9
ADAM_B2 = 0.999
ADAM_EPS = 1e-08
ADAM_WD = 0.01
ADAM_STEP = 10

N_DEV = 8
LANES = 128
TQ = 512
TK = 128
DIAG_TILES = TQ // TK
FWD_UNROLL = 4
BWD_UNROLL = 2
VMEM_LIMIT = 56 << 20

EXT_QKV = 0
EXT_CQ = 1536
EXT_CKV = 1920
EXT_KR = 2176
EXT_GL = 2304
EXT_N = 4352

EARLY = (("w_in", (1024, 532), 1), ("w_uq", (384, 96), 1), ("w_ukv", (256, 128), 1))
LATE = (("w_proj_sb", (512, 128), 1), ("w_proj_mla", (512, 128), 1), ("w_out", (128, 1024), 0),
        ("w_gate_up", (1024, 704), 1), ("w_down", (352, 1024), 0))
SHARDED = EARLY + LATE
EARLY_ROWS = 4800
SMALL = (("norm_mix_pre", 1024), ("norm_mix_post", 1024), ("b_gate", 2048), ("q_norm", 384),
         ("kv_norm", 256), ("norm_ffn_pre", 1024), ("norm_ffn_post", 1024))
SMALL_ROWS = 56


def _dot(a, b):
    return jnp.dot(a, b, preferred_element_type=F32)


def _dot_nt(a, b):
    return lax.dot_general(a, b, (((1,), (1,)), ((), ())), preferred_element_type=F32)


def _dot_tn(a, b):
    return lax.dot_general(a, b, (((0,), (0,)), ((), ())), preferred_element_type=F32)


def _rms(x):
    r = lax.rsqrt(jnp.mean(x * x, axis=-1, keepdims=True) + EPS)
    return x * r, r


def _rms_bwd(dn, n, r):
    return r * (dn - n * jnp.mean(dn * n, axis=-1, keepdims=True))


def _colsum8(x):
    return jnp.sum(x.reshape(x.shape[0] // 8, 8, x.shape[1]), axis=0)


def _split(x):
    hi = x.astype(BF16)
    return hi, (x - hi.astype(F32)).astype(BF16)


def _rot(x):
    lane = lax.broadcasted_iota(jnp.int32, x.shape, 1)
    up = pltpu.roll(x, 112, 1)
    down = pltpu.roll(x, 16, 1)
    return jnp.where((lane >= 64) & (lane < 80), -up, jnp.where((lane >= 80) & (lane < 96), down, 0.0))


def _params(sem):
    return pltpu.CompilerParams(dimension_semantics=sem, vmem_limit_bytes=VMEM_LIMIT)


def _rows_call(name, body, n_rows, tm, row_ins, const_ins, row_outs, acc_outs):
    in_specs = [pl.BlockSpec((tm, a.shape[1]), lambda i: (i, 0)) for a in row_ins]
    in_specs += [pl.BlockSpec(a.shape, lambda i: (0, 0), pipeline_mode=pl.Buffered(1)) for a in const_ins]
    out_specs = [pl.BlockSpec((tm, n), lambda i: (i, 0)) for n, _ in row_outs]
    out_specs += [pl.BlockSpec(s, lambda i: (0, 0)) for s in acc_outs]
    out_shape = [jax.ShapeDtypeStruct((n_rows, n), dt) for n, dt in row_outs]
    out_shape += [jax.ShapeDtypeStruct(s, F32) for s in acc_outs]
    return pl.pallas_call(
        body, grid=(n_rows // tm,), in_specs=in_specs, out_specs=out_specs, out_shape=out_shape,
        name=name, compiler_params=_params(("arbitrary",)),
    )(*row_ins, *const_ins)


def _in_proj(x, g_pre, w_ext, tm):
    def body(x_ref, g_ref, w_ref, qkv_ref, cq_ref, ckv_ref, kr_ref, gl_ref, h_ref):
        n, _ = _rms(x_ref[...])
        hb = (n * g_ref[...]).astype(BF16)
        h_ref[...] = hb
        for c in range(0, 1536, 512):
            qkv_ref[:, c:c + 512] = _dot(hb, w_ref[:, c:c + 512]).astype(BF16)
        cq_ref[...] = _dot(hb, w_ref[:, EXT_CQ:EXT_CKV])
        ckv_ref[...] = _dot(hb, w_ref[:, EXT_CKV:EXT_KR])
        kr_ref[...] = _dot(hb, w_ref[:, EXT_KR:EXT_GL])
        for c in range(0, 2048, 512):
            gl_ref[:, c:c + 512] = _dot(hb, w_ref[:, EXT_GL + c:EXT_GL + c + 512])

    return _rows_call("in_proj", body, x.shape[0], tm, [x], [g_pre, w_ext],
                      [(1536, BF16), (384, F32), (256, F32), (128, F32), (2048, F32), (1024, BF16)], [])


def _mla_up(cq, ckv, kr, cr, sr, q_norm, kv_norm, wa, wk, wv, tm):
    def body(cq_ref, ckv_ref, kr_ref, cr_ref, sr_ref, qn_ref, kvn_ref, wa_ref, wk_ref, wv_ref,
             q_ref, k_ref, v_ref, cqn_ref, ckvn_ref):
        nq, _ = _rms(cq_ref[...])
        cqn = (nq * qn_ref[...]).astype(BF16)
        cqn_ref[...] = cqn
        nk, _ = _rms(ckv_ref[...])
        ckvn = (nk * kvn_ref[...]).astype(BF16)
        ckvn_ref[...] = ckvn
        cr = cr_ref[...]
        sr = sr_ref[...]
        lane = lax.broadcasted_iota(jnp.int32, cr.shape, 1)
        cm = cr + (lane < 64).astype(F32)
        kr = kr_ref[...]
        krp = kr * cr + _rot(kr) * sr
        for h in range(N_HEADS):
            hs = slice(h * LANES, (h + 1) * LANES)
            a = _dot(cqn, wa_ref[:, hs])
            q_ref[:, hs] = (a * cm + _rot(a) * sr).astype(BF16)
            k_ref[:, hs] = (_dot(ckvn, wk_ref[:, hs]) + krp).astype(BF16)
        v_ref[...] = _dot(ckvn, wv_ref[...]).astype(BF16)

    return _rows_call("mla_up", body, cq.shape[0], tm, [cq, ckv, kr, cr, sr], [q_norm, kv_norm, wa, wk, wv],
                      [(1024, BF16), (1024, BF16), (512, BF16), (384, BF16), (256, BF16)], [])


def _mix_out(o_sb, o_mla, gl, x, b_gate, g_post, w_sb, w_mla, w_out, tm):
    def body(osb_ref, omla_ref, gl_ref, x_ref, b_ref, gp_ref, wsb_ref, wmla_ref, wout_ref,
             x1_ref, y_ref, mb_ref, osbb_ref, omlab_ref):
        osb = osb_ref[...].astype(BF16)
        omla = omla_ref[...].astype(BF16)
        osbb_ref[...] = osb
        omlab_ref[...] = omla
        psb = _dot(osb, wsb_ref[...])
        pmla = _dot(omla, wmla_ref[...])
        gates = jax.nn.sigmoid(gl_ref[...] + b_ref[...])
        mb = (gates[:, :D_MODEL] * psb + gates[:, D_MODEL:] * pmla).astype(BF16)
        mb_ref[...] = mb
        y = _dot(mb, wout_ref[...])
        y_ref[...] = y
        n, _ = _rms(y)
        x1_ref[...] = x_ref[...] + n * gp_ref[...]

    return _rows_call("mix_out", body, x.shape[0], tm, [o_sb, o_mla, gl, x], [b_gate, g_post, w_sb, w_mla, w_out],
                      [(1024, F32), (1024, F32), (1024, BF16), (512, BF16), (512, BF16)], [])


FF_CHUNK = 1408


def _ffn_fwd(x1, target, g_pre, g_post, w_gu, w_down, tm):
    def body(x1_ref, t_ref, gpre_ref, gpost_ref, wgu_ref, wd_ref, dx2_ref, f_ref, h2_ref, loss_ref):
        x1 = x1_ref[...]
        n, _ = _rms(x1)
        h2 = (n * gpre_ref[...]).astype(BF16)
        h2_ref[...] = h2
        f = jnp.zeros((tm, D_MODEL), F32)
        for c in range(0, D_FF, FF_CHUNK):
            g = _dot(h2, wgu_ref[:, c:c + FF_CHUNK])
            u = _dot(h2, wgu_ref[:, D_FF + c:D_FF + c + FF_CHUNK])
            act = (g * jax.nn.sigmoid(g) * u).astype(BF16)
            f = f + _dot(act, wd_ref[c:c + FF_CHUNK, :])
        f_ref[...] = f
        nf, _ = _rms(f)
        err = x1 + nf * gpost_ref[...] - t_ref[...]
        dx2_ref[...] = err * (1.0 / D_MODEL)
        e8 = _colsum8(err * err)
        part = e8[:, 0:LANES]
        for c in range(LANES, D_MODEL, LANES):
            part = part + e8[:, c:c + LANES]

        @pl.when(pl.program_id(0) == 0)
        def _():
            loss_ref[...] = jnp.zeros_like(loss_ref)

        loss_ref[...] += part

    return _rows_call("ffn_fwd", body, x1.shape[0], tm, [x1, target], [g_pre, g_post, w_gu, w_down],
                      [(1024, F32), (1024, F32), (1024, BF16)], [(8, LANES)])


def _ffn_bwd(dx2, f, x1, g_pre, g_post, w_gu, w_down, tm):
    def body(dx2_ref, f_ref, x1_ref, gpre_ref, gpost_ref, wgu_ref, wd_ref,
             dx1_ref, act_ref, dgu_ref, dfb_ref, dgpre_ref, dgpost_ref):
        @pl.when(pl.program_id(0) == 0)
        def _():
            dgpre_ref[...] = jnp.zeros_like(dgpre_ref)
            dgpost_ref[...] = jnp.zeros_like(dgpost_ref)

        dx2 = dx2_ref[...]
        nf, rf = _rms(f_ref[...])
        dgpost_ref[...] += _colsum8(dx2 * nf)
        dfb = _rms_bwd(dx2 * gpost_ref[...], nf, rf).astype(BF16)
        dfb_ref[...] = dfb
        x1 = x1_ref[...]
        n1, r1 = _rms(x1)
        h2 = (n1 * gpre_ref[...]).astype(BF16)
        dh2 = jnp.zeros((tm, D_MODEL), F32)
        for c in range(0, D_FF, FF_CHUNK):
            cs, us = slice(c, c + FF_CHUNK), slice(D_FF + c, D_FF + c + FF_CHUNK)
            g = _dot(h2, wgu_ref[:, cs])
            u = _dot(h2, wgu_ref[:, us])
            sg = jax.nn.sigmoid(g)
            si = g * sg
            act_ref[:, cs] = (si * u).astype(BF16)
            dact = _dot_nt(dfb, wd_ref[cs, :])
            dg = (dact * u * (sg * (1.0 + g * (1.0 - sg)))).astype(BF16)
            du = (dact * si).astype(BF16)
            dgu_ref[:, cs] = dg
            dgu_ref[:, us] = du
            dh2 = dh2 + _dot_nt(dg, wgu_ref[:, cs]) + _dot_nt(du, wgu_ref[:, us])
        dgpre_ref[...] += _colsum8(dh2 * n1)
        dx1_ref[...] = dx2 + _rms_bwd(dh2 * gpre_ref[...], n1, r1)

    return _rows_call("ffn_bwd", body, dx2.shape[0], tm, [dx2, f, x1], [g_pre, g_post, w_gu, w_down],
                      [(1024, F32), (D_FF, BF16), (2 * D_FF, BF16), (1024, BF16)], [(8, 1024), (8, 1024)])


def _mix_bwd(dx1, y, o_sb, o_mla, gl, b_gate, g_post, w_sb, w_mla, w_out, tm):
    def body(dx1_ref, y_ref, osb_ref, omla_ref, gl_ref, b_ref, gp_ref, wsb_ref, wmla_ref, wout_ref,
             dyb_ref, dpsb_ref, dpmla_ref, dgl_ref, dosb_ref, domla_ref, dgpost_ref, dbg_ref):
        @pl.when(pl.program_id(0) == 0)
        def _():
            dgpost_ref[...] = jnp.zeros_like(dgpost_ref)
            dbg_ref[...] = jnp.zeros_like(dbg_ref)

        dx1 = dx1_ref[...]
        ny, ry = _rms(y_ref[...])
        dgpost_ref[...] += _colsum8(dx1 * ny)
        dyb = _rms_bwd(dx1 * gp_ref[...], ny, ry).astype(BF16)
        dyb_ref[...] = dyb
        dm = _dot_nt(dyb, wout_ref[...])
        psb = _dot(osb_ref[...].astype(BF16), wsb_ref[...])
        pmla = _dot(omla_ref[...].astype(BF16), wmla_ref[...])
        gates = jax.nn.sigmoid(gl_ref[...] + b_ref[...])
        g0, g1 = gates[:, :D_MODEL], gates[:, D_MODEL:]
        dpsb = (dm * g0).astype(BF16)
        dpmla = (dm * g1).astype(BF16)
        dpsb_ref[...] = dpsb
        dpmla_ref[...] = dpmla
        dgl0 = dm * psb * g0 * (1.0 - g0)
        dgl1 = dm * pmla * g1 * (1.0 - g1)
        dgl_ref[:, :D_MODEL] = dgl0.astype(BF16)
        dgl_ref[:, D_MODEL:] = dgl1.astype(BF16)
        dbg_ref[:, :D_MODEL] += _colsum8(dgl0)
        dbg_ref[:, D_MODEL:] += _colsum8(dgl1)
        dosb_ref[...] = _dot_nt(dpsb, wsb_ref[...]).astype(BF16)
        domla_ref[...] = _dot_nt(dpmla, wmla_ref[...]).astype(BF16)

    return _rows_call("mix_bwd", body, dx1.shape[0], tm, [dx1, y, o_sb, o_mla, gl],
                      [b_gate, g_post, w_sb, w_mla, w_out],
                      [(1024, BF16), (1024, BF16), (1024, BF16), (2048, BF16), (512, BF16), (512, BF16)],
                      [(8, 1024), (8, 2048)])


def _mla_up_bwd(dq, dk, dv, cq, ckv, cr, sr, q_norm, kv_norm, wa, wk, wv, tm):
    def body(dq_ref, dk_ref, dv_ref, cq_ref, ckv_ref, cr_ref, sr_ref, qn_ref, kvn_ref, wa_ref, wk_ref, wv_ref,
             da_ref, dkb_ref, dvb_ref, dlat_ref, dqn_ref, dkvn_ref):
        @pl.when(pl.program_id(0) == 0)
        def _():
            dqn_ref[...] = jnp.zeros_like(dqn_ref)
            dkvn_ref[...] = jnp.zeros_like(dkvn_ref)

        cr = cr_ref[...]
        sr = sr_ref[...]
        lane = lax.broadcasted_iota(jnp.int32, cr.shape, 1)
        cm = cr + (lane < 64).astype(F32)
        nq, rq = _rms(cq_ref[...])
        nk, rk = _rms(ckv_ref[...])
        dcqn = jnp.zeros((tm, MLA_Q_RANK), F32)
        dckvn = jnp.zeros((tm, MLA_KV_RANK), F32)
        dkrp = jnp.zeros((tm, LANES), F32)
        for h in range(N_HEADS):
            hs = slice(h * LANES, (h + 1) * LANES)
            dqh = dq_ref[:, hs]
            da = (dqh * cm - _rot(dqh * sr)).astype(BF16)
            da_ref[:, hs] = da
            dcqn = dcqn + _dot_nt(da, wa_ref[:, hs])
            dkh = dk_ref[:, hs]
            dkb = dkh.astype(BF16)
            dkb_ref[:, hs] = dkb
            dckvn = dckvn + _dot_nt(dkb, wk_ref[:, hs])
            dkrp = dkrp + dkh
        dvb = dv_ref[...].astype(BF16)
        dvb_ref[...] = dvb
        dckvn = dckvn + _dot_nt(dvb, wv_ref[...])
        dkr = dkrp * cr - _rot(dkrp * sr)
        dqn_ref[...] += _colsum8(dcqn * nq)
        dkvn_ref[...] += _colsum8(dckvn * nk)
        dlat_ref[:, 0:384] = _rms_bwd(dcqn * qn_ref[...], nq, rq).astype(BF16)
        dlat_ref[:, 384:640] = _rms_bwd(dckvn * kvn_ref[...], nk, rk).astype(BF16)
        dlat_ref[:, 640:768] = dkr.astype(BF16)

    return _rows_call("mla_up_bwd", body, dq.shape[0], tm, [dq, dk, dv, cq, ckv, cr, sr],
                      [q_norm, kv_norm, wa, wk, wv],
                      [(1024, BF16), (1024, BF16), (512, BF16), (768, BF16)], [(8, 384), (8, 256)])


def _in_proj_bwd(x, dx1, dq_sb, dk_sb, dv_sb, dlat, dgl, g_pre, w_ext, tm):
    def body(x_ref, dx1_ref, dq_ref, dk_ref, dv_ref, dlat_ref, dgl_ref, g_ref, w_ref,
             dx_ref, dproj_ref, dg_ref):
        @pl.when(pl.program_id(0) == 0)
        def _():
            dg_ref[...] = jnp.zeros_like(dg_ref)

        dproj_ref[:, 0:512] = dq_ref[...].astype(BF16)
        dproj_ref[:, 512:1024] = dk_ref[...].astype(BF16)
        dproj_ref[:, 1024:1536] = dv_ref[...].astype(BF16)
        dproj_ref[:, EXT_CQ:EXT_GL] = dlat_ref[...]
        dproj_ref[:, EXT_GL:EXT_N] = dgl_ref[...]
        dh = jnp.zeros((tm, D_MODEL), F32)
        for c in range(0, EXT_N, 2176):
            dh = dh + _dot_nt(dproj_ref[:, c:c + 2176], w_ref[:, c:c + 2176])
        n, r = _rms(x_ref[...])
        dg_ref[...] += _colsum8(dh * n)
        dx_ref[...] = dx1_ref[...] + _rms_bwd(dh * g_ref[...], n, r)

    return _rows_call("in_proj_bwd", body, x.shape[0], tm, [x, dx1, dq_sb, dk_sb, dv_sb, dlat, dgl],
                      [g_pre, w_ext], [(1024, F32), (EXT_N, BF16)], [(8, 1024)])


def _head_masked(x):
    lane = lax.broadcasted_iota(jnp.int32, x.shape, 1)
    zero = jnp.zeros_like(x)
    return jnp.where(lane < 64, x, zero), jnp.where(lane >= 64, x, zero)


def _cum_weights():
    row = lax.broadcasted_iota(jnp.int32, (TK, TK), 0)
    col = lax.broadcasted_iota(jnp.int32, (TK, TK), 1)
    half = jnp.concatenate([(row > col).astype(BF16), jnp.ones((TK, TK), BF16)], axis=1)
    return jnp.concatenate([half, half], axis=0)


def _split_cat(x):
    hi, lo = _split(x)
    return jnp.concatenate([hi, lo], axis=1)


def _sweep(i, tiles, unroll, skip_rows=False):
    if skip_rows:
        for t in range(DIAG_TILES // unroll):
            top = DIAG_TILES - 1 - t * unroll
            r0 = (top - unroll + 1) * TK
            col = lax.broadcasted_iota(jnp.int32, (TQ - r0, TK), 1)
            tiles(i * DIAG_TILES + top, [(top - u) * TK - r0 + col for u in range(unroll)], r0)
    else:
        col = lax.broadcasted_iota(jnp.int32, (TQ, TK), 1)

        def diag(t, _):
            top = DIAG_TILES - 1 - t * unroll
            tiles(i * DIAG_TILES + top, [(top - u) * TK + col for u in range(unroll)])
            return 0

        lax.fori_loop(0, DIAG_TILES // unroll, diag, 0)

    def full(g, _):
        tiles(i * DIAG_TILES - 1 - g * unroll, [None] * unroll)
        return 0

    lax.fori_loop(0, (i * DIAG_TILES) // unroll, full, 0)


def _causal(key, strict):
    if key is None:
        return None
    row = lax.broadcasted_iota(jnp.int32, key.shape, 0)
    return key < row if strict else key <= row


def _sb_logs(z, valid, w_cum):
    soft = jnp.log(1.0 + jnp.exp(-jnp.abs(z)))
    lsm = -jnp.maximum(z, 0.0) - soft
    cat = _split_cat(lsm if valid is None else jnp.where(valid, lsm, 0.0))
    return z + lsm, _dot(cat, w_cum)


def _sb_weight(log_beta, cs, r, valid):
    a = jnp.exp(log_beta + cs[:, :TK] + r)
    if valid is not None:
        a = jnp.where(valid, a, 0.0)
    return a, r + cs[:, TK:]


def _block_diag(x):
    x0, x1 = x[:, :LANES], x[:, LANES:]
    zero = jnp.zeros_like(x0)
    return jnp.concatenate([jnp.concatenate([x0, zero], axis=1), jnp.concatenate([zero, x1], axis=1)], axis=0)


def _attn_fwd(qkv, q, k, v):
    s_len = qkv.shape[0]

    def body(qs_ref, ks_ref, vs_ref, qm_ref, km_ref, vm_ref, osb_ref, omla_ref, lse_ref,
             sacc_ref, r_ref, macc_ref, m_ref):
        i = pl.program_id(1)
        lane = lax.broadcasted_iota(jnp.int32, (TQ, LANES), 1)
        r2 = lax.broadcasted_iota(jnp.int32, (2 * TK, LANES), 0)
        c2 = lax.broadcasted_iota(jnp.int32, (2 * TK, LANES), 1)
        head_ones = ((r2 < TK) == (c2 < 64)).astype(BF16)
        w_cum = _cum_weights()
        qs = qs_ref[...] * SB_SCALE
        qm = qm_ref[...]
        sacc_ref[...] = jnp.zeros_like(sacc_ref)
        r_ref[...] = jnp.zeros_like(r_ref)
        macc_ref[...] = jnp.zeros_like(macc_ref)
        m_ref[...] = jnp.full(m_ref.shape, NEG_BIG, F32)

        def sb_scores(top, valids):
            out = []
            for u, valid in enumerate(valids):
                off = pl.multiple_of((top - u) * TK, TK)
                z01 = _dot_nt(qs, jnp.concatenate(_head_masked(ks_ref[pl.ds(off, TK), :]), axis=0))
                out.append([_sb_logs(z, valid, w_cum) for z in (z01[:, :TK], z01[:, TK:])])
            return out

        def sb_accumulate(top, valids, scores):
            parts = [[None, None] for _ in valids]
            for hh in range(2):
                r = r_ref[hh]
                for u, valid in enumerate(valids):
                    a, r = _sb_weight(*scores[u][hh], r, valid)
                    parts[u][hh] = _split_cat(a)
                r_ref[hh] = r
            vs = []
            for u in range(len(valids)):
                off = pl.multiple_of((top - u) * TK, TK)
                v0, v1 = _head_masked(vs_ref[pl.ds(off, TK), :])
                vs += [v0, v0, v1, v1]
            sacc_ref[...] += _dot(jnp.concatenate([p for pair in parts for p in pair], axis=1),
                                  jnp.concatenate(vs, axis=0))

        def mla_scores(top, valids):
            s01s = []
            for u in range(len(valids)):
                off = pl.multiple_of((top - u) * TK, TK)
                s01s.append(_dot_nt(qm, _block_diag(km_ref[pl.ds(off, TK), :])))
            heads = []
            for hh in range(2):
                ss = []
                for u, valid in enumerate(valids):
                    s = s01s[u][:, hh * TK:(hh + 1) * TK] * MLA_SCALE
                    ss.append(s if valid is None else jnp.where(valid, s, NEG_BIG))
                m_old = m_ref[hh]
                m = jnp.maximum(m_old, jnp.max(functools.reduce(jnp.maximum, ss), axis=1, keepdims=True))
                m_ref[hh] = m
                heads.append((ss, m, jnp.exp(m_old - m)))
            return heads

        def mla_accumulate(top, heads):
            vs = []
            for u in range(len(heads[0][0])):
                off = pl.multiple_of((top - u) * TK, TK)
                vv = jnp.concatenate(_head_masked(vm_ref[pl.ds(off, TK), :]), axis=0)
                vs.append(jnp.concatenate([vv, head_ones], axis=1))
            ps = [[jnp.exp(s - m).astype(BF16) for s in ss] for ss, m, _ in heads]
            scale = jnp.where(lane < 64, heads[0][2], heads[1][2])
            p_all = jnp.concatenate([ps[hh][u] for u in range(len(vs)) for hh in range(2)], axis=1)
            macc_ref[...] = (macc_ref[...] * jnp.concatenate([scale, scale], axis=1)
                             + _dot(p_all, jnp.concatenate(vs, axis=0)))

        def tiles(top, keys):
            strict = [_causal(key, True) for key in keys]
            heads = mla_scores(top, [_causal(key, False) for key in keys])
            scores = sb_scores(top, strict)
            mla_accumulate(top, heads)
            sb_accumulate(top, strict, scores)

        _sweep(i, tiles, FWD_UNROLL)
        osb_ref[...] = sacc_ref[...]
        acc = macc_ref[...]
        den = acc[:, LANES:]
        omla_ref[...] = acc[:, :LANES] / den
        for hh, mask in enumerate((lane < 64, lane >= 64)):
            l = jnp.max(jnp.where(mask, den, 0.0), axis=1, keepdims=True)
            lse_ref[hh] = jnp.broadcast_to(m_ref[hh] + jnp.log(l), (TQ, LANES))

    tile = pl.BlockSpec((TQ, LANES), lambda h, i: (i, h))
    return pl.pallas_call(
        body, grid=(4, s_len // TQ),
        in_specs=[tile,
                  pl.BlockSpec((s_len, LANES), lambda h, i: (0, 4 + h)),
                  pl.BlockSpec((s_len, LANES), lambda h, i: (0, 8 + h)),
                  pl.BlockSpec((TQ, 2 * LANES), lambda h, i: (i, h)),
                  pl.BlockSpec((s_len, 2 * LANES), lambda h, i: (0, h)),
                  pl.BlockSpec((s_len, LANES), lambda h, i: (0, h))],
        out_specs=[tile, tile, pl.BlockSpec((2, TQ, LANES), lambda h, i: (h, i, 0))],
        out_shape=[jax.ShapeDtypeStruct((s_len, SB_WIDTH), F32), jax.ShapeDtypeStruct((s_len, 512), F32),
                   jax.ShapeDtypeStruct((N_HEADS, s_len, LANES), F32)],
        scratch_shapes=[pltpu.VMEM((TQ, LANES), F32), pltpu.VMEM((2, TQ, LANES), F32),
                        pltpu.VMEM((TQ, 2 * LANES), F32), pltpu.VMEM((2, TQ, 1), F32)],
        name="attn_fwd", compiler_params=_params(("arbitrary", "arbitrary")),
    )(qkv, qkv, qkv, q, k, v)


def _row_dots(do, o):
    prod = do.astype(F32) * o
    p0, p1 = _head_masked(prod)
    return tuple(jnp.broadcast_to(jnp.sum(p, axis=1, keepdims=True), prod.shape) for p in (p0, p1))


def _attn_bwd(qkv, do_sb, o_sb, q, k, v, do_mla, o_mla, lse, after):
    s_len = qkv.shape[0]
    n_q = s_len // TQ

    def body(qs_ref, ks_ref, vs_ref, dos_ref, os_ref, qm_ref, km_ref, vm_ref, dom_ref, om_ref, lse_ref, after_ref,
             dqs_ref, dqm_ref, dks_hbm, dvs_hbm, dkm_hbm, dvm_hbm,
             dqs_acc, dqm_acc, dks_acc, dvs_acc, dkm_acc, dvm_acc, r_ref, g_ref, ds_ref, dm_ref, out_sems):
        h = pl.program_id(0)
        i = pl.program_id(1)

        @pl.when(i == 0)
        def _():
            for acc in (dks_acc, dvs_acc, dkm_acc, dvm_acc):
                acc[...] = jnp.zeros_like(acc)

        w_cum = _cum_weights()
        qs = qs_ref[...] * SB_SCALE
        dos = dos_ref[...]
        ds_ref[0], ds_ref[1] = _row_dots(dos, os_ref[...])
        qm = qm_ref[...]
        dom = dom_ref[...]
        dm_ref[0], dm_ref[1] = _row_dots(dom, om_ref[...])
        qs_heads, dos_heads, dom_heads = _head_masked(qs), _head_masked(dos), _head_masked(dom)
        by_head = lambda pair, rows: jnp.concatenate([x[rows] for x in pair], axis=0)
        all_rows = slice(0, TQ)
        qs_rows, dos_rows, dom_rows = (by_head(p, all_rows) for p in (qs_heads, dos_heads, dom_heads))
        qm_diag = _block_diag(qm)
        for ref in (dqs_acc, dqm_acc, r_ref, g_ref):
            ref[...] = jnp.zeros_like(ref)
        heads = (slice(0, TK), slice(TK, 2 * TK))

        def weigh(log_beta, cs, r, da, valid):
            a, r = _sb_weight(log_beta, cs, r, valid)
            g = a * da
            return a.astype(BF16), g, _split_cat(g), r

        def logit_grad(g, gs, carried, d, log_beta, valid):
            upto = d - (gs[:, :TK] + carried)
            dz = g - jnp.exp(log_beta) * upto
            if valid is not None:
                dz = jnp.where(valid, dz, 0.0)
            return dz.astype(BF16), carried + gs[:, TK:]

        def tiles(top, keys, r0=0):
            n = len(keys)
            rows = slice(r0, TQ)
            strict = [_causal(key, True) for key in keys]
            loose = [_causal(key, False) for key in keys]
            offs = [pl.multiple_of((top - u) * TK, TK) for u in range(n)]
            kds = [_block_diag(km_ref[pl.ds(off, TK), :]) for off in offs]
            vms = [jnp.concatenate(_head_masked(vm_ref[pl.ds(off, TK), :]), axis=0) for off in offs]
            s01s = [_dot_nt(qm[rows], kd) for kd in kds]
            dp01s = [_dot_nt(dom[rows], vv) for vv in vms]
            kks = [jnp.concatenate(_head_masked(ks_ref[pl.ds(off, TK), :]), axis=0) for off in offs]
            vvs = [jnp.concatenate(_head_masked(vs_ref[pl.ds(off, TK), :]), axis=0) for off in offs]
            z01s = [_dot_nt(qs[rows], kk) for kk in kks]
            da01s = [_dot_nt(dos[rows], vv) for vv in vvs]
            logs = [[_sb_logs(z01s[u][:, hs], strict[u], w_cum) for hs in heads] for u in range(n)]

            dss = [[None, None] for _ in range(n)]
            ps = [[None, None] for _ in range(n)]
            for u in range(n):
                for hh, hs in enumerate(heads):
                    p = jnp.exp(s01s[u][:, hs] * MLA_SCALE - lse_ref[hh, rows, :])
                    if loose[u] is not None:
                        p = jnp.where(loose[u], p, 0.0)
                    dss[u][hh] = (p * (dp01s[u][:, hs] - dm_ref[hh, rows, :]) * MLA_SCALE).astype(BF16)
                    ps[u][hh] = p.astype(BF16)

            dzs = [[None, None] for _ in range(n)]
            avs = [[None, None] for _ in range(n)]
            sums = [[None, None] for _ in range(n)]
            for hh, hs in enumerate(heads):
                r = r_ref[hh, rows, :]
                for u in range(n):
                    avs[u][hh], g, cat, r = weigh(*logs[u][hh], r, da01s[u][:, hs], strict[u])
                    sums[u][hh] = (g, _dot(cat, w_cum))
                r_ref[hh, rows, :] = r

            dqm_acc[rows, :] += _dot(jnp.concatenate([d for pair in dss for d in pair], axis=1),
                                     jnp.concatenate(kds, axis=0))
            span = pl.ds(offs[-1], n * TK)
            by_key = lambda pairs: jnp.concatenate([jnp.concatenate(pair, axis=0) for pair in pairs[::-1]], axis=1)
            whole = r0 == 0
            dkm_acc[span, :] += _dot_tn(by_key(dss), qm_diag if whole else _block_diag(qm[rows]))
            dvm_acc[span, :] += _dot_tn(by_key(ps), dom_rows if whole else by_head(dom_heads, rows))

            for hh in range(2):
                carried = g_ref[hh, rows, :]
                for u in range(n):
                    dzs[u][hh], carried = logit_grad(*sums[u][hh], carried, ds_ref[hh, rows, :], logs[u][hh][0],
                                                     strict[u])
                g_ref[hh, rows, :] = carried
            dqs_acc[rows, :] += _dot(jnp.concatenate([dz for pair in dzs for dz in pair], axis=1),
                                     jnp.concatenate(kks, axis=0))
            dks_acc[span, :] += _dot_tn(by_key(dzs), qs_rows if whole else by_head(qs_heads, rows))
            dvs_acc[span, :] += _dot_tn(by_key(avs), dos_rows if whole else by_head(dos_heads, rows))

        _sweep(i, tiles, BWD_UNROLL, skip_rows=True)
        dqs_ref[...] = dqs_acc[...] * SB_SCALE
        dqm_ref[...] = dqm_acc[...]

        @pl.when(i == n_q - 1)
        def _():
            narrow = pl.ds(pl.multiple_of(h * LANES, LANES), LANES)
            wide = pl.ds(pl.multiple_of(h * 2 * LANES, 2 * LANES), 2 * LANES)
            copies = [pltpu.make_async_copy(dks_acc, dks_hbm.at[:, narrow], out_sems.at[0]),
                      pltpu.make_async_copy(dvs_acc, dvs_hbm.at[:, narrow], out_sems.at[1]),
                      pltpu.make_async_copy(dkm_acc, dkm_hbm.at[:, wide], out_sems.at[2]),
                      pltpu.make_async_copy(dvm_acc, dvm_hbm.at[:, narrow], out_sems.at[3])]
            for cp in copies:
                cp.start()
            for cp in copies:
                cp.wait()

    tile = pl.BlockSpec((TQ, LANES), lambda h, i: (i, h))
    wide_tile = pl.BlockSpec((TQ, 2 * LANES), lambda h, i: (i, h))
    once = pl.Buffered(1)
    hbm = pl.BlockSpec(memory_space=pl.ANY)
    return pl.pallas_call(
        body, grid=(4, n_q),
        in_specs=[tile,
                  pl.BlockSpec((s_len, LANES), lambda h, i: (0, 4 + h), pipeline_mode=once),
                  pl.BlockSpec((s_len, LANES), lambda h, i: (0, 8 + h), pipeline_mode=once),
                  tile, tile, wide_tile,
                  pl.BlockSpec((s_len, 2 * LANES), lambda h, i: (0, h), pipeline_mode=once),
                  pl.BlockSpec((s_len, LANES), lambda h, i: (0, h), pipeline_mode=once),
                  tile, tile, pl.BlockSpec((2, TQ, LANES), lambda h, i: (h, i, 0)),
                  pl.BlockSpec((8, LANES), lambda h, i: (0, 0))],
        out_specs=[tile, wide_tile, hbm, hbm, hbm, hbm],
        out_shape=[jax.ShapeDtypeStruct((s_len, SB_WIDTH), F32), jax.ShapeDtypeStruct((s_len, 1024), F32),
                   jax.ShapeDtypeStruct((s_len, SB_WIDTH), F32), jax.ShapeDtypeStruct((s_len, SB_WIDTH), F32),
                   jax.ShapeDtypeStruct((s_len, 1024), F32), jax.ShapeDtypeStruct((s_len, 512), F32)],
        scratch_shapes=[pltpu.VMEM((TQ, LANES), F32), pltpu.VMEM((TQ, 2 * LANES), F32),
                        pltpu.VMEM((s_len, LANES), F32), pltpu.VMEM((s_len, LANES), F32),
                        pltpu.VMEM((s_len, 2 * LANES), F32), pltpu.VMEM((s_len, LANES), F32),
                        pltpu.VMEM((2, TQ, LANES), F32), pltpu.VMEM((2, TQ, LANES), F32),
                        pltpu.VMEM((2, TQ, LANES), F32), pltpu.VMEM((2, TQ, LANES), F32),
                        pltpu.SemaphoreType.DMA((4,))],
        name="attn_bwd", compiler_params=_params(("arbitrary", "arbitrary")),
    )(qkv, qkv, qkv, do_sb, o_sb, q, k, v, do_mla, o_mla, lse, after)


def _pick(n, options):
    for t in options:
        if n % t == 0:
            return t
    raise ValueError(n)


def _matmul_tn(name, a, b):
    s_len, m = a.shape
    n = b.shape[1]
    tm = _pick(m, (1024, 1408, 2176, 512))
    tn = _pick(n, (1024, 512, 384, 256))
    tk = _pick(s_len, (1024, 512, 256, 128))
    n_k = s_len // tk

    def body(a_ref, b_ref, o_ref, acc_ref):
        @pl.when(pl.program_id(2) == 0)
        def _():
            acc_ref[...] = jnp.zeros_like(acc_ref)

        acc_ref[...] += _dot_tn(a_ref[...], b_ref[...])

        @pl.when(pl.program_id(2) == n_k - 1)
        def _():
            o_ref[...] = acc_ref[...].astype(BF16)

    return pl.pallas_call(
        body, grid=(m // tm, n // tn, n_k),
        in_specs=[pl.BlockSpec((tk, tm), lambda i, j, l: (l, i)), pl.BlockSpec((tk, tn), lambda i, j, l: (l, j))],
        out_specs=pl.BlockSpec((tm, tn), lambda i, j, l: (i, j)),
        out_shape=jax.ShapeDtypeStruct((m, n), BF16),
        scratch_shapes=[pltpu.VMEM((tm, tn), F32)],
        name=name, compiler_params=_params(("arbitrary", "arbitrary", "arbitrary")),
    )(a, b)


def _mesh_pos():
    return lax.axis_index("x"), lax.axis_index("y"), lax.axis_index("c")


def _peer(pos, k):
    x, y, c = pos
    return (1 - x if k & 4 else x, 1 - y if k & 2 else y, 1 - c if k & 1 else c)


def _flat_index(pos):
    return 4 * pos[0] + 2 * pos[1] + pos[2]


def _all_gather(shard):
    rows = shard.shape[0]

    def body(x_ref, out_ref, send_sems, recv_sems, local_sem):
        me = _mesh_pos()
        x, y, c = me
        sibling = (x, y, 1 - c)
        chips = [(1 - x, y), (x, 1 - y), (1 - x, 1 - y)]

        def copy(k, block, to, src=None):
            slot = out_ref.at[_flat_index(block)]
            return pltpu.make_async_remote_copy(
                src_ref=slot if src is None else src, dst_ref=slot,
                send_sem=send_sems.at[k], recv_sem=recv_sems.at[k],
                device_id=to, device_id_type=pl.DeviceIdType.MESH)

        mine = pltpu.make_async_copy(x_ref, out_ref.at[_flat_index(me)], local_sem)
        mine.start()
        first = [copy(0, me, sibling, src=x_ref)]
        first += [copy(1 + j, me, (*chip, c), src=x_ref) for j, chip in enumerate(chips)]
        for cp in first:
            cp.start()
        passed = [copy(4 + j, (*chip, c), sibling) for j, chip in enumerate(chips)]
        for j, chip in enumerate(chips):
            copy(1 + j, (*chip, c), me).wait_recv()
            passed[j].start()
        copy(0, sibling, me).wait_recv()
        for j, chip in enumerate(chips):
            copy(4 + j, (*chip, 1 - c), me).wait_recv()
        for cp in first + passed:
            cp.wait_send()
        mine.wait()

    return pl.pallas_call(
        body, out_shape=jax.ShapeDtypeStruct((N_DEV, rows, LANES), shard.dtype),
        in_specs=[pl.BlockSpec(memory_space=pl.ANY)], out_specs=pl.BlockSpec(memory_space=pl.ANY),
        scratch_shapes=[pltpu.SemaphoreType.DMA((7,)), pltpu.SemaphoreType.DMA((7,)), pltpu.SemaphoreType.DMA],
        name="weights_all_gather",
    )(shard)


def _grad_exchange(big, small):
    def body(big_ref, small_ref, big_out, small_out, bsend, brecv, ssend, srecv, local_sems):
        me = _mesh_pos()
        mine = _flat_index(me)
        loc = [pltpu.make_async_copy(big_ref.at[mine], big_out.at[mine], local_sems.at[0]),
               pltpu.make_async_copy(small_ref, small_out.at[mine], local_sems.at[1])]
        for cp in loc:
            cp.start()

        def copies(k):
            peer = _peer(me, k)
            theirs = _flat_index(peer)
            send = (pltpu.make_async_remote_copy(
                        src_ref=big_ref.at[theirs], dst_ref=big_out.at[mine], send_sem=bsend.at[k - 1],
                        recv_sem=brecv.at[k - 1], device_id=peer, device_id_type=pl.DeviceIdType.MESH),
                    pltpu.make_async_remote_copy(
                        src_ref=small_ref, dst_ref=small_out.at[mine], send_sem=ssend.at[k - 1],
                        recv_sem=srecv.at[k - 1], device_id=peer, device_id_type=pl.DeviceIdType.MESH))
            recv = (pltpu.make_async_remote_copy(
                        src_ref=big_ref.at[mine], dst_ref=big_out.at[theirs], send_sem=bsend.at[k - 1],
                        recv_sem=brecv.at[k - 1], device_id=me, device_id_type=pl.DeviceIdType.MESH),
                    pltpu.make_async_remote_copy(
                        src_ref=small_ref, dst_ref=small_out.at[theirs], send_sem=ssend.at[k - 1],
                        recv_sem=srecv.at[k - 1], device_id=me, device_id_type=pl.DeviceIdType.MESH))
            return send, recv

        plan = [copies(k) for k in range(1, N_DEV)]
        for send, _ in plan:
            for cp in send:
                cp.start()
        for _, recv in plan:
            for cp in recv:
                cp.wait_recv()
        for send, _ in plan:
            for cp in send:
                cp.wait_send()
        for cp in loc:
            cp.wait()

    any_spec = pl.BlockSpec(memory_space=pl.ANY)
    return pl.pallas_call(
        body,
        out_shape=[jax.ShapeDtypeStruct(big.shape, big.dtype),
                   jax.ShapeDtypeStruct((N_DEV,) + small.shape, small.dtype)],
        in_specs=[any_spec, any_spec], out_specs=[any_spec, any_spec],
        scratch_shapes=[pltpu.SemaphoreType.DMA((7,)), pltpu.SemaphoreType.DMA((7,)),
                        pltpu.SemaphoreType.DMA((7,)), pltpu.SemaphoreType.DMA((7,)),
                        pltpu.SemaphoreType.DMA((2,))],
        name="grad_exchange",
    )(big, small)


def _push_start(name, srcs, per_peer):
    n = len(srcs)
    lands = [lax.empty((N_DEV,) + src.shape[-2:], src.dtype) for src in srcs]

    def body(*refs):
        src_refs, land_refs = refs[:n], refs[n:2 * n]
        send_sems, recv_sems = refs[2 * n], refs[2 * n + 1]
        token = refs[-1]
        me = _mesh_pos()
        mine = _flat_index(me)
        for j in range(n):
            for k in range(1, N_DEV):
                peer = _peer(me, k)
                pltpu.make_async_remote_copy(
                    src_ref=src_refs[j].at[_flat_index(peer)] if per_peer else src_refs[j],
                    dst_ref=land_refs[j].at[mine], send_sem=send_sems.at[7 * j + k - 1],
                    recv_sem=recv_sems.at[7 * j + k - 1],
                    device_id=peer, device_id_type=pl.DeviceIdType.MESH).start()
        token[...] = jnp.zeros_like(token)

    hbm = pl.BlockSpec(memory_space=pltpu.HBM)
    sem = pl.BlockSpec(memory_space=pltpu.SEMAPHORE)
    sems = pltpu.SemaphoreType.DMA((n * (N_DEV - 1),))
    out = pl.pallas_call(
        body, name=name,
        out_shape=(sems, sems) + tuple(pltpu.HBM(x.shape, x.dtype) for x in srcs + lands)
                  + (jax.ShapeDtypeStruct((8, LANES), F32),),
        in_specs=(hbm,) * (2 * n), out_specs=(sem, sem) + (hbm,) * (2 * n) + (pl.BlockSpec(memory_space=pltpu.VMEM),),
        input_output_aliases={j: 2 + j for j in range(2 * n)},
        compiler_params=pltpu.CompilerParams(has_side_effects=pltpu.SideEffectType.DATAFLOW_SIDE_EFFECTING),
    )(*[pltpu.with_memory_space_constraint(x, pltpu.HBM) for x in srcs + lands])
    return out[0], out[1], list(out[2:2 + n]), list(out[2 + n:2 + 2 * n]), out[-1]


def _push_wait(name, started, per_peer, after):
    send_sems, recv_sems, srcs, lands, _ = started
    n = len(srcs)

    def body(*refs):
        src_refs, land_refs = refs[:n], refs[n:2 * n]
        send_sems, recv_sems = refs[2 * n], refs[2 * n + 1]
        me = _mesh_pos()
        for j in range(n):
            for k in range(1, N_DEV):
                theirs = _flat_index(_peer(me, k))
                copy = pltpu.make_async_remote_copy(
                    src_ref=src_refs[j].at[theirs] if per_peer else src_refs[j], dst_ref=land_refs[j].at[theirs],
                    send_sem=send_sems.at[7 * j + k - 1], recv_sem=recv_sems.at[7 * j + k - 1],
                    device_id=me, device_id_type=pl.DeviceIdType.MESH)
                copy.wait_send()
                copy.wait_recv()

    hbm = pl.BlockSpec(memory_space=pltpu.HBM)
    sem = pl.BlockSpec(memory_space=pltpu.SEMAPHORE)
    out = pl.pallas_call(
        body, name=name,
        out_shape=tuple(pltpu.HBM(x.shape, x.dtype) for x in srcs + lands),
        in_specs=(hbm,) * (2 * n) + (sem, sem, pl.BlockSpec(memory_space=pl.ANY)), out_specs=(hbm,) * (2 * n),
        input_output_aliases={j: j for j in range(2 * n)},
        compiler_params=pltpu.CompilerParams(has_side_effects=pltpu.SideEffectType.DATAFLOW_SIDE_EFFECTING),
    )(*srcs, *lands, send_sems, recv_sems, after)
    return list(out[:n]), list(out[n:])


def _slot_sum(name, slots, tr):
    rows, cols = slots.shape[1:]

    def body(s_ref, o_ref):
        acc = s_ref[0].astype(F32)
        for d in range(1, N_DEV):
            acc = acc + s_ref[d].astype(F32)
        o_ref[...] = acc

    return pl.pallas_call(
        body, grid=(rows // tr,),
        in_specs=[pl.BlockSpec((N_DEV, tr, cols), lambda i: (0, i, 0))],
        out_specs=pl.BlockSpec((tr, cols), lambda i: (i, 0)),
        out_shape=jax.ShapeDtypeStruct((rows, cols), F32),
        name=name, compiler_params=_params(("arbitrary",)),
    )(slots)


def _adamw(name, w, g, m, v):
    rows, cols = w.shape
    tr = _pick(rows, (256, 128, 88, 32, 1))
    c1 = 1.0 - ADAM_B1 ** ADAM_STEP
    c2 = 1.0 - ADAM_B2 ** ADAM_STEP

    def body(w_ref, g_ref, m_ref, v_ref, d_ref, nm_ref, nv_ref):
        g = g_ref[...]
        nm = ADAM_B1 * m_ref[...] + (1.0 - ADAM_B1) * g
        nv = ADAM_B2 * v_ref[...] + (1.0 - ADAM_B2) * (g * g)
        nm_ref[...] = nm
        nv_ref[...] = nv
        d_ref[...] = -ADAM_LR * ((nm / c1) / (jnp.sqrt(nv / c2) + ADAM_EPS) + ADAM_WD * w_ref[...])

    spec = pl.BlockSpec((tr, cols), lambda i: (i, 0))
    return pl.pallas_call(
        body, grid=(rows // tr,), in_specs=[spec] * 4, out_specs=[spec] * 3,
        out_shape=[jax.ShapeDtypeStruct((rows, cols), F32)] * 3,
        name=name, compiler_params=_params(("arbitrary",)),
    )(w, g, m, v)


def _pack_shards(shards, group, rows):
    parts = []
    for name, _, axis in group:
        w = shards[name].astype(BF16)
        parts.append((w.T if axis == 1 else w).reshape(-1))
    flat = jnp.concatenate(parts)
    flat = jnp.pad(flat, (0, rows * LANES - flat.shape[0]))
    return flat.reshape(rows, LANES)


def _unpack_full(gathered, group):
    flat = gathered.reshape(N_DEV, -1)
    out, off = {}, 0
    for name, (r, c), axis in group:
        rr, cc = (c, r) if axis == 1 else (r, c)
        out[name] = flat[:, off:off + r * c].reshape(N_DEV * rr, cc)
        off += r * c
    return out


def _pack_full_grads(grads, group, rows):
    parts = [grads[name].reshape(N_DEV, r * c) for name, (r, c), _ in group]
    flat = jnp.concatenate(parts, axis=1).astype(BF16)
    flat = jnp.pad(flat, ((0, 0), (0, rows * LANES - flat.shape[1])))
    return flat.reshape(N_DEV, rows, LANES)


def _unpack_shard_grads(flat, group):
    flat = flat.reshape(-1)
    out, off = {}, 0
    for name, (r, c), axis in group:
        seg = flat[off:off + r * c]
        out[name] = seg.reshape(c, r).T if axis == 1 else seg.reshape(r, c)
        off += r * c
    return out


def _own_slot(slots, own):
    mine = _flat_index(_mesh_pos())
    return lax.dynamic_update_slice(slots, own[None], (mine, 0, 0))


def _rope_tables(positions):
    inv_freq = ROPE_THETA ** (-jnp.arange(0, MLA_ROPE_DIM, 2, dtype=F32) / MLA_ROPE_DIM)
    ang = positions.astype(F32)[:, None] * inv_freq
    z64 = jnp.zeros((positions.shape[0], 64), F32)
    z32 = jnp.zeros((positions.shape[0], 32), F32)
    cos, sin = jnp.cos(ang), jnp.sin(ang)
    return (jnp.concatenate([z64, cos, cos, z32], axis=1), jnp.concatenate([z64, sin, sin, z32], axis=1))


def _row_tile(s_len, want):
    return _pick(s_len, (want, 256, 128))


def kernel(x, positions, norm_mix_pre, norm_mix_post, w_in, b_gate, q_norm, w_uq, kv_norm, w_ukv, w_proj_sb, w_proj_mla, w_out, norm_ffn_pre, norm_ffn_post, w_gate_up, w_down, loss_target, m_norm_mix_pre, m_norm_mix_post, m_w_in, m_b_gate, m_q_norm, m_w_uq, m_kv_norm, m_w_ukv, m_w_proj_sb, m_w_proj_mla, m_w_out, m_norm_ffn_pre, m_norm_ffn_post, m_w_gate_up, m_w_down, v_norm_mix_pre, v_norm_mix_post, v_w_in, v_b_gate, v_q_norm, v_w_uq, v_kv_norm, v_w_ukv, v_w_proj_sb, v_w_proj_mla, v_w_out, v_norm_ffn_pre, v_norm_ffn_post, v_w_gate_up, v_w_down):
    weights = dict(norm_mix_pre=norm_mix_pre, norm_mix_post=norm_mix_post, w_in=w_in, b_gate=b_gate, q_norm=q_norm,
                   w_uq=w_uq, kv_norm=kv_norm, w_ukv=w_ukv, w_proj_sb=w_proj_sb, w_proj_mla=w_proj_mla, w_out=w_out,
                   norm_ffn_pre=norm_ffn_pre, norm_ffn_post=norm_ffn_post, w_gate_up=w_gate_up, w_down=w_down)
    m_in = dict(norm_mix_pre=m_norm_mix_pre, norm_mix_post=m_norm_mix_post, w_in=m_w_in, b_gate=m_b_gate,
                q_norm=m_q_norm, w_uq=m_w_uq, kv_norm=m_kv_norm, w_ukv=m_w_ukv, w_proj_sb=m_w_proj_sb,
                w_proj_mla=m_w_proj_mla, w_out=m_w_out, norm_ffn_pre=m_norm_ffn_pre, norm_ffn_post=m_norm_ffn_post,
                w_gate_up=m_w_gate_up, w_down=m_w_down)
    v_in = dict(norm_mix_pre=v_norm_mix_pre, norm_mix_post=v_norm_mix_post, w_in=v_w_in, b_gate=v_b_gate,
                q_norm=v_q_norm, w_uq=v_w_uq, kv_norm=v_kv_norm, w_ukv=v_w_ukv, w_proj_sb=v_w_proj_sb,
                w_proj_mla=v_w_proj_mla, w_out=v_w_out, norm_ffn_pre=v_norm_ffn_pre, norm_ffn_post=v_norm_ffn_post,
                w_gate_up=v_w_gate_up, w_down=v_w_down)
    order = list(weights)

    xs = x[0]
    target = loss_target[0]
    s_len = xs.shape[0]
    tm_fwd = _row_tile(s_len, 512)
    tm = _row_tile(s_len, 256)
    tm_ffn = tm

    shards = {name: weights[name][0] for name, _, _ in SHARDED}
    late_shards = [(shards[name].T if axis == 1 else shards[name]).astype(BF16) for name, _, axis in LATE]
    late_weights = _push_start("weights_late_start", late_shards, False)
    full = _unpack_full(_all_gather(_pack_shards(shards, EARLY, EARLY_ROWS)), EARLY)
    wt = full["w_in"]
    zr = lambda n: jnp.zeros((n, D_MODEL), BF16)
    w_ext = jnp.concatenate([wt[:2176], zr(64), wt[2176:2208], zr(32), wt[2208:]], axis=0).T
    wa = jnp.pad(full["w_uq"].reshape(N_HEADS, MLA_QK_DIM, MLA_Q_RANK), ((0, 0), (0, 32), (0, 0))
                 ).reshape(N_HEADS * LANES, MLA_Q_RANK).T
    ukv = full["w_ukv"].reshape(N_HEADS, LANES, MLA_KV_RANK)
    wk = jnp.pad(ukv[:, :64], ((0, 0), (0, 64), (0, 0))).reshape(N_HEADS * LANES, MLA_KV_RANK).T
    wv = ukv[:, 64:].reshape(512, MLA_KV_RANK).T
    g_mix_pre = norm_mix_pre + late_weights[4][0:1, 0:1]
    cr, sr = _rope_tables(positions[0])

    qkv, cq, ckv, kr, gl, hb = _in_proj(xs, g_mix_pre, w_ext, tm_fwd)
    q_mla, k_mla, v_mla, cqn, ckvn = _mla_up(cq, ckv, kr, cr, sr, q_norm, kv_norm, wa, wk, wv, tm_fwd)
    o_sb, o_mla, lse = _attn_fwd(qkv, q_mla, k_mla, v_mla)
    late_shards, late_slots = _push_wait("weights_late_wait", late_weights, False, o_sb)
    full = {name: _own_slot(slots, own).reshape(-1, own.shape[1])
            for (name, _, _), slots, own in zip(LATE, late_slots, late_shards)}
    w_sb, w_mla, w_o, w_gu, w_dn = (full["w_proj_sb"].T, full["w_proj_mla"].T, full["w_out"], full["w_gate_up"].T,
                                    full["w_down"])
    x1, y, merged, o_sb_b, o_mla_b = _mix_out(o_sb, o_mla, gl, xs, b_gate, norm_mix_post, w_sb, w_mla, w_o, tm_fwd)
    dx2, f, h2, loss_part = _ffn_fwd(x1, target, norm_ffn_pre, norm_ffn_post, w_gu, w_dn, tm_fwd)
    loss_local = (0.5 / D_MODEL * jnp.sum(loss_part)).reshape(1)

    dx1, act, dgu, dfb, dg_ffn_pre, dg_ffn_post = _ffn_bwd(dx2, f, x1, norm_ffn_pre, norm_ffn_post, w_gu, w_dn, tm_ffn)
    dyb, dpsb, dpmla, dgl, do_sb, do_mla, dg_mix_post, db_gate = _mix_bwd(
        dx1, y, o_sb, o_mla, gl, b_gate, norm_mix_post, w_sb, w_mla, w_o, tm)
    late_grads = {
        "w_proj_sb": _matmul_tn("dw_proj_sb", dpsb, o_sb_b),
        "w_proj_mla": _matmul_tn("dw_proj_mla", dpmla, o_mla_b),
        "w_out": _matmul_tn("dw_out", merged, dyb),
        "w_gate_up": _matmul_tn("dw_gate_up", dgu, h2),
        "w_down": _matmul_tn("dw_down", act, dfb),
    }
    late_grads = [late_grads[name].reshape(N_DEV, -1, late_grads[name].shape[1]) for name, _, _ in LATE]
    late_exchange = _push_start("grads_late_start", late_grads, True)
    dq_sb, dq_mla, dk_sb, dv_sb, dk_mla, dv_mla = _attn_bwd(qkv, do_sb, o_sb, q_mla, k_mla, v_mla, do_mla, o_mla, lse,
                                                             late_exchange[4])
    da, dkb, dvb, dlat, dg_q, dg_kv = _mla_up_bwd(dq_mla, dk_mla, dv_mla, cq, ckv, cr, sr, q_norm, kv_norm,
                                                  wa, wk, wv, tm)
    dx, dproj, dg_mix_pre = _in_proj_bwd(xs, dx1, dq_sb, dk_sb, dv_sb, dlat, dgl, norm_mix_pre, w_ext, tm)
    d_ext = _matmul_tn("dw_in", dproj, hb)
    d_wa = _matmul_tn("dw_uq", da, cqn)
    d_wk = _matmul_tn("dw_uk", dkb, ckvn)
    d_wv = _matmul_tn("dw_uv", dvb, ckvn)
    early_grads = _pack_full_grads({
        "w_in": jnp.concatenate([d_ext[:2176], d_ext[2240:2272], d_ext[EXT_GL:]], axis=0),
        "w_uq": d_wa.reshape(N_HEADS, LANES, MLA_Q_RANK)[:, :MLA_QK_DIM].reshape(768, MLA_Q_RANK),
        "w_ukv": jnp.concatenate([d_wk.reshape(N_HEADS, LANES, MLA_KV_RANK)[:, :64],
                                  d_wv.reshape(N_HEADS, 64, MLA_KV_RANK)], axis=1).reshape(1024, MLA_KV_RANK),
    }, EARLY, EARLY_ROWS)
    small_parts = dict(norm_mix_pre=dg_mix_pre, norm_mix_post=dg_mix_post, b_gate=db_gate, q_norm=dg_q,
                       kv_norm=dg_kv, norm_ffn_pre=dg_ffn_pre, norm_ffn_post=dg_ffn_post)
    small = jnp.concatenate([small_parts[name].sum(axis=0) for name, _ in SMALL] + [loss_local])
    small = jnp.pad(small, (0, SMALL_ROWS * LANES - small.shape[0])).reshape(SMALL_ROWS, LANES)

    early_slots, small_slots = _grad_exchange(early_grads, small)
    late_grads, late_slots = _push_wait("grads_late_wait", late_exchange, True, early_slots)
    mine = _flat_index(_mesh_pos())
    g_out = _unpack_shard_grads(_slot_sum("grad_early_sum", early_slots, 960), EARLY)
    for (name, _, axis), slots, own in zip(LATE, late_slots, late_grads):
        slots = _own_slot(slots, lax.dynamic_index_in_dim(own, mine, 0, keepdims=False))
        total = _slot_sum("grad_sum_" + name, slots, _pick(slots.shape[1], (352, 128)))
        g_out[name] = total.T if axis == 1 else total
    s_flat = _slot_sum("grad_small_sum", small_slots, SMALL_ROWS).reshape(-1)
    off = 0
    for name, n in SMALL:
        g_out[name] = s_flat[off:off + n].reshape(1, n)
        off += n
    loss = s_flat[off]

    deltas, new_m, new_v = {}, {}, {}
    for name in order:
        w2 = weights[name].reshape(g_out[name].shape)
        d, nm, nv = _adamw("adamw_" + name, w2, g_out[name], m_in[name].reshape(w2.shape), v_in[name].reshape(w2.shape))
        shape = weights[name].shape
        deltas[name], new_m[name], new_v[name] = d.reshape(shape), nm.reshape(shape), nv.reshape(shape)
        g_out[name] = g_out[name].reshape(shape)

    return (loss, dx[None], *[g_out[n] for n in order], *[deltas[n] for n in order],
            *[new_m[n] for n in order], *[new_v[n] for n in order])
```

```python
import functools
import math

import jax
import jax.numpy as jnp
from jax import lax
from jax.experimental import pallas as pl
from jax.experimental.pallas import tpu as pltpu

F32 = jnp.float32
BF16 = jnp.bfloat16

D_MODEL = 1024
N_HEADS = 8
SB_WIDTH = 512
MLA_Q_RANK = 384
MLA_KV_RANK = 256
MLA_ROPE_DIM = 32
MLA_QK_DIM = 96
D_FF = 2816
ROPE_THETA = 10000.0
EPS = 1e-6
SB_SCALE = 1.0 / math.sqrt(64.0)
MLA_SCALE = 1.0 / math.sqrt(96.0)
NEG_BIG = -1e30

ADAM_LR = 0.001
ADAM_B1 = 0.9
ADAM_B2 = 0.999
ADAM_EPS = 1e-08
ADAM_WD = 0.01
ADAM_STEP = 10

N_DEV = 8
LANES = 128
TQ = 512
TK = 128
DIAG_TILES = TQ // TK
FWD_UNROLL = 4
BWD_UNROLL = 2
VMEM_LIMIT = 56 << 20

EXT_QKV = 0
EXT_CQ = 1536
EXT_CKV = 1920
EXT_KR = 2176
EXT_GL = 2304
EXT_N = 4352

EARLY = (("w_in", (1024, 532), 1), ("w_uq", (384, 96), 1), ("w_ukv", (256, 128), 1))
LATE = (("w_proj_sb", (512, 128), 1), ("w_proj_mla", (512, 128), 1), ("w_out", (128, 1024), 0),
        ("w_gate_up", (1024, 704), 1), ("w_down", (352, 1024), 0))
SHARDED = EARLY + LATE
EARLY_ROWS = 4800
SMALL = (("norm_mix_pre", 1024), ("norm_mix_post", 1024), ("b_gate", 2048), ("q_norm", 384),
         ("kv_norm", 256), ("norm_ffn_pre", 1024), ("norm_ffn_post", 1024))
SMALL_ROWS = 56


def _dot(a, b):
    return jnp.dot(a, b, preferred_element_type=F32)


def _dot_nt(a, b):
    return lax.dot_general(a, b, (((1,), (1,)), ((), ())), preferred_element_type=F32)


def _dot_tn(a, b):
    return lax.dot_general(a, b, (((0,), (0,)), ((), ())), preferred_element_type=F32)


def _rms(x):
    r = lax.rsqrt(jnp.mean(x * x, axis=-1, keepdims=True) + EPS)
    return x * r, r


def _rms_bwd(dn, n, r):
    return r * (dn - n * jnp.mean(dn * n, axis=-1, keepdims=True))


def _colsum8(x):
    return jnp.sum(x.reshape(x.shape[0] // 8, 8, x.shape[1]), axis=0)


def _split(x):
    hi = x.astype(BF16)
    return hi, (x - hi.astype(F32)).astype(BF16)


def _rot(x):
    lane = lax.broadcasted_iota(jnp.int32, x.shape, 1)
    up = pltpu.roll(x, 112, 1)
    down = pltpu.roll(x, 16, 1)
    return jnp.where((lane >= 64) & (lane < 80), -up, jnp.where((lane >= 80) & (lane < 96), down, 0.0))


def _params(sem):
    return pltpu.CompilerParams(dimension_semantics=sem, vmem_limit_bytes=VMEM_LIMIT)


def _rows_call(name, body, n_rows, tm, row_ins, const_ins, row_outs, acc_outs):
    in_specs = [pl.BlockSpec((tm, a.shape[1]), lambda i: (i, 0)) for a in row_ins]
    in_specs += [pl.BlockSpec(a.shape, lambda i: (0, 0), pipeline_mode=pl.Buffered(1)) for a in const_ins]
    out_specs = [pl.BlockSpec((tm, n), lambda i: (i, 0)) for n, _ in row_outs]
    out_specs += [pl.BlockSpec(s, lambda i: (0, 0)) for s in acc_outs]
    out_shape = [jax.ShapeDtypeStruct((n_rows, n), dt) for n, dt in row_outs]
    out_shape += [jax.ShapeDtypeStruct(s, F32) for s in acc_outs]
    return pl.pallas_call(
        body, grid=(n_rows // tm,), in_specs=in_specs, out_specs=out_specs, out_shape=out_shape,
        name=name, compiler_params=_params(("arbitrary",)),
    )(*row_ins, *const_ins)


def _in_proj(x, g_pre, w_ext, tm):
    def body(x_ref, g_ref, w_ref, qkv_ref, cq_ref, ckv_ref, kr_ref, gl_ref, h_ref):
        n, _ = _rms(x_ref[...])
        hb = (n * g_ref[...]).astype(BF16)
        h_ref[...] = hb
        for c in range(0, 1536, 512):
            qkv_ref[:, c:c + 512] = _dot(hb, w_ref[:, c:c + 512]).astype(BF16)
        cq_ref[...] = _dot(hb, w_ref[:, EXT_CQ:EXT_CKV])
        ckv_ref[...] = _dot(hb, w_ref[:, EXT_CKV:EXT_KR])
        kr_ref[...] = _dot(hb, w_ref[:, EXT_KR:EXT_GL])
        for c in range(0, 2048, 512):
            gl_ref[:, c:c + 512] = _dot(hb, w_ref[:, EXT_GL + c:EXT_GL + c + 512])

    return _rows_call("in_proj", body, x.shape[0], tm, [x], [g_pre, w_ext],
                      [(1536, BF16), (384, F32), (256, F32), (128, F32), (2048, F32), (1024, BF16)], [])


def _mla_up(cq, ckv, kr, cr, sr, q_norm, kv_norm, wa, wk, wv, tm):
    def body(cq_ref, ckv_ref, kr_ref, cr_ref, sr_ref, qn_ref, kvn_ref, wa_ref, wk_ref, wv_ref,
             q_ref, k_ref, v_ref, cqn_ref, ckvn_ref):
        nq, _ = _rms(cq_ref[...])
        cqn = (nq * qn_ref[...]).astype(BF16)
        cqn_ref[...] = cqn
        nk, _ = _rms(ckv_ref[...])
        ckvn = (nk * kvn_ref[...]).astype(BF16)
        ckvn_ref[...] = ckvn
        cr = cr_ref[...]
        sr = sr_ref[...]
        lane = lax.broadcasted_iota(jnp.int32, cr.shape, 1)
        cm = cr + (lane < 64).astype(F32)
        kr = kr_ref[...]
        krp = kr * cr + _rot(kr) * sr
        for h in range(N_HEADS):
            hs = slice(h * LANES, (h + 1) * LANES)
            a = _dot(cqn, wa_ref[:, hs])
            q_ref[:, hs] = (a * cm + _rot(a) * sr).astype(BF16)
            k_ref[:, hs] = (_dot(ckvn, wk_ref[:, hs]) + krp).astype(BF16)
        v_ref[...] = _dot(ckvn, wv_ref[...]).astype(BF16)

    return _rows_call("mla_up", body, cq.shape[0], tm, [cq, ckv, kr, cr, sr], [q_norm, kv_norm, wa, wk, wv],
                      [(1024, BF16), (1024, BF16), (512, BF16), (384, BF16), (256, BF16)], [])


def _mix_out(o_sb, o_mla, gl, x, b_gate, g_post, w_sb, w_mla, w_out, tm):
    def body(osb_ref, omla_ref, gl_ref, x_ref, b_ref, gp_ref, wsb_ref, wmla_ref, wout_ref,
             x1_ref, y_ref, mb_ref, osbb_ref, omlab_ref):
        osb = osb_ref[...].astype(BF16)
        omla = omla_ref[...].astype(BF16)
        osbb_ref[...] = osb
        omlab_ref[...] = omla
        psb = _dot(osb, wsb_ref[...])
        pmla = _dot(omla, wmla_ref[...])
        gates = jax.nn.sigmoid(gl_ref[...] + b_ref[...])
        mb = (gates[:, :D_MODEL] * psb + gates[:, D_MODEL:] * pmla).astype(BF16)
        mb_ref[...] = mb
        y = _dot(mb, wout_ref[...])
        y_ref[...] = y
        n, _ = _rms(y)
        x1_ref[...] = x_ref[...] + n * gp_ref[...]

    return _rows_call("mix_out", body, x.shape[0], tm, [o_sb, o_mla, gl, x], [b_gate, g_post, w_sb, w_mla, w_out],
                      [(1024, F32), (1024, F32), (1024, BF16), (512, BF16), (512, BF16)], [])


FF_CHUNK = 1408


def _ffn_fwd(x1, target, g_pre, g_post, w_gu, w_down, tm):
    def body(x1_ref, t_ref, gpre_ref, gpost_ref, wgu_ref, wd_ref, dx2_ref, f_ref, h2_ref, loss_ref):
        x1 = x1_ref[...]
        n, _ = _rms(x1)
        h2 = (n * gpre_ref[...]).astype(BF16)
        h2_ref[...] = h2
        f = jnp.zeros((tm, D_MODEL), F32)
        for c in range(0, D_FF, FF_CHUNK):
            g = _dot(h2, wgu_ref[:, c:c + FF_CHUNK])
            u = _dot(h2, wgu_ref[:, D_FF + c:D_FF + c + FF_CHUNK])
            act = (g * jax.nn.sigmoid(g) * u).astype(BF16)
            f = f + _dot(act, wd_ref[c:c + FF_CHUNK, :])
        f_ref[...] = f
        nf, _ = _rms(f)
        err = x1 + nf * gpost_ref[...] - t_ref[...]
        dx2_ref[...] = err * (1.0 / D_MODEL)
        e8 = _colsum8(err * err)
        part = e8[:, 0:LANES]
        for c in range(LANES, D_MODEL, LANES):
            part = part + e8[:, c:c + LANES]

        @pl.when(pl.program_id(0) == 0)
        def _():
            loss_ref[...] = jnp.zeros_like(loss_ref)

        loss_ref[...] += part

    return _rows_call("ffn_fwd", body, x1.shape[0], tm, [x1, target], [g_pre, g_post, w_gu, w_down],
                      [(1024, F32), (1024, F32), (1024, BF16)], [(8, LANES)])


def _ffn_bwd(dx2, f, x1, g_pre, g_post, w_gu, w_down, tm):
    def body(dx2_ref, f_ref, x1_ref, gpre_ref, gpost_ref, wgu_ref, wd_ref,
             dx1_ref, act_ref, dgu_ref, dfb_ref, dgpre_ref, dgpost_ref):
        @pl.when(pl.program_id(0) == 0)
        def _():
            dgpre_ref[...] = jnp.zeros_like(dgpre_ref)
            dgpost_ref[...] = jnp.zeros_like(dgpost_ref)

        dx2 = dx2_ref[...]
        nf, rf = _rms(f_ref[...])
        dgpost_ref[...] += _colsum8(dx2 * nf)
        dfb = _rms_bwd(dx2 * gpost_ref[...], nf, rf).astype(BF16)
        dfb_ref[...] = dfb
        x1 = x1_ref[...]
        n1, r1 = _rms(x1)
        h2 = (n1 * gpre_ref[...]).astype(BF16)
        dh2 = jnp.zeros((tm, D_MODEL), F32)
        for c in range(0, D_FF, FF_CHUNK):
            cs, us = slice(c, c + FF_CHUNK), slice(D_FF + c, D_FF + c + FF_CHUNK)
            g = _dot(h2, wgu_ref[:, cs])
            u = _dot(h2, wgu_ref[:, us])
            sg = jax.nn.sigmoid(g)
            si = g * sg
            act_ref[:, cs] = (si * u).astype(BF16)
            dact = _dot_nt(dfb, wd_ref[cs, :])
            dg = (dact * u * (sg * (1.0 + g * (1.0 - sg)))).astype(BF16)
            du = (dact * si).astype(BF16)
            dgu_ref[:, cs] = dg
            dgu_ref[:, us] = du
            dh2 = dh2 + _dot_nt(dg, wgu_ref[:, cs]) + _dot_nt(du, wgu_ref[:, us])
        dgpre_ref[...] += _colsum8(dh2 * n1)
        dx1_ref[...] = dx2 + _rms_bwd(dh2 * gpre_ref[...], n1, r1)

    return _rows_call("ffn_bwd", body, dx2.shape[0], tm, [dx2, f, x1], [g_pre, g_post, w_gu, w_down],
                      [(1024, F32), (D_FF, BF16), (2 * D_FF, BF16), (1024, BF16)], [(8, 1024), (8, 1024)])


def _mix_bwd(dx1, y, o_sb, o_mla, gl, b_gate, g_post, w_sb, w_mla, w_out, tm):
    def body(dx1_ref, y_ref, osb_ref, omla_ref, gl_ref, b_ref, gp_ref, wsb_ref, wmla_ref, wout_ref,
             dyb_ref, dpsb_ref, dpmla_ref, dgl_ref, dosb_ref, domla_ref, dgpost_ref, dbg_ref):
        @pl.when(pl.program_id(0) == 0)
        def _():
            dgpost_ref[...] = jnp.zeros_like(dgpost_ref)
            dbg_ref[...] = jnp.zeros_like(dbg_ref)

        dx1 = dx1_ref[...]
        ny, ry = _rms(y_ref[...])
        dgpost_ref[...] += _colsum8(dx1 * ny)
        dyb = _rms_bwd(dx1 * gp_ref[...], ny, ry).astype(BF16)
        dyb_ref[...] = dyb
        dm = _dot_nt(dyb, wout_ref[...])
        psb = _dot(osb_ref[...].astype(BF16), wsb_ref[...])
        pmla = _dot(omla_ref[...].astype(BF16), wmla_ref[...])
        gates = jax.nn.sigmoid(gl_ref[...] + b_ref[...])
        g0, g1 = gates[:, :D_MODEL], gates[:, D_MODEL:]
        dpsb = (dm * g0).astype(BF16)
        dpmla = (dm * g1).astype(BF16)
        dpsb_ref[...] = dpsb
        dpmla_ref[...] = dpmla
        dgl0 = dm * psb * g0 * (1.0 - g0)
        dgl1 = dm * pmla * g1 * (1.0 - g1)
        dgl_ref[:, :D_MODEL] = dgl0.astype(BF16)
        dgl_ref[:, D_MODEL:] = dgl1.astype(BF16)
        dbg_ref[:, :D_MODEL] += _colsum8(dgl0)
        dbg_ref[:, D_MODEL:] += _colsum8(dgl1)
        dosb_ref[...] = _dot_nt(dpsb, wsb_ref[...]).astype(BF16)
        domla_ref[...] = _dot_nt(dpmla, wmla_ref[...]).astype(BF16)

    return _rows_call("mix_bwd", body, dx1.shape[0], tm, [dx1, y, o_sb, o_mla, gl],
                      [b_gate, g_post, w_sb, w_mla, w_out],
                      [(1024, BF16), (1024, BF16), (1024, BF16), (2048, BF16), (512, BF16), (512, BF16)],
                      [(8, 1024), (8, 2048)])


def _mla_up_bwd(dq, dk, dv, cq, ckv, cr, sr, q_norm, kv_norm, wa, wk, wv, tm):
    def body(dq_ref, dk_ref, dv_ref, cq_ref, ckv_ref, cr_ref, sr_ref, qn_ref, kvn_ref, wa_ref, wk_ref, wv_ref,
             da_ref, dkb_ref, dvb_ref, dlat_ref, dqn_ref, dkvn_ref):
        @pl.when(pl.program_id(0) == 0)
        def _():
            dqn_ref[...] = jnp.zeros_like(dqn_ref)
            dkvn_ref[...] = jnp.zeros_like(dkvn_ref)

        cr = cr_ref[...]
        sr = sr_ref[...]
        lane = lax.broadcasted_iota(jnp.int32, cr.shape, 1)
        cm = cr + (lane < 64).astype(F32)
        nq, rq = _rms(cq_ref[...])
        nk, rk = _rms(ckv_ref[...])
        dcqn = jnp.zeros((tm, MLA_Q_RANK), F32)
        dckvn = jnp.zeros((tm, MLA_KV_RANK), F32)
        dkrp = jnp.zeros((tm, LANES), F32)
        for h in range(N_HEADS):
            hs = slice(h * LANES, (h + 1) * LANES)
            dqh = dq_ref[:, hs]
            da = (dqh * cm - _rot(dqh * sr)).astype(BF16)
            da_ref[:, hs] = da
            dcqn = dcqn + _dot_nt(da, wa_ref[:, hs])
            dkh = dk_ref[:, hs]
            dkb = dkh.astype(BF16)
            dkb_ref[:, hs] = dkb
            dckvn = dckvn + _dot_nt(dkb, wk_ref[:, hs])
            dkrp = dkrp + dkh
        dvb = dv_ref[...].astype(BF16)
        dvb_ref[...] = dvb
        dckvn = dckvn + _dot_nt(dvb, wv_ref[...])
        dkr = dkrp * cr - _rot(dkrp * sr)
        dqn_ref[...] += _colsum8(dcqn * nq)
        dkvn_ref[...] += _colsum8(dckvn * nk)
        dlat_ref[:, 0:384] = _rms_bwd(dcqn * qn_ref[...], nq, rq).astype(BF16)
        dlat_ref[:, 384:640] = _rms_bwd(dckvn * kvn_ref[...], nk, rk).astype(BF16)
        dlat_ref[:, 640:768] = dkr.astype(BF16)

    return _rows_call("mla_up_bwd", body, dq.shape[0], tm, [dq, dk, dv, cq, ckv, cr, sr],
                      [q_norm, kv_norm, wa, wk, wv],
                      [(1024, BF16), (1024, BF16), (512, BF16), (768, BF16)], [(8, 384), (8, 256)])


def _in_proj_bwd(x, dx1, dq_sb, dk_sb, dv_sb, dlat, dgl, g_pre, w_ext, tm):
    def body(x_ref, dx1_ref, dq_ref, dk_ref, dv_ref, dlat_ref, dgl_ref, g_ref, w_ref,
             dx_ref, dproj_ref, dg_ref):
        @pl.when(pl.program_id(0) == 0)
        def _():
            dg_ref[...] = jnp.zeros_like(dg_ref)

        dproj_ref[:, 0:512] = dq_ref[...].astype(BF16)
        dproj_ref[:, 512:1024] = dk_ref[...].astype(BF16)
        dproj_ref[:, 1024:1536] = dv_ref[...].astype(BF16)
        dproj_ref[:, EXT_CQ:EXT_GL] = dlat_ref[...]
        dproj_ref[:, EXT_GL:EXT_N] = dgl_ref[...]
        dh = jnp.zeros((tm, D_MODEL), F32)
        for c in range(0, EXT_N, 2176):
            dh = dh + _dot_nt(dproj_ref[:, c:c + 2176], w_ref[:, c:c + 2176])
        n, r = _rms(x_ref[...])
        dg_ref[...] += _colsum8(dh * n)
        dx_ref[...] = dx1_ref[...] + _rms_bwd(dh * g_ref[...], n, r)

    return _rows_call("in_proj_bwd", body, x.shape[0], tm, [x, dx1, dq_sb, dk_sb, dv_sb, dlat, dgl],
                      [g_pre, w_ext], [(1024, F32), (EXT_N, BF16)], [(8, 1024)])


def _head_masked(x):
    lane = lax.broadcasted_iota(jnp.int32, x.shape, 1)
    zero = jnp.zeros_like(x)
    return jnp.where(lane < 64, x, zero), jnp.where(lane >= 64, x, zero)


def _cum_weights():
    row = lax.broadcasted_iota(jnp.int32, (TK, TK), 0)
    col = lax.broadcasted_iota(jnp.int32, (TK, TK), 1)
    half = jnp.concatenate([(row > col).astype(BF16), jnp.ones((TK, TK), BF16)], axis=1)
    return jnp.concatenate([half, half], axis=0)


def _split_cat(x):
    hi, lo = _split(x)
    return jnp.concatenate([hi, lo], axis=1)


def _sweep(i, tiles, unroll, skip_rows=False):
    if skip_rows:
        for t in range(DIAG_TILES // unroll):
            top = DIAG_TILES - 1 - t * unroll
            r0 = (top - unroll + 1) * TK
            col = lax.broadcasted_iota(jnp.int32, (TQ - r0, TK), 1)
            tiles(i * DIAG_TILES + top, [(top - u) * TK - r0 + col for u in range(unroll)], r0)
    else:
        col = lax.broadcasted_iota(jnp.int32, (TQ, TK), 1)

        def diag(t, _):
            top = DIAG_TILES - 1 - t * unroll
            tiles(i * DIAG_TILES + top, [(top - u) * TK + col for u in range(unroll)])
            return 0

        lax.fori_loop(0, DIAG_TILES // unroll, diag, 0)

    def full(g, _):
        tiles(i * DIAG_TILES - 1 - g * unroll, [None] * unroll)
        return 0

    lax.fori_loop(0, (i * DIAG_TILES) // unroll, full, 0)


def _causal(key, strict):
    if key is None:
        return None
    row = lax.broadcasted_iota(jnp.int32, key.shape, 0)
    return key < row if strict else key <= row


def _sb_logs(z, valid, w_cum):
    soft = jnp.log(1.0 + jnp.exp(-jnp.abs(z)))
    lsm = -jnp.maximum(z, 0.0) - soft
    cat = _split_cat(lsm if valid is None else jnp.where(valid, lsm, 0.0))
    return z + lsm, _dot(cat, w_cum)


def _sb_weight(log_beta, cs, r, valid):
    a = jnp.exp(log_beta + cs[:, :TK] + r)
    if valid is not None:
        a = jnp.where(valid, a, 0.0)
    return a, r + cs[:, TK:]


def _block_diag(x):
    x0, x1 = x[:, :LANES], x[:, LANES:]
    zero = jnp.zeros_like(x0)
    return jnp.concatenate([jnp.concatenate([x0, zero], axis=1), jnp.concatenate([zero, x1], axis=1)], axis=0)


def _attn_fwd(qkv, q, k, v):
    s_len = qkv.shape[0]

    def body(qs_ref, ks_ref, vs_ref, qm_ref, km_ref, vm_ref, osb_ref, omla_ref, lse_ref,
             sacc_ref, r_ref, macc_ref, m_ref):
        i = pl.program_id(1)
        lane = lax.broadcasted_iota(jnp.int32, (TQ, LANES), 1)
        r2 = lax.broadcasted_iota(jnp.int32, (2 * TK, LANES), 0)
        c2 = lax.broadcasted_iota(jnp.int32, (2 * TK, LANES), 1)
        head_ones = ((r2 < TK) == (c2 < 64)).astype(BF16)
        w_cum = _cum_weights()
        qs = qs_ref[...] * SB_SCALE
        qm = qm_ref[...]
        sacc_ref[...] = jnp.zeros_like(sacc_ref)
        r_ref[...] = jnp.zeros_like(r_ref)
        macc_ref[...] = jnp.zeros_like(macc_ref)
        m_ref[...] = jnp.full(m_ref.shape, NEG_BIG, F32)

        def sb_scores(top, valids):
            out = []
            for u, valid in enumerate(valids):
                off = pl.multiple_of((top - u) * TK, TK)
                z01 = _dot_nt(qs, jnp.concatenate(_head_masked(ks_ref[pl.ds(off, TK), :]), axis=0))
                out.append([_sb_logs(z, valid, w_cum) for z in (z01[:, :TK], z01[:, TK:])])
            return out

        def sb_accumulate(top, valids, scores):
            parts = [[None, None] for _ in valids]
            for hh in range(2):
                r = r_ref[hh]
                for u, valid in enumerate(valids):
                    a, r = _sb_weight(*scores[u][hh], r, valid)
                    parts[u][hh] = _split_cat(a)
                r_ref[hh] = r
            vs = []
            for u in range(len(valids)):
                off = pl.multiple_of((top - u) * TK, TK)
                v0, v1 = _head_masked(vs_ref[pl.ds(off, TK), :])
                vs += [v0, v0, v1, v1]
            sacc_ref[...] += _dot(jnp.concatenate([p for pair in parts for p in pair], axis=1),
                                  jnp.concatenate(vs, axis=0))

        def mla_scores(top, valids):
            s01s = []
            for u in range(len(valids)):
                off = pl.multiple_of((top - u) * TK, TK)
                s01s.append(_dot_nt(qm, _block_diag(km_ref[pl.ds(off, TK), :])))
            heads = []
            for hh in range(2):
                ss = []
                for u, valid in enumerate(valids):
                    s = s01s[u][:, hh * TK:(hh + 1) * TK] * MLA_SCALE
                    ss.append(s if valid is None else jnp.where(valid, s, NEG_BIG))
                m_old = m_ref[hh]
                m = jnp.maximum(m_old, jnp.max(functools.reduce(jnp.maximum, ss), axis=1, keepdims=True))
                m_ref[hh] = m
                heads.append((ss, m, jnp.exp(m_old - m)))
            return heads

        def mla_accumulate(top, heads):
            vs = []
            for u in range(len(heads[0][0])):
                off = pl.multiple_of((top - u) * TK, TK)
                vv = jnp.concatenate(_head_masked(vm_ref[pl.ds(off, TK), :]), axis=0)
                vs.append(jnp.concatenate([vv, head_ones], axis=1))
            ps = [[jnp.exp(s - m).astype(BF16) for s in ss] for ss, m, _ in heads]
            scale = jnp.where(lane < 64, heads[0][2], heads[1][2])
            p_all = jnp.concatenate([ps[hh][u] for u in range(len(vs)) for hh in range(2)], axis=1)
            macc_ref[...] = (macc_ref[...] * jnp.concatenate([scale, scale], axis=1)
                             + _dot(p_all, jnp.concatenate(vs, axis=0)))

        def tiles(top, keys):
            strict = [_causal(key, True) for key in keys]
            heads = mla_scores(top, [_causal(key, False) for key in keys])
            scores = sb_scores(top, strict)
            mla_accumulate(top, heads)
            sb_accumulate(top, strict, scores)

        _sweep(i, tiles, FWD_UNROLL)
        osb_ref[...] = sacc_ref[...]
        acc = macc_ref[...]
        den = acc[:, LANES:]
        omla_ref[...] = acc[:, :LANES] / den
        for hh, mask in enumerate((lane < 64, lane >= 64)):
            l = jnp.max(jnp.where(mask, den, 0.0), axis=1, keepdims=True)
            lse_ref[hh] = jnp.broadcast_to(m_ref[hh] + jnp.log(l), (TQ, LANES))

    tile = pl.BlockSpec((TQ, LANES), lambda h, i: (i, h))
    return pl.pallas_call(
        body, grid=(4, s_len // TQ),
        in_specs=[tile,
                  pl.BlockSpec((s_len, LANES), lambda h, i: (0, 4 + h)),
                  pl.BlockSpec((s_len, LANES), lambda h, i: (0, 8 + h)),
                  pl.BlockSpec((TQ, 2 * LANES), lambda h, i: (i, h)),
                  pl.BlockSpec((s_len, 2 * LANES), lambda h, i: (0, h)),
                  pl.BlockSpec((s_len, LANES), lambda h, i: (0, h))],
        out_specs=[tile, tile, pl.BlockSpec((2, TQ, LANES), lambda h, i: (h, i, 0))],
        out_shape=[jax.ShapeDtypeStruct((s_len, SB_WIDTH), F32), jax.ShapeDtypeStruct((s_len, 512), F32),
                   jax.ShapeDtypeStruct((N_HEADS, s_len, LANES), F32)],
        scratch_shapes=[pltpu.VMEM((TQ, LANES), F32), pltpu.VMEM((2, TQ, LANES), F32),
                        pltpu.VMEM((TQ, 2 * LANES), F32), pltpu.VMEM((2, TQ, 1), F32)],
        name="attn_fwd", compiler_params=_params(("arbitrary", "arbitrary")),
    )(qkv, qkv, qkv, q, k, v)


def _row_dots(do, o):
    prod = do.astype(F32) * o
    p0, p1 = _head_masked(prod)
    return tuple(jnp.broadcast_to(jnp.sum(p, axis=1, keepdims=True), prod.shape) for p in (p0, p1))


def _attn_bwd(qkv, do_sb, o_sb, q, k, v, do_mla, o_mla, lse, after):
    s_len = qkv.shape[0]
    n_q = s_len // TQ

    def body(qs_ref, ks_ref, vs_ref, dos_ref, os_ref, qm_ref, km_ref, vm_ref, dom_ref, om_ref, lse_ref, after_ref,
             dqs_ref, dqm_ref, dks_hbm, dvs_hbm, dkm_hbm, dvm_hbm,
             dqs_acc, dqm_acc, dks_acc, dvs_acc, dkm_acc, dvm_acc, r_ref, g_ref, ds_ref, dm_ref, out_sems):
        h = pl.program_id(0)
        i = pl.program_id(1)

        @pl.when(i == 0)
        def _():
            for acc in (dks_acc, dvs_acc, dkm_acc, dvm_acc):
                acc[...] = jnp.zeros_like(acc)

        w_cum = _cum_weights()
        qs = qs_ref[...] * SB_SCALE
        dos = dos_ref[...]
        ds_ref[0], ds_ref[1] = _row_dots(dos, os_ref[...])
        qm = qm_ref[...]
        dom = dom_ref[...]
        dm_ref[0], dm_ref[1] = _row_dots(dom, om_ref[...])
        qs_heads, dos_heads, dom_heads = _head_masked(qs), _head_masked(dos), _head_masked(dom)
        by_head = lambda pair, rows: jnp.concatenate([x[rows] for x in pair], axis=0)
        all_rows = slice(0, TQ)
        qs_rows, dos_rows, dom_rows = (by_head(p, all_rows) for p in (qs_heads, dos_heads, dom_heads))
        qm_diag = _block_diag(qm)
        for ref in (dqs_acc, dqm_acc, r_ref, g_ref):
            ref[...] = jnp.zeros_like(ref)
        heads = (slice(0, TK), slice(TK, 2 * TK))

        def weigh(log_beta, cs, r, da, valid):
            a, r = _sb_weight(log_beta, cs, r, valid)
            g = a * da
            return a.astype(BF16), g, _split_cat(g), r

        def logit_grad(g, gs, carried, d, log_beta, valid):
            upto = d - (gs[:, :TK] + carried)
            dz = g - jnp.exp(log_beta) * upto
            if valid is not None:
                dz = jnp.where(valid, dz, 0.0)
            return dz.astype(BF16), carried + gs[:, TK:]

        def tiles(top, keys, r0=0):
            n = len(keys)
            rows = slice(r0, TQ)
            strict = [_causal(key, True) for key in keys]
            loose = [_causal(key, False) for key in keys]
            offs = [pl.multiple_of((top - u) * TK, TK) for u in range(n)]
            kds = [_block_diag(km_ref[pl.ds(off, TK), :]) for off in offs]
            vms = [jnp.concatenate(_head_masked(vm_ref[pl.ds(off, TK), :]), axis=0) for off in offs]
            s01s = [_dot_nt(qm[rows], kd) for kd in kds]
            dp01s = [_dot_nt(dom[rows], vv) for vv in vms]
            kks = [jnp.concatenate(_head_masked(ks_ref[pl.ds(off, TK), :]), axis=0) for off in offs]
            vvs = [jnp.concatenate(_head_masked(vs_ref[pl.ds(off, TK), :]), axis=0) for off in offs]
            z01s = [_dot_nt(qs[rows], kk) for kk in kks]
            da01s = [_dot_nt(dos[rows], vv) for vv in vvs]
            logs = [[_sb_logs(z01s[u][:, hs], strict[u], w_cum) for hs in heads] for u in range(n)]

            dss = [[None, None] for _ in range(n)]
            ps = [[None, None] for _ in range(n)]
            for u in range(n):
                for hh, hs in enumerate(heads):
                    p = jnp.exp(s01s[u][:, hs] * MLA_SCALE - lse_ref[hh, rows, :])
                    if loose[u] is not None:
                        p = jnp.where(loose[u], p, 0.0)
                    dss[u][hh] = (p * (dp01s[u][:, hs] - dm_ref[hh, rows, :]) * MLA_SCALE).astype(BF16)
                    ps[u][hh] = p.astype(BF16)

            dzs = [[None, None] for _ in range(n)]
            avs = [[None, None] for _ in range(n)]
            sums = [[None, None] for _ in range(n)]
            for hh, hs in enumerate(heads):
                r = r_ref[hh, rows, :]
                for u in range(n):
                    avs[u][hh], g, cat, r = weigh(*logs[u][hh], r, da01s[u][:, hs], strict[u])
                    sums[u][hh] = (g, _dot(cat, w_cum))
                r_ref[hh, rows, :] = r

            dqm_acc[rows, :] += _dot(jnp.concatenate([d for pair in dss for d in pair], axis=1),
                                     jnp.concatenate(kds, axis=0))
            span = pl.ds(offs[-1], n * TK)
            by_key = lambda pairs: jnp.concatenate([jnp.concatenate(pair, axis=0) for pair in pairs[::-1]], axis=1)
            whole = r0 == 0
            dkm_acc[span, :] += _dot_tn(by_key(dss), qm_diag if whole else _block_diag(qm[rows]))
            dvm_acc[span, :] += _dot_tn(by_key(ps), dom_rows if whole else by_head(dom_heads, rows))

            for hh in range(2):
                carried = g_ref[hh, rows, :]
                for u in range(n):
                    dzs[u][hh], carried = logit_grad(*sums[u][hh], carried, ds_ref[hh, rows, :], logs[u][hh][0],
                                                     strict[u])
                g_ref[hh, rows, :] = carried
            dqs_acc[rows, :] += _dot(jnp.concatenate([dz for pair in dzs for dz in pair], axis=1),
                                     jnp.concatenate(kks, axis=0))
            dks_acc[span, :] += _dot_tn(by_key(dzs), qs_rows if whole else by_head(qs_heads, rows))
            dvs_acc[span, :] += _dot_tn(by_key(avs), dos_rows if whole else by_head(dos_heads, rows))

        _sweep(i, tiles, BWD_UNROLL, skip_rows=True)
        dqs_ref[...] = dqs_acc[...] * SB_SCALE
        dqm_ref[...] = dqm_acc[...]

        @pl.when(i == n_q - 1)
        def _():
            narrow = pl.ds(pl.multiple_of(h * LANES, LANES), LANES)
            wide = pl.ds(pl.multiple_of(h * 2 * LANES, 2 * LANES), 2 * LANES)
            copies = [pltpu.make_async_copy(dks_acc, dks_hbm.at[:, narrow], out_sems.at[0]),
                      pltpu.make_async_copy(dvs_acc, dvs_hbm.at[:, narrow], out_sems.at[1]),
                      pltpu.make_async_copy(dkm_acc, dkm_hbm.at[:, wide], out_sems.at[2]),
                      pltpu.make_async_copy(dvm_acc, dvm_hbm.at[:, narrow], out_sems.at[3])]
            for cp in copies:
                cp.start()
            for cp in copies:
                cp.wait()

    tile = pl.BlockSpec((TQ, LANES), lambda h, i: (i, h))
    wide_tile = pl.BlockSpec((TQ, 2 * LANES), lambda h, i: (i, h))
    once = pl.Buffered(1)
    hbm = pl.BlockSpec(memory_space=pl.ANY)
    return pl.pallas_call(
        body, grid=(4, n_q),
        in_specs=[tile,
                  pl.BlockSpec((s_len, LANES), lambda h, i: (0, 4 + h), pipeline_mode=once),
                  pl.BlockSpec((s_len, LANES), lambda h, i: (0, 8 + h), pipeline_mode=once),
                  tile, tile, wide_tile,
                  pl.BlockSpec((s_len, 2 * LANES), lambda h, i: (0, h), pipeline_mode=once),
                  pl.BlockSpec((s_len, LANES), lambda h, i: (0, h), pipeline_mode=once),
                  tile, tile, pl.BlockSpec((2, TQ, LANES), lambda h, i: (h, i, 0)),
                  pl.BlockSpec((8, LANES), lambda h, i: (0, 0))],
        out_specs=[tile, wide_tile, hbm, hbm, hbm, hbm],
        out_shape=[jax.ShapeDtypeStruct((s_len, SB_WIDTH), F32), jax.ShapeDtypeStruct((s_len, 1024), F32),
                   jax.ShapeDtypeStruct((s_len, SB_WIDTH), F32), jax.ShapeDtypeStruct((s_len, SB_WIDTH), F32),
                   jax.ShapeDtypeStruct((s_len, 1024), F32), jax.ShapeDtypeStruct((s_len, 512), F32)],
        scratch_shapes=[pltpu.VMEM((TQ, LANES), F32), pltpu.VMEM((TQ, 2 * LANES), F32),
                        pltpu.VMEM((s_len, LANES), F32), pltpu.VMEM((s_len, LANES), F32),
                        pltpu.VMEM((s_len, 2 * LANES), F32), pltpu.VMEM((s_len, LANES), F32),
                        pltpu.VMEM((2, TQ, LANES), F32), pltpu.VMEM((2, TQ, LANES), F32),
                        pltpu.VMEM((2, TQ, LANES), F32), pltpu.VMEM((2, TQ, LANES), F32),
                        pltpu.SemaphoreType.DMA((4,))],
        name="attn_bwd", compiler_params=_params(("arbitrary", "arbitrary")),
    )(qkv, qkv, qkv, do_sb, o_sb, q, k, v, do_mla, o_mla, lse, after)


def _pick(n, options):
    for t in options:
        if n % t == 0:
            return t
    raise ValueError(n)


def _matmul_tn(name, a, b):
    s_len, m = a.shape
    n = b.shape[1]
    tm = _pick(m, (1024, 1408, 2176, 512))
    tn = _pick(n, (1024, 512, 384, 256))
    tk = _pick(s_len, (1024, 512, 256, 128))
    n_k = s_len // tk

    def body(a_ref, b_ref, o_ref, acc_ref):
        @pl.when(pl.program_id(2) == 0)
        def _():
            acc_ref[...] = jnp.zeros_like(acc_ref)

        acc_ref[...] += _dot_tn(a_ref[...], b_ref[...])

        @pl.when(pl.program_id(2) == n_k - 1)
        def _():
            o_ref[...] = acc_ref[...].astype(BF16)

    return pl.pallas_call(
        body, grid=(m // tm, n // tn, n_k),
        in_specs=[pl.BlockSpec((tk, tm), lambda i, j, l: (l, i)), pl.BlockSpec((tk, tn), lambda i, j, l: (l, j))],
        out_specs=pl.BlockSpec((tm, tn), lambda i, j, l: (i, j)),
        out_shape=jax.ShapeDtypeStruct((m, n), BF16),
        scratch_shapes=[pltpu.VMEM((tm, tn), F32)],
        name=name, compiler_params=_params(("arbitrary", "arbitrary", "arbitrary")),
    )(a, b)


def _mesh_pos():
    return lax.axis_index("x"), lax.axis_index("y"), lax.axis_index("c")


def _peer(pos, k):
    x, y, c = pos
    return (1 - x if k & 4 else x, 1 - y if k & 2 else y, 1 - c if k & 1 else c)


def _flat_index(pos):
    return 4 * pos[0] + 2 * pos[1] + pos[2]


def _all_gather(shard):
    rows = shard.shape[0]

    def body(x_ref, out_ref, send_sems, recv_sems, local_sem):
        me = _mesh_pos()
        x, y, c = me
        sibling = (x, y, 1 - c)
        chips = [(1 - x, y), (x, 1 - y), (1 - x, 1 - y)]

        def copy(k, block, to, src=None):
            slot = out_ref.at[_flat_index(block)]
            return pltpu.make_async_remote_copy(
                src_ref=slot if src is None else src, dst_ref=slot,
                send_sem=send_sems.at[k], recv_sem=recv_sems.at[k],
                device_id=to, device_id_type=pl.DeviceIdType.MESH)

        mine = pltpu.make_async_copy(x_ref, out_ref.at[_flat_index(me)], local_sem)
        mine.start()
        first = [copy(0, me, sibling, src=x_ref)]
        first += [copy(1 + j, me, (*chip, c), src=x_ref) for j, chip in enumerate(chips)]
        for cp in first:
            cp.start()
        passed = [copy(4 + j, (*chip, c), sibling) for j, chip in enumerate(chips)]
        for j, chip in enumerate(chips):
            copy(1 + j, (*chip, c), me).wait_recv()
            passed[j].start()
        copy(0, sibling, me).wait_recv()
        for j, chip in enumerate(chips):
            copy(4 + j, (*chip, 1 - c), me).wait_recv()
        for cp in first + passed:
            cp.wait_send()
        mine.wait()

    return pl.pallas_call(
        body, out_shape=jax.ShapeDtypeStruct((N_DEV, rows, LANES), shard.dtype),
        in_specs=[pl.BlockSpec(memory_space=pl.ANY)], out_specs=pl.BlockSpec(memory_space=pl.ANY),
        scratch_shapes=[pltpu.SemaphoreType.DMA((7,)), pltpu.SemaphoreType.DMA((7,)), pltpu.SemaphoreType.DMA],
        name="weights_all_gather",
    )(shard)


def _push_start(name, srcs, per_peer):
    n = len(srcs)
    lands = [lax.empty((N_DEV,) + src.shape[-2:], src.dtype) for src in srcs]

    def body(*refs):
        src_refs, land_refs = refs[:n], refs[n:2 * n]
        send_sems, recv_sems = refs[2 * n], refs[2 * n + 1]
        token = refs[-1]
        me = _mesh_pos()
        mine = _flat_index(me)
        for j in range(n):
            for k in range(1, N_DEV):
                peer = _peer(me, k)
                pltpu.make_async_remote_copy(
                    src_ref=src_refs[j].at[_flat_index(peer)] if per_peer[j] else src_refs[j],
                    dst_ref=land_refs[j].at[mine], send_sem=send_sems.at[7 * j + k - 1],
                    recv_sem=recv_sems.at[7 * j + k - 1],
                    device_id=peer, device_id_type=pl.DeviceIdType.MESH).start()
        token[...] = jnp.zeros_like(token)

    hbm = pl.BlockSpec(memory_space=pltpu.HBM)
    sem = pl.BlockSpec(memory_space=pltpu.SEMAPHORE)
    sems = pltpu.SemaphoreType.DMA((n * (N_DEV - 1),))
    out = pl.pallas_call(
        body, name=name,
        out_shape=(sems, sems) + tuple(pltpu.HBM(x.shape, x.dtype) for x in srcs + lands)
                  + (jax.ShapeDtypeStruct((8, LANES), F32),),
        in_specs=(hbm,) * (2 * n), out_specs=(sem, sem) + (hbm,) * (2 * n) + (pl.BlockSpec(memory_space=pltpu.VMEM),),
        input_output_aliases={j: 2 + j for j in range(2 * n)},
        compiler_params=pltpu.CompilerParams(has_side_effects=pltpu.SideEffectType.DATAFLOW_SIDE_EFFECTING),
    )(*[pltpu.with_memory_space_constraint(x, pltpu.HBM) for x in srcs + lands])
    return out[0], out[1], list(out[2:2 + n]), list(out[2 + n:2 + 2 * n]), out[-1]


def _push_wait(name, started, per_peer, after):
    send_sems, recv_sems, srcs, lands, _ = started
    n = len(srcs)

    def body(*refs):
        src_refs, land_refs = refs[:n], refs[n:2 * n]
        send_sems, recv_sems = refs[2 * n], refs[2 * n + 1]
        me = _mesh_pos()
        for j in range(n):
            for k in range(1, N_DEV):
                theirs = _flat_index(_peer(me, k))
                copy = pltpu.make_async_remote_copy(
                    src_ref=src_refs[j].at[theirs] if per_peer[j] else src_refs[j], dst_ref=land_refs[j].at[theirs],
                    send_sem=send_sems.at[7 * j + k - 1], recv_sem=recv_sems.at[7 * j + k - 1],
                    device_id=me, device_id_type=pl.DeviceIdType.MESH)
                copy.wait_send()
                copy.wait_recv()

    hbm = pl.BlockSpec(memory_space=pltpu.HBM)
    sem = pl.BlockSpec(memory_space=pltpu.SEMAPHORE)
    out = pl.pallas_call(
        body, name=name,
        out_shape=tuple(pltpu.HBM(x.shape, x.dtype) for x in srcs + lands),
        in_specs=(hbm,) * (2 * n) + (sem, sem, pl.BlockSpec(memory_space=pl.ANY)), out_specs=(hbm,) * (2 * n),
        input_output_aliases={j: j for j in range(2 * n)},
        compiler_params=pltpu.CompilerParams(has_side_effects=pltpu.SideEffectType.DATAFLOW_SIDE_EFFECTING),
    )(*srcs, *lands, send_sems, recv_sems, after)
    return list(out[:n]), list(out[n:])


def _slot_sum(name, slots, tr):
    rows, cols = slots.shape[1:]

    def body(s_ref, o_ref):
        acc = s_ref[0].astype(F32)
        for d in range(1, N_DEV):
            acc = acc + s_ref[d].astype(F32)
        o_ref[...] = acc

    return pl.pallas_call(
        body, grid=(rows // tr,),
        in_specs=[pl.BlockSpec((N_DEV, tr, cols), lambda i: (0, i, 0))],
        out_specs=pl.BlockSpec((tr, cols), lambda i: (i, 0)),
        out_shape=jax.ShapeDtypeStruct((rows, cols), F32),
        name=name, compiler_params=_params(("arbitrary",)),
    )(slots)


def _adamw(name, w, g, m, v):
    rows, cols = w.shape
    tr = _pick(rows, (256, 128, 88, 32, 1))
    c1 = 1.0 - ADAM_B1 ** ADAM_STEP
    c2 = 1.0 - ADAM_B2 ** ADAM_STEP

    def body(w_ref, g_ref, m_ref, v_ref, d_ref, nm_ref, nv_ref):
        g = g_ref[...]
        nm = ADAM_B1 * m_ref[...] + (1.0 - ADAM_B1) * g
        nv = ADAM_B2 * v_ref[...] + (1.0 - ADAM_B2) * (g * g)
        nm_ref[...] = nm
        nv_ref[...] = nv
        d_ref[...] = -ADAM_LR * ((nm / c1) / (jnp.sqrt(nv / c2) + ADAM_EPS) + ADAM_WD * w_ref[...])

    spec = pl.BlockSpec((tr, cols), lambda i: (i, 0))
    return pl.pallas_call(
        body, grid=(rows // tr,), in_specs=[spec] * 4, out_specs=[spec] * 3,
        out_shape=[jax.ShapeDtypeStruct((rows, cols), F32)] * 3,
        name=name, compiler_params=_params(("arbitrary",)),
    )(w, g, m, v)


def _pack_shards(shards, group, rows):
    parts = []
    for name, _, axis in group:
        w = shards[name].astype(BF16)
        parts.append((w.T if axis == 1 else w).reshape(-1))
    flat = jnp.concatenate(parts)
    flat = jnp.pad(flat, (0, rows * LANES - flat.shape[0]))
    return flat.reshape(rows, LANES)


def _unpack_full(gathered, group):
    flat = gathered.reshape(N_DEV, -1)
    out, off = {}, 0
    for name, (r, c), axis in group:
        rr, cc = (c, r) if axis == 1 else (r, c)
        out[name] = flat[:, off:off + r * c].reshape(N_DEV * rr, cc)
        off += r * c
    return out


def _pack_full_grads(grads, group, rows):
    parts = [grads[name].reshape(N_DEV, r * c) for name, (r, c), _ in group]
    flat = jnp.concatenate(parts, axis=1).astype(BF16)
    flat = jnp.pad(flat, ((0, 0), (0, rows * LANES - flat.shape[1])))
    return flat.reshape(N_DEV, rows, LANES)


def _unpack_shard_grads(flat, group):
    flat = flat.reshape(-1)
    out, off = {}, 0
    for name, (r, c), axis in group:
        seg = flat[off:off + r * c]
        out[name] = seg.reshape(c, r).T if axis == 1 else seg.reshape(r, c)
        off += r * c
    return out


def _own_slot(slots, own):
    mine = _flat_index(_mesh_pos())
    return lax.dynamic_update_slice(slots, own[None], (mine, 0, 0))


def _rope_tables(positions):
    inv_freq = ROPE_THETA ** (-jnp.arange(0, MLA_ROPE_DIM, 2, dtype=F32) / MLA_ROPE_DIM)
    ang = positions.astype(F32)[:, None] * inv_freq
    z64 = jnp.zeros((positions.shape[0], 64), F32)
    z32 = jnp.zeros((positions.shape[0], 32), F32)
    cos, sin = jnp.cos(ang), jnp.sin(ang)
    return (jnp.concatenate([z64, cos, cos, z32], axis=1), jnp.concatenate([z64, sin, sin, z32], axis=1))


def _row_tile(s_len, want):
    return _pick(s_len, (want, 256, 128))


def kernel(x, positions, norm_mix_pre, norm_mix_post, w_in, b_gate, q_norm, w_uq, kv_norm, w_ukv, w_proj_sb, w_proj_mla, w_out, norm_ffn_pre, norm_ffn_post, w_gate_up, w_down, loss_target, m_norm_mix_pre, m_norm_mix_post, m_w_in, m_b_gate, m_q_norm, m_w_uq, m_kv_norm, m_w_ukv, m_w_proj_sb, m_w_proj_mla, m_w_out, m_norm_ffn_pre, m_norm_ffn_post, m_w_gate_up, m_w_down, v_norm_mix_pre, v_norm_mix_post, v_w_in, v_b_gate, v_q_norm, v_w_uq, v_kv_norm, v_w_ukv, v_w_proj_sb, v_w_proj_mla, v_w_out, v_norm_ffn_pre, v_norm_ffn_post, v_w_gate_up, v_w_down):
    weights = dict(norm_mix_pre=norm_mix_pre, norm_mix_post=norm_mix_post, w_in=w_in, b_gate=b_gate, q_norm=q_norm,
                   w_uq=w_uq, kv_norm=kv_norm, w_ukv=w_ukv, w_proj_sb=w_proj_sb, w_proj_mla=w_proj_mla, w_out=w_out,
                   norm_ffn_pre=norm_ffn_pre, norm_ffn_post=norm_ffn_post, w_gate_up=w_gate_up, w_down=w_down)
    m_in = dict(norm_mix_pre=m_norm_mix_pre, norm_mix_post=m_norm_mix_post, w_in=m_w_in, b_gate=m_b_gate,
                q_norm=m_q_norm, w_uq=m_w_uq, kv_norm=m_kv_norm, w_ukv=m_w_ukv, w_proj_sb=m_w_proj_sb,
                w_proj_mla=m_w_proj_mla, w_out=m_w_out, norm_ffn_pre=m_norm_ffn_pre, norm_ffn_post=m_norm_ffn_post,
                w_gate_up=m_w_gate_up, w_down=m_w_down)
    v_in = dict(norm_mix_pre=v_norm_mix_pre, norm_mix_post=v_norm_mix_post, w_in=v_w_in, b_gate=v_b_gate,
                q_norm=v_q_norm, w_uq=v_w_uq, kv_norm=v_kv_norm, w_ukv=v_w_ukv, w_proj_sb=v_w_proj_sb,
                w_proj_mla=v_w_proj_mla, w_out=v_w_out, norm_ffn_pre=v_norm_ffn_pre, norm_ffn_post=v_norm_ffn_post,
                w_gate_up=v_w_gate_up, w_down=v_w_down)
    order = list(weights)

    xs = x[0]
    target = loss_target[0]
    s_len = xs.shape[0]
    tm_fwd = _row_tile(s_len, 512)
    tm = _row_tile(s_len, 256)
    tm_ffn = tm

    shards = {name: weights[name][0] for name, _, _ in SHARDED}
    late_shards = [(shards[name].T if axis == 1 else shards[name]).astype(BF16) for name, _, axis in LATE]
    late_weights = _push_start("weights_late_start", late_shards, [False] * len(LATE))
    full = _unpack_full(_all_gather(_pack_shards(shards, EARLY, EARLY_ROWS)), EARLY)
    wt = full["w_in"]
    zr = lambda n: jnp.zeros((n, D_MODEL), BF16)
    w_ext = jnp.concatenate([wt[:2176], zr(64), wt[2176:2208], zr(32), wt[2208:]], axis=0).T
    wa = jnp.pad(full["w_uq"].reshape(N_HEADS, MLA_QK_DIM, MLA_Q_RANK), ((0, 0), (0, 32), (0, 0))
                 ).reshape(N_HEADS * LANES, MLA_Q_RANK).T
    ukv = full["w_ukv"].reshape(N_HEADS, LANES, MLA_KV_RANK)
    wk = jnp.pad(ukv[:, :64], ((0, 0), (0, 64), (0, 0))).reshape(N_HEADS * LANES, MLA_KV_RANK).T
    wv = ukv[:, 64:].reshape(512, MLA_KV_RANK).T
    g_mix_pre = norm_mix_pre + late_weights[4][0:1, 0:1]
    cr, sr = _rope_tables(positions[0])

    qkv, cq, ckv, kr, gl, hb = _in_proj(xs, g_mix_pre, w_ext, tm_fwd)
    q_mla, k_mla, v_mla, cqn, ckvn = _mla_up(cq, ckv, kr, cr, sr, q_norm, kv_norm, wa, wk, wv, tm_fwd)
    o_sb, o_mla, lse = _attn_fwd(qkv, q_mla, k_mla, v_mla)
    late_shards, late_slots = _push_wait("weights_late_wait", late_weights, [False] * len(LATE), o_sb)
    full = {name: _own_slot(slots, own).reshape(-1, own.shape[1])
            for (name, _, _), slots, own in zip(LATE, late_slots, late_shards)}
    w_sb, w_mla, w_o, w_gu, w_dn = (full["w_proj_sb"].T, full["w_proj_mla"].T, full["w_out"], full["w_gate_up"].T,
                                    full["w_down"])
    x1, y, merged, o_sb_b, o_mla_b = _mix_out(o_sb, o_mla, gl, xs, b_gate, norm_mix_post, w_sb, w_mla, w_o, tm_fwd)
    dx2, f, h2, loss_part = _ffn_fwd(x1, target, norm_ffn_pre, norm_ffn_post, w_gu, w_dn, tm_fwd)
    loss_local = (0.5 / D_MODEL * jnp.sum(loss_part)).reshape(1)

    dx1, act, dgu, dfb, dg_ffn_pre, dg_ffn_post = _ffn_bwd(dx2, f, x1, norm_ffn_pre, norm_ffn_post, w_gu, w_dn, tm_ffn)
    dyb, dpsb, dpmla, dgl, do_sb, do_mla, dg_mix_post, db_gate = _mix_bwd(
        dx1, y, o_sb, o_mla, gl, b_gate, norm_mix_post, w_sb, w_mla, w_o, tm)
    late_grads = {
        "w_proj_sb": _matmul_tn("dw_proj_sb", dpsb, o_sb_b),
        "w_proj_mla": _matmul_tn("dw_proj_mla", dpmla, o_mla_b),
        "w_out": _matmul_tn("dw_out", merged, dyb),
        "w_gate_up": _matmul_tn("dw_gate_up", dgu, h2),
        "w_down": _matmul_tn("dw_down", act, dfb),
    }
    late_grads = [late_grads[name].reshape(N_DEV, -1, late_grads[name].shape[1]) for name, _, _ in LATE]
    late_exchange = _push_start("grads_late_start", late_grads, [True] * len(LATE))
    dq_sb, dq_mla, dk_sb, dv_sb, dk_mla, dv_mla = _attn_bwd(qkv, do_sb, o_sb, q_mla, k_mla, v_mla, do_mla, o_mla, lse,
                                                             late_exchange[4])
    da, dkb, dvb, dlat, dg_q, dg_kv = _mla_up_bwd(dq_mla, dk_mla, dv_mla, cq, ckv, cr, sr, q_norm, kv_norm,
                                                  wa, wk, wv, tm)
    dx, dproj, dg_mix_pre = _in_proj_bwd(xs, dx1, dq_sb, dk_sb, dv_sb, dlat, dgl, norm_mix_pre, w_ext, tm)
    d_ext = _matmul_tn("dw_in", dproj, hb)
    d_wa = _matmul_tn("dw_uq", da, cqn)
    d_wk = _matmul_tn("dw_uk", dkb, ckvn)
    d_wv = _matmul_tn("dw_uv", dvb, ckvn)
    early_grads = _pack_full_grads({
        "w_in": jnp.concatenate([d_ext[:2176], d_ext[2240:2272], d_ext[EXT_GL:]], axis=0),
        "w_uq": d_wa.reshape(N_HEADS, LANES, MLA_Q_RANK)[:, :MLA_QK_DIM].reshape(768, MLA_Q_RANK),
        "w_ukv": jnp.concatenate([d_wk.reshape(N_HEADS, LANES, MLA_KV_RANK)[:, :64],
                                  d_wv.reshape(N_HEADS, 64, MLA_KV_RANK)], axis=1).reshape(1024, MLA_KV_RANK),
    }, EARLY, EARLY_ROWS)
    small_parts = dict(norm_mix_pre=dg_mix_pre, norm_mix_post=dg_mix_post, b_gate=db_gate, q_norm=dg_q,
                       kv_norm=dg_kv, norm_ffn_pre=dg_ffn_pre, norm_ffn_post=dg_ffn_post)
    small = jnp.concatenate([small_parts[name].sum(axis=0) for name, _ in SMALL] + [loss_local])
    small = jnp.pad(small, (0, SMALL_ROWS * LANES - small.shape[0])).reshape(SMALL_ROWS, LANES)

    early_exchange = _push_start("grads_early_start", [early_grads, small], [True, False])
    late_grads, late_slots = _push_wait("grads_late_wait", late_exchange, [True] * len(LATE), early_exchange[4])
    mine = _flat_index(_mesh_pos())
    g_out, deltas, new_m, new_v = {}, {}, {}, {}

    def adamw(name):
        w2 = weights[name].reshape(g_out[name].shape)
        d, nm, nv = _adamw("adamw_" + name, w2, g_out[name], m_in[name].reshape(w2.shape), v_in[name].reshape(w2.shape))
        shape = weights[name].shape
        deltas[name], new_m[name], new_v[name] = d.reshape(shape), nm.reshape(shape), nv.reshape(shape)
        g_out[name] = g_out[name].reshape(shape)

    for (name, _, axis), slots, own in zip(LATE, late_slots, late_grads):
        slots = _own_slot(slots, lax.dynamic_index_in_dim(own, mine, 0, keepdims=False))
        total = _slot_sum("grad_sum_" + name, slots, _pick(slots.shape[1], (352, 128)))
        g_out[name] = total.T if axis == 1 else total
        adamw(name)
    (early_grads, small), (early_slots, small_slots) = _push_wait(
        "grads_early_wait", early_exchange, [True, False], deltas["w_down"])
    early_slots = _own_slot(early_slots, lax.dynamic_index_in_dim(early_grads, mine, 0, keepdims=False))
    g_out.update(_unpack_shard_grads(_slot_sum("grad_early_sum", early_slots, 960), EARLY))
    s_flat = _slot_sum("grad_small_sum", _own_slot(small_slots, small), SMALL_ROWS).reshape(-1)
    off = 0
    for name, n in SMALL:
        g_out[name] = s_flat[off:off + n].reshape(1, n)
        off += n
    loss = s_flat[off]
    for name in order:
        if name not in deltas:
            adamw(name)

    return (loss, dx[None], *[g_out[n] for n in order], *[deltas[n] for n in order],
            *[new_m[n] for n in order], *[new_v[n] for n in order])
```

```python
import functools
import math

import jax
import jax.numpy as jnp
from jax import lax
from jax.experimental import pallas as pl
from jax.experimental.pallas import tpu as pltpu

F32 = jnp.float32
BF16 = jnp.bfloat16

D_MODEL = 1024
N_HEADS = 8
SB_WIDTH = 512
MLA_Q_RANK = 384
MLA_KV_RANK = 256
MLA_ROPE_DIM = 32
MLA_QK_DIM = 96
D_FF = 2816
ROPE_THETA = 10000.0
EPS = 1e-6
SB_SCALE = 1.0 / math.sqrt(64.0)
MLA_SCALE = 1.0 / math.sqrt(96.0)
NEG_BIG = -1e30

ADAM_LR = 0.001
ADAM_B1 = 0.9
ADAM_B2 = 0.999
ADAM_EPS = 1e-08
ADAM_WD = 0.01
ADAM_STEP = 10

N_DEV = 8
LANES = 128
TQ = 512
TK = 128
DIAG_TILES = TQ // TK
FWD_UNROLL = 4
BWD_UNROLL = 2
VMEM_LIMIT = 56 << 20

EXT_QKV = 0
EXT_CQ = 1536
EXT_CKV = 1920
EXT_KR = 2176
EXT_GL = 2304
EXT_N = 4352

EARLY = (("w_in", (1024, 532), 1), ("w_uq", (384, 96), 1), ("w_ukv", (256, 128), 1))
LATE = (("w_proj_sb", (512, 128), 1), ("w_proj_mla", (512, 128), 1), ("w_out", (128, 1024), 0),
        ("w_gate_up", (1024, 704), 1), ("w_down", (352, 1024), 0))
SHARDED = EARLY + LATE
EARLY_ROWS = 4800
SMALL = (("norm_mix_pre", 1024), ("norm_mix_post", 1024), ("b_gate", 2048), ("q_norm", 384),
         ("kv_norm", 256), ("norm_ffn_pre", 1024), ("norm_ffn_post", 1024))
SMALL_ROWS = 56


def _dot(a, b):
    return jnp.dot(a, b, preferred_element_type=F32)


def _dot_nt(a, b):
    return lax.dot_general(a, b, (((1,), (1,)), ((), ())), preferred_element_type=F32)


def _dot_tn(a, b):
    return lax.dot_general(a, b, (((0,), (0,)), ((), ())), preferred_element_type=F32)


def _rms(x):
    r = lax.rsqrt(jnp.mean(x * x, axis=-1, keepdims=True) + EPS)
    return x * r, r


def _rms_bwd(dn, n, r):
    return r * (dn - n * jnp.mean(dn * n, axis=-1, keepdims=True))


def _colsum8(x):
    return jnp.sum(x.reshape(x.shape[0] // 8, 8, x.shape[1]), axis=0)


def _split(x):
    hi = x.astype(BF16)
    return hi, (x - hi.astype(F32)).astype(BF16)


def _rot(x):
    lane = lax.broadcasted_iota(jnp.int32, x.shape, 1)
    up = pltpu.roll(x, 112, 1)
    down = pltpu.roll(x, 16, 1)
    return jnp.where((lane >= 64) & (lane < 80), -up, jnp.where((lane >= 80) & (lane < 96), down, 0.0))


def _params(sem):
    return pltpu.CompilerParams(dimension_semantics=sem, vmem_limit_bytes=VMEM_LIMIT)


def _rows_call(name, body, n_rows, tm, row_ins, const_ins, row_outs, acc_outs):
    in_specs = [pl.BlockSpec((tm, a.shape[1]), lambda i: (i, 0)) for a in row_ins]
    in_specs += [pl.BlockSpec(a.shape, lambda i: (0, 0), pipeline_mode=pl.Buffered(1)) for a in const_ins]
    out_specs = [pl.BlockSpec((tm, n), lambda i: (i, 0)) for n, _ in row_outs]
    out_specs += [pl.BlockSpec(s, lambda i: (0, 0)) for s in acc_outs]
    out_shape = [jax.ShapeDtypeStruct((n_rows, n), dt) for n, dt in row_outs]
    out_shape += [jax.ShapeDtypeStruct(s, F32) for s in acc_outs]
    return pl.pallas_call(
        body, grid=(n_rows // tm,), in_specs=in_specs, out_specs=out_specs, out_shape=out_shape,
        name=name, compiler_params=_params(("arbitrary",)),
    )(*row_ins, *const_ins)


def _in_proj(x, g_pre, w_ext, tm):
    def body(x_ref, g_ref, w_ref, qkv_ref, cq_ref, ckv_ref, kr_ref, gl_ref, h_ref):
        n, _ = _rms(x_ref[...])
        hb = (n * g_ref[...]).astype(BF16)
        h_ref[...] = hb
        for c in range(0, 1536, 512):
            qkv_ref[:, c:c + 512] = _dot(hb, w_ref[:, c:c + 512]).astype(BF16)
        cq_ref[...] = _dot(hb, w_ref[:, EXT_CQ:EXT_CKV])
        ckv_ref[...] = _dot(hb, w_ref[:, EXT_CKV:EXT_KR])
        kr_ref[...] = _dot(hb, w_ref[:, EXT_KR:EXT_GL])
        for c in range(0, 2048, 512):
            gl_ref[:, c:c + 512] = _dot(hb, w_ref[:, EXT_GL + c:EXT_GL + c + 512])

    return _rows_call("in_proj", body, x.shape[0], tm, [x], [g_pre, w_ext],
                      [(1536, BF16), (384, F32), (256, F32), (128, F32), (2048, F32), (1024, BF16)], [])


def _mla_up(cq, ckv, kr, cr, sr, q_norm, kv_norm, wa, wk, wv, tm):
    def body(cq_ref, ckv_ref, kr_ref, cr_ref, sr_ref, qn_ref, kvn_ref, wa_ref, wk_ref, wv_ref,
             q_ref, k_ref, v_ref, cqn_ref, ckvn_ref):
        nq, _ = _rms(cq_ref[...])
        cqn = (nq * qn_ref[...]).astype(BF16)
        cqn_ref[...] = cqn
        nk, _ = _rms(ckv_ref[...])
        ckvn = (nk * kvn_ref[...]).astype(BF16)
        ckvn_ref[...] = ckvn
        cr = cr_ref[...]
        sr = sr_ref[...]
        lane = lax.broadcasted_iota(jnp.int32, cr.shape, 1)
        cm = cr + (lane < 64).astype(F32)
        kr = kr_ref[...]
        krp = kr * cr + _rot(kr) * sr
        for h in range(N_HEADS):
            hs = slice(h * LANES, (h + 1) * LANES)
            a = _dot(cqn, wa_ref[:, hs])
            q_ref[:, hs] = (a * cm + _rot(a) * sr).astype(BF16)
            k_ref[:, hs] = (_dot(ckvn, wk_ref[:, hs]) + krp).astype(BF16)
        v_ref[...] = _dot(ckvn, wv_ref[...]).astype(BF16)

    return _rows_call("mla_up", body, cq.shape[0], tm, [cq, ckv, kr, cr, sr], [q_norm, kv_norm, wa, wk, wv],
                      [(1024, BF16), (1024, BF16), (512, BF16), (384, BF16), (256, BF16)], [])


def _mix_out(o_sb, o_mla, gl, x, b_gate, g_post, w_sb, w_mla, w_out, tm):
    def body(osb_ref, omla_ref, gl_ref, x_ref, b_ref, gp_ref, wsb_ref, wmla_ref, wout_ref,
             x1_ref, y_ref, mb_ref, osbb_ref, omlab_ref):
        osb = osb_ref[...].astype(BF16)
        omla = omla_ref[...].astype(BF16)
        osbb_ref[...] = osb
        omlab_ref[...] = omla
        psb = _dot(osb, wsb_ref[...])
        pmla = _dot(omla, wmla_ref[...])
        gates = jax.nn.sigmoid(gl_ref[...] + b_ref[...])
        mb = (gates[:, :D_MODEL] * psb + gates[:, D_MODEL:] * pmla).astype(BF16)
        mb_ref[...] = mb
        y = _dot(mb, wout_ref[...])
        y_ref[...] = y
        n, _ = _rms(y)
        x1_ref[...] = x_ref[...] + n * gp_ref[...]

    return _rows_call("mix_out", body, x.shape[0], tm, [o_sb, o_mla, gl, x], [b_gate, g_post, w_sb, w_mla, w_out],
                      [(1024, F32), (1024, F32), (1024, BF16), (512, BF16), (512, BF16)], [])


FF_CHUNK = 1408


def _ffn_fwd(x1, target, g_pre, g_post, w_gu, w_down, tm):
    def body(x1_ref, t_ref, gpre_ref, gpost_ref, wgu_ref, wd_ref, dx2_ref, f_ref, h2_ref, loss_ref):
        x1 = x1_ref[...]
        n, _ = _rms(x1)
        h2 = (n * gpre_ref[...]).astype(BF16)
        h2_ref[...] = h2
        f = jnp.zeros((tm, D_MODEL), F32)
        for c in range(0, D_FF, FF_CHUNK):
            g = _dot(h2, wgu_ref[:, c:c + FF_CHUNK])
            u = _dot(h2, wgu_ref[:, D_FF + c:D_FF + c + FF_CHUNK])
            act = (g * jax.nn.sigmoid(g) * u).astype(BF16)
            f = f + _dot(act, wd_ref[c:c + FF_CHUNK, :])
        f_ref[...] = f
        nf, _ = _rms(f)
        err = x1 + nf * gpost_ref[...] - t_ref[...]
        dx2_ref[...] = err * (1.0 / D_MODEL)
        e8 = _colsum8(err * err)
        part = e8[:, 0:LANES]
        for c in range(LANES, D_MODEL, LANES):
            part = part + e8[:, c:c + LANES]

        @pl.when(pl.program_id(0) == 0)
        def _():
            loss_ref[...] = jnp.zeros_like(loss_ref)

        loss_ref[...] += part

    return _rows_call("ffn_fwd", body, x1.shape[0], tm, [x1, target], [g_pre, g_post, w_gu, w_down],
                      [(1024, F32), (1024, F32), (1024, BF16)], [(8, LANES)])


def _ffn_bwd(dx2, f, x1, g_pre, g_post, w_gu, w_down, tm):
    def body(dx2_ref, f_ref, x1_ref, gpre_ref, gpost_ref, wgu_ref, wd_ref,
             dx1_ref, act_ref, dgu_ref, dfb_ref, dgpre_ref, dgpost_ref):
        @pl.when(pl.program_id(0) == 0)
        def _():
            dgpre_ref[...] = jnp.zeros_like(dgpre_ref)
            dgpost_ref[...] = jnp.zeros_like(dgpost_ref)

        dx2 = dx2_ref[...]
        nf, rf = _rms(f_ref[...])
        dgpost_ref[...] += _colsum8(dx2 * nf)
        dfb = _rms_bwd(dx2 * gpost_ref[...], nf, rf).astype(BF16)
        dfb_ref[...] = dfb
        x1 = x1_ref[...]
        n1, r1 = _rms(x1)
        h2 = (n1 * gpre_ref[...]).astype(BF16)
        dh2 = jnp.zeros((tm, D_MODEL), F32)
        for c in range(0, D_FF, FF_CHUNK):
            cs, us = slice(c, c + FF_CHUNK), slice(D_FF + c, D_FF + c + FF_CHUNK)
            g = _dot(h2, wgu_ref[:, cs])
            u = _dot(h2, wgu_ref[:, us])
            sg = jax.nn.sigmoid(g)
            si = g * sg
            act_ref[:, cs] = (si * u).astype(BF16)
            dact = _dot_nt(dfb, wd_ref[cs, :])
            dg = (dact * u * (sg * (1.0 + g * (1.0 - sg)))).astype(BF16)
            du = (dact * si).astype(BF16)
            dgu_ref[:, cs] = dg
            dgu_ref[:, us] = du
            dh2 = dh2 + _dot_nt(dg, wgu_ref[:, cs]) + _dot_nt(du, wgu_ref[:, us])
        dgpre_ref[...] += _colsum8(dh2 * n1)
        dx1_ref[...] = dx2 + _rms_bwd(dh2 * gpre_ref[...], n1, r1)

    return _rows_call("ffn_bwd", body, dx2.shape[0], tm, [dx2, f, x1], [g_pre, g_post, w_gu, w_down],
                      [(1024, F32), (D_FF, BF16), (2 * D_FF, BF16), (1024, BF16)], [(8, 1024), (8, 1024)])


def _mix_bwd(dx1, y, o_sb, o_mla, gl, b_gate, g_post, w_sb, w_mla, w_out, tm):
    def body(dx1_ref, y_ref, osb_ref, omla_ref, gl_ref, b_ref, gp_ref, wsb_ref, wmla_ref, wout_ref,
             dyb_ref, dpsb_ref, dpmla_ref, dgl_ref, dosb_ref, domla_ref, dgpost_ref, dbg_ref):
        @pl.when(pl.program_id(0) == 0)
        def _():
            dgpost_ref[...] = jnp.zeros_like(dgpost_ref)
            dbg_ref[...] = jnp.zeros_like(dbg_ref)

        dx1 = dx1_ref[...]
        ny, ry = _rms(y_ref[...])
        dgpost_ref[...] += _colsum8(dx1 * ny)
        dyb = _rms_bwd(dx1 * gp_ref[...], ny, ry).astype(BF16)
        dyb_ref[...] = dyb
        dm = _dot_nt(dyb, wout_ref[...])
        psb = _dot(osb_ref[...].astype(BF16), wsb_ref[...])
        pmla = _dot(omla_ref[...].astype(BF16), wmla_ref[...])
        gates = jax.nn.sigmoid(gl_ref[...] + b_ref[...])
        g0, g1 = gates[:, :D_MODEL], gates[:, D_MODEL:]
        dpsb = (dm * g0).astype(BF16)
        dpmla = (dm * g1).astype(BF16)
        dpsb_ref[...] = dpsb
        dpmla_ref[...] = dpmla
        dgl0 = dm * psb * g0 * (1.0 - g0)
        dgl1 = dm * pmla * g1 * (1.0 - g1)
        dgl_ref[:, :D_MODEL] = dgl0.astype(BF16)
        dgl_ref[:, D_MODEL:] = dgl1.astype(BF16)
        dbg_ref[:, :D_MODEL] += _colsum8(dgl0)
        dbg_ref[:, D_MODEL:] += _colsum8(dgl1)
        dosb_ref[...] = _dot_nt(dpsb, wsb_ref[...]).astype(BF16)
        domla_ref[...] = _dot_nt(dpmla, wmla_ref[...]).astype(BF16)

    return _rows_call("mix_bwd", body, dx1.shape[0], tm, [dx1, y, o_sb, o_mla, gl],
                      [b_gate, g_post, w_sb, w_mla, w_out],
                      [(1024, BF16), (1024, BF16), (1024, BF16), (2048, BF16), (512, BF16), (512, BF16)],
                      [(8, 1024), (8, 2048)])


def _mla_up_bwd(dq, dk, dv, cq, ckv, cr, sr, q_norm, kv_norm, wa, wk, wv, tm):
    def body(dq_ref, dk_ref, dv_ref, cq_ref, ckv_ref, cr_ref, sr_ref, qn_ref, kvn_ref, wa_ref, wk_ref, wv_ref,
             da_ref, dkb_ref, dvb_ref, dlat_ref, dqn_ref, dkvn_ref):
        @pl.when(pl.program_id(0) == 0)
        def _():
            dqn_ref[...] = jnp.zeros_like(dqn_ref)
            dkvn_ref[...] = jnp.zeros_like(dkvn_ref)

        cr = cr_ref[...]
        sr = sr_ref[...]
        lane = lax.broadcasted_iota(jnp.int32, cr.shape, 1)
        cm = cr + (lane < 64).astype(F32)
        nq, rq = _rms(cq_ref[...])
        nk, rk = _rms(ckv_ref[...])
        dcqn = jnp.zeros((tm, MLA_Q_RANK), F32)
        dckvn = jnp.zeros((tm, MLA_KV_RANK), F32)
        dkrp = jnp.zeros((tm, LANES), F32)
        for h in range(N_HEADS):
            hs = slice(h * LANES, (h + 1) * LANES)
            dqh = dq_ref[:, hs]
            da = (dqh * cm - _rot(dqh * sr)).astype(BF16)
            da_ref[:, hs] = da
            dcqn = dcqn + _dot_nt(da, wa_ref[:, hs])
            dkh = dk_ref[:, hs]
            dkb = dkh.astype(BF16)
            dkb_ref[:, hs] = dkb
            dckvn = dckvn + _dot_nt(dkb, wk_ref[:, hs])
            dkrp = dkrp + dkh
        dvb = dv_ref[...].astype(BF16)
        dvb_ref[...] = dvb
        dckvn = dckvn + _dot_nt(dvb, wv_ref[...])
        dkr = dkrp * cr - _rot(dkrp * sr)
        dqn_ref[...] += _colsum8(dcqn * nq)
        dkvn_ref[...] += _colsum8(dckvn * nk)
        dlat_ref[:, 0:384] = _rms_bwd(dcqn * qn_ref[...], nq, rq).astype(BF16)
        dlat_ref[:, 384:640] = _rms_bwd(dckvn * kvn_ref[...], nk, rk).astype(BF16)
        dlat_ref[:, 640:768] = dkr.astype(BF16)

    return _rows_call("mla_up_bwd", body, dq.shape[0], tm, [dq, dk, dv, cq, ckv, cr, sr],
                      [q_norm, kv_norm, wa, wk, wv],
                      [(1024, BF16), (1024, BF16), (512, BF16), (768, BF16)], [(8, 384), (8, 256)])


def _in_proj_bwd(x, dx1, dq_sb, dk_sb, dv_sb, dlat, dgl, g_pre, w_ext, tm):
    def body(x_ref, dx1_ref, dq_ref, dk_ref, dv_ref, dlat_ref, dgl_ref, g_ref, w_ref,
             dx_ref, dproj_ref, dg_ref):
        @pl.when(pl.program_id(0) == 0)
        def _():
            dg_ref[...] = jnp.zeros_like(dg_ref)

        dproj_ref[:, 0:512] = dq_ref[...].astype(BF16)
        dproj_ref[:, 512:1024] = dk_ref[...].astype(BF16)
        dproj_ref[:, 1024:1536] = dv_ref[...].astype(BF16)
        dproj_ref[:, EXT_CQ:EXT_GL] = dlat_ref[...]
        dproj_ref[:, EXT_GL:EXT_N] = dgl_ref[...]
        dh = jnp.zeros((tm, D_MODEL), F32)
        for c in range(0, EXT_N, 2176):
            dh = dh + _dot_nt(dproj_ref[:, c:c + 2176], w_ref[:, c:c + 2176])
        n, r = _rms(x_ref[...])
        dg_ref[...] += _colsum8(dh * n)
        dx_ref[...] = dx1_ref[...] + _rms_bwd(dh * g_ref[...], n, r)

    return _rows_call("in_proj_bwd", body, x.shape[0], tm, [x, dx1, dq_sb, dk_sb, dv_sb, dlat, dgl],
                      [g_pre, w_ext], [(1024, F32), (EXT_N, BF16)], [(8, 1024)])


def _head_masked(x):
    lane = lax.broadcasted_iota(jnp.int32, x.shape, 1)
    zero = jnp.zeros_like(x)
    return jnp.where(lane < 64, x, zero), jnp.where(lane >= 64, x, zero)


def _cum_weights():
    row = lax.broadcasted_iota(jnp.int32, (TK, TK), 0)
    col = lax.broadcasted_iota(jnp.int32, (TK, TK), 1)
    half = jnp.concatenate([(row > col).astype(BF16), jnp.ones((TK, TK), BF16)], axis=1)
    return jnp.concatenate([half, half], axis=0)


def _split_cat(x):
    hi, lo = _split(x)
    return jnp.concatenate([hi, lo], axis=1)


def _sweep(i, tiles, unroll, skip_rows=False):
    if skip_rows:
        for t in range(DIAG_TILES // unroll):
            top = DIAG_TILES - 1 - t * unroll
            r0 = (top - unroll + 1) * TK
            col = lax.broadcasted_iota(jnp.int32, (TQ - r0, TK), 1)
            tiles(i * DIAG_TILES + top, [(top - u) * TK - r0 + col for u in range(unroll)], r0)
    else:
        col = lax.broadcasted_iota(jnp.int32, (TQ, TK), 1)

        def diag(t, _):
            top = DIAG_TILES - 1 - t * unroll
            tiles(i * DIAG_TILES + top, [(top - u) * TK + col for u in range(unroll)])
            return 0

        lax.fori_loop(0, DIAG_TILES // unroll, diag, 0)

    def full(g, _):
        tiles(i * DIAG_TILES - 1 - g * unroll, [None] * unroll)
        return 0

    lax.fori_loop(0, (i * DIAG_TILES) // unroll, full, 0)


def _causal(key, strict):
    if key is None:
        return None
    row = lax.broadcasted_iota(jnp.int32, key.shape, 0)
    return key < row if strict else key <= row


def _sb_logs(z, valid, w_cum):
    soft = jnp.log(1.0 + jnp.exp(-jnp.abs(z)))
    lsm = -jnp.maximum(z, 0.0) - soft
    cat = _split_cat(lsm if valid is None else jnp.where(valid, lsm, 0.0))
    return z + lsm, _dot(cat, w_cum)


def _sb_weight(log_beta, cs, r, valid):
    a = jnp.exp(log_beta + cs[:, :TK] + r)
    if valid is not None:
        a = jnp.where(valid, a, 0.0)
    return a, r + cs[:, TK:]


def _block_diag(x):
    x0, x1 = x[:, :LANES], x[:, LANES:]
    zero = jnp.zeros_like(x0)
    return jnp.concatenate([jnp.concatenate([x0, zero], axis=1), jnp.concatenate([zero, x1], axis=1)], axis=0)


def _attn_fwd(qkv, q, k, v):
    s_len = qkv.shape[0]

    def body(qs_ref, ks_ref, vs_ref, qm_ref, km_ref, vm_ref, osb_ref, omla_ref, lse_ref,
             sacc_ref, r_ref, macc_ref, m_ref):
        i = pl.program_id(1)
        lane = lax.broadcasted_iota(jnp.int32, (TQ, LANES), 1)
        r2 = lax.broadcasted_iota(jnp.int32, (2 * TK, LANES), 0)
        c2 = lax.broadcasted_iota(jnp.int32, (2 * TK, LANES), 1)
        head_ones = ((r2 < TK) == (c2 < 64)).astype(BF16)
        w_cum = _cum_weights()
        qs = qs_ref[...] * SB_SCALE
        qm = qm_ref[...]
        sacc_ref[...] = jnp.zeros_like(sacc_ref)
        r_ref[...] = jnp.zeros_like(r_ref)
        macc_ref[...] = jnp.zeros_like(macc_ref)
        m_ref[...] = jnp.full(m_ref.shape, NEG_BIG, F32)

        def sb_scores(top, valids):
            out = []
            for u, valid in enumerate(valids):
                off = pl.multiple_of((top - u) * TK, TK)
                z01 = _dot_nt(qs, jnp.concatenate(_head_masked(ks_ref[pl.ds(off, TK), :]), axis=0))
                out.append([_sb_logs(z, valid, w_cum) for z in (z01[:, :TK], z01[:, TK:])])
            return out

        def sb_accumulate(top, valids, scores):
            parts = [[None, None] for _ in valids]
            for hh in range(2):
                r = r_ref[hh]
                for u, valid in enumerate(valids):
                    a, r = _sb_weight(*scores[u][hh], r, valid)
                    parts[u][hh] = _split_cat(a)
                r_ref[hh] = r
            vs = []
            for u in range(len(valids)):
                off = pl.multiple_of((top - u) * TK, TK)
                v0, v1 = _head_masked(vs_ref[pl.ds(off, TK), :])
                vs += [v0, v0, v1, v1]
            sacc_ref[...] += _dot(jnp.concatenate([p for pair in parts for p in pair], axis=1),
                                  jnp.concatenate(vs, axis=0))

        def mla_scores(top, valids):
            s01s = []
            for u in range(len(valids)):
                off = pl.multiple_of((top - u) * TK, TK)
                s01s.append(_dot_nt(qm, _block_diag(km_ref[pl.ds(off, TK), :])))
            heads = []
            for hh in range(2):
                ss = []
                for u, valid in enumerate(valids):
                    s = s01s[u][:, hh * TK:(hh + 1) * TK] * MLA_SCALE
                    ss.append(s if valid is None else jnp.where(valid, s, NEG_BIG))
                m_old = m_ref[hh]
                m = jnp.maximum(m_old, jnp.max(functools.reduce(jnp.maximum, ss), axis=1, keepdims=True))
                m_ref[hh] = m
                heads.append((ss, m, jnp.exp(m_old - m)))
            return heads

        def mla_accumulate(top, heads):
            vs = []
            for u in range(len(heads[0][0])):
                off = pl.multiple_of((top - u) * TK, TK)
                vv = jnp.concatenate(_head_masked(vm_ref[pl.ds(off, TK), :]), axis=0)
                vs.append(jnp.concatenate([vv, head_ones], axis=1))
            ps = [[jnp.exp(s - m).astype(BF16) for s in ss] for ss, m, _ in heads]
            scale = jnp.where(lane < 64, heads[0][2], heads[1][2])
            p_all = jnp.concatenate([ps[hh][u] for u in range(len(vs)) for hh in range(2)], axis=1)
            macc_ref[...] = (macc_ref[...] * jnp.concatenate([scale, scale], axis=1)
                             + _dot(p_all, jnp.concatenate(vs, axis=0)))

        def tiles(top, keys):
            strict = [_causal(key, True) for key in keys]
            heads = mla_scores(top, [_causal(key, False) for key in keys])
            scores = sb_scores(top, strict)
            mla_accumulate(top, heads)
            sb_accumulate(top, strict, scores)

        _sweep(i, tiles, FWD_UNROLL)
        osb_ref[...] = sacc_ref[...]
        acc = macc_ref[...]
        den = acc[:, LANES:]
        omla_ref[...] = acc[:, :LANES] / den
        for hh, mask in enumerate((lane < 64, lane >= 64)):
            l = jnp.max(jnp.where(mask, den, 0.0), axis=1, keepdims=True)
            lse_ref[hh] = jnp.broadcast_to(m_ref[hh] + jnp.log(l), (TQ, LANES))

    tile = pl.BlockSpec((TQ, LANES), lambda h, i: (i, h))
    return pl.pallas_call(
        body, grid=(4, s_len // TQ),
        in_specs=[tile,
                  pl.BlockSpec((s_len, LANES), lambda h, i: (0, 4 + h)),
                  pl.BlockSpec((s_len, LANES), lambda h, i: (0, 8 + h)),
                  pl.BlockSpec((TQ, 2 * LANES), lambda h, i: (i, h)),
                  pl.BlockSpec((s_len, 2 * LANES), lambda h, i: (0, h)),
                  pl.BlockSpec((s_len, LANES), lambda h, i: (0, h))],
        out_specs=[tile, tile, pl.BlockSpec((2, TQ, LANES), lambda h, i: (h, i, 0))],
        out_shape=[jax.ShapeDtypeStruct((s_len, SB_WIDTH), F32), jax.ShapeDtypeStruct((s_len, 512), F32),
                   jax.ShapeDtypeStruct((N_HEADS, s_len, LANES), F32)],
        scratch_shapes=[pltpu.VMEM((TQ, LANES), F32), pltpu.VMEM((2, TQ, LANES), F32),
                        pltpu.VMEM((TQ, 2 * LANES), F32), pltpu.VMEM((2, TQ, 1), F32)],
        name="attn_fwd", compiler_params=_params(("arbitrary", "arbitrary")),
    )(qkv, qkv, qkv, q, k, v)


def _row_dots(do, o):
    prod = do.astype(F32) * o
    p0, p1 = _head_masked(prod)
    return tuple(jnp.broadcast_to(jnp.sum(p, axis=1, keepdims=True), prod.shape) for p in (p0, p1))


def _attn_bwd(qkv, do_sb, o_sb, q, k, v, do_mla, o_mla, lse, after):
    s_len = qkv.shape[0]
    n_q = s_len // TQ

    def body(qs_ref, ks_ref, vs_ref, dos_ref, os_ref, qm_ref, km_ref, vm_ref, dom_ref, om_ref, lse_ref, after_ref,
             dqs_ref, dqm_ref, dks_hbm, dvs_hbm, dkm_hbm, dvm_hbm,
             dqs_acc, dqm_acc, dks_acc, dvs_acc, dkm_acc, dvm_acc, r_ref, g_ref, ds_ref, dm_ref, out_sems):
        h = pl.program_id(0)
        i = pl.program_id(1)

        @pl.when(i == 0)
        def _():
            for acc in (dks_acc, dvs_acc, dkm_acc, dvm_acc):
                acc[...] = jnp.zeros_like(acc)

        w_cum = _cum_weights()
        qs = qs_ref[...] * SB_SCALE
        dos = dos_ref[...]
        ds_ref[0], ds_ref[1] = _row_dots(dos, os_ref[...])
        qm = qm_ref[...]
        dom = dom_ref[...]
        dm_ref[0], dm_ref[1] = _row_dots(dom, om_ref[...])
        qs_heads, dos_heads, dom_heads = _head_masked(qs), _head_masked(dos), _head_masked(dom)
        by_head = lambda pair, rows: jnp.concatenate([x[rows] for x in pair], axis=0)
        all_rows = slice(0, TQ)
        qs_rows, dos_rows, dom_rows = (by_head(p, all_rows) for p in (qs_heads, dos_heads, dom_heads))
        qm_diag = _block_diag(qm)
        for ref in (dqs_acc, dqm_acc, r_ref, g_ref):
            ref[...] = jnp.zeros_like(ref)
        heads = (slice(0, TK), slice(TK, 2 * TK))

        def weigh(log_beta, cs, r, da, valid):
            a, r = _sb_weight(log_beta, cs, r, valid)
            g = a * da
            return a.astype(BF16), g, _split_cat(g), r

        def logit_grad(g, gs, carried, d, log_beta, valid):
            upto = d - (gs[:, :TK] + carried)
            dz = g - jnp.exp(log_beta) * upto
            if valid is not None:
                dz = jnp.where(valid, dz, 0.0)
            return dz.astype(BF16), carried + gs[:, TK:]

        def tiles(top, keys, r0=0):
            n = len(keys)
            rows = slice(r0, TQ)
            strict = [_causal(key, True) for key in keys]
            loose = [_causal(key, False) for key in keys]
            offs = [pl.multiple_of((top - u) * TK, TK) for u in range(n)]
            kds = [_block_diag(km_ref[pl.ds(off, TK), :]) for off in offs]
            vms = [jnp.concatenate(_head_masked(vm_ref[pl.ds(off, TK), :]), axis=0) for off in offs]
            s01s = [_dot_nt(qm[rows], kd) for kd in kds]
            dp01s = [_dot_nt(dom[rows], vv) for vv in vms]
            kks = [jnp.concatenate(_head_masked(ks_ref[pl.ds(off, TK), :]), axis=0) for off in offs]
            vvs = [jnp.concatenate(_head_masked(vs_ref[pl.ds(off, TK), :]), axis=0) for off in offs]
            z01s = [_dot_nt(qs[rows], kk) for kk in kks]
            da01s = [_dot_nt(dos[rows], vv) for vv in vvs]
            logs = [[_sb_logs(z01s[u][:, hs], strict[u], w_cum) for hs in heads] for u in range(n)]

            dss = [[None, None] for _ in range(n)]
            ps = [[None, None] for _ in range(n)]
            for u in range(n):
                for hh, hs in enumerate(heads):
                    p = jnp.exp(s01s[u][:, hs] * MLA_SCALE - lse_ref[hh, rows, :])
                    if loose[u] is not None:
                        p = jnp.where(loose[u], p, 0.0)
                    dss[u][hh] = (p * (dp01s[u][:, hs] - dm_ref[hh, rows, :]) * MLA_SCALE).astype(BF16)
                    ps[u][hh] = p.astype(BF16)

            dzs = [[None, None] for _ in range(n)]
            avs = [[None, None] for _ in range(n)]
            sums = [[None, None] for _ in range(n)]
            for hh, hs in enumerate(heads):
                r = r_ref[hh, rows, :]
                for u in range(n):
                    avs[u][hh], g, cat, r = weigh(*logs[u][hh], r, da01s[u][:, hs], strict[u])
                    sums[u][hh] = (g, _dot(cat, w_cum))
                r_ref[hh, rows, :] = r

            dqm_acc[rows, :] += _dot(jnp.concatenate([d for pair in dss for d in pair], axis=1),
                                     jnp.concatenate(kds, axis=0))
            span = pl.ds(offs[-1], n * TK)
            by_key = lambda pairs: jnp.concatenate([jnp.concatenate(pair, axis=0) for pair in pairs[::-1]], axis=1)
            whole = r0 == 0
            dkm_acc[span, :] += _dot_tn(by_key(dss), qm_diag if whole else _block_diag(qm[rows]))
            dvm_acc[span, :] += _dot_tn(by_key(ps), dom_rows if whole else by_head(dom_heads, rows))

            for hh in range(2):
                carried = g_ref[hh, rows, :]
                for u in range(n):
                    dzs[u][hh], carried = logit_grad(*sums[u][hh], carried, ds_ref[hh, rows, :], logs[u][hh][0],
                                                     strict[u])
                g_ref[hh, rows, :] = carried
            dqs_acc[rows, :] += _dot(jnp.concatenate([dz for pair in dzs for dz in pair], axis=1),
                                     jnp.concatenate(kks, axis=0))
            dks_acc[span, :] += _dot_tn(by_key(dzs), qs_rows if whole else by_head(qs_heads, rows))
            dvs_acc[span, :] += _dot_tn(by_key(avs), dos_rows if whole else by_head(dos_heads, rows))

        _sweep(i, tiles, BWD_UNROLL, skip_rows=True)
        dqs_ref[...] = dqs_acc[...] * SB_SCALE
        dqm_ref[...] = dqm_acc[...]

        @pl.when(i == n_q - 1)
        def _():
            narrow = pl.ds(pl.multiple_of(h * LANES, LANES), LANES)
            wide = pl.ds(pl.multiple_of(h * 2 * LANES, 2 * LANES), 2 * LANES)
            copies = [pltpu.make_async_copy(dks_acc, dks_hbm.at[:, narrow], out_sems.at[0]),
                      pltpu.make_async_copy(dvs_acc, dvs_hbm.at[:, narrow], out_sems.at[1]),
                      pltpu.make_async_copy(dkm_acc, dkm_hbm.at[:, wide], out_sems.at[2]),
                      pltpu.make_async_copy(dvm_acc, dvm_hbm.at[:, narrow], out_sems.at[3])]
            for cp in copies:
                cp.start()
            for cp in copies:
                cp.wait()

    tile = pl.BlockSpec((TQ, LANES), lambda h, i: (i, h))
    wide_tile = pl.BlockSpec((TQ, 2 * LANES), lambda h, i: (i, h))
    once = pl.Buffered(1)
    hbm = pl.BlockSpec(memory_space=pl.ANY)
    return pl.pallas_call(
        body, grid=(4, n_q),
        in_specs=[tile,
                  pl.BlockSpec((s_len, LANES), lambda h, i: (0, 4 + h), pipeline_mode=once),
                  pl.BlockSpec((s_len, LANES), lambda h, i: (0, 8 + h), pipeline_mode=once),
                  tile, tile, wide_tile,
                  pl.BlockSpec((s_len, 2 * LANES), lambda h, i: (0, h), pipeline_mode=once),
                  pl.BlockSpec((s_len, LANES), lambda h, i: (0, h), pipeline_mode=once),
                  tile, tile, pl.BlockSpec((2, TQ, LANES), lambda h, i: (h, i, 0)),
                  pl.BlockSpec((8, LANES), lambda h, i: (0, 0))],
        out_specs=[tile, wide_tile, hbm, hbm, hbm, hbm],
        out_shape=[jax.ShapeDtypeStruct((s_len, SB_WIDTH), F32), jax.ShapeDtypeStruct((s_len, 1024), F32),
                   jax.ShapeDtypeStruct((s_len, SB_WIDTH), F32), jax.ShapeDtypeStruct((s_len, SB_WIDTH), F32),
                   jax.ShapeDtypeStruct((s_len, 1024), F32), jax.ShapeDtypeStruct((s_len, 512), F32)],
        scratch_shapes=[pltpu.VMEM((TQ, LANES), F32), pltpu.VMEM((TQ, 2 * LANES), F32),
                        pltpu.VMEM((s_len, LANES), F32), pltpu.VMEM((s_len, LANES), F32),
                        pltpu.VMEM((s_len, 2 * LANES), F32), pltpu.VMEM((s_len, LANES), F32),
                        pltpu.VMEM((2, TQ, LANES), F32), pltpu.VMEM((2, TQ, LANES), F32),
                        pltpu.VMEM((2, TQ, LANES), F32), pltpu.VMEM((2, TQ, LANES), F32),
                        pltpu.SemaphoreType.DMA((4,))],
        name="attn_bwd", compiler_params=_params(("arbitrary", "arbitrary")),
    )(qkv, qkv, qkv, do_sb, o_sb, q, k, v, do_mla, o_mla, lse, after)


def _pick(n, options):
    for t in options:
        if n % t == 0:
            return t
    raise ValueError(n)


def _matmul_tn(name, a, b):
    s_len, m = a.shape
    n = b.shape[1]
    tm = _pick(m, (1024, 1408, 2176, 512))
    tn = _pick(n, (1024, 512, 384, 256))
    tk = _pick(s_len, (1024, 512, 256, 128))
    n_k = s_len // tk

    def body(a_ref, b_ref, o_ref, acc_ref):
        @pl.when(pl.program_id(2) == 0)
        def _():
            acc_ref[...] = jnp.zeros_like(acc_ref)

        acc_ref[...] += _dot_tn(a_ref[...], b_ref[...])

        @pl.when(pl.program_id(2) == n_k - 1)
        def _():
            o_ref[...] = acc_ref[...].astype(BF16)

    return pl.pallas_call(
        body, grid=(m // tm, n // tn, n_k),
        in_specs=[pl.BlockSpec((tk, tm), lambda i, j, l: (l, i)), pl.BlockSpec((tk, tn), lambda i, j, l: (l, j))],
        out_specs=pl.BlockSpec((tm, tn), lambda i, j, l: (i, j)),
        out_shape=jax.ShapeDtypeStruct((m, n), BF16),
        scratch_shapes=[pltpu.VMEM((tm, tn), F32)],
        name=name, compiler_params=_params(("arbitrary", "arbitrary", "arbitrary")),
    )(a, b)


def _mesh_pos():
    return lax.axis_index("x"), lax.axis_index("y"), lax.axis_index("c")


def _peer(pos, k):
    x, y, c = pos
    return (1 - x if k & 4 else x, 1 - y if k & 2 else y, 1 - c if k & 1 else c)


def _flat_index(pos):
    return 4 * pos[0] + 2 * pos[1] + pos[2]


def _all_gather(shard):
    rows = shard.shape[0]

    def body(x_ref, out_ref, send_sems, recv_sems, local_sem):
        me = _mesh_pos()
        x, y, c = me
        sibling = (x, y, 1 - c)
        chips = [(1 - x, y), (x, 1 - y), (1 - x, 1 - y)]

        def copy(k, block, to, src=None):
            slot = out_ref.at[_flat_index(block)]
            return pltpu.make_async_remote_copy(
                src_ref=slot if src is None else src, dst_ref=slot,
                send_sem=send_sems.at[k], recv_sem=recv_sems.at[k],
                device_id=to, device_id_type=pl.DeviceIdType.MESH)

        mine = pltpu.make_async_copy(x_ref, out_ref.at[_flat_index(me)], local_sem)
        mine.start()
        first = [copy(0, me, sibling, src=x_ref)]
        first += [copy(1 + j, me, (*chip, c), src=x_ref) for j, chip in enumerate(chips)]
        for cp in first:
            cp.start()
        passed = [copy(4 + j, (*chip, c), sibling) for j, chip in enumerate(chips)]
        for j, chip in enumerate(chips):
            copy(1 + j, (*chip, c), me).wait_recv()
            passed[j].start()
        copy(0, sibling, me).wait_recv()
        for j, chip in enumerate(chips):
            copy(4 + j, (*chip, 1 - c), me).wait_recv()
        for cp in first + passed:
            cp.wait_send()
        mine.wait()

    return pl.pallas_call(
        body, out_shape=jax.ShapeDtypeStruct((N_DEV, rows, LANES), shard.dtype),
        in_specs=[pl.BlockSpec(memory_space=pl.ANY)], out_specs=pl.BlockSpec(memory_space=pl.ANY),
        scratch_shapes=[pltpu.SemaphoreType.DMA((7,)), pltpu.SemaphoreType.DMA((7,)), pltpu.SemaphoreType.DMA],
        name="weights_all_gather",
    )(shard)


def _push_start(name, srcs, per_peer):
    n = len(srcs)
    lands = [lax.empty((N_DEV,) + src.shape[-2:], src.dtype) for src in srcs]

    def body(*refs):
        src_refs, land_refs = refs[:n], refs[n:2 * n]
        send_sems, recv_sems = refs[2 * n], refs[2 * n + 1]
        token = refs[-1]
        me = _mesh_pos()
        mine = _flat_index(me)
        for j in range(n):
            for k in range(1, N_DEV):
                peer = _peer(me, k)
                pltpu.make_async_remote_copy(
                    src_ref=src_refs[j].at[_flat_index(peer)] if per_peer[j] else src_refs[j],
                    dst_ref=land_refs[j].at[mine], send_sem=send_sems.at[7 * j + k - 1],
                    recv_sem=recv_sems.at[7 * j + k - 1],
                    device_id=peer, device_id_type=pl.DeviceIdType.MESH).start()
        token[...] = jnp.zeros_like(token)

    hbm = pl.BlockSpec(memory_space=pltpu.HBM)
    sem = pl.BlockSpec(memory_space=pltpu.SEMAPHORE)
    sems = pltpu.SemaphoreType.DMA((n * (N_DEV - 1),))
    out = pl.pallas_call(
        body, name=name,
        out_shape=(sems, sems) + tuple(pltpu.HBM(x.shape, x.dtype) for x in srcs + lands)
                  + (jax.ShapeDtypeStruct((8, LANES), F32),),
        in_specs=(hbm,) * (2 * n), out_specs=(sem, sem) + (hbm,) * (2 * n) + (pl.BlockSpec(memory_space=pltpu.VMEM),),
        input_output_aliases={j: 2 + j for j in range(2 * n)},
        compiler_params=pltpu.CompilerParams(has_side_effects=pltpu.SideEffectType.DATAFLOW_SIDE_EFFECTING),
    )(*[pltpu.with_memory_space_constraint(x, pltpu.HBM) for x in srcs + lands])
    return out[0], out[1], list(out[2:2 + n]), list(out[2 + n:2 + 2 * n]), out[-1]


def _push_wait(name, started, per_peer, after):
    send_sems, recv_sems, srcs, lands, _ = started
    n = len(srcs)

    def body(*refs):
        src_refs, land_refs = refs[:n], refs[n:2 * n]
        send_sems, recv_sems = refs[2 * n], refs[2 * n + 1]
        me = _mesh_pos()
        for j in range(n):
            for k in range(1, N_DEV):
                theirs = _flat_index(_peer(me, k))
                copy = pltpu.make_async_remote_copy(
                    src_ref=src_refs[j].at[theirs] if per_peer[j] else src_refs[j], dst_ref=land_refs[j].at[theirs],
                    send_sem=send_sems.at[7 * j + k - 1], recv_sem=recv_sems.at[7 * j + k - 1],
                    device_id=me, device_id_type=pl.DeviceIdType.MESH)
                copy.wait_send()
                copy.wait_recv()

    hbm = pl.BlockSpec(memory_space=pltpu.HBM)
    sem = pl.BlockSpec(memory_space=pltpu.SEMAPHORE)
    out = pl.pallas_call(
        body, name=name,
        out_shape=tuple(pltpu.HBM(x.shape, x.dtype) for x in srcs + lands),
        in_specs=(hbm,) * (2 * n) + (sem, sem, pl.BlockSpec(memory_space=pl.ANY)), out_specs=(hbm,) * (2 * n),
        input_output_aliases={j: j for j in range(2 * n)},
        compiler_params=pltpu.CompilerParams(has_side_effects=pltpu.SideEffectType.DATAFLOW_SIDE_EFFECTING),
    )(*srcs, *lands, send_sems, recv_sems, after)
    return list(out[:n]), list(out[n:])


def _slot_sum(name, slots, tr):
    rows, cols = slots.shape[1:]

    def body(s_ref, o_ref):
        acc = s_ref[0].astype(F32)
        for d in range(1, N_DEV):
            acc = acc + s_ref[d].astype(F32)
        o_ref[...] = acc

    return pl.pallas_call(
        body, grid=(rows // tr,),
        in_specs=[pl.BlockSpec((N_DEV, tr, cols), lambda i: (0, i, 0))],
        out_specs=pl.BlockSpec((tr, cols), lambda i: (i, 0)),
        out_shape=jax.ShapeDtypeStruct((rows, cols), F32),
        name=name, compiler_params=_params(("arbitrary",)),
    )(slots)


def _adamw(name, w, g, m, v):
    rows, cols = w.shape
    tr = _pick(rows, (256, 128, 88, 32, 1))
    c1 = 1.0 - ADAM_B1 ** ADAM_STEP
    c2 = 1.0 - ADAM_B2 ** ADAM_STEP

    def body(w_ref, g_ref, m_ref, v_ref, d_ref, nm_ref, nv_ref):
        g = g_ref[...]
        nm = ADAM_B1 * m_ref[...] + (1.0 - ADAM_B1) * g
        nv = ADAM_B2 * v_ref[...] + (1.0 - ADAM_B2) * (g * g)
        nm_ref[...] = nm
        nv_ref[...] = nv
        d_ref[...] = -ADAM_LR * ((nm / c1) / (jnp.sqrt(nv / c2) + ADAM_EPS) + ADAM_WD * w_ref[...])

    spec = pl.BlockSpec((tr, cols), lambda i: (i, 0))
    return pl.pallas_call(
        body, grid=(rows // tr,), in_specs=[spec] * 4, out_specs=[spec] * 3,
        out_shape=[jax.ShapeDtypeStruct((rows, cols), F32)] * 3,
        name=name, compiler_params=_params(("arbitrary",)),
    )(w, g, m, v)


def _pack_shards(shards, group, rows):
    parts = []
    for name, _, axis in group:
        w = shards[name].astype(BF16)
        parts.append((w.T if axis == 1 else w).reshape(-1))
    flat = jnp.concatenate(parts)
    flat = jnp.pad(flat, (0, rows * LANES - flat.shape[0]))
    return flat.reshape(rows, LANES)


def _unpack_full(gathered, group):
    flat = gathered.reshape(N_DEV, -1)
    out, off = {}, 0
    for name, (r, c), axis in group:
        rr, cc = (c, r) if axis == 1 else (r, c)
        out[name] = flat[:, off:off + r * c].reshape(N_DEV * rr, cc)
        off += r * c
    return out


def _pack_full_grads(grads, group, rows):
    parts = [grads[name].reshape(N_DEV, r * c) for name, (r, c), _ in group]
    flat = jnp.concatenate(parts, axis=1).astype(BF16)
    flat = jnp.pad(flat, ((0, 0), (0, rows * LANES - flat.shape[1])))
    return flat.reshape(N_DEV, rows, LANES)


def _unpack_shard_grads(flat, group):
    flat = flat.reshape(-1)
    out, off = {}, 0
    for name, (r, c), axis in group:
        seg = flat[off:off + r * c]
        out[name] = seg.reshape(c, r).T if axis == 1 else seg.reshape(r, c)
        off += r * c
    return out


def _own_slot(slots, own):
    mine = _flat_index(_mesh_pos())
    return lax.dynamic_update_slice(slots, own[None], (mine, 0, 0))


def _rope_tables(positions):
    inv_freq = ROPE_THETA ** (-jnp.arange(0, MLA_ROPE_DIM, 2, dtype=F32) / MLA_ROPE_DIM)
    ang = positions.astype(F32)[:, None] * inv_freq
    z64 = jnp.zeros((positions.shape[0], 64), F32)
    z32 = jnp.zeros((positions.shape[0], 32), F32)
    cos, sin = jnp.cos(ang), jnp.sin(ang)
    return (jnp.concatenate([z64, cos, cos, z32], axis=1), jnp.concatenate([z64, sin, sin, z32], axis=1))


def _row_tile(s_len, want):
    return _pick(s_len, (want, 256, 128))


def kernel(x, positions, norm_mix_pre, norm_mix_post, w_in, b_gate, q_norm, w_uq, kv_norm, w_ukv, w_proj_sb, w_proj_mla, w_out, norm_ffn_pre, norm_ffn_post, w_gate_up, w_down, loss_target, m_norm_mix_pre, m_norm_mix_post, m_w_in, m_b_gate, m_q_norm, m_w_uq, m_kv_norm, m_w_ukv, m_w_proj_sb, m_w_proj_mla, m_w_out, m_norm_ffn_pre, m_norm_ffn_post, m_w_gate_up, m_w_down, v_norm_mix_pre, v_norm_mix_post, v_w_in, v_b_gate, v_q_norm, v_w_uq, v_kv_norm, v_w_ukv, v_w_proj_sb, v_w_proj_mla, v_w_out, v_norm_ffn_pre, v_norm_ffn_post, v_w_gate_up, v_w_down):
    weights = dict(norm_mix_pre=norm_mix_pre, norm_mix_post=norm_mix_post, w_in=w_in, b_gate=b_gate, q_norm=q_norm,
                   w_uq=w_uq, kv_norm=kv_norm, w_ukv=w_ukv, w_proj_sb=w_proj_sb, w_proj_mla=w_proj_mla, w_out=w_out,
                   norm_ffn_pre=norm_ffn_pre, norm_ffn_post=norm_ffn_post, w_gate_up=w_gate_up, w_down=w_down)
    m_in = dict(norm_mix_pre=m_norm_mix_pre, norm_mix_post=m_norm_mix_post, w_in=m_w_in, b_gate=m_b_gate,
                q_norm=m_q_norm, w_uq=m_w_uq, kv_norm=m_kv_norm, w_ukv=m_w_ukv, w_proj_sb=m_w_proj_sb,
                w_proj_mla=m_w_proj_mla, w_out=m_w_out, norm_ffn_pre=m_norm_ffn_pre, norm_ffn_post=m_norm_ffn_post,
                w_gate_up=m_w_gate_up, w_down=m_w_down)
    v_in = dict(norm_mix_pre=v_norm_mix_pre, norm_mix_post=v_norm_mix_post, w_in=v_w_in, b_gate=v_b_gate,
                q_norm=v_q_norm, w_uq=v_w_uq, kv_norm=v_kv_norm, w_ukv=v_w_ukv, w_proj_sb=v_w_proj_sb,
                w_proj_mla=v_w_proj_mla, w_out=v_w_out, norm_ffn_pre=v_norm_ffn_pre, norm_ffn_post=v_norm_ffn_post,
                w_gate_up=v_w_gate_up, w_down=v_w_down)
    order = list(weights)

    xs = x[0]
    target = loss_target[0]
    s_len = xs.shape[0]
    tm_fwd = _row_tile(s_len, 512)
    tm = _row_tile(s_len, 256)
    tm_ffn = tm

    shards = {name: weights[name][0] for name, _, _ in SHARDED}
    late_shards = [(shards[name].T if axis == 1 else shards[name]).astype(BF16) for name, _, axis in LATE]
    late_weights = _push_start("weights_late_start", late_shards, [False] * len(LATE))
    full = _unpack_full(_all_gather(_pack_shards(shards, EARLY, EARLY_ROWS)), EARLY)
    wt = full["w_in"]
    zr = lambda n: jnp.zeros((n, D_MODEL), BF16)
    w_ext = jnp.concatenate([wt[:2176], zr(64), wt[2176:2208], zr(32), wt[2208:]], axis=0).T
    wa = jnp.pad(full["w_uq"].reshape(N_HEADS, MLA_QK_DIM, MLA_Q_RANK), ((0, 0), (0, 32), (0, 0))
                 ).reshape(N_HEADS * LANES, MLA_Q_RANK).T
    ukv = full["w_ukv"].reshape(N_HEADS, LANES, MLA_KV_RANK)
    wk = jnp.pad(ukv[:, :64], ((0, 0), (0, 64), (0, 0))).reshape(N_HEADS * LANES, MLA_KV_RANK).T
    wv = ukv[:, 64:].reshape(512, MLA_KV_RANK).T
    g_mix_pre = norm_mix_pre + late_weights[4][0:1, 0:1]
    cr, sr = _rope_tables(positions[0])

    qkv, cq, ckv, kr, gl, hb = _in_proj(xs, g_mix_pre, w_ext, tm_fwd)
    q_mla, k_mla, v_mla, cqn, ckvn = _mla_up(cq, ckv, kr, cr, sr, q_norm, kv_norm, wa, wk, wv, tm_fwd)
    o_sb, o_mla, lse = _attn_fwd(qkv, q_mla, k_mla, v_mla)
    late_shards, late_slots = _push_wait("weights_late_wait", late_weights, [False] * len(LATE), o_sb)
    full = {name: _own_slot(slots, own).reshape(-1, own.shape[1])
            for (name, _, _), slots, own in zip(LATE, late_slots, late_shards)}
    w_sb, w_mla, w_o, w_gu, w_dn = (full["w_proj_sb"].T, full["w_proj_mla"].T, full["w_out"], full["w_gate_up"].T,
                                    full["w_down"])
    x1, y, merged, o_sb_b, o_mla_b = _mix_out(o_sb, o_mla, gl, xs, b_gate, norm_mix_post, w_sb, w_mla, w_o, tm_fwd)
    dx2, f, h2, loss_part = _ffn_fwd(x1, target, norm_ffn_pre, norm_ffn_post, w_gu, w_dn, tm_fwd)
    loss_local = (0.5 / D_MODEL * jnp.sum(loss_part)).reshape(1)

    dx1, act, dgu, dfb, dg_ffn_pre, dg_ffn_post = _ffn_bwd(dx2, f, x1, norm_ffn_pre, norm_ffn_post, w_gu, w_dn, tm_ffn)
    dyb, dpsb, dpmla, dgl, do_sb, do_mla, dg_mix_post, db_gate = _mix_bwd(
        dx1, y, o_sb, o_mla, gl, b_gate, norm_mix_post, w_sb, w_mla, w_o, tm)
    late_grads = {
        "w_proj_sb": _matmul_tn("dw_proj_sb", dpsb, o_sb_b),
        "w_proj_mla": _matmul_tn("dw_proj_mla", dpmla, o_mla_b),
        "w_out": _matmul_tn("dw_out", merged, dyb),
        "w_gate_up": _matmul_tn("dw_gate_up", dgu, h2),
        "w_down": _matmul_tn("dw_down", act, dfb),
    }
    late_grads = [late_grads[name].reshape(N_DEV, -1, late_grads[name].shape[1]) for name, _, _ in LATE]
    late_exchange = _push_start("grads_late_start", late_grads, [True] * len(LATE))
    dq_sb, dq_mla, dk_sb, dv_sb, dk_mla, dv_mla = _attn_bwd(qkv, do_sb, o_sb, q_mla, k_mla, v_mla, do_mla, o_mla, lse,
                                                             late_exchange[4])
    da, dkb, dvb, dlat, dg_q, dg_kv = _mla_up_bwd(dq_mla, dk_mla, dv_mla, cq, ckv, cr, sr, q_norm, kv_norm,
                                                  wa, wk, wv, tm_fwd)
    dx, dproj, dg_mix_pre = _in_proj_bwd(xs, dx1, dq_sb, dk_sb, dv_sb, dlat, dgl, norm_mix_pre, w_ext, tm_fwd)
    d_ext = _matmul_tn("dw_in", dproj, hb)
    d_wa = _matmul_tn("dw_uq", da, cqn)
    d_wk = _matmul_tn("dw_uk", dkb, ckvn)
    d_wv = _matmul_tn("dw_uv", dvb, ckvn)
    early_grads = _pack_full_grads({
        "w_in": jnp.concatenate([d_ext[:2176], d_ext[2240:2272], d_ext[EXT_GL:]], axis=0),
        "w_uq": d_wa.reshape(N_HEADS, LANES, MLA_Q_RANK)[:, :MLA_QK_DIM].reshape(768, MLA_Q_RANK),
        "w_ukv": jnp.concatenate([d_wk.reshape(N_HEADS, LANES, MLA_KV_RANK)[:, :64],
                                  d_wv.reshape(N_HEADS, 64, MLA_KV_RANK)], axis=1).reshape(1024, MLA_KV_RANK),
    }, EARLY, EARLY_ROWS)
    small_parts = dict(norm_mix_pre=dg_mix_pre, norm_mix_post=dg_mix_post, b_gate=db_gate, q_norm=dg_q,
                       kv_norm=dg_kv, norm_ffn_pre=dg_ffn_pre, norm_ffn_post=dg_ffn_post)
    small = jnp.concatenate([small_parts[name].sum(axis=0) for name, _ in SMALL] + [loss_local])
    small = jnp.pad(small, (0, SMALL_ROWS * LANES - small.shape[0])).reshape(SMALL_ROWS, LANES)

    early_exchange = _push_start("grads_early_start", [early_grads, small], [True, False])
    late_grads, late_slots = _push_wait("grads_late_wait", late_exchange, [True] * len(LATE), early_exchange[4])
    mine = _flat_index(_mesh_pos())
    g_out, deltas, new_m, new_v = {}, {}, {}, {}

    def adamw(name):
        w2 = weights[name].reshape(g_out[name].shape)
        d, nm, nv = _adamw("adamw_" + name, w2, g_out[name], m_in[name].reshape(w2.shape), v_in[name].reshape(w2.shape))
        shape = weights[name].shape
        deltas[name], new_m[name], new_v[name] = d.reshape(shape), nm.reshape(shape), nv.reshape(shape)
        g_out[name] = g_out[name].reshape(shape)

    for (name, _, axis), slots, own in zip(LATE, late_slots, late_grads):
        slots = _own_slot(slots, lax.dynamic_index_in_dim(own, mine, 0, keepdims=False))
        total = _slot_sum("grad_sum_" + name, slots, _pick(slots.shape[1], (352, 128)))
        g_out[name] = total.T if axis == 1 else total
        adamw(name)
    (early_grads, small), (early_slots, small_slots) = _push_wait(
        "grads_early_wait", early_exchange, [True, False], deltas["w_down"])
    early_slots = _own_slot(early_slots, lax.dynamic_index_in_dim(early_grads, mine, 0, keepdims=False))
    g_out.update(_unpack_shard_grads(_slot_sum("grad_early_sum", early_slots, 960), EARLY))
    s_flat = _slot_sum("grad_small_sum", _own_slot(small_slots, small), SMALL_ROWS).reshape(-1)
    off = 0
    for name, n in SMALL:
        g_out[name] = s_flat[off:off + n].reshape(1, n)
        off += n
    loss = s_flat[off]
    for name in order:
        if name not in deltas:
            adamw(name)

    return (loss, dx[None], *[g_out[n] for n in order], *[deltas[n] for n in order],
            *[new_m[n] for n in order], *[new_v[n] for n in order])
```

```python
import functools
import math

import jax
import jax.numpy as jnp
from jax import lax
from jax.experimental import pallas as pl
from jax.experimental.pallas import tpu as pltpu

F32 = jnp.float32
BF16 = jnp.bfloat16

D_MODEL = 1024
N_HEADS = 8
SB_WIDTH = 512
MLA_Q_RANK = 384
MLA_KV_RANK = 256
MLA_ROPE_DIM = 32
MLA_QK_DIM = 96
D_FF = 2816
ROPE_THETA = 10000.0
EPS = 1e-6
SB_SCALE = 1.0 / math.sqrt(64.0)
MLA_SCALE = 1.0 / math.sqrt(96.0)
NEG_BIG = -1e30

ADAM_LR = 0.001
ADAM_B1 = 0.9
ADAM_B2 = 0.999
ADAM_EPS = 1e-08
ADAM_WD = 0.01
ADAM_STEP = 10

N_DEV = 8
LANES = 128
TQ = 512
TK = 128
DIAG_TILES = TQ // TK
FWD_UNROLL = 4
BWD_UNROLL = 2
VMEM_LIMIT = 56 << 20

EXT_QKV = 0
EXT_CQ = 1536
EXT_CKV = 1920
EXT_KR = 2176
EXT_GL = 2304
EXT_N = 4352

EARLY = (("w_in", (1024, 532), 1), ("w_uq", (384, 96), 1), ("w_ukv", (256, 128), 1))
LATE = (("w_proj_sb", (512, 128), 1), ("w_proj_mla", (512, 128), 1), ("w_out", (128, 1024), 0),
        ("w_gate_up", (1024, 704), 1), ("w_down", (352, 1024), 0))
SHARDED = EARLY + LATE
EARLY_ROWS = 4800
EARLY_REST_ROWS = 544
SMALL = (("norm_mix_pre", 1024), ("norm_mix_post", 1024), ("b_gate", 2048), ("q_norm", 384),
         ("kv_norm", 256), ("norm_ffn_pre", 1024), ("norm_ffn_post", 1024))
SMALL_ROWS = 56


def _dot(a, b):
    return jnp.dot(a, b, preferred_element_type=F32)


def _dot_nt(a, b):
    return lax.dot_general(a, b, (((1,), (1,)), ((), ())), preferred_element_type=F32)


def _dot_tn(a, b):
    return lax.dot_general(a, b, (((0,), (0,)), ((), ())), preferred_element_type=F32)


def _rms(x):
    r = lax.rsqrt(jnp.mean(x * x, axis=-1, keepdims=True) + EPS)
    return x * r, r


def _rms_bwd(dn, n, r):
    return r * (dn - n * jnp.mean(dn * n, axis=-1, keepdims=True))


def _colsum8(x):
    return jnp.sum(x.reshape(x.shape[0] // 8, 8, x.shape[1]), axis=0)


def _split(x):
    hi = x.astype(BF16)
    return hi, (x - hi.astype(F32)).astype(BF16)


def _rot(x):
    lane = lax.broadcasted_iota(jnp.int32, x.shape, 1)
    up = pltpu.roll(x, 112, 1)
    down = pltpu.roll(x, 16, 1)
    return jnp.where((lane >= 64) & (lane < 80), -up, jnp.where((lane >= 80) & (lane < 96), down, 0.0))


def _params(sem):
    return pltpu.CompilerParams(dimension_semantics=sem, vmem_limit_bytes=VMEM_LIMIT)


def _rows_call(name, body, n_rows, tm, row_ins, const_ins, row_outs, acc_outs):
    in_specs = [pl.BlockSpec((tm, a.shape[1]), lambda i: (i, 0)) for a in row_ins]
    in_specs += [pl.BlockSpec(a.shape, lambda i: (0, 0), pipeline_mode=pl.Buffered(1)) for a in const_ins]
    out_specs = [pl.BlockSpec((tm, n), lambda i: (i, 0)) for n, _ in row_outs]
    out_specs += [pl.BlockSpec(s, lambda i: (0, 0)) for s in acc_outs]
    out_shape = [jax.ShapeDtypeStruct((n_rows, n), dt) for n, dt in row_outs]
    out_shape += [jax.ShapeDtypeStruct(s, F32) for s in acc_outs]
    return pl.pallas_call(
        body, grid=(n_rows // tm,), in_specs=in_specs, out_specs=out_specs, out_shape=out_shape,
        name=name, compiler_params=_params(("arbitrary",)),
    )(*row_ins, *const_ins)


def _in_proj(x, g_pre, w_ext, tm):
    def body(x_ref, g_ref, w_ref, qkv_ref, cq_ref, ckv_ref, kr_ref, gl_ref, h_ref):
        n, _ = _rms(x_ref[...])
        hb = (n * g_ref[...]).astype(BF16)
        h_ref[...] = hb
        for c in range(0, 1536, 512):
            qkv_ref[:, c:c + 512] = _dot(hb, w_ref[:, c:c + 512]).astype(BF16)
        cq_ref[...] = _dot(hb, w_ref[:, EXT_CQ:EXT_CKV])
        ckv_ref[...] = _dot(hb, w_ref[:, EXT_CKV:EXT_KR])
        kr_ref[...] = _dot(hb, w_ref[:, EXT_KR:EXT_GL])
        for c in range(0, 2048, 512):
            gl_ref[:, c:c + 512] = _dot(hb, w_ref[:, EXT_GL + c:EXT_GL + c + 512])

    return _rows_call("in_proj", body, x.shape[0], tm, [x], [g_pre, w_ext],
                      [(1536, BF16), (384, F32), (256, F32), (128, F32), (2048, F32), (1024, BF16)], [])


def _mla_up(cq, ckv, kr, cr, sr, q_norm, kv_norm, wa, wk, wv, tm):
    def body(cq_ref, ckv_ref, kr_ref, cr_ref, sr_ref, qn_ref, kvn_ref, wa_ref, wk_ref, wv_ref,
             q_ref, k_ref, v_ref, cqn_ref, ckvn_ref):
        nq, _ = _rms(cq_ref[...])
        cqn = (nq * qn_ref[...]).astype(BF16)
        cqn_ref[...] = cqn
        nk, _ = _rms(ckv_ref[...])
        ckvn = (nk * kvn_ref[...]).astype(BF16)
        ckvn_ref[...] = ckvn
        cr = cr_ref[...]
        sr = sr_ref[...]
        lane = lax.broadcasted_iota(jnp.int32, cr.shape, 1)
        cm = cr + (lane < 64).astype(F32)
        kr = kr_ref[...]
        krp = kr * cr + _rot(kr) * sr
        for h in range(N_HEADS):
            hs = slice(h * LANES, (h + 1) * LANES)
            a = _dot(cqn, wa_ref[:, hs])
            q_ref[:, hs] = (a * cm + _rot(a) * sr).astype(BF16)
            k_ref[:, hs] = (_dot(ckvn, wk_ref[:, hs]) + krp).astype(BF16)
        v_ref[...] = _dot(ckvn, wv_ref[...]).astype(BF16)

    return _rows_call("mla_up", body, cq.shape[0], tm, [cq, ckv, kr, cr, sr], [q_norm, kv_norm, wa, wk, wv],
                      [(1024, BF16), (1024, BF16), (512, BF16), (384, BF16), (256, BF16)], [])


def _mix_out(o_sb, o_mla, gl, x, b_gate, g_post, w_sb, w_mla, w_out, tm):
    def body(osb_ref, omla_ref, gl_ref, x_ref, b_ref, gp_ref, wsb_ref, wmla_ref, wout_ref,
             x1_ref, y_ref, mb_ref, osbb_ref, omlab_ref):
        osb = osb_ref[...].astype(BF16)
        omla = omla_ref[...].astype(BF16)
        osbb_ref[...] = osb
        omlab_ref[...] = omla
        psb = _dot(osb, wsb_ref[...])
        pmla = _dot(omla, wmla_ref[...])
        gates = jax.nn.sigmoid(gl_ref[...] + b_ref[...])
        mb = (gates[:, :D_MODEL] * psb + gates[:, D_MODEL:] * pmla).astype(BF16)
        mb_ref[...] = mb
        y = _dot(mb, wout_ref[...])
        y_ref[...] = y
        n, _ = _rms(y)
        x1_ref[...] = x_ref[...] + n * gp_ref[...]

    return _rows_call("mix_out", body, x.shape[0], tm, [o_sb, o_mla, gl, x], [b_gate, g_post, w_sb, w_mla, w_out],
                      [(1024, F32), (1024, F32), (1024, BF16), (512, BF16), (512, BF16)], [])


FF_CHUNK = 1408


def _ffn_fwd(x1, target, g_pre, g_post, w_gu, w_down, tm):
    def body(x1_ref, t_ref, gpre_ref, gpost_ref, wgu_ref, wd_ref, dx2_ref, f_ref, h2_ref, loss_ref):
        x1 = x1_ref[...]
        n, _ = _rms(x1)
        h2 = (n * gpre_ref[...]).astype(BF16)
        h2_ref[...] = h2
        f = jnp.zeros((tm, D_MODEL), F32)
        for c in range(0, D_FF, FF_CHUNK):
            g = _dot(h2, wgu_ref[:, c:c + FF_CHUNK])
            u = _dot(h2, wgu_ref[:, D_FF + c:D_FF + c + FF_CHUNK])
            act = (g * jax.nn.sigmoid(g) * u).astype(BF16)
            f = f + _dot(act, wd_ref[c:c + FF_CHUNK, :])
        f_ref[...] = f
        nf, _ = _rms(f)
        err = x1 + nf * gpost_ref[...] - t_ref[...]
        dx2_ref[...] = err * (1.0 / D_MODEL)
        e8 = _colsum8(err * err)
        part = e8[:, 0:LANES]
        for c in range(LANES, D_MODEL, LANES):
            part = part + e8[:, c:c + LANES]

        @pl.when(pl.program_id(0) == 0)
        def _():
            loss_ref[...] = jnp.zeros_like(loss_ref)

        loss_ref[...] += part

    return _rows_call("ffn_fwd", body, x1.shape[0], tm, [x1, target], [g_pre, g_post, w_gu, w_down],
                      [(1024, F32), (1024, F32), (1024, BF16)], [(8, LANES)])


def _ffn_bwd(dx2, f, x1, g_pre, g_post, w_gu, w_down, tm):
    def body(dx2_ref, f_ref, x1_ref, gpre_ref, gpost_ref, wgu_ref, wd_ref,
             dx1_ref, act_ref, dgu_ref, dfb_ref, dgpre_ref, dgpost_ref):
        @pl.when(pl.program_id(0) == 0)
        def _():
            dgpre_ref[...] = jnp.zeros_like(dgpre_ref)
            dgpost_ref[...] = jnp.zeros_like(dgpost_ref)

        dx2 = dx2_ref[...]
        nf, rf = _rms(f_ref[...])
        dgpost_ref[...] += _colsum8(dx2 * nf)
        dfb = _rms_bwd(dx2 * gpost_ref[...], nf, rf).astype(BF16)
        dfb_ref[...] = dfb
        x1 = x1_ref[...]
        n1, r1 = _rms(x1)
        h2 = (n1 * gpre_ref[...]).astype(BF16)
        dh2 = jnp.zeros((tm, D_MODEL), F32)
        for c in range(0, D_FF, FF_CHUNK):
            cs, us = slice(c, c + FF_CHUNK), slice(D_FF + c, D_FF + c + FF_CHUNK)
            g = _dot(h2, wgu_ref[:, cs])
            u = _dot(h2, wgu_ref[:, us])
            sg = jax.nn.sigmoid(g)
            si = g * sg
            act_ref[:, cs] = (si * u).astype(BF16)
            dact = _dot_nt(dfb, wd_ref[cs, :])
            dg = (dact * u * (sg * (1.0 + g * (1.0 - sg)))).astype(BF16)
            du = (dact * si).astype(BF16)
            dgu_ref[:, cs] = dg
            dgu_ref[:, us] = du
            dh2 = dh2 + _dot_nt(dg, wgu_ref[:, cs]) + _dot_nt(du, wgu_ref[:, us])
        dgpre_ref[...] += _colsum8(dh2 * n1)
        dx1_ref[...] = dx2 + _rms_bwd(dh2 * gpre_ref[...], n1, r1)

    return _rows_call("ffn_bwd", body, dx2.shape[0], tm, [dx2, f, x1], [g_pre, g_post, w_gu, w_down],
                      [(1024, F32), (D_FF, BF16), (2 * D_FF, BF16), (1024, BF16)], [(8, 1024), (8, 1024)])


def _mix_bwd(dx1, y, o_sb, o_mla, gl, b_gate, g_post, w_sb, w_mla, w_out, tm):
    def body(dx1_ref, y_ref, osb_ref, omla_ref, gl_ref, b_ref, gp_ref, wsb_ref, wmla_ref, wout_ref,
             dyb_ref, dpsb_ref, dpmla_ref, dgl_ref, dosb_ref, domla_ref, dgpost_ref, dbg_ref):
        @pl.when(pl.program_id(0) == 0)
        def _():
            dgpost_ref[...] = jnp.zeros_like(dgpost_ref)
            dbg_ref[...] = jnp.zeros_like(dbg_ref)

        dx1 = dx1_ref[...]
        ny, ry = _rms(y_ref[...])
        dgpost_ref[...] += _colsum8(dx1 * ny)
        dyb = _rms_bwd(dx1 * gp_ref[...], ny, ry).astype(BF16)
        dyb_ref[...] = dyb
        dm = _dot_nt(dyb, wout_ref[...])
        psb = _dot(osb_ref[...].astype(BF16), wsb_ref[...])
        pmla = _dot(omla_ref[...].astype(BF16), wmla_ref[...])
        gates = jax.nn.sigmoid(gl_ref[...] + b_ref[...])
        g0, g1 = gates[:, :D_MODEL], gates[:, D_MODEL:]
        dpsb = (dm * g0).astype(BF16)
        dpmla = (dm * g1).astype(BF16)
        dpsb_ref[...] = dpsb
        dpmla_ref[...] = dpmla
        dgl0 = dm * psb * g0 * (1.0 - g0)
        dgl1 = dm * pmla * g1 * (1.0 - g1)
        dgl_ref[:, :D_MODEL] = dgl0.astype(BF16)
        dgl_ref[:, D_MODEL:] = dgl1.astype(BF16)
        dbg_ref[:, :D_MODEL] += _colsum8(dgl0)
        dbg_ref[:, D_MODEL:] += _colsum8(dgl1)
        dosb_ref[...] = _dot_nt(dpsb, wsb_ref[...]).astype(BF16)
        domla_ref[...] = _dot_nt(dpmla, wmla_ref[...]).astype(BF16)

    return _rows_call("mix_bwd", body, dx1.shape[0], tm, [dx1, y, o_sb, o_mla, gl],
                      [b_gate, g_post, w_sb, w_mla, w_out],
                      [(1024, BF16), (1024, BF16), (1024, BF16), (2048, BF16), (512, BF16), (512, BF16)],
                      [(8, 1024), (8, 2048)])


def _mla_up_bwd(dq, dk, dv, cq, ckv, cr, sr, q_norm, kv_norm, wa, wk, wv, tm):
    def body(dq_ref, dk_ref, dv_ref, cq_ref, ckv_ref, cr_ref, sr_ref, qn_ref, kvn_ref, wa_ref, wk_ref, wv_ref,
             da_ref, dkb_ref, dvb_ref, dlat_ref, dqn_ref, dkvn_ref):
        @pl.when(pl.program_id(0) == 0)
        def _():
            dqn_ref[...] = jnp.zeros_like(dqn_ref)
            dkvn_ref[...] = jnp.zeros_like(dkvn_ref)

        cr = cr_ref[...]
        sr = sr_ref[...]
        lane = lax.broadcasted_iota(jnp.int32, cr.shape, 1)
        cm = cr + (lane < 64).astype(F32)
        nq, rq = _rms(cq_ref[...])
        nk, rk = _rms(ckv_ref[...])
        dcqn = jnp.zeros((tm, MLA_Q_RANK), F32)
        dckvn = jnp.zeros((tm, MLA_KV_RANK), F32)
        dkrp = jnp.zeros((tm, LANES), F32)
        for h in range(N_HEADS):
            hs = slice(h * LANES, (h + 1) * LANES)
            dqh = dq_ref[:, hs]
            da = (dqh * cm - _rot(dqh * sr)).astype(BF16)
            da_ref[:, hs] = da
            dcqn = dcqn + _dot_nt(da, wa_ref[:, hs])
            dkh = dk_ref[:, hs]
            dkb = dkh.astype(BF16)
            dkb_ref[:, hs] = dkb
            dckvn = dckvn + _dot_nt(dkb, wk_ref[:, hs])
            dkrp = dkrp + dkh
        dvb = dv_ref[...].astype(BF16)
        dvb_ref[...] = dvb
        dckvn = dckvn + _dot_nt(dvb, wv_ref[...])
        dkr = dkrp * cr - _rot(dkrp * sr)
        dqn_ref[...] += _colsum8(dcqn * nq)
        dkvn_ref[...] += _colsum8(dckvn * nk)
        dlat_ref[:, 0:384] = _rms_bwd(dcqn * qn_ref[...], nq, rq).astype(BF16)
        dlat_ref[:, 384:640] = _rms_bwd(dckvn * kvn_ref[...], nk, rk).astype(BF16)
        dlat_ref[:, 640:768] = dkr.astype(BF16)

    return _rows_call("mla_up_bwd", body, dq.shape[0], tm, [dq, dk, dv, cq, ckv, cr, sr],
                      [q_norm, kv_norm, wa, wk, wv],
                      [(1024, BF16), (1024, BF16), (512, BF16), (768, BF16)], [(8, 384), (8, 256)])


def _in_proj_bwd(x, dx1, dq_sb, dk_sb, dv_sb, dlat, dgl, g_pre, w_ext, tm):
    def body(x_ref, dx1_ref, dq_ref, dk_ref, dv_ref, dlat_ref, dgl_ref, g_ref, w_ref,
             dx_ref, dproj_ref, dg_ref):
        @pl.when(pl.program_id(0) == 0)
        def _():
            dg_ref[...] = jnp.zeros_like(dg_ref)

        dproj_ref[:, 0:512] = dq_ref[...].astype(BF16)
        dproj_ref[:, 512:1024] = dk_ref[...].astype(BF16)
        dproj_ref[:, 1024:1536] = dv_ref[...].astype(BF16)
        dproj_ref[:, EXT_CQ:EXT_GL] = dlat_ref[...]
        dproj_ref[:, EXT_GL:EXT_N] = dgl_ref[...]
        dh = jnp.zeros((tm, D_MODEL), F32)
        for c in range(0, EXT_N, 2176):
            dh = dh + _dot_nt(dproj_ref[:, c:c + 2176], w_ref[:, c:c + 2176])
        n, r = _rms(x_ref[...])
        dg_ref[...] += _colsum8(dh * n)
        dx_ref[...] = dx1_ref[...] + _rms_bwd(dh * g_ref[...], n, r)

    return _rows_call("in_proj_bwd", body, x.shape[0], tm, [x, dx1, dq_sb, dk_sb, dv_sb, dlat, dgl],
                      [g_pre, w_ext], [(1024, F32), (EXT_N, BF16)], [(8, 1024)])


def _head_masked(x):
    lane = lax.broadcasted_iota(jnp.int32, x.shape, 1)
    zero = jnp.zeros_like(x)
    return jnp.where(lane < 64, x, zero), jnp.where(lane >= 64, x, zero)


def _cum_weights():
    row = lax.broadcasted_iota(jnp.int32, (TK, TK), 0)
    col = lax.broadcasted_iota(jnp.int32, (TK, TK), 1)
    half = jnp.concatenate([(row > col).astype(BF16), jnp.ones((TK, TK), BF16)], axis=1)
    return jnp.concatenate([half, half], axis=0)


def _split_cat(x):
    hi, lo = _split(x)
    return jnp.concatenate([hi, lo], axis=1)


def _sweep(i, tiles, unroll, skip_rows=False):
    if skip_rows:
        for t in range(DIAG_TILES // unroll):
            top = DIAG_TILES - 1 - t * unroll
            r0 = (top - unroll + 1) * TK
            col = lax.broadcasted_iota(jnp.int32, (TQ - r0, TK), 1)
            tiles(i * DIAG_TILES + top, [(top - u) * TK - r0 + col for u in range(unroll)], r0)
    else:
        col = lax.broadcasted_iota(jnp.int32, (TQ, TK), 1)

        def diag(t, _):
            top = DIAG_TILES - 1 - t * unroll
            tiles(i * DIAG_TILES + top, [(top - u) * TK + col for u in range(unroll)])
            return 0

        lax.fori_loop(0, DIAG_TILES // unroll, diag, 0)

    def full(g, _):
        tiles(i * DIAG_TILES - 1 - g * unroll, [None] * unroll)
        return 0

    lax.fori_loop(0, (i * DIAG_TILES) // unroll, full, 0)


def _causal(key, strict):
    if key is None:
        return None
    row = lax.broadcasted_iota(jnp.int32, key.shape, 0)
    return key < row if strict else key <= row


def _sb_logs(z, valid, w_cum):
    soft = jnp.log(1.0 + jnp.exp(-jnp.abs(z)))
    lsm = -jnp.maximum(z, 0.0) - soft
    cat = _split_cat(lsm if valid is None else jnp.where(valid, lsm, 0.0))
    return z + lsm, _dot(cat, w_cum)


def _sb_weight(log_beta, cs, r, valid):
    a = jnp.exp(log_beta + cs[:, :TK] + r)
    if valid is not None:
        a = jnp.where(valid, a, 0.0)
    return a, r + cs[:, TK:]


def _block_diag(x):
    x0, x1 = x[:, :LANES], x[:, LANES:]
    zero = jnp.zeros_like(x0)
    return jnp.concatenate([jnp.concatenate([x0, zero], axis=1), jnp.concatenate([zero, x1], axis=1)], axis=0)


def _attn_fwd(qkv, q, k, v):
    s_len = qkv.shape[0]

    def body(qs_ref, ks_ref, vs_ref, qm_ref, km_ref, vm_ref, osb_ref, omla_ref, lse_ref,
             sacc_ref, r_ref, macc_ref, m_ref):
        i = pl.program_id(1)
        lane = lax.broadcasted_iota(jnp.int32, (TQ, LANES), 1)
        r2 = lax.broadcasted_iota(jnp.int32, (2 * TK, LANES), 0)
        c2 = lax.broadcasted_iota(jnp.int32, (2 * TK, LANES), 1)
        head_ones = ((r2 < TK) == (c2 < 64)).astype(BF16)
        w_cum = _cum_weights()
        qs = qs_ref[...] * SB_SCALE
        qm = qm_ref[...]
        sacc_ref[...] = jnp.zeros_like(sacc_ref)
        r_ref[...] = jnp.zeros_like(r_ref)
        macc_ref[...] = jnp.zeros_like(macc_ref)
        m_ref[...] = jnp.full(m_ref.shape, NEG_BIG, F32)

        def sb_scores(top, valids):
            out = []
            for u, valid in enumerate(valids):
                off = pl.multiple_of((top - u) * TK, TK)
                z01 = _dot_nt(qs, jnp.concatenate(_head_masked(ks_ref[pl.ds(off, TK), :]), axis=0))
                out.append([_sb_logs(z, valid, w_cum) for z in (z01[:, :TK], z01[:, TK:])])
            return out

        def sb_accumulate(top, valids, scores):
            parts = [[None, None] for _ in valids]
            for hh in range(2):
                r = r_ref[hh]
                for u, valid in enumerate(valids):
                    a, r = _sb_weight(*scores[u][hh], r, valid)
                    parts[u][hh] = _split_cat(a)
                r_ref[hh] = r
            vs = []
            for u in range(len(valids)):
                off = pl.multiple_of((top - u) * TK, TK)
                v0, v1 = _head_masked(vs_ref[pl.ds(off, TK), :])
                vs += [v0, v0, v1, v1]
            sacc_ref[...] += _dot(jnp.concatenate([p for pair in parts for p in pair], axis=1),
                                  jnp.concatenate(vs, axis=0))

        def mla_scores(top, valids):
            s01s = []
            for u in range(len(valids)):
                off = pl.multiple_of((top - u) * TK, TK)
                s01s.append(_dot_nt(qm, _block_diag(km_ref[pl.ds(off, TK), :])))
            heads = []
            for hh in range(2):
                ss = []
                for u, valid in enumerate(valids):
                    s = s01s[u][:, hh * TK:(hh + 1) * TK] * MLA_SCALE
                    ss.append(s if valid is None else jnp.where(valid, s, NEG_BIG))
                m_old = m_ref[hh]
                m = jnp.maximum(m_old, jnp.max(functools.reduce(jnp.maximum, ss), axis=1, keepdims=True))
                m_ref[hh] = m
                heads.append((ss, m, jnp.exp(m_old - m)))
            return heads

        def mla_accumulate(top, heads):
            vs = []
            for u in range(len(heads[0][0])):
                off = pl.multiple_of((top - u) * TK, TK)
                vv = jnp.concatenate(_head_masked(vm_ref[pl.ds(off, TK), :]), axis=0)
                vs.append(jnp.concatenate([vv, head_ones], axis=1))
            ps = [[jnp.exp(s - m).astype(BF16) for s in ss] for ss, m, _ in heads]
            scale = jnp.where(lane < 64, heads[0][2], heads[1][2])
            p_all = jnp.concatenate([ps[hh][u] for u in range(len(vs)) for hh in range(2)], axis=1)
            macc_ref[...] = (macc_ref[...] * jnp.concatenate([scale, scale], axis=1)
                             + _dot(p_all, jnp.concatenate(vs, axis=0)))

        def tiles(top, keys):
            strict = [_causal(key, True) for key in keys]
            heads = mla_scores(top, [_causal(key, False) for key in keys])
            scores = sb_scores(top, strict)
            mla_accumulate(top, heads)
            sb_accumulate(top, strict, scores)

        _sweep(i, tiles, FWD_UNROLL)
        osb_ref[...] = sacc_ref[...]
        acc = macc_ref[...]
        den = acc[:, LANES:]
        omla_ref[...] = acc[:, :LANES] / den
        for hh, mask in enumerate((lane < 64, lane >= 64)):
            l = jnp.max(jnp.where(mask, den, 0.0), axis=1, keepdims=True)
            lse_ref[hh] = jnp.broadcast_to(m_ref[hh] + jnp.log(l), (TQ, LANES))

    tile = pl.BlockSpec((TQ, LANES), lambda h, i: (i, h))
    return pl.pallas_call(
        body, grid=(4, s_len // TQ),
        in_specs=[tile,
                  pl.BlockSpec((s_len, LANES), lambda h, i: (0, 4 + h)),
                  pl.BlockSpec((s_len, LANES), lambda h, i: (0, 8 + h)),
                  pl.BlockSpec((TQ, 2 * LANES), lambda h, i: (i, h)),
                  pl.BlockSpec((s_len, 2 * LANES), lambda h, i: (0, h)),
                  pl.BlockSpec((s_len, LANES), lambda h, i: (0, h))],
        out_specs=[tile, tile, pl.BlockSpec((2, TQ, LANES), lambda h, i: (h, i, 0))],
        out_shape=[jax.ShapeDtypeStruct((s_len, SB_WIDTH), F32), jax.ShapeDtypeStruct((s_len, 512), F32),
                   jax.ShapeDtypeStruct((N_HEADS, s_len, LANES), F32)],
        scratch_shapes=[pltpu.VMEM((TQ, LANES), F32), pltpu.VMEM((2, TQ, LANES), F32),
                        pltpu.VMEM((TQ, 2 * LANES), F32), pltpu.VMEM((2, TQ, 1), F32)],
        name="attn_fwd", compiler_params=_params(("arbitrary", "arbitrary")),
    )(qkv, qkv, qkv, q, k, v)


def _row_dots(do, o):
    prod = do.astype(F32) * o
    p0, p1 = _head_masked(prod)
    return tuple(jnp.broadcast_to(jnp.sum(p, axis=1, keepdims=True), prod.shape) for p in (p0, p1))


def _attn_bwd(qkv, do_sb, o_sb, q, k, v, do_mla, o_mla, lse, after):
    s_len = qkv.shape[0]
    n_q = s_len // TQ

    def body(qs_ref, ks_ref, vs_ref, dos_ref, os_ref, qm_ref, km_ref, vm_ref, dom_ref, om_ref, lse_ref, after_ref,
             dqs_ref, dqm_ref, dks_hbm, dvs_hbm, dkm_hbm, dvm_hbm,
             dqs_acc, dqm_acc, dks_acc, dvs_acc, dkm_acc, dvm_acc, r_ref, g_ref, ds_ref, dm_ref, out_sems):
        h = pl.program_id(0)
        i = pl.program_id(1)

        @pl.when(i == 0)
        def _():
            for acc in (dks_acc, dvs_acc, dkm_acc, dvm_acc):
                acc[...] = jnp.zeros_like(acc)

        w_cum = _cum_weights()
        qs = qs_ref[...] * SB_SCALE
        dos = dos_ref[...]
        ds_ref[0], ds_ref[1] = _row_dots(dos, os_ref[...])
        qm = qm_ref[...]
        dom = dom_ref[...]
        dm_ref[0], dm_ref[1] = _row_dots(dom, om_ref[...])
        qs_heads, dos_heads, dom_heads = _head_masked(qs), _head_masked(dos), _head_masked(dom)
        by_head = lambda pair, rows: jnp.concatenate([x[rows] for x in pair], axis=0)
        all_rows = slice(0, TQ)
        qs_rows, dos_rows, dom_rows = (by_head(p, all_rows) for p in (qs_heads, dos_heads, dom_heads))
        qm_diag = _block_diag(qm)
        for ref in (dqs_acc, dqm_acc, r_ref, g_ref):
            ref[...] = jnp.zeros_like(ref)
        heads = (slice(0, TK), slice(TK, 2 * TK))

        def weigh(log_beta, cs, r, da, valid):
            a, r = _sb_weight(log_beta, cs, r, valid)
            g = a * da
            return a.astype(BF16), g, _split_cat(g), r

        def logit_grad(g, gs, carried, d, log_beta, valid):
            upto = d - (gs[:, :TK] + carried)
            dz = g - jnp.exp(log_beta) * upto
            if valid is not None:
                dz = jnp.where(valid, dz, 0.0)
            return dz.astype(BF16), carried + gs[:, TK:]

        def tiles(top, keys, r0=0):
            n = len(keys)
            rows = slice(r0, TQ)
            strict = [_causal(key, True) for key in keys]
            loose = [_causal(key, False) for key in keys]
            offs = [pl.multiple_of((top - u) * TK, TK) for u in range(n)]
            kds = [_block_diag(km_ref[pl.ds(off, TK), :]) for off in offs]
            vms = [jnp.concatenate(_head_masked(vm_ref[pl.ds(off, TK), :]), axis=0) for off in offs]
            s01s = [_dot_nt(qm[rows], kd) for kd in kds]
            dp01s = [_dot_nt(dom[rows], vv) for vv in vms]
            kks = [jnp.concatenate(_head_masked(ks_ref[pl.ds(off, TK), :]), axis=0) for off in offs]
            vvs = [jnp.concatenate(_head_masked(vs_ref[pl.ds(off, TK), :]), axis=0) for off in offs]
            z01s = [_dot_nt(qs[rows], kk) for kk in kks]
            da01s = [_dot_nt(dos[rows], vv) for vv in vvs]
            logs = [[_sb_logs(z01s[u][:, hs], strict[u], w_cum) for hs in heads] for u in range(n)]

            dss = [[None, None] for _ in range(n)]
            ps = [[None, None] for _ in range(n)]
            for u in range(n):
                for hh, hs in enumerate(heads):
                    p = jnp.exp(s01s[u][:, hs] * MLA_SCALE - lse_ref[hh, rows, :])
                    if loose[u] is not None:
                        p = jnp.where(loose[u], p, 0.0)
                    dss[u][hh] = (p * (dp01s[u][:, hs] - dm_ref[hh, rows, :]) * MLA_SCALE).astype(BF16)
                    ps[u][hh] = p.astype(BF16)

            dzs = [[None, None] for _ in range(n)]
            avs = [[None, None] for _ in range(n)]
            sums = [[None, None] for _ in range(n)]
            for hh, hs in enumerate(heads):
                r = r_ref[hh, rows, :]
                for u in range(n):
                    avs[u][hh], g, cat, r = weigh(*logs[u][hh], r, da01s[u][:, hs], strict[u])
                    sums[u][hh] = (g, _dot(cat, w_cum))
                r_ref[hh, rows, :] = r

            dqm_acc[rows, :] += _dot(jnp.concatenate([d for pair in dss for d in pair], axis=1),
                                     jnp.concatenate(kds, axis=0))
            span = pl.ds(offs[-1], n * TK)
            by_key = lambda pairs: jnp.concatenate([jnp.concatenate(pair, axis=0) for pair in pairs[::-1]], axis=1)
            whole = r0 == 0
            dkm_acc[span, :] += _dot_tn(by_key(dss), qm_diag if whole else _block_diag(qm[rows]))
            dvm_acc[span, :] += _dot_tn(by_key(ps), dom_rows if whole else by_head(dom_heads, rows))

            for hh in range(2):
                carried = g_ref[hh, rows, :]
                for u in range(n):
                    dzs[u][hh], carried = logit_grad(*sums[u][hh], carried, ds_ref[hh, rows, :], logs[u][hh][0],
                                                     strict[u])
                g_ref[hh, rows, :] = carried
            dqs_acc[rows, :] += _dot(jnp.concatenate([dz for pair in dzs for dz in pair], axis=1),
                                     jnp.concatenate(kks, axis=0))
            dks_acc[span, :] += _dot_tn(by_key(dzs), qs_rows if whole else by_head(qs_heads, rows))
            dvs_acc[span, :] += _dot_tn(by_key(avs), dos_rows if whole else by_head(dos_heads, rows))

        _sweep(i, tiles, BWD_UNROLL, skip_rows=True)
        dqs_ref[...] = dqs_acc[...] * SB_SCALE
        dqm_ref[...] = dqm_acc[...]

        @pl.when(i == n_q - 1)
        def _():
            narrow = pl.ds(pl.multiple_of(h * LANES, LANES), LANES)
            wide = pl.ds(pl.multiple_of(h * 2 * LANES, 2 * LANES), 2 * LANES)
            copies = [pltpu.make_async_copy(dks_acc, dks_hbm.at[:, narrow], out_sems.at[0]),
                      pltpu.make_async_copy(dvs_acc, dvs_hbm.at[:, narrow], out_sems.at[1]),
                      pltpu.make_async_copy(dkm_acc, dkm_hbm.at[:, wide], out_sems.at[2]),
                      pltpu.make_async_copy(dvm_acc, dvm_hbm.at[:, narrow], out_sems.at[3])]
            for cp in copies:
                cp.start()
            for cp in copies:
                cp.wait()

    tile = pl.BlockSpec((TQ, LANES), lambda h, i: (i, h))
    wide_tile = pl.BlockSpec((TQ, 2 * LANES), lambda h, i: (i, h))
    once = pl.Buffered(1)
    hbm = pl.BlockSpec(memory_space=pl.ANY)
    return pl.pallas_call(
        body, grid=(4, n_q),
        in_specs=[tile,
                  pl.BlockSpec((s_len, LANES), lambda h, i: (0, 4 + h), pipeline_mode=once),
                  pl.BlockSpec((s_len, LANES), lambda h, i: (0, 8 + h), pipeline_mode=once),
                  tile, tile, wide_tile,
                  pl.BlockSpec((s_len, 2 * LANES), lambda h, i: (0, h), pipeline_mode=once),
                  pl.BlockSpec((s_len, LANES), lambda h, i: (0, h), pipeline_mode=once),
                  tile, tile, pl.BlockSpec((2, TQ, LANES), lambda h, i: (h, i, 0)),
                  pl.BlockSpec((8, LANES), lambda h, i: (0, 0))],
        out_specs=[tile, wide_tile, hbm, hbm, hbm, hbm],
        out_shape=[jax.ShapeDtypeStruct((s_len, SB_WIDTH), F32), jax.ShapeDtypeStruct((s_len, 1024), F32),
                   jax.ShapeDtypeStruct((s_len, SB_WIDTH), F32), jax.ShapeDtypeStruct((s_len, SB_WIDTH), F32),
                   jax.ShapeDtypeStruct((s_len, 1024), F32), jax.ShapeDtypeStruct((s_len, 512), F32)],
        scratch_shapes=[pltpu.VMEM((TQ, LANES), F32), pltpu.VMEM((TQ, 2 * LANES), F32),
                        pltpu.VMEM((s_len, LANES), F32), pltpu.VMEM((s_len, LANES), F32),
                        pltpu.VMEM((s_len, 2 * LANES), F32), pltpu.VMEM((s_len, LANES), F32),
                        pltpu.VMEM((2, TQ, LANES), F32), pltpu.VMEM((2, TQ, LANES), F32),
                        pltpu.VMEM((2, TQ, LANES), F32), pltpu.VMEM((2, TQ, LANES), F32),
                        pltpu.SemaphoreType.DMA((4,))],
        name="attn_bwd", compiler_params=_params(("arbitrary", "arbitrary")),
    )(qkv, qkv, qkv, do_sb, o_sb, q, k, v, do_mla, o_mla, lse, after)


def _pick(n, options):
    for t in options:
        if n % t == 0:
            return t
    raise ValueError(n)


def _matmul_tn(name, a, b):
    s_len, m = a.shape
    n = b.shape[1]
    tm = _pick(m, (1024, 1408, 2176, 512))
    tn = _pick(n, (1024, 512, 384, 256))
    tk = _pick(s_len, (1024, 512, 256, 128))
    n_k = s_len // tk

    def body(a_ref, b_ref, o_ref, acc_ref):
        @pl.when(pl.program_id(2) == 0)
        def _():
            acc_ref[...] = jnp.zeros_like(acc_ref)

        acc_ref[...] += _dot_tn(a_ref[...], b_ref[...])

        @pl.when(pl.program_id(2) == n_k - 1)
        def _():
            o_ref[...] = acc_ref[...].astype(BF16)

    return pl.pallas_call(
        body, grid=(m // tm, n // tn, n_k),
        in_specs=[pl.BlockSpec((tk, tm), lambda i, j, l: (l, i)), pl.BlockSpec((tk, tn), lambda i, j, l: (l, j))],
        out_specs=pl.BlockSpec((tm, tn), lambda i, j, l: (i, j)),
        out_shape=jax.ShapeDtypeStruct((m, n), BF16),
        scratch_shapes=[pltpu.VMEM((tm, tn), F32)],
        name=name, compiler_params=_params(("arbitrary", "arbitrary", "arbitrary")),
    )(a, b)


def _mesh_pos():
    return lax.axis_index("x"), lax.axis_index("y"), lax.axis_index("c")


def _peer(pos, k):
    x, y, c = pos
    return (1 - x if k & 4 else x, 1 - y if k & 2 else y, 1 - c if k & 1 else c)


def _flat_index(pos):
    return 4 * pos[0] + 2 * pos[1] + pos[2]


def _all_gather(shard):
    rows = shard.shape[0]

    def body(x_ref, out_ref, send_sems, recv_sems, local_sem):
        me = _mesh_pos()
        x, y, c = me
        sibling = (x, y, 1 - c)
        chips = [(1 - x, y), (x, 1 - y), (1 - x, 1 - y)]

        def copy(k, block, to, src=None):
            slot = out_ref.at[_flat_index(block)]
            return pltpu.make_async_remote_copy(
                src_ref=slot if src is None else src, dst_ref=slot,
                send_sem=send_sems.at[k], recv_sem=recv_sems.at[k],
                device_id=to, device_id_type=pl.DeviceIdType.MESH)

        mine = pltpu.make_async_copy(x_ref, out_ref.at[_flat_index(me)], local_sem)
        mine.start()
        first = [copy(0, me, sibling, src=x_ref)]
        first += [copy(1 + j, me, (*chip, c), src=x_ref) for j, chip in enumerate(chips)]
        for cp in first:
            cp.start()
        passed = [copy(4 + j, (*chip, c), sibling) for j, chip in enumerate(chips)]
        for j, chip in enumerate(chips):
            copy(1 + j, (*chip, c), me).wait_recv()
            passed[j].start()
        copy(0, sibling, me).wait_recv()
        for j, chip in enumerate(chips):
            copy(4 + j, (*chip, 1 - c), me).wait_recv()
        for cp in first + passed:
            cp.wait_send()
        mine.wait()

    return pl.pallas_call(
        body, out_shape=jax.ShapeDtypeStruct((N_DEV, rows, LANES), shard.dtype),
        in_specs=[pl.BlockSpec(memory_space=pl.ANY)], out_specs=pl.BlockSpec(memory_space=pl.ANY),
        scratch_shapes=[pltpu.SemaphoreType.DMA((7,)), pltpu.SemaphoreType.DMA((7,)), pltpu.SemaphoreType.DMA],
        name="weights_all_gather",
    )(shard)


def _push_start(name, srcs, per_peer):
    n = len(srcs)
    lands = [lax.empty((N_DEV,) + src.shape[-2:], src.dtype) for src in srcs]

    def body(*refs):
        src_refs, land_refs = refs[:n], refs[n:2 * n]
        send_sems, recv_sems = refs[2 * n], refs[2 * n + 1]
        token = refs[-1]
        me = _mesh_pos()
        mine = _flat_index(me)
        for j in range(n):
            for k in range(1, N_DEV):
                peer = _peer(me, k)
                pltpu.make_async_remote_copy(
                    src_ref=src_refs[j].at[_flat_index(peer)] if per_peer[j] else src_refs[j],
                    dst_ref=land_refs[j].at[mine], send_sem=send_sems.at[7 * j + k - 1],
                    recv_sem=recv_sems.at[7 * j + k - 1],
                    device_id=peer, device_id_type=pl.DeviceIdType.MESH).start()
        token[...] = jnp.zeros_like(token)

    hbm = pl.BlockSpec(memory_space=pltpu.HBM)
    sem = pl.BlockSpec(memory_space=pltpu.SEMAPHORE)
    sems = pltpu.SemaphoreType.DMA((n * (N_DEV - 1),))
    out = pl.pallas_call(
        body, name=name,
        out_shape=(sems, sems) + tuple(pltpu.HBM(x.shape, x.dtype) for x in srcs + lands)
                  + (jax.ShapeDtypeStruct((8, LANES), F32),),
        in_specs=(hbm,) * (2 * n), out_specs=(sem, sem) + (hbm,) * (2 * n) + (pl.BlockSpec(memory_space=pltpu.VMEM),),
        input_output_aliases={j: 2 + j for j in range(2 * n)},
        compiler_params=pltpu.CompilerParams(has_side_effects=pltpu.SideEffectType.DATAFLOW_SIDE_EFFECTING),
    )(*[pltpu.with_memory_space_constraint(x, pltpu.HBM) for x in srcs + lands])
    return out[0], out[1], list(out[2:2 + n]), list(out[2 + n:2 + 2 * n]), out[-1]


def _push_wait(name, started, per_peer, after):
    send_sems, recv_sems, srcs, lands, _ = started
    n = len(srcs)

    def body(*refs):
        src_refs, land_refs = refs[:n], refs[n:2 * n]
        send_sems, recv_sems = refs[2 * n], refs[2 * n + 1]
        me = _mesh_pos()
        for j in range(n):
            for k in range(1, N_DEV):
                theirs = _flat_index(_peer(me, k))
                copy = pltpu.make_async_remote_copy(
                    src_ref=src_refs[j].at[theirs] if per_peer[j] else src_refs[j], dst_ref=land_refs[j].at[theirs],
                    send_sem=send_sems.at[7 * j + k - 1], recv_sem=recv_sems.at[7 * j + k - 1],
                    device_id=me, device_id_type=pl.DeviceIdType.MESH)
                copy.wait_send()
                copy.wait_recv()

    hbm = pl.BlockSpec(memory_space=pltpu.HBM)
    sem = pl.BlockSpec(memory_space=pltpu.SEMAPHORE)
    out = pl.pallas_call(
        body, name=name,
        out_shape=tuple(pltpu.HBM(x.shape, x.dtype) for x in srcs + lands),
        in_specs=(hbm,) * (2 * n) + (sem, sem, pl.BlockSpec(memory_space=pl.ANY)), out_specs=(hbm,) * (2 * n),
        input_output_aliases={j: j for j in range(2 * n)},
        compiler_params=pltpu.CompilerParams(has_side_effects=pltpu.SideEffectType.DATAFLOW_SIDE_EFFECTING),
    )(*srcs, *lands, send_sems, recv_sems, after)
    return list(out[:n]), list(out[n:])


def _slot_sum(name, slots, tr):
    rows, cols = slots.shape[1:]

    def body(s_ref, o_ref):
        acc = s_ref[0].astype(F32)
        for d in range(1, N_DEV):
            acc = acc + s_ref[d].astype(F32)
        o_ref[...] = acc

    return pl.pallas_call(
        body, grid=(rows // tr,),
        in_specs=[pl.BlockSpec((N_DEV, tr, cols), lambda i: (0, i, 0))],
        out_specs=pl.BlockSpec((tr, cols), lambda i: (i, 0)),
        out_shape=jax.ShapeDtypeStruct((rows, cols), F32),
        name=name, compiler_params=_params(("arbitrary",)),
    )(slots)


def _adamw(name, w, g, m, v):
    rows, cols = w.shape
    tr = _pick(rows, (256, 128, 88, 32, 1))
    c1 = 1.0 - ADAM_B1 ** ADAM_STEP
    c2 = 1.0 - ADAM_B2 ** ADAM_STEP

    def body(w_ref, g_ref, m_ref, v_ref, d_ref, nm_ref, nv_ref):
        g = g_ref[...]
        nm = ADAM_B1 * m_ref[...] + (1.0 - ADAM_B1) * g
        nv = ADAM_B2 * v_ref[...] + (1.0 - ADAM_B2) * (g * g)
        nm_ref[...] = nm
        nv_ref[...] = nv
        d_ref[...] = -ADAM_LR * ((nm / c1) / (jnp.sqrt(nv / c2) + ADAM_EPS) + ADAM_WD * w_ref[...])

    spec = pl.BlockSpec((tr, cols), lambda i: (i, 0))
    return pl.pallas_call(
        body, grid=(rows // tr,), in_specs=[spec] * 4, out_specs=[spec] * 3,
        out_shape=[jax.ShapeDtypeStruct((rows, cols), F32)] * 3,
        name=name, compiler_params=_params(("arbitrary",)),
    )(w, g, m, v)


def _pack_shards(shards, group, rows):
    parts = []
    for name, _, axis in group:
        w = shards[name].astype(BF16)
        parts.append((w.T if axis == 1 else w).reshape(-1))
    flat = jnp.concatenate(parts)
    flat = jnp.pad(flat, (0, rows * LANES - flat.shape[0]))
    return flat.reshape(rows, LANES)


def _unpack_full(gathered, group):
    flat = gathered.reshape(N_DEV, -1)
    out, off = {}, 0
    for name, (r, c), axis in group:
        rr, cc = (c, r) if axis == 1 else (r, c)
        out[name] = flat[:, off:off + r * c].reshape(N_DEV * rr, cc)
        off += r * c
    return out


def _pack_full_grads(grads, group, rows):
    parts = [grads[name].reshape(N_DEV, r * c) for name, (r, c), _ in group]
    flat = jnp.concatenate(parts, axis=1).astype(BF16)
    flat = jnp.pad(flat, ((0, 0), (0, rows * LANES - flat.shape[1])))
    return flat.reshape(N_DEV, rows, LANES)


def _unpack_shard_grads(flat, group):
    flat = flat.reshape(-1)
    out, off = {}, 0
    for name, (r, c), axis in group:
        seg = flat[off:off + r * c]
        out[name] = seg.reshape(c, r).T if axis == 1 else seg.reshape(r, c)
        off += r * c
    return out


def _own_slot(slots, own):
    mine = _flat_index(_mesh_pos())
    return lax.dynamic_update_slice(slots, own[None], (mine, 0, 0))


def _rope_tables(positions):
    inv_freq = ROPE_THETA ** (-jnp.arange(0, MLA_ROPE_DIM, 2, dtype=F32) / MLA_ROPE_DIM)
    ang = positions.astype(F32)[:, None] * inv_freq
    z64 = jnp.zeros((positions.shape[0], 64), F32)
    z32 = jnp.zeros((positions.shape[0], 32), F32)
    cos, sin = jnp.cos(ang), jnp.sin(ang)
    return (jnp.concatenate([z64, cos, cos, z32], axis=1), jnp.concatenate([z64, sin, sin, z32], axis=1))


def _row_tile(s_len, want):
    return _pick(s_len, (want, 256, 128))


def kernel(x, positions, norm_mix_pre, norm_mix_post, w_in, b_gate, q_norm, w_uq, kv_norm, w_ukv, w_proj_sb, w_proj_mla, w_out, norm_ffn_pre, norm_ffn_post, w_gate_up, w_down, loss_target, m_norm_mix_pre, m_norm_mix_post, m_w_in, m_b_gate, m_q_norm, m_w_uq, m_kv_norm, m_w_ukv, m_w_proj_sb, m_w_proj_mla, m_w_out, m_norm_ffn_pre, m_norm_ffn_post, m_w_gate_up, m_w_down, v_norm_mix_pre, v_norm_mix_post, v_w_in, v_b_gate, v_q_norm, v_w_uq, v_kv_norm, v_w_ukv, v_w_proj_sb, v_w_proj_mla, v_w_out, v_norm_ffn_pre, v_norm_ffn_post, v_w_gate_up, v_w_down):
    weights = dict(norm_mix_pre=norm_mix_pre, norm_mix_post=norm_mix_post, w_in=w_in, b_gate=b_gate, q_norm=q_norm,
                   w_uq=w_uq, kv_norm=kv_norm, w_ukv=w_ukv, w_proj_sb=w_proj_sb, w_proj_mla=w_proj_mla, w_out=w_out,
                   norm_ffn_pre=norm_ffn_pre, norm_ffn_post=norm_ffn_post, w_gate_up=w_gate_up, w_down=w_down)
    m_in = dict(norm_mix_pre=m_norm_mix_pre, norm_mix_post=m_norm_mix_post, w_in=m_w_in, b_gate=m_b_gate,
                q_norm=m_q_norm, w_uq=m_w_uq, kv_norm=m_kv_norm, w_ukv=m_w_ukv, w_proj_sb=m_w_proj_sb,
                w_proj_mla=m_w_proj_mla, w_out=m_w_out, norm_ffn_pre=m_norm_ffn_pre, norm_ffn_post=m_norm_ffn_post,
                w_gate_up=m_w_gate_up, w_down=m_w_down)
    v_in = dict(norm_mix_pre=v_norm_mix_pre, norm_mix_post=v_norm_mix_post, w_in=v_w_in, b_gate=v_b_gate,
                q_norm=v_q_norm, w_uq=v_w_uq, kv_norm=v_kv_norm, w_ukv=v_w_ukv, w_proj_sb=v_w_proj_sb,
                w_proj_mla=v_w_proj_mla, w_out=v_w_out, norm_ffn_pre=v_norm_ffn_pre, norm_ffn_post=v_norm_ffn_post,
                w_gate_up=v_w_gate_up, w_down=v_w_down)
    order = list(weights)

    xs = x[0]
    target = loss_target[0]
    s_len = xs.shape[0]
    tm_fwd = _row_tile(s_len, 512)
    tm = _row_tile(s_len, 256)
    tm_ffn = tm

    shards = {name: weights[name][0] for name, _, _ in SHARDED}
    late_shards = [(shards[name].T if axis == 1 else shards[name]).astype(BF16) for name, _, axis in LATE]
    late_weights = _push_start("weights_late_start", late_shards, [False] * len(LATE))
    full = _unpack_full(_all_gather(_pack_shards(shards, EARLY, EARLY_ROWS)), EARLY)
    wt = full["w_in"]
    zr = lambda n: jnp.zeros((n, D_MODEL), BF16)
    w_ext = jnp.concatenate([wt[:2176], zr(64), wt[2176:2208], zr(32), wt[2208:]], axis=0).T
    wa = jnp.pad(full["w_uq"].reshape(N_HEADS, MLA_QK_DIM, MLA_Q_RANK), ((0, 0), (0, 32), (0, 0))
                 ).reshape(N_HEADS * LANES, MLA_Q_RANK).T
    ukv = full["w_ukv"].reshape(N_HEADS, LANES, MLA_KV_RANK)
    wk = jnp.pad(ukv[:, :64], ((0, 0), (0, 64), (0, 0))).reshape(N_HEADS * LANES, MLA_KV_RANK).T
    wv = ukv[:, 64:].reshape(512, MLA_KV_RANK).T
    g_mix_pre = norm_mix_pre + late_weights[4][0:1, 0:1]
    cr, sr = _rope_tables(positions[0])

    qkv, cq, ckv, kr, gl, hb = _in_proj(xs, g_mix_pre, w_ext, tm_fwd)
    q_mla, k_mla, v_mla, cqn, ckvn = _mla_up(cq, ckv, kr, cr, sr, q_norm, kv_norm, wa, wk, wv, tm_fwd)
    o_sb, o_mla, lse = _attn_fwd(qkv, q_mla, k_mla, v_mla)
    late_shards, late_slots = _push_wait("weights_late_wait", late_weights, [False] * len(LATE), o_sb)
    full = {name: _own_slot(slots, own).reshape(-1, own.shape[1])
            for (name, _, _), slots, own in zip(LATE, late_slots, late_shards)}
    w_sb, w_mla, w_o, w_gu, w_dn = (full["w_proj_sb"].T, full["w_proj_mla"].T, full["w_out"], full["w_gate_up"].T,
                                    full["w_down"])
    x1, y, merged, o_sb_b, o_mla_b = _mix_out(o_sb, o_mla, gl, xs, b_gate, norm_mix_post, w_sb, w_mla, w_o, tm_fwd)
    dx2, f, h2, loss_part = _ffn_fwd(x1, target, norm_ffn_pre, norm_ffn_post, w_gu, w_dn, tm_fwd)
    loss_local = (0.5 / D_MODEL * jnp.sum(loss_part)).reshape(1)

    dx1, act, dgu, dfb, dg_ffn_pre, dg_ffn_post = _ffn_bwd(dx2, f, x1, norm_ffn_pre, norm_ffn_post, w_gu, w_dn, tm_ffn)
    dyb, dpsb, dpmla, dgl, do_sb, do_mla, dg_mix_post, db_gate = _mix_bwd(
        dx1, y, o_sb, o_mla, gl, b_gate, norm_mix_post, w_sb, w_mla, w_o, tm)
    late_grads = {
        "w_proj_sb": _matmul_tn("dw_proj_sb", dpsb, o_sb_b),
        "w_proj_mla": _matmul_tn("dw_proj_mla", dpmla, o_mla_b),
        "w_out": _matmul_tn("dw_out", merged, dyb),
        "w_gate_up": _matmul_tn("dw_gate_up", dgu, h2),
        "w_down": _matmul_tn("dw_down", act, dfb),
    }
    late_grads = [late_grads[name].reshape(N_DEV, -1, late_grads[name].shape[1]) for name, _, _ in LATE]
    late_exchange = _push_start("grads_late_start", late_grads, [True] * len(LATE))
    dq_sb, dq_mla, dk_sb, dv_sb, dk_mla, dv_mla = _attn_bwd(qkv, do_sb, o_sb, q_mla, k_mla, v_mla, do_mla, o_mla, lse,
                                                             late_exchange[4])
    da, dkb, dvb, dlat, dg_q, dg_kv = _mla_up_bwd(dq_mla, dk_mla, dv_mla, cq, ckv, cr, sr, q_norm, kv_norm,
                                                  wa, wk, wv, tm_fwd)
    dx, dproj, dg_mix_pre = _in_proj_bwd(xs, dx1, dq_sb, dk_sb, dv_sb, dlat, dgl, norm_mix_pre, w_ext, tm_fwd)
    d_ext = _matmul_tn("dw_in", dproj, hb)
    w_in_grad = jnp.concatenate([d_ext[:2176], d_ext[2240:2272], d_ext[EXT_GL:]], axis=0).reshape(N_DEV, 532, D_MODEL)
    w_in_exchange = _push_start("grads_w_in_start", [w_in_grad], [True])
    behind = w_in_exchange[4][0:1, 0:1].astype(BF16)
    d_wa = _matmul_tn("dw_uq", da, cqn + behind)
    d_wk = _matmul_tn("dw_uk", dkb, ckvn)
    d_wv = _matmul_tn("dw_uv", dvb, ckvn)
    early_grads = _pack_full_grads({
        "w_uq": d_wa.reshape(N_HEADS, LANES, MLA_Q_RANK)[:, :MLA_QK_DIM].reshape(768, MLA_Q_RANK),
        "w_ukv": jnp.concatenate([d_wk.reshape(N_HEADS, LANES, MLA_KV_RANK)[:, :64],
                                  d_wv.reshape(N_HEADS, 64, MLA_KV_RANK)], axis=1).reshape(1024, MLA_KV_RANK),
    }, EARLY[1:], EARLY_REST_ROWS)
    small_parts = dict(norm_mix_pre=dg_mix_pre, norm_mix_post=dg_mix_post, b_gate=db_gate, q_norm=dg_q,
                       kv_norm=dg_kv, norm_ffn_pre=dg_ffn_pre, norm_ffn_post=dg_ffn_post)
    small = jnp.concatenate([small_parts[name].sum(axis=0) for name, _ in SMALL] + [loss_local])
    small = jnp.pad(small, (0, SMALL_ROWS * LANES - small.shape[0])).reshape(SMALL_ROWS, LANES)

    early_exchange = _push_start("grads_early_start", [early_grads, small], [True, False])
    late_grads, late_slots = _push_wait("grads_late_wait", late_exchange, [True] * len(LATE), early_exchange[4])
    mine = _flat_index(_mesh_pos())
    g_out, deltas, new_m, new_v = {}, {}, {}, {}

    def adamw(name):
        w2 = weights[name].reshape(g_out[name].shape)
        d, nm, nv = _adamw("adamw_" + name, w2, g_out[name], m_in[name].reshape(w2.shape), v_in[name].reshape(w2.shape))
        shape = weights[name].shape
        deltas[name], new_m[name], new_v[name] = d.reshape(shape), nm.reshape(shape), nv.reshape(shape)
        g_out[name] = g_out[name].reshape(shape)

    for (name, _, axis), slots, own in zip(LATE, late_slots, late_grads):
        slots = _own_slot(slots, lax.dynamic_index_in_dim(own, mine, 0, keepdims=False))
        total = _slot_sum("grad_sum_" + name, slots, _pick(slots.shape[1], (352, 128)))
        g_out[name] = total.T if axis == 1 else total
        adamw(name)
    (early_grads, small), (early_slots, small_slots) = _push_wait(
        "grads_early_wait", early_exchange, [True, False], deltas["w_down"])
    early_slots = _own_slot(early_slots, lax.dynamic_index_in_dim(early_grads, mine, 0, keepdims=False))
    g_out.update(_unpack_shard_grads(_slot_sum("grad_early_sum", early_slots, EARLY_REST_ROWS), EARLY[1:]))
    (w_in_grad,), (w_in_slots,) = _push_wait("grads_w_in_wait", w_in_exchange, [True], early_slots)
    w_in_slots = _own_slot(w_in_slots, lax.dynamic_index_in_dim(w_in_grad, mine, 0, keepdims=False))
    g_out["w_in"] = _slot_sum("grad_sum_w_in", w_in_slots, 532).T
    s_flat = _slot_sum("grad_small_sum", _own_slot(small_slots, small), SMALL_ROWS).reshape(-1)
    off = 0
    for name, n in SMALL:
        g_out[name] = s_flat[off:off + n].reshape(1, n)
        off += n
    loss = s_flat[off]
    for name in order:
        if name not in deltas:
            adamw(name)

    return (loss, dx[None], *[g_out[n] for n in order], *[deltas[n] for n in order],
            *[new_m[n] for n in order], *[new_v[n] for n in order])
```
